```python
import math
import jax, jax.numpy as jnp
from jax import lax
import numpy as np

D_MODEL = 1024
BATCH = 8
SEQ = 4096
DEPTH = 1

EPS = 1e-6
ROPE_THETA = 500000.0
ROPE_FRACTION = 4
ROT_HEAD_DIM = 64
ROPE_DIM = ROT_HEAD_DIM // ROPE_FRACTION
Q_BLOCK = 128
MAX_POS_OFFSET = 2048

DA_HEADS = 4
DA_QK_DIM = 64
DA_V_DIM = 2 * DA_QK_DIM
DA_WIDTH = DA_HEADS * DA_V_DIM
DS_HEADS = 8
DS_KV_HEADS = 2
DS_HEAD_DIM = 64
DS_WIDTH = DS_HEADS * DS_HEAD_DIM
IDX_HEADS = 8
IDX_DIM = 64
TOPK_MAX = 256
MEM_TOKENS = 256
MEM_HEADS = 4
MEM_HEAD_DIM = 128
MEM_WIDTH = MEM_HEADS * MEM_HEAD_DIM
PEER_HEADS = 8
PEER_N_KEYS = 128
PEER_N_EXPERTS = PEER_N_KEYS * PEER_N_KEYS
PEER_QUERY_DIM = 128
PEER_HALF = PEER_QUERY_DIM // 2
PEER_TOPK = 16
PEER_CHUNK = 128

DA_Q_COLS = DA_HEADS * 2 * DA_QK_DIM
DA_K_COLS = DA_HEADS * 2 * DA_QK_DIM
DA_V_COLS = DA_WIDTH
DS_Q_COLS = DS_WIDTH
DS_K_COLS = DS_KV_HEADS * DS_HEAD_DIM
DS_V_COLS = DS_KV_HEADS * DS_HEAD_DIM
IX_Q_COLS = IDX_HEADS * IDX_DIM
IX_K_COLS = IDX_DIM
IX_W_COLS = IDX_HEADS
GATE_COLS = 2 * D_MODEL
COL_SIZES = (DA_Q_COLS, DA_K_COLS, DA_V_COLS, DS_Q_COLS, DS_K_COLS, DS_V_COLS, IX_Q_COLS, IX_K_COLS, IX_W_COLS, GATE_COLS)
SPLITS = tuple(int(c) for c in np.cumsum(COL_SIZES)[:-1])
IN_COLS = int(sum(COL_SIZES))

kernel_name = "hybrid_diffattn_dsa_peer_block"


def rms_norm(x, g):
    xf = x.astype(jnp.float32)
    y = xf * lax.rsqrt(jnp.mean(xf * xf, axis=-1, keepdims=True) + EPS)
    return (y * g.astype(jnp.float32)).astype(x.dtype)


def rope_tables(positions):
    half = ROPE_DIM // 2
    inv_freq = ROPE_THETA ** (-(jnp.arange(half, dtype=jnp.float32) * 2.0) / ROPE_DIM)
    ang = positions.astype(jnp.float32)[..., None] * inv_freq
    return jnp.cos(ang), jnp.sin(ang)


def apply_partial_rope(t, cos, sin):
    r = t.shape[-1] // ROPE_FRACTION
    half = r // 2
    bshape = cos.shape[:2] + (1,) * (t.ndim - 3) + (half,)
    c = cos.reshape(bshape).astype(t.dtype)
    s = sin.reshape(bshape).astype(t.dtype)
    t1 = t[..., :half]
    t2 = t[..., half:r]
    return jnp.concatenate([t1 * c - t2 * s, t2 * c + t1 * s, t[..., r:]], axis=-1)


def to_blocks(a):
    b, s = a.shape[:2]
    return jnp.moveaxis(a.reshape((b, s // Q_BLOCK, Q_BLOCK) + a.shape[2:]), 1, 0)


def from_blocks(a):
    a = jnp.moveaxis(a, 0, 1)
    return a.reshape((a.shape[0], a.shape[1] * a.shape[2]) + a.shape[3:])


def diff_attention(q, k, v, lam, lam_init, subln_g):
    b, s_len = q.shape[:2]
    scale = DA_QK_DIM ** -0.5
    key_pos = jnp.arange(s_len)

    def block(args):
        qb, i = args
        sc = jnp.einsum('bqhmd,bshmd->bhmqs', qb, k).astype(jnp.float32) * scale
        qpos = i * Q_BLOCK + jnp.arange(Q_BLOCK)
        causal = key_pos[None, :] <= qpos[:, None]
        sc = jnp.where(causal, sc, -jnp.inf)
        p = jax.nn.softmax(sc, axis=-1)
        a = p[:, :, 0] - lam * p[:, :, 1]
        return jnp.einsum('bhqs,bshd->bqhd', a.astype(v.dtype), v)

    o = from_blocks(lax.map(block, (to_blocks(q), jnp.arange(s_len // Q_BLOCK))))
    o = rms_norm(o, subln_g) * (1.0 - lam_init)
    return o.reshape(b, s_len, DA_WIDTH)


def dsa_attention(q, k, v, qi, ki, wi):
    b, s_len = q.shape[:2]
    n_sel = min(TOPK_MAX, s_len // 4)
    key_pos = jnp.arange(s_len)
    scale = DS_HEAD_DIM ** -0.5
    take = jax.vmap(lambda a, idx: a[idx])

    def block(args):
        qb, qib, wib, i = args
        qpos = i * Q_BLOCK + jnp.arange(Q_BLOCK)
        causal = key_pos[None, :] <= qpos[:, None]
        idx_s = jax.nn.relu(jnp.einsum('bqhd,bsd->bqhs', qib, ki))
        idx_s = jnp.einsum('bqh,bqhs->bqs', wib, idx_s).astype(jnp.float32)
        idx_s = jnp.where(causal[None], idx_s, -jnp.inf)
        _, sel = lax.top_k(idx_s, n_sel)
        valid = sel <= qpos[None, :, None]
        ks = take(k, sel)
        vs = take(v, sel)
        qg = qb.reshape(b, Q_BLOCK, DS_KV_HEADS, DS_HEADS // DS_KV_HEADS, DS_HEAD_DIM)
        sc = jnp.einsum('bqgrd,bqkgd->bqgrk', qg, ks).astype(jnp.float32) * scale
        sc = jnp.where(valid[:, :, None, None, :], sc, -jnp.inf)
        p = jax.nn.softmax(sc, axis=-1)
        o = jnp.einsum('bqgrk,bqkgd->bqgrd', p.astype(vs.dtype), vs)
        return o.reshape(b, Q_BLOCK, DS_WIDTH)

    o = lax.map(block, (to_blocks(q), to_blocks(qi), to_blocks(wi), jnp.arange(s_len // Q_BLOCK)))
    return from_blocks(o)


def memory_attention(h, mem_n, w_q, w_kv, w_o):
    b, s_len, _ = h.shape
    m = mem_n.shape[1]
    q = (h @ w_q).reshape(b, s_len, MEM_HEADS, MEM_HEAD_DIM)
    kv = (mem_n @ w_kv).reshape(b, m, 2, MEM_HEADS, MEM_HEAD_DIM)
    k, v = kv[:, :, 0], kv[:, :, 1]
    sc = jnp.einsum('bshd,bmhd->bhsm', q, k).astype(jnp.float32) * (MEM_HEAD_DIM ** -0.5)
    p = jax.nn.softmax(sc, axis=-1)
    o = jnp.einsum('bhsm,bmhd->bshd', p.astype(v.dtype), v).reshape(b, s_len, MEM_WIDTH)
    return o @ w_o


def peer_ffn(h, w_q, sub_keys, u, v):
    b, s_len, d = h.shape
    n = b * s_len
    hf = h.reshape(n, d)
    q = (hf @ w_q).reshape(n, PEER_HEADS, 2, PEER_HALF)
    sc = jnp.einsum('nhcd,hckd->nhck', q, sub_keys).astype(jnp.float32)
    s_top, i_top = lax.top_k(sc, PEER_TOPK)
    cand_s = s_top[:, :, 0, :, None] + s_top[:, :, 1, None, :]
    cand_i = i_top[:, :, 0, :, None] * PEER_N_KEYS + i_top[:, :, 1, None, :]
    cand_s = cand_s.reshape(n, PEER_HEADS, PEER_TOPK * PEER_TOPK)
    cand_i = cand_i.reshape(n, PEER_HEADS, PEER_TOPK * PEER_TOPK)
    best_s, pos = lax.top_k(cand_s, PEER_TOPK)
    best_i = jnp.take_along_axis(cand_i, pos, axis=-1)
    g = jax.nn.softmax(best_s, axis=-1)
    idx = best_i.reshape(n, PEER_HEADS * PEER_TOPK)
    g = g.reshape(n, PEER_HEADS * PEER_TOPK)
    nc = n // PEER_CHUNK

    def chunk(args):
        hc, ic, gc = args
        a = jnp.einsum('cd,ced->ce', hc, u[ic])
        a = jax.nn.gelu(a, approximate=False) * gc.astype(hc.dtype)
        return jnp.einsum('ce,ced->cd', a, v[ic])

    out = lax.map(chunk, (hf.reshape(nc, PEER_CHUNK, d), idx.reshape(nc, PEER_CHUNK, -1), g.reshape(nc, PEER_CHUNK, -1)))
    return out.reshape(b, s_len, d)


def setup_inputs(seed: int = 0) -> dict:
    key = jax.random.key(seed)
    ks = jax.random.split(key, 24)
    f32 = jnp.float32

    def nrm(k, shape, scale):
        return jax.random.normal(k, shape, f32) * scale

    def gain(k, shape):
        return 1.0 + 0.02 * jax.random.normal(k, shape, f32)

    D = D_MODEL
    x = nrm(ks[0], (BATCH, SEQ, D), 1.0)
    mem = nrm(ks[1], (BATCH, MEM_TOKENS, D), 1.0)
    positions = jnp.arange(SEQ, dtype=jnp.int32)[None, :] + jax.random.randint(ks[2], (BATCH, 1), 0, MAX_POS_OFFSET, dtype=jnp.int32)
    return {
        "x": x,
        "mem": mem,
        "positions": positions,
        "norm_mix_g": gain(ks[3], (DEPTH, D)),
        "w_in": nrm(ks[4], (DEPTH, D, IN_COLS), D ** -0.5),
        "da_lambda": nrm(ks[5], (DEPTH, 4, DA_QK_DIM), 0.1),
        "da_subln_g": gain(ks[6], (DEPTH, DA_V_DIM)),
        "w_branch_a": nrm(ks[7], (DEPTH, DA_WIDTH, D), DA_WIDTH ** -0.5),
        "w_branch_b": nrm(ks[8], (DEPTH, DS_WIDTH, D), DS_WIDTH ** -0.5),
        "gate_bias": nrm(ks[9], (DEPTH, GATE_COLS), 0.02),
        "w_out": nrm(ks[10], (DEPTH, D, D), D ** -0.5),
        "norm_mem_g": gain(ks[11], (DEPTH, D)),
        "mem_kv_norm_g": gain(ks[12], (DEPTH, D)),
        "w_mem_q": nrm(ks[13], (DEPTH, D, MEM_WIDTH), D ** -0.5),
        "w_mem_kv": nrm(ks[14], (DEPTH, D, 2 * MEM_WIDTH), D ** -0.5),
        "w_mem_o": nrm(ks[15], (DEPTH, MEM_WIDTH, D), MEM_WIDTH ** -0.5),
        "norm_ffn_g": gain(ks[16], (DEPTH, D)),
        "peer_w_q": nrm(ks[17], (DEPTH, D, PEER_HEADS * PEER_QUERY_DIM), D ** -0.5),
        "peer_sub_keys": nrm(ks[18], (DEPTH, PEER_HEADS, 2, PEER_N_KEYS, PEER_HALF), PEER_HALF ** -0.5),
        "peer_u": nrm(ks[19], (DEPTH, PEER_N_EXPERTS, D), D ** -0.5),
        "peer_v": nrm(ks[20], (DEPTH, PEER_N_EXPERTS, D), PEER_TOPK ** -0.5),
        "final_norm_g": gain(ks[21], (D,)),
    }


def reference(x, mem, positions, norm_mix_g, w_in, da_lambda, da_subln_g, w_branch_a, w_branch_b, gate_bias, w_out, norm_mem_g, mem_kv_norm_g, w_mem_q, w_mem_kv, w_mem_o, norm_ffn_g, peer_w_q, peer_sub_keys, peer_u, peer_v, final_norm_g):
    b, s_len, d = x.shape
    cos, sin = rope_tables(positions)
    for l in range(DEPTH):
        h = rms_norm(x, norm_mix_g[l])
        proj = h @ w_in[l]
        da_q, da_k, da_v, ds_q, ds_k, ds_v, ix_q, ix_k, ix_w, gates = jnp.split(proj, SPLITS, axis=-1)

        lam_init = 0.8 - 0.6 * math.exp(-0.3 * l)
        lp = da_lambda[l].astype(jnp.float32)
        lam = jnp.exp(jnp.sum(lp[0] * lp[1])) - jnp.exp(jnp.sum(lp[2] * lp[3])) + lam_init
        qa = apply_partial_rope(da_q.reshape(b, s_len, DA_HEADS, 2, DA_QK_DIM), cos, sin)
        ka = apply_partial_rope(da_k.reshape(b, s_len, DA_HEADS, 2, DA_QK_DIM), cos, sin)
        va = da_v.reshape(b, s_len, DA_HEADS, DA_V_DIM)
        o_a = diff_attention(qa, ka, va, lam, lam_init, da_subln_g[l])

        qb = apply_partial_rope(ds_q.reshape(b, s_len, DS_HEADS, DS_HEAD_DIM), cos, sin)
        kb = apply_partial_rope(ds_k.reshape(b, s_len, DS_KV_HEADS, DS_HEAD_DIM), cos, sin)
        vb = ds_v.reshape(b, s_len, DS_KV_HEADS, DS_HEAD_DIM)
        qi = apply_partial_rope(ix_q.reshape(b, s_len, IDX_HEADS, IDX_DIM), cos, sin)
        ki = apply_partial_rope(ix_k, cos, sin)
        o_b = dsa_attention(qb, kb, vb, qi, ki, ix_w)

        g = jax.nn.sigmoid((gates + gate_bias[l]).astype(jnp.float32)).astype(x.dtype)
        mix = g[..., :d] * (o_a @ w_branch_a[l]) + g[..., d:] * (o_b @ w_branch_b[l])
        x = x + mix @ w_out[l]

        x = x + memory_attention(rms_norm(x, norm_mem_g[l]), rms_norm(mem, mem_kv_norm_g[l]), w_mem_q[l], w_mem_kv[l], w_mem_o[l])

        x = x + peer_ffn(rms_norm(x, norm_ffn_g[l]), peer_w_q[l], peer_sub_keys[l], peer_u[l], peer_v[l])
    return rms_norm(x, final_norm_g)
```

```python
import functools
import math

import jax
import jax.numpy as jnp
from jax import lax
from jax.experimental import pallas as pl
from jax.experimental.pallas import tpu as pltpu

F32 = jnp.float32
BF16 = jnp.bfloat16
I32 = jnp.int32

D_MODEL = 1024
EPS = 1e-6
ROPE_THETA = 500000.0
ROT_HEAD_DIM = 64
ROPE_DIM = 16
ROPE_HALF = ROPE_DIM // 2

DA_HEADS = 4
DA_QK_DIM = 64
DA_V_DIM = 128
DA_WIDTH = 512
DS_HEADS = 8
DS_KV_HEADS = 2
DS_HEAD_DIM = 64
DS_WIDTH = 512
IDX_HEADS = 8
IDX_DIM = 64
TOPK_MAX = 256
MEM_HEADS = 4
MEM_HEAD_DIM = 128
MEM_WIDTH = 512
PEER_HEADS = 8
PEER_N_KEYS = 128
PEER_HALF = 64
PEER_TOPK = 16
PEER_SEL = PEER_HEADS * PEER_TOPK

COL_SIZES = (512, 512, 512, 512, 128, 128, 512, 64, 8, 2048)

LANES = 128
SUBLANES = 8
NEG_BIG = -1e30
INT_MIN = -(2 ** 31)
VMEM_LIMIT = 56 * 1024 * 1024

NT_DIMS = (((1,), (1,)), ((), ()))


def _rms(x, g):
    var = jnp.mean(x * x, axis=-1, keepdims=True)
    return x * lax.rsqrt(var + EPS) * g


def _in_proj_kernel(x_ref, g_ref, pos_ref, invf_ref, bias_ref, wr_ref, wp_ref, wg_ref, ww_ref,
                    pr_ref, pp_ref, gate_ref, iw_ref):
    h = _rms(x_ref[...], g_ref[...]).astype(BF16)
    tm = h.shape[0]
    ang = pos_ref[...] * invf_ref[...]
    cos = jnp.cos(ang)
    sin = jnp.sin(ang)
    lane = lax.broadcasted_iota(I32, (tm, LANES), 1) % ROT_HEAD_DIM
    c_t = jnp.where(lane < ROPE_DIM, cos, 1.0)
    s_lo = jnp.where(lane < ROPE_HALF, -sin, 0.0)
    s_hi = jnp.where((lane >= ROPE_HALF) & (lane < ROPE_DIM), sin, 0.0)
    n_r = wr_ref.shape[1]
    for c0 in range(0, n_r, 512):
        c1 = min(c0 + 512, n_r)
        w = c1 - c0
        y = jnp.dot(h, wr_ref[:, c0:c1], preferred_element_type=F32)
        reps = w // LANES
        ct = jnp.concatenate([c_t] * reps, axis=1)
        sl = jnp.concatenate([s_lo] * reps, axis=1)
        sh = jnp.concatenate([s_hi] * reps, axis=1)
        y = y * ct + pltpu.roll(y, w - ROPE_HALF, 1) * sl + pltpu.roll(y, ROPE_HALF, 1) * sh
        pr_ref[:, c0:c1] = y.astype(pr_ref.dtype)
    pp_ref[...] = jnp.dot(h, wp_ref[...], preferred_element_type=F32).astype(pp_ref.dtype)
    n_g = wg_ref.shape[1]
    for c0 in range(0, n_g, 512):
        y = jnp.dot(h, wg_ref[:, c0:c0 + 512], preferred_element_type=F32)
        gate_ref[:, c0:c0 + 512] = jax.nn.sigmoid(y + bias_ref[:, c0:c0 + 512])
    iw_ref[...] = jnp.dot(h, ww_ref[...], preferred_element_type=F32)


def _in_proj(x2, g, pos, invf, bias, wr, wp, wg, ww, tm):
    n = x2.shape[0]
    full = lambda a: pl.BlockSpec(a.shape, lambda i: (0, 0))
    rows = lambda w: pl.BlockSpec((tm, w), lambda i: (i, 0))
    return pl.pallas_call(
        _in_proj_kernel,
        grid=(n // tm,),
        in_specs=[rows(D_MODEL), full(g), rows(1), full(invf), full(bias), full(wr), full(wp), full(wg), full(ww)],
        out_specs=[rows(wr.shape[1]), rows(wp.shape[1]), rows(wg.shape[1]), rows(ww.shape[1])],
        out_shape=[jax.ShapeDtypeStruct((n, wr.shape[1]), BF16), jax.ShapeDtypeStruct((n, wp.shape[1]), BF16),
                   jax.ShapeDtypeStruct((n, wg.shape[1]), F32), jax.ShapeDtypeStruct((n, ww.shape[1]), F32)],
        compiler_params=pltpu.CompilerParams(dimension_semantics=("arbitrary",), vmem_limit_bytes=VMEM_LIMIT),
        name="in_proj",
    )(x2, g, pos, invf, bias, wr, wp, wg, ww)


def _da_kernel(lam_ref, q_ref, k_ref, v_ref, g_ref, o_ref, *, tq, tk, lam_init):
    i = pl.program_id(2)
    q = q_ref[...]
    lane = lax.broadcasted_iota(I32, q.shape, 1)
    qs = q * jnp.asarray(DA_QK_DIM ** -0.5, q.dtype)
    zero = jnp.zeros_like(qs)
    q_maps = (jnp.where(lane < DA_QK_DIM, qs, zero), jnp.where(lane >= DA_QK_DIM, qs, zero))
    row = i * tq + lax.broadcasted_iota(I32, (tq, tk), 0)
    col0 = lax.broadcasted_iota(I32, (tq, tk), 1)

    def step(j, carry, masked):
        start = pl.multiple_of(j * tk, tk)
        k = k_ref[pl.ds(start, tk), :]
        v = v_ref[pl.ds(start, tk), :]
        out = []
        for mp in range(2):
            m, l, a = carry[3 * mp:3 * mp + 3]
            s = lax.dot_general(q_maps[mp], k, NT_DIMS, preferred_element_type=F32)
            if masked:
                s = jnp.where(col0 + j * tk <= row, s, NEG_BIG)
            m_new = jnp.maximum(m, jnp.max(s, axis=1, keepdims=True))
            alpha = jnp.exp(m - m_new)
            p = jnp.exp(s - m_new)
            l = alpha * l + jnp.sum(p, axis=1, keepdims=True)
            a = alpha * a + jnp.dot(p.astype(v.dtype), v, preferred_element_type=F32)
            out += [m_new, l, a]
        return tuple(out)

    init = []
    for _ in range(2):
        init += [jnp.full((tq, 1), NEG_BIG, F32), jnp.zeros((tq, 1), F32), jnp.zeros((tq, DA_V_DIM), F32)]
    n_full = (i * tq + 1) // tk
    n_tot = ((i + 1) * tq + tk - 1) // tk
    carry = lax.fori_loop(0, n_full, lambda j, c: step(j, c, False), tuple(init))
    carry = lax.fori_loop(n_full, n_tot, lambda j, c: step(j, c, True), carry)
    _, l1, a1, _, l2, a2 = carry

    lp = lam_ref[...]
    lam = (jnp.exp(jnp.sum(lp[0:1] * lp[1:2], axis=1, keepdims=True))
           - jnp.exp(jnp.sum(lp[2:3] * lp[3:4], axis=1, keepdims=True)) + lam_init)
    o = a1 / l1 - lam * (a2 / l2)
    o = _rms(o, g_ref[...]) * (1.0 - lam_init)
    o_ref[...] = o.astype(o_ref.dtype)


def _diff_attention(lam_p, pr, pp, subln_g, b, s_len, tq, tk, lam_init):
    nq = s_len // tq
    n = b * s_len
    kern = functools.partial(_da_kernel, tq=tq, tk=tk, lam_init=lam_init)
    return pl.pallas_call(
        kern,
        grid=(b, DA_HEADS, nq),
        in_specs=[
            pl.BlockSpec(lam_p.shape, lambda bi, h, i: (0, 0)),
            pl.BlockSpec((tq, LANES), lambda bi, h, i: (bi * nq + i, h)),
            pl.BlockSpec((s_len, LANES), lambda bi, h, i: (bi, DA_HEADS + h)),
            pl.BlockSpec((s_len, LANES), lambda bi, h, i: (bi, h)),
            pl.BlockSpec(subln_g.shape, lambda bi, h, i: (0, 0)),
        ],
        out_specs=pl.BlockSpec((tq, LANES), lambda bi, h, i: (bi * nq + i, h)),
        out_shape=jax.ShapeDtypeStruct((n, DA_WIDTH), BF16),
        compiler_params=pltpu.CompilerParams(dimension_semantics=("arbitrary",) * 3, vmem_limit_bytes=VMEM_LIMIT),
        name="diff_attn",
    )(lam_p, pr, pr, pp, subln_g)


PR_DSQ = 8
PR_IXQ = 12
PR_DSK = 16
PR_IXK = 18
PP_DSV = 4


def _dsa_kernel(qi_ref, ki_ref, w_ref, q_ref, k0_ref, k1_ref, v_ref, o_ref, keys_scr, *, tq, n_sel, n_bits):
    tk = tq
    i = pl.program_id(1)
    nck = i + 1
    lane = lax.broadcasted_iota(I32, (tq, LANES), 1)
    low_half = lane < DS_HEAD_DIM
    row = i * tq + lax.broadcasted_iota(I32, (tq, tk), 0)
    col0 = lax.broadcasted_iota(I32, (tq, tk), 1)

    def half_masked(ref, h):
        blk = ref[:, (h // 2) * LANES:(h // 2 + 1) * LANES]
        keep = low_half if h % 2 == 0 else jnp.logical_not(low_half)
        return jnp.where(keep, blk, jnp.zeros_like(blk))

    w = w_ref[...]

    def score_chunk(c, carry):
        kc = ki_ref[pl.ds(pl.multiple_of(c * tk, tk), tk), :]
        acc = jnp.zeros((tq, tk), F32)
        for h in range(IDX_HEADS):
            s = lax.dot_general(half_masked(qi_ref, h), kc, NT_DIMS, preferred_element_type=F32)
            acc = acc + w[:, h:h + 1] * jnp.maximum(s, 0.0)
        bits = pltpu.bitcast(acc, I32)
        key = bits ^ ((bits >> 31) & 0x7FFFFFFF)
        key = jnp.where(acc == 0.0, 0, key)
        key = jnp.where(col0 + c * tk <= row, key, INT_MIN)
        keys_scr[c] = key
        return carry

    lax.fori_loop(0, nck, score_chunk, 0)

    def count(pred):
        def body(c, acc):
            hit = pred(keys_scr[c], col0 + c * tk)
            return acc + hit[:, :LANES] + hit[:, LANES:] if tk == 2 * LANES else acc + hit
        width = LANES if tk == 2 * LANES else tk
        acc = lax.fori_loop(0, nck, body, jnp.zeros((tq, width), I32))
        return jnp.sum(acc, axis=1, keepdims=True)

    def thr_bit(t, ans_u):
        cand_u = ans_u | jnp.left_shift(jnp.int32(1), 31 - t)
        cand = cand_u ^ INT_MIN
        cnt = count(lambda k, col: jnp.where(k >= cand, 1, 0))
        return jnp.where(cnt >= n_sel, cand_u, ans_u)

    thr = lax.fori_loop(0, 32, thr_bit, jnp.zeros((tq, 1), I32)) ^ INT_MIN
    n_gt = count(lambda k, col: jnp.where(k > thr, 1, 0))
    need = n_sel - n_gt

    def tie_bit(t, j_cut):
        cand = j_cut | jnp.left_shift(jnp.int32(1), n_bits - 1 - t)
        cnt = count(lambda k, col: jnp.where(k == thr, jnp.where(col < cand, 1, 0), 0))
        return jnp.where(cnt < need, cand, j_cut)

    j_cut = lax.fori_loop(0, n_bits, tie_bit, jnp.zeros((tq, 1), I32))
    j_cut = jnp.where(thr == INT_MIN, -1, j_cut)

    def bias_chunk(c, carry):
        k = keys_scr[c]
        sel = (k > thr) | ((k == thr) & (col0 + c * tk <= j_cut))
        keys_scr[c] = pltpu.bitcast(jnp.where(sel, 0.0, NEG_BIG).astype(F32), I32)
        return carry

    lax.fori_loop(0, nck, bias_chunk, 0)

    scale = jnp.asarray(DS_HEAD_DIM ** -0.5, BF16)
    outs = []
    for h in range(DS_HEADS):
        grp = h // (DS_HEADS // DS_KV_HEADS)
        k_ref = k0_ref if grp == 0 else k1_ref
        qh = half_masked(q_ref, h) * scale

        def att_chunk(c, carry, k_ref=k_ref, qh=qh):
            m, l, a = carry
            start = pl.multiple_of(c * tk, tk)
            s = lax.dot_general(qh, k_ref[pl.ds(start, tk), :], NT_DIMS, preferred_element_type=F32)
            s = s + pltpu.bitcast(keys_scr[c], F32)
            m_new = jnp.maximum(m, jnp.max(s, axis=1, keepdims=True))
            alpha = jnp.exp(m - m_new)
            p = jnp.exp(s - m_new)
            l = alpha * l + jnp.sum(p, axis=1, keepdims=True)
            v = v_ref[pl.ds(start, tk), :]
            a = alpha * a + jnp.dot(p.astype(v.dtype), v, preferred_element_type=F32)
            return m_new, l, a

        init = (jnp.full((tq, 1), NEG_BIG, F32), jnp.zeros((tq, 1), F32), jnp.zeros((tq, LANES), F32))
        _, l, a = lax.fori_loop(0, nck, att_chunk, init)
        o = a / l
        want_low = h % 2 == 0
        have_low = grp == 0
        outs.append(o if want_low == have_low else pltpu.roll(o, DS_HEAD_DIM, 1))
    for pair in range(DS_HEADS // 2):
        blk = jnp.where(low_half, outs[2 * pair], outs[2 * pair + 1])
        o_ref[:, pair * LANES:(pair + 1) * LANES] = blk.astype(o_ref.dtype)


def _dsa_attention(pr, pp, iw, b, s_len, tq):
    nq = s_len // tq
    n = b * s_len
    n_sel = min(TOPK_MAX, s_len // 4)
    n_bits = max(1, (s_len - 1).bit_length())
    kern = functools.partial(_dsa_kernel, tq=tq, n_sel=n_sel, n_bits=n_bits)
    seq = lambda blk: pl.BlockSpec((s_len, LANES), lambda bi, i: (bi, blk))
    return pl.pallas_call(
        kern,
        grid=(b, nq),
        in_specs=[
            pl.BlockSpec((tq, 4 * LANES), lambda bi, i: (bi * nq + i, PR_IXQ // 4)),
            seq(PR_IXK),
            pl.BlockSpec((tq, LANES), lambda bi, i: (bi * nq + i, 0)),
            pl.BlockSpec((tq, 4 * LANES), lambda bi, i: (bi * nq + i, PR_DSQ // 4)),
            seq(PR_DSK), seq(PR_DSK + 1),
            seq(PP_DSV),
        ],
        out_specs=pl.BlockSpec((tq, DS_WIDTH), lambda bi, i: (bi * nq + i, 0)),
        out_shape=jax.ShapeDtypeStruct((n, DS_WIDTH), BF16),
        scratch_shapes=[pltpu.VMEM((nq, tq, tq), I32)],
        compiler_params=pltpu.CompilerParams(dimension_semantics=("arbitrary",) * 2, vmem_limit_bytes=VMEM_LIMIT),
        name="dsa_attn",
    )(pr, pr, iw, pr, pr, pr, pp)


def _merge_kernel(oa_ref, ob_ref, gate_ref, x_ref, wa_ref, wb_ref, wo_ref, o_ref):
    ya = jnp.dot(oa_ref[...], wa_ref[...], preferred_element_type=F32)
    yb = jnp.dot(ob_ref[...], wb_ref[...], preferred_element_type=F32)
    mix = gate_ref[:, :D_MODEL] * ya + gate_ref[:, D_MODEL:] * yb
    o_ref[...] = x_ref[...] + jnp.dot(mix.astype(BF16), wo_ref[...], preferred_element_type=F32)


def _merge(oa, ob, gate, x2, wa, wb, wo, tm):
    n = x2.shape[0]
    full = lambda a: pl.BlockSpec(a.shape, lambda i: (0, 0))
    rows = lambda w: pl.BlockSpec((tm, w), lambda i: (i, 0))
    return pl.pallas_call(
        _merge_kernel,
        grid=(n // tm,),
        in_specs=[rows(DA_WIDTH), rows(DS_WIDTH), rows(2 * D_MODEL), rows(D_MODEL), full(wa), full(wb), full(wo)],
        out_specs=rows(D_MODEL),
        out_shape=jax.ShapeDtypeStruct((n, D_MODEL), F32),
        compiler_params=pltpu.CompilerParams(dimension_semantics=("arbitrary",), vmem_limit_bytes=VMEM_LIMIT),
        name="merge",
    )(oa, ob, gate, x2, wa, wb, wo)


def _norm_matmul_kernel(x_ref, g_ref, w_ref, o_ref):
    h = _rms(x_ref[...], g_ref[...]).astype(BF16)
    o_ref[...] = jnp.dot(h, w_ref[...], preferred_element_type=F32).astype(o_ref.dtype)


def _norm_matmul(x2, g, w, tm):
    n = x2.shape[0]
    return pl.pallas_call(
        _norm_matmul_kernel,
        grid=(n // tm,),
        in_specs=[pl.BlockSpec((tm, x2.shape[1]), lambda i: (i, 0)), pl.BlockSpec(g.shape, lambda i: (0, 0)),
                  pl.BlockSpec(w.shape, lambda i: (0, 0))],
        out_specs=pl.BlockSpec((tm, w.shape[1]), lambda i: (i, 0)),
        out_shape=jax.ShapeDtypeStruct((n, w.shape[1]), BF16),
        compiler_params=pltpu.CompilerParams(dimension_semantics=("arbitrary",), vmem_limit_bytes=VMEM_LIMIT),
        name="mem_kv_proj",
    )(x2, g, w)


def _mem_attn_kernel(x_ref, g_ref, wq_ref, kv_ref, wo_ref, o_ref):
    x = x_ref[...]
    hn = _rms(x, g_ref[...]).astype(BF16)
    q = jnp.dot(hn, wq_ref[...], preferred_element_type=F32).astype(BF16)
    scale = MEM_HEAD_DIM ** -0.5
    heads = []
    for h in range(MEM_HEADS):
        qh = q[:, h * MEM_HEAD_DIM:(h + 1) * MEM_HEAD_DIM]
        kh = kv_ref[:, h * MEM_HEAD_DIM:(h + 1) * MEM_HEAD_DIM]
        vh = kv_ref[:, MEM_WIDTH + h * MEM_HEAD_DIM:MEM_WIDTH + (h + 1) * MEM_HEAD_DIM]
        s = lax.dot_general(qh, kh, NT_DIMS, preferred_element_type=F32) * scale
        m = jnp.max(s, axis=1, keepdims=True)
        p = jnp.exp(s - m)
        l = jnp.sum(p, axis=1, keepdims=True)
        heads.append(jnp.dot(p.astype(BF16), vh, preferred_element_type=F32) / l)
    o = jnp.concatenate(heads, axis=1).astype(BF16)
    o_ref[...] = x + jnp.dot(o, wo_ref[...], preferred_element_type=F32)


def _mem_attn(x2, g, wq, kv, wo, b, s_len, m_tok, tm):
    nb = s_len // tm
    n = x2.shape[0]
    full = lambda a: pl.BlockSpec(a.shape, lambda bi, i: (0, 0))
    return pl.pallas_call(
        _mem_attn_kernel,
        grid=(b, nb),
        in_specs=[pl.BlockSpec((tm, D_MODEL), lambda bi, i: (bi * nb + i, 0)), full(g), full(wq),
                  pl.BlockSpec((m_tok, 2 * MEM_WIDTH), lambda bi, i: (bi, 0)), full(wo)],
        out_specs=pl.BlockSpec((tm, D_MODEL), lambda bi, i: (bi * nb + i, 0)),
        out_shape=jax.ShapeDtypeStruct((n, D_MODEL), F32),
        compiler_params=pltpu.CompilerParams(dimension_semantics=("arbitrary",) * 2, vmem_limit_bytes=VMEM_LIMIT),
        name="mem_attn",
    )(x2, g, wq, kv, wo)


N_SUB = 2 * PEER_HEADS
EXPERT_BITS = 14


def _route_kernel(x_ref, g_ref, wq_ref, keys_ref, hf_ref, idx_ref, gate_ref, sc_scr, ts_scr, ti_scr):
    hf = _rms(x_ref[...], g_ref[...])
    tm = hf.shape[0]
    for s in range(SUBLANES):
        hf_ref[:, s, :] = hf[:, s * LANES:(s + 1) * LANES]
    q = jnp.dot(hf.astype(BF16), wq_ref[...], preferred_element_type=F32).astype(BF16)
    for g in range(N_SUB):
        blk = q[:, (g // 2) * LANES:(g // 2 + 1) * LANES]
        sc_scr[g] = lax.dot_general(keys_ref[g], blk, NT_DIMS, preferred_element_type=F32)

    key_id = lax.broadcasted_iota(I32, (PEER_N_KEYS, tm), 0)

    def sub_topk(g, carry):
        x = sc_scr[g]
        vals, ids = [], []
        for _ in range(PEER_TOPK):
            m = jnp.max(x, axis=0, keepdims=True)
            idx = jnp.min(jnp.where(x == m, key_id, PEER_N_KEYS), axis=0, keepdims=True)
            vals.append(m)
            ids.append(idx)
            x = jnp.where(key_id == idx, -jnp.inf, x)
        ts_scr[g] = jnp.concatenate(vals, axis=0)
        ti_scr[g] = jnp.concatenate(ids, axis=0)
        return carry

    lax.fori_loop(0, N_SUB, sub_topk, 0)

    n_cand = PEER_TOPK * PEER_TOPK
    pos = lax.broadcasted_iota(I32, (n_cand, tm), 0)

    def head_topk(h, carry):
        s0, s1 = ts_scr[2 * h], ts_scr[2 * h + 1]
        i0, i1 = ti_scr[2 * h], ti_scr[2 * h + 1]
        cand = jnp.concatenate([s0[a:a + 1] + s1 for a in range(PEER_TOPK)], axis=0)
        cid = jnp.concatenate([i0[a:a + 1] * PEER_N_KEYS + i1 for a in range(PEER_TOPK)], axis=0)
        code = (pos << EXPERT_BITS) | cid
        big = jnp.int32(2 ** 30)
        vals, ids = [], []
        for _ in range(PEER_TOPK):
            m = jnp.max(cand, axis=0, keepdims=True)
            best = jnp.min(jnp.where(cand == m, code, big), axis=0, keepdims=True)
            vals.append(m)
            ids.append(best & (2 ** EXPERT_BITS - 1))
            cand = jnp.where(code == best, -jnp.inf, cand)
        best_s = jnp.concatenate(vals, axis=0)
        e = jnp.exp(best_s - best_s[0:1])
        gate_ref[h] = e / jnp.sum(e, axis=0, keepdims=True)
        idx_ref[h] = jnp.concatenate(ids, axis=0)
        return carry

    lax.fori_loop(0, PEER_HEADS, head_topk, 0)


def _route(x2, g, wq, keys_p, tm):
    n = x2.shape[0]
    return pl.pallas_call(
        _route_kernel,
        grid=(n // tm,),
        in_specs=[pl.BlockSpec((tm, D_MODEL), lambda i: (i, 0)), pl.BlockSpec(g.shape, lambda i: (0, 0)),
                  pl.BlockSpec(wq.shape, lambda i: (0, 0)), pl.BlockSpec(keys_p.shape, lambda i: (0, 0, 0))],
        out_specs=[pl.BlockSpec((tm, SUBLANES, LANES), lambda i: (i, 0, 0)),
                   pl.BlockSpec((PEER_HEADS, PEER_TOPK, tm), lambda i: (0, 0, i)),
                   pl.BlockSpec((PEER_HEADS, PEER_TOPK, tm), lambda i: (0, 0, i))],
        out_shape=[jax.ShapeDtypeStruct((n, SUBLANES, LANES), F32),
                   jax.ShapeDtypeStruct((PEER_HEADS, PEER_TOPK, n), I32),
                   jax.ShapeDtypeStruct((PEER_HEADS, PEER_TOPK, n), F32)],
        scratch_shapes=[pltpu.VMEM((N_SUB, PEER_N_KEYS, tm), F32), pltpu.VMEM((N_SUB, PEER_TOPK, tm), F32),
                        pltpu.VMEM((N_SUB, PEER_TOPK, tm), I32)],
        compiler_params=pltpu.CompilerParams(dimension_semantics=("arbitrary",), vmem_limit_bytes=VMEM_LIMIT),
        name="peer_route",
    )(x2, g, wq, keys_p)


def _pack_table(t):
    e = t.shape[0]
    bits = lax.bitcast_convert_type(t.astype(BF16), jnp.uint16).astype(jnp.uint32)
    words = bits[:, :D_MODEL // 2] | (bits[:, D_MODEL // 2:] << 16)
    return words.reshape(e // 2, SUBLANES, LANES)


def _unpack(w):
    lo = pltpu.bitcast(w << 16, F32)
    hi = pltpu.bitcast(w & jnp.uint32(0xFFFF0000), F32)
    return lo, hi


def _sublane_sums(ps, row):
    dist = SUBLANES // 2
    while dist >= 1:
        first = (row & dist) == 0
        nxt = []
        for a, b in zip(ps[:len(ps) // 2], ps[len(ps) // 2:]):
            nxt.append(jnp.where(first, a + pltpu.roll(a, SUBLANES - dist, 0), b + pltpu.roll(b, dist, 0)))
        ps = nxt
        dist //= 2
    return ps[0]


def _peer_dot_kernel(idx_ref, tab_ref, h_ref, gate_ref, idxv_ref, tile_ref, ce_ref, co_ref, hp_scr, r_scr, *, tt):
    row = lax.broadcasted_iota(I32, (SUBLANES, LANES), 0)
    low = row < SUBLANES // 2
    ones = jnp.ones((SUBLANES, LANES), BF16)

    def token(t, carry):
        ht = h_ref[t]
        hr = pltpu.roll(ht, SUBLANES // 2, 0)
        zero = jnp.zeros_like(ht)
        hp_scr[0] = jnp.where(low, ht, zero)
        hp_scr[1] = jnp.where(low, hr, zero)
        hp_scr[2] = jnp.where(low, zero, hr)
        hp_scr[3] = jnp.where(low, zero, ht)
        base = t * PEER_SEL
        for grp in range(PEER_SEL // SUBLANES):
            ps = []
            for kk in range(SUBLANES):
                e = idx_ref[base + grp * SUBLANES + kk]
                lo, hi = _unpack(tab_ref[e >> 1])
                par = e & 1
                ps.append(lo * hp_scr[2 * par] + hi * hp_scr[2 * par + 1])
            r_scr[grp * SUBLANES:(grp + 1) * SUBLANES, :] = _sublane_sums(ps, row)
        r = r_scr[...]
        r1 = r.astype(BF16)
        d1 = r - r1.astype(F32)
        r2 = d1.astype(BF16)
        r3 = (d1 - r2.astype(F32)).astype(BF16)
        a = (lax.dot_general(ones, r1, NT_DIMS, preferred_element_type=F32)
             + lax.dot_general(ones, r2, NT_DIMS, preferred_element_type=F32)
             + lax.dot_general(ones, r3, NT_DIMS, preferred_element_type=F32))[0:1]
        c = 0.5 * a * (1.0 + lax.erf(a * (2.0 ** -0.5))) * gate_ref[pl.ds(t, 1), :]
        ev = idxv_ref[pl.ds(t, 1), :]
        even = (ev & 1) == 0
        ce_ref[pl.ds(t, 1), :] = jnp.where(even, c, 0.0)
        co_ref[pl.ds(t, 1), :] = jnp.where(even, 0.0, c)
        tile_ref[pl.ds(t, 1), :] = ev >> 1
        return carry

    lax.fori_loop(0, tt, token, 0)


def _peer_dot(idx_flat, tab, h3, gate, idx2, tt):
    n = h3.shape[0]
    kern = functools.partial(_peer_dot_kernel, tt=tt)
    rows = pl.BlockSpec((tt, PEER_SEL), lambda i: (i, 0))
    return pl.pallas_call(
        kern,
        grid=(n // tt,),
        in_specs=[pl.BlockSpec((tt * PEER_SEL,), lambda i: (i,), memory_space=pltpu.SMEM),
                  pl.BlockSpec(memory_space=pltpu.VMEM),
                  pl.BlockSpec((tt, SUBLANES, LANES), lambda i: (i, 0, 0)),
                  rows, rows],
        out_specs=[rows, rows, rows],
        out_shape=[jax.ShapeDtypeStruct((n, PEER_SEL), I32), jax.ShapeDtypeStruct((n, PEER_SEL), F32),
                   jax.ShapeDtypeStruct((n, PEER_SEL), F32)],
        scratch_shapes=[pltpu.VMEM((4, SUBLANES, LANES), F32), pltpu.VMEM((PEER_SEL, LANES), F32)],
        compiler_params=pltpu.CompilerParams(dimension_semantics=("arbitrary",), vmem_limit_bytes=VMEM_LIMIT),
        name="peer_dot",
    )(idx_flat, tab, h3, gate, idx2)


def _peer_sum_kernel(tile_ref, ce_ref, co_ref, tab_ref, o_ref, *, tt):
    row = lax.broadcasted_iota(I32, (SUBLANES, LANES), 0)
    low = row < SUBLANES // 2

    def token(t, carry):
        base = t * PEER_SEL

        def group(gi, acc):
            acc_lo, acc_hi = acc
            for kk in range(SUBLANES):
                k = base + gi * SUBLANES + kk
                lo, hi = _unpack(tab_ref[tile_ref[k]])
                cv = jnp.where(low, ce_ref[k], co_ref[k])
                acc_lo = acc_lo + cv * lo
                acc_hi = acc_hi + cv * hi
            return acc_lo, acc_hi

        zero = jnp.zeros((SUBLANES, LANES), F32)
        acc_lo, acc_hi = lax.fori_loop(0, PEER_SEL // SUBLANES, group, (zero, zero))
        half = SUBLANES // 2
        out_lo = acc_lo + pltpu.roll(acc_lo, half, 0)
        out_hi = acc_hi + pltpu.roll(acc_hi, half, 0)
        o_ref[t] = jnp.where(low, out_lo, out_hi)
        return carry

    lax.fori_loop(0, tt, token, 0)


def _peer_sum(tile_flat, ce_flat, co_flat, tab, n, tt):
    kern = functools.partial(_peer_sum_kernel, tt=tt)
    flat = pl.BlockSpec((tt * PEER_SEL,), lambda i: (i,), memory_space=pltpu.SMEM)
    return pl.pallas_call(
        kern,
        grid=(n // tt,),
        in_specs=[flat, flat, flat, pl.BlockSpec(memory_space=pltpu.VMEM)],
        out_specs=pl.BlockSpec((tt, SUBLANES, LANES), lambda i: (i, 0, 0)),
        out_shape=jax.ShapeDtypeStruct((n, SUBLANES, LANES), F32),
        compiler_params=pltpu.CompilerParams(dimension_semantics=("arbitrary",), vmem_limit_bytes=VMEM_LIMIT),
        name="peer_sum",
    )(tile_flat, ce_flat, co_flat, tab)


def _final_kernel(x_ref, p_ref, g_ref, o_ref):
    peer = jnp.concatenate([p_ref[:, s, :] for s in range(SUBLANES)], axis=1)
    o_ref[...] = _rms(x_ref[...] + peer, g_ref[...])


def _final(x2, peer3, g, tm):
    n = x2.shape[0]
    return pl.pallas_call(
        _final_kernel,
        grid=(n // tm,),
        in_specs=[pl.BlockSpec((tm, D_MODEL), lambda i: (i, 0)),
                  pl.BlockSpec((tm, SUBLANES, LANES), lambda i: (i, 0, 0)),
                  pl.BlockSpec(g.shape, lambda i: (0, 0))],
        out_specs=pl.BlockSpec((tm, D_MODEL), lambda i: (i, 0)),
        out_shape=jax.ShapeDtypeStruct((n, D_MODEL), F32),
        compiler_params=pltpu.CompilerParams(dimension_semantics=("arbitrary",), vmem_limit_bytes=VMEM_LIMIT),
        name="final_norm",
    )(x2, peer3, g)


def _tile(n, pref):
    t = pref
    while n % t:
        t //= 2
    return t


def _layer(l, x2, mem2, pos, invf, b, s_len, m_tok, norm_mix_g, w_in, da_lambda, da_subln_g, w_branch_a,
           w_branch_b, gate_bias, w_out, norm_mem_g, mem_kv_norm_g, w_mem_q, w_mem_kv, w_mem_o, norm_ffn_g,
           peer_w_q, peer_sub_keys, peer_u, peer_v):
    n = b * s_len
    row2 = lambda v: v.reshape(1, -1)
    splits = [0]
    for c in COL_SIZES:
        splits.append(splits[-1] + c)
    da_q, da_k, da_v, ds_q, ds_k, ds_v, ix_q, ix_k, ix_w, gates = (
        w_in[l][:, splits[j]:splits[j + 1]] for j in range(len(COL_SIZES)))
    k0, k1 = ds_k[:, :DS_HEAD_DIM], ds_k[:, DS_HEAD_DIM:]
    w_rope = jnp.concatenate([da_q, da_k, ds_q, ix_q, k0, k0, k1, k1, ix_k, ix_k], axis=1).astype(BF16)
    w_plain = jnp.concatenate([da_v, ds_v], axis=1).astype(BF16)
    w_idx = jnp.pad(ix_w, ((0, 0), (0, LANES - IDX_HEADS))).astype(BF16)
    pr, pp, gate, iw = _in_proj(x2, row2(norm_mix_g[l]), pos, invf, row2(gate_bias[l]), w_rope, w_plain,
                                gates.astype(BF16), w_idx, _tile(n, 256))

    lam_init = 0.8 - 0.6 * math.exp(-0.3 * l)
    t_att = _tile(s_len, 256)
    o_a = _diff_attention(da_lambda[l], pr, pp, row2(da_subln_g[l]), b, s_len, t_att, t_att, lam_init)
    o_b = _dsa_attention(pr, pp, iw, b, s_len, t_att)
    x2 = _merge(o_a, o_b, gate, x2, w_branch_a[l].astype(BF16), w_branch_b[l].astype(BF16),
                w_out[l].astype(BF16), _tile(n, 256))

    kv = _norm_matmul(mem2, row2(mem_kv_norm_g[l]), w_mem_kv[l].astype(BF16), _tile(mem2.shape[0], 256))
    x2 = _mem_attn(x2, row2(norm_mem_g[l]), w_mem_q[l].astype(BF16), kv, w_mem_o[l].astype(BF16),
                   b, s_len, m_tok, _tile(s_len, 256))

    sk = peer_sub_keys[l].reshape(N_SUB, PEER_N_KEYS, PEER_HALF)
    z = jnp.zeros_like(sk)
    keys_p = jnp.where((jnp.arange(N_SUB) % 2 == 0)[:, None, None],
                       jnp.concatenate([sk, z], axis=2), jnp.concatenate([z, sk], axis=2)).astype(BF16)
    h3, idx_t, gate_t = _route(x2, row2(norm_ffn_g[l]), peer_w_q[l].astype(BF16), keys_p, _tile(n, 256))
    idx2 = idx_t.reshape(PEER_SEL, n).T
    gate2 = gate_t.reshape(PEER_SEL, n).T
    tt = _tile(n, 128)
    tile2, ce2, co2 = _peer_dot(idx2.reshape(-1), _pack_table(peer_u[l]), h3, gate2, idx2, tt)
    peer3 = _peer_sum(tile2.reshape(-1), ce2.reshape(-1), co2.reshape(-1), _pack_table(peer_v[l]), n, tt)
    return x2, peer3


def kernel(x, mem, positions, norm_mix_g, w_in, da_lambda, da_subln_g, w_branch_a, w_branch_b, gate_bias, w_out, norm_mem_g, mem_kv_norm_g, w_mem_q, w_mem_kv, w_mem_o, norm_ffn_g, peer_w_q, peer_sub_keys, peer_u, peer_v, final_norm_g):
    b, s_len, d = x.shape
    m_tok = mem.shape[1]
    n = b * s_len
    depth = w_in.shape[0]
    x2 = x.reshape(n, d)
    mem2 = mem.reshape(b * m_tok, d)
    pos = positions.astype(F32).reshape(n, 1)
    inv_freq = ROPE_THETA ** (-(jnp.arange(ROPE_HALF, dtype=F32) * 2.0) / ROPE_DIM)
    invf = jnp.tile(inv_freq, LANES // ROPE_HALF).reshape(1, LANES)
    peer3 = None
    for l in range(depth):
        if peer3 is not None:
            x2 = x2 + peer3.reshape(n, d)
        x2, peer3 = _layer(l, x2, mem2, pos, invf, b, s_len, m_tok, norm_mix_g, w_in, da_lambda, da_subln_g,
                           w_branch_a, w_branch_b, gate_bias, w_out, norm_mem_g, mem_kv_norm_g, w_mem_q, w_mem_kv,
                           w_mem_o, norm_ffn_g, peer_w_q, peer_sub_keys, peer_u, peer_v)
    out = _final(x2, peer3, final_norm_g.reshape(1, d), _tile(n, 256))
    return out.reshape(b, s_len, d)
```

```python
import functools
import math

import jax
import jax.numpy as jnp
from jax import lax
from jax.experimental import pallas as pl
from jax.experimental.pallas import tpu as pltpu

F32 = jnp.float32
BF16 = jnp.bfloat16
I32 = jnp.int32

D_MODEL = 1024
EPS = 1e-6
ROPE_THETA = 500000.0
ROT_HEAD_DIM = 64
ROPE_DIM = 16
ROPE_HALF = ROPE_DIM // 2

DA_HEADS = 4
DA_QK_DIM = 64
DA_V_DIM = 128
DA_WIDTH = 512
DS_HEADS = 8
DS_KV_HEADS = 2
DS_HEAD_DIM = 64
DS_WIDTH = 512
IDX_HEADS = 8
IDX_DIM = 64
TOPK_MAX = 256
MEM_HEADS = 4
MEM_HEAD_DIM = 128
MEM_WIDTH = 512
PEER_HEADS = 8
PEER_N_KEYS = 128
PEER_HALF = 64
PEER_TOPK = 16
PEER_SEL = PEER_HEADS * PEER_TOPK

COL_SIZES = (512, 512, 512, 512, 128, 128, 512, 64, 8, 2048)

LANES = 128
SUBLANES = 8
NEG_BIG = -1e30
INT_MIN = -(2 ** 31)
VMEM_LIMIT = 56 * 1024 * 1024

NT_DIMS = (((1,), (1,)), ((), ()))


def _rms(x, g):
    var = jnp.mean(x * x, axis=-1, keepdims=True)
    return x * lax.rsqrt(var + EPS) * g


def _in_proj_kernel(x_ref, g_ref, pos_ref, invf_ref, bias_ref, wr_ref, wp_ref, wg_ref, ww_ref,
                    pr_ref, pp_ref, gate_ref, iw_ref):
    h = _rms(x_ref[...], g_ref[...]).astype(BF16)
    tm = h.shape[0]
    ang = pos_ref[...] * invf_ref[...]
    cos = jnp.cos(ang)
    sin = jnp.sin(ang)
    lane = lax.broadcasted_iota(I32, (tm, LANES), 1) % ROT_HEAD_DIM
    c_t = jnp.where(lane < ROPE_DIM, cos, 1.0)
    s_lo = jnp.where(lane < ROPE_HALF, -sin, 0.0)
    s_hi = jnp.where((lane >= ROPE_HALF) & (lane < ROPE_DIM), sin, 0.0)
    n_r = wr_ref.shape[1]
    for c0 in range(0, n_r, 512):
        c1 = min(c0 + 512, n_r)
        w = c1 - c0
        y = jnp.dot(h, wr_ref[:, c0:c1], preferred_element_type=F32)
        reps = w // LANES
        ct = jnp.concatenate([c_t] * reps, axis=1)
        sl = jnp.concatenate([s_lo] * reps, axis=1)
        sh = jnp.concatenate([s_hi] * reps, axis=1)
        y = y * ct + pltpu.roll(y, w - ROPE_HALF, 1) * sl + pltpu.roll(y, ROPE_HALF, 1) * sh
        pr_ref[:, c0:c1] = y.astype(pr_ref.dtype)
    pp_ref[...] = jnp.dot(h, wp_ref[...], preferred_element_type=F32).astype(pp_ref.dtype)
    n_g = wg_ref.shape[1]
    for c0 in range(0, n_g, 512):
        y = jnp.dot(h, wg_ref[:, c0:c0 + 512], preferred_element_type=F32)
        gate_ref[:, c0:c0 + 512] = jax.nn.sigmoid(y + bias_ref[:, c0:c0 + 512])
    iw_ref[...] = jnp.dot(h, ww_ref[...], preferred_element_type=F32)


def _in_proj(x2, g, pos, invf, bias, wr, wp, wg, ww, tm):
    n = x2.shape[0]
    full = lambda a: pl.BlockSpec(a.shape, lambda i: (0, 0))
    rows = lambda w: pl.BlockSpec((tm, w), lambda i: (i, 0))
    return pl.pallas_call(
        _in_proj_kernel,
        grid=(n // tm,),
        in_specs=[rows(D_MODEL), full(g), rows(1), full(invf), full(bias), full(wr), full(wp), full(wg), full(ww)],
        out_specs=[rows(wr.shape[1]), rows(wp.shape[1]), rows(wg.shape[1]), rows(ww.shape[1])],
        out_shape=[jax.ShapeDtypeStruct((n, wr.shape[1]), BF16), jax.ShapeDtypeStruct((n, wp.shape[1]), BF16),
                   jax.ShapeDtypeStruct((n, wg.shape[1]), F32), jax.ShapeDtypeStruct((n, ww.shape[1]), F32)],
        compiler_params=pltpu.CompilerParams(dimension_semantics=("arbitrary",), vmem_limit_bytes=VMEM_LIMIT),
        name="in_proj",
    )(x2, g, pos, invf, bias, wr, wp, wg, ww)


def _da_kernel(lam_ref, q_ref, k_ref, v_ref, g_ref, o_ref, *, tq, tk, lam_init):
    i = pl.program_id(2)
    q = q_ref[...]
    lane = lax.broadcasted_iota(I32, q.shape, 1)
    qs = q * jnp.asarray(DA_QK_DIM ** -0.5, q.dtype)
    zero = jnp.zeros_like(qs)
    q_maps = (jnp.where(lane < DA_QK_DIM, qs, zero), jnp.where(lane >= DA_QK_DIM, qs, zero))
    row = i * tq + lax.broadcasted_iota(I32, (tq, tk), 0)
    col0 = lax.broadcasted_iota(I32, (tq, tk), 1)

    def step(j, carry, masked):
        start = pl.multiple_of(j * tk, tk)
        k = k_ref[pl.ds(start, tk), :]
        v = v_ref[pl.ds(start, tk), :]
        out = []
        for mp in range(2):
            m, l, a = carry[3 * mp:3 * mp + 3]
            s = lax.dot_general(q_maps[mp], k, NT_DIMS, preferred_element_type=F32)
            if masked:
                s = jnp.where(col0 + j * tk <= row, s, NEG_BIG)
            m_new = jnp.maximum(m, jnp.max(s, axis=1, keepdims=True))
            alpha = jnp.exp(m - m_new)
            p = jnp.exp(s - m_new)
            l = alpha * l + jnp.sum(p, axis=1, keepdims=True)
            a = alpha * a + jnp.dot(p.astype(v.dtype), v, preferred_element_type=F32)
            out += [m_new, l, a]
        return tuple(out)

    init = []
    for _ in range(2):
        init += [jnp.full((tq, 1), NEG_BIG, F32), jnp.zeros((tq, 1), F32), jnp.zeros((tq, DA_V_DIM), F32)]
    n_full = (i * tq + 1) // tk
    n_tot = ((i + 1) * tq + tk - 1) // tk
    carry = lax.fori_loop(0, n_full, lambda j, c: step(j, c, False), tuple(init))
    carry = lax.fori_loop(n_full, n_tot, lambda j, c: step(j, c, True), carry)
    _, l1, a1, _, l2, a2 = carry

    lp = lam_ref[...]
    lam = (jnp.exp(jnp.sum(lp[0:1] * lp[1:2], axis=1, keepdims=True))
           - jnp.exp(jnp.sum(lp[2:3] * lp[3:4], axis=1, keepdims=True)) + lam_init)
    o = a1 / l1 - lam * (a2 / l2)
    o = _rms(o, g_ref[...]) * (1.0 - lam_init)
    o_ref[...] = o.astype(o_ref.dtype)


def _diff_attention(lam_p, pr, pp, subln_g, b, s_len, tq, tk, lam_init):
    nq = s_len // tq
    n = b * s_len
    kern = functools.partial(_da_kernel, tq=tq, tk=tk, lam_init=lam_init)
    return pl.pallas_call(
        kern,
        grid=(b, DA_HEADS, nq),
        in_specs=[
            pl.BlockSpec(lam_p.shape, lambda bi, h, i: (0, 0)),
            pl.BlockSpec((tq, LANES), lambda bi, h, i: (bi * nq + i, h)),
            pl.BlockSpec((s_len, LANES), lambda bi, h, i: (bi, DA_HEADS + h)),
            pl.BlockSpec((s_len, LANES), lambda bi, h, i: (bi, h)),
            pl.BlockSpec(subln_g.shape, lambda bi, h, i: (0, 0)),
        ],
        out_specs=pl.BlockSpec((tq, LANES), lambda bi, h, i: (bi * nq + i, h)),
        out_shape=jax.ShapeDtypeStruct((n, DA_WIDTH), BF16),
        compiler_params=pltpu.CompilerParams(dimension_semantics=("arbitrary",) * 3, vmem_limit_bytes=VMEM_LIMIT),
        name="diff_attn",
    )(lam_p, pr, pr, pp, subln_g)


PR_DSQ = 8
PR_IXQ = 12
PR_DSK = 16
PR_IXK = 18
PP_DSV = 4


def _dsa_kernel(qi_ref, ki_ref, w_ref, q_ref, k0_ref, k1_ref, v_ref, o_ref, keys_scr, *, tq, n_sel, n_bits):
    tk = tq
    i = pl.program_id(1)
    nck = i + 1
    lane = lax.broadcasted_iota(I32, (tq, LANES), 1)
    low_half = lane < DS_HEAD_DIM
    row = i * tq + lax.broadcasted_iota(I32, (tq, tk), 0)
    col0 = lax.broadcasted_iota(I32, (tq, tk), 1)

    def half_masked(ref, h):
        blk = ref[:, (h // 2) * LANES:(h // 2 + 1) * LANES]
        keep = low_half if h % 2 == 0 else jnp.logical_not(low_half)
        return jnp.where(keep, blk, jnp.zeros_like(blk))

    w = w_ref[...]

    def score_chunk(c, carry):
        kc = ki_ref[pl.ds(pl.multiple_of(c * tk, tk), tk), :]
        acc = jnp.zeros((tq, tk), F32)
        for h in range(IDX_HEADS):
            s = lax.dot_general(half_masked(qi_ref, h), kc, NT_DIMS, preferred_element_type=F32)
            acc = acc + w[:, h:h + 1] * jnp.maximum(s, 0.0)
        bits = pltpu.bitcast(acc, I32)
        key = bits ^ ((bits >> 31) & 0x7FFFFFFF)
        key = jnp.where(acc == 0.0, 0, key)
        key = jnp.where(col0 + c * tk <= row, key, INT_MIN)
        keys_scr[c] = key
        return carry

    lax.fori_loop(0, nck, score_chunk, 0)

    def count(pred):
        def body(c, acc):
            hit = pred(keys_scr[c], col0 + c * tk)
            return acc + hit[:, :LANES] + hit[:, LANES:] if tk == 2 * LANES else acc + hit
        width = LANES if tk == 2 * LANES else tk
        acc = lax.fori_loop(0, nck, body, jnp.zeros((tq, width), I32))
        return jnp.sum(acc, axis=1, keepdims=True)

    def thr_bit(t, ans_u):
        cand_u = ans_u | jnp.left_shift(jnp.int32(1), 31 - t)
        cand = cand_u ^ INT_MIN
        cnt = count(lambda k, col: jnp.where(k >= cand, 1, 0))
        return jnp.where(cnt >= n_sel, cand_u, ans_u)

    thr = lax.fori_loop(0, 32, thr_bit, jnp.zeros((tq, 1), I32)) ^ INT_MIN
    n_gt = count(lambda k, col: jnp.where(k > thr, 1, 0))
    need = n_sel - n_gt

    def tie_bit(t, j_cut):
        cand = j_cut | jnp.left_shift(jnp.int32(1), n_bits - 1 - t)
        cnt = count(lambda k, col: jnp.where(k == thr, jnp.where(col < cand, 1, 0), 0))
        return jnp.where(cnt < need, cand, j_cut)

    j_cut = lax.fori_loop(0, n_bits, tie_bit, jnp.zeros((tq, 1), I32))
    j_cut = jnp.where(thr == INT_MIN, -1, j_cut)

    def bias_chunk(c, carry):
        k = keys_scr[c]
        sel = (k > thr) | ((k == thr) & (col0 + c * tk <= j_cut))
        keys_scr[c] = pltpu.bitcast(jnp.where(sel, 0.0, NEG_BIG).astype(F32), I32)
        return carry

    lax.fori_loop(0, nck, bias_chunk, 0)

    scale = jnp.asarray(DS_HEAD_DIM ** -0.5, BF16)
    outs = []
    for h in range(DS_HEADS):
        grp = h // (DS_HEADS // DS_KV_HEADS)
        k_ref = k0_ref if grp == 0 else k1_ref
        qh = half_masked(q_ref, h) * scale

        def att_chunk(c, carry, k_ref=k_ref, qh=qh):
            m, l, a = carry
            start = pl.multiple_of(c * tk, tk)
            s = lax.dot_general(qh, k_ref[pl.ds(start, tk), :], NT_DIMS, preferred_element_type=F32)
            s = s + pltpu.bitcast(keys_scr[c], F32)
            m_new = jnp.maximum(m, jnp.max(s, axis=1, keepdims=True))
            alpha = jnp.exp(m - m_new)
            p = jnp.exp(s - m_new)
            l = alpha * l + jnp.sum(p, axis=1, keepdims=True)
            v = v_ref[pl.ds(start, tk), :]
            a = alpha * a + jnp.dot(p.astype(v.dtype), v, preferred_element_type=F32)
            return m_new, l, a

        init = (jnp.full((tq, 1), NEG_BIG, F32), jnp.zeros((tq, 1), F32), jnp.zeros((tq, LANES), F32))
        _, l, a = lax.fori_loop(0, nck, att_chunk, init)
        o = a / l
        want_low = h % 2 == 0
        have_low = grp == 0
        outs.append(o if want_low == have_low else pltpu.roll(o, DS_HEAD_DIM, 1))
    for pair in range(DS_HEADS // 2):
        blk = jnp.where(low_half, outs[2 * pair], outs[2 * pair + 1])
        o_ref[:, pair * LANES:(pair + 1) * LANES] = blk.astype(o_ref.dtype)


def _dsa_attention(pr, pp, iw, b, s_len, tq):
    nq = s_len // tq
    n = b * s_len
    n_sel = min(TOPK_MAX, s_len // 4)
    n_bits = max(1, (s_len - 1).bit_length())
    kern = functools.partial(_dsa_kernel, tq=tq, n_sel=n_sel, n_bits=n_bits)
    seq = lambda blk: pl.BlockSpec((s_len, LANES), lambda bi, i: (bi, blk))
    return pl.pallas_call(
        kern,
        grid=(b, nq),
        in_specs=[
            pl.BlockSpec((tq, 4 * LANES), lambda bi, i: (bi * nq + i, PR_IXQ // 4)),
            seq(PR_IXK),
            pl.BlockSpec((tq, LANES), lambda bi, i: (bi * nq + i, 0)),
            pl.BlockSpec((tq, 4 * LANES), lambda bi, i: (bi * nq + i, PR_DSQ // 4)),
            seq(PR_DSK), seq(PR_DSK + 1),
            seq(PP_DSV),
        ],
        out_specs=pl.BlockSpec((tq, DS_WIDTH), lambda bi, i: (bi * nq + i, 0)),
        out_shape=jax.ShapeDtypeStruct((n, DS_WIDTH), BF16),
        scratch_shapes=[pltpu.VMEM((nq, tq, tq), I32)],
        compiler_params=pltpu.CompilerParams(dimension_semantics=("arbitrary",) * 2, vmem_limit_bytes=VMEM_LIMIT),
        name="dsa_attn",
    )(pr, pr, iw, pr, pr, pr, pp)


def _merge_kernel(oa_ref, ob_ref, gate_ref, x_ref, wa_ref, wb_ref, wo_ref, o_ref):
    ya = jnp.dot(oa_ref[...], wa_ref[...], preferred_element_type=F32)
    yb = jnp.dot(ob_ref[...], wb_ref[...], preferred_element_type=F32)
    mix = gate_ref[:, :D_MODEL] * ya + gate_ref[:, D_MODEL:] * yb
    o_ref[...] = x_ref[...] + jnp.dot(mix.astype(BF16), wo_ref[...], preferred_element_type=F32)


def _merge(oa, ob, gate, x2, wa, wb, wo, tm):
    n = x2.shape[0]
    full = lambda a: pl.BlockSpec(a.shape, lambda i: (0, 0))
    rows = lambda w: pl.BlockSpec((tm, w), lambda i: (i, 0))
    return pl.pallas_call(
        _merge_kernel,
        grid=(n // tm,),
        in_specs=[rows(DA_WIDTH), rows(DS_WIDTH), rows(2 * D_MODEL), rows(D_MODEL), full(wa), full(wb), full(wo)],
        out_specs=rows(D_MODEL),
        out_shape=jax.ShapeDtypeStruct((n, D_MODEL), F32),
        compiler_params=pltpu.CompilerParams(dimension_semantics=("arbitrary",), vmem_limit_bytes=VMEM_LIMIT),
        name="merge",
    )(oa, ob, gate, x2, wa, wb, wo)


def _norm_matmul_kernel(x_ref, g_ref, w_ref, o_ref):
    h = _rms(x_ref[...], g_ref[...]).astype(BF16)
    o_ref[...] = jnp.dot(h, w_ref[...], preferred_element_type=F32).astype(o_ref.dtype)


def _norm_matmul(x2, g, w, tm):
    n = x2.shape[0]
    return pl.pallas_call(
        _norm_matmul_kernel,
        grid=(n // tm,),
        in_specs=[pl.BlockSpec((tm, x2.shape[1]), lambda i: (i, 0)), pl.BlockSpec(g.shape, lambda i: (0, 0)),
                  pl.BlockSpec(w.shape, lambda i: (0, 0))],
        out_specs=pl.BlockSpec((tm, w.shape[1]), lambda i: (i, 0)),
        out_shape=jax.ShapeDtypeStruct((n, w.shape[1]), BF16),
        compiler_params=pltpu.CompilerParams(dimension_semantics=("arbitrary",), vmem_limit_bytes=VMEM_LIMIT),
        name="mem_kv_proj",
    )(x2, g, w)


def _mem_attn_kernel(x_ref, g_ref, wq_ref, kv_ref, wo_ref, o_ref):
    x = x_ref[...]
    hn = _rms(x, g_ref[...]).astype(BF16)
    q = jnp.dot(hn, wq_ref[...], preferred_element_type=F32).astype(BF16)
    scale = MEM_HEAD_DIM ** -0.5
    heads = []
    for h in range(MEM_HEADS):
        qh = q[:, h * MEM_HEAD_DIM:(h + 1) * MEM_HEAD_DIM]
        kh = kv_ref[:, h * MEM_HEAD_DIM:(h + 1) * MEM_HEAD_DIM]
        vh = kv_ref[:, MEM_WIDTH + h * MEM_HEAD_DIM:MEM_WIDTH + (h + 1) * MEM_HEAD_DIM]
        s = lax.dot_general(qh, kh, NT_DIMS, preferred_element_type=F32) * scale
        m = jnp.max(s, axis=1, keepdims=True)
        p = jnp.exp(s - m)
        l = jnp.sum(p, axis=1, keepdims=True)
        heads.append(jnp.dot(p.astype(BF16), vh, preferred_element_type=F32) / l)
    o = jnp.concatenate(heads, axis=1).astype(BF16)
    o_ref[...] = x + jnp.dot(o, wo_ref[...], preferred_element_type=F32)


def _mem_attn(x2, g, wq, kv, wo, b, s_len, m_tok, tm):
    nb = s_len // tm
    n = x2.shape[0]
    full = lambda a: pl.BlockSpec(a.shape, lambda bi, i: (0, 0))
    return pl.pallas_call(
        _mem_attn_kernel,
        grid=(b, nb),
        in_specs=[pl.BlockSpec((tm, D_MODEL), lambda bi, i: (bi * nb + i, 0)), full(g), full(wq),
                  pl.BlockSpec((m_tok, 2 * MEM_WIDTH), lambda bi, i: (bi, 0)), full(wo)],
        out_specs=pl.BlockSpec((tm, D_MODEL), lambda bi, i: (bi * nb + i, 0)),
        out_shape=jax.ShapeDtypeStruct((n, D_MODEL), F32),
        compiler_params=pltpu.CompilerParams(dimension_semantics=("arbitrary",) * 2, vmem_limit_bytes=VMEM_LIMIT),
        name="mem_attn",
    )(x2, g, wq, kv, wo)


N_SUB = 2 * PEER_HEADS
EXPERT_BITS = 14


def _route_kernel(x_ref, g_ref, wq_ref, keys_ref, hf_ref, idx_ref, row_ref, gate_ref, sc_scr, ts_scr, ti_scr):
    hf = _rms(x_ref[...], g_ref[...])
    tm = hf.shape[0]
    hb = hf.astype(BF16)
    bits = pltpu.bitcast(hb.astype(F32), jnp.uint32)
    half_d = D_MODEL // 2
    for s in range(SUBLANES // 2):
        lo = bits[:, s * LANES:(s + 1) * LANES] >> 16
        hi = bits[:, half_d + s * LANES:half_d + (s + 1) * LANES] & jnp.uint32(0xFFFF0000)
        hf_ref[:, s, :] = lo | hi
        hf_ref[:, s + SUBLANES // 2, :] = lo | hi
    q = jnp.dot(hb, wq_ref[...], preferred_element_type=F32).astype(BF16)
    for g in range(N_SUB):
        blk = q[:, (g // 2) * LANES:(g // 2 + 1) * LANES]
        sc_scr[g] = lax.dot_general(keys_ref[g], blk, NT_DIMS, preferred_element_type=F32)

    key_id = lax.broadcasted_iota(I32, (PEER_N_KEYS, tm), 0)

    def sub_topk(g, carry):
        x = sc_scr[g]
        vals, ids = [], []
        for _ in range(PEER_TOPK):
            m = jnp.max(x, axis=0, keepdims=True)
            idx = jnp.min(jnp.where(x == m, key_id, PEER_N_KEYS), axis=0, keepdims=True)
            vals.append(m)
            ids.append(idx)
            x = jnp.where(key_id == idx, -jnp.inf, x)
        ts_scr[g] = jnp.concatenate(vals, axis=0)
        ti_scr[g] = jnp.concatenate(ids, axis=0)
        return carry

    lax.fori_loop(0, N_SUB, sub_topk, 0)

    n_cand = PEER_TOPK * PEER_TOPK
    pos = lax.broadcasted_iota(I32, (n_cand, tm), 0)

    def head_topk(h, carry):
        s0, s1 = ts_scr[2 * h], ts_scr[2 * h + 1]
        i0, i1 = ti_scr[2 * h], ti_scr[2 * h + 1]
        cand = jnp.concatenate([s0[a:a + 1] + s1 for a in range(PEER_TOPK)], axis=0)
        cid = jnp.concatenate([i0[a:a + 1] * PEER_N_KEYS + i1 for a in range(PEER_TOPK)], axis=0)
        code = (pos << EXPERT_BITS) | cid
        big = jnp.int32(2 ** 30)
        vals, ids = [], []
        for _ in range(PEER_TOPK):
            m = jnp.max(cand, axis=0, keepdims=True)
            best = jnp.min(jnp.where(cand == m, code, big), axis=0, keepdims=True)
            vals.append(m)
            ids.append(best & (2 ** EXPERT_BITS - 1))
            cand = jnp.where(code == best, -jnp.inf, cand)
        best_s = jnp.concatenate(vals, axis=0)
        e = jnp.exp(best_s - best_s[0:1])
        gate_ref[h] = e / jnp.sum(e, axis=0, keepdims=True)
        best_i = jnp.concatenate(ids, axis=0)
        idx_ref[h] = best_i
        row_ref[h] = (best_i >> 1) * SUBLANES
        return carry

    lax.fori_loop(0, PEER_HEADS, head_topk, 0)


def _route(x2, g, wq, keys_p, tm):
    n = x2.shape[0]
    return pl.pallas_call(
        _route_kernel,
        grid=(n // tm,),
        in_specs=[pl.BlockSpec((tm, D_MODEL), lambda i: (i, 0)), pl.BlockSpec(g.shape, lambda i: (0, 0)),
                  pl.BlockSpec(wq.shape, lambda i: (0, 0)), pl.BlockSpec(keys_p.shape, lambda i: (0, 0, 0))],
        out_specs=[pl.BlockSpec((tm, SUBLANES, LANES), lambda i: (i, 0, 0))]
        + [pl.BlockSpec((PEER_HEADS, PEER_TOPK, tm), lambda i: (0, 0, i))] * 3,
        out_shape=[jax.ShapeDtypeStruct((n, SUBLANES, LANES), jnp.uint32),
                   jax.ShapeDtypeStruct((PEER_HEADS, PEER_TOPK, n), I32),
                   jax.ShapeDtypeStruct((PEER_HEADS, PEER_TOPK, n), I32),
                   jax.ShapeDtypeStruct((PEER_HEADS, PEER_TOPK, n), F32)],
        scratch_shapes=[pltpu.VMEM((N_SUB, PEER_N_KEYS, tm), F32), pltpu.VMEM((N_SUB, PEER_TOPK, tm), F32),
                        pltpu.VMEM((N_SUB, PEER_TOPK, tm), I32)],
        compiler_params=pltpu.CompilerParams(dimension_semantics=("arbitrary",), vmem_limit_bytes=VMEM_LIMIT),
        name="peer_route",
    )(x2, g, wq, keys_p)


TILE_ROWS = 2 * SUBLANES
PAIRS_PER_KTILE = 16
TOK_UNROLL = 2


def _pack_table(t):
    e = t.shape[0]
    bits = lax.bitcast_convert_type(t.astype(BF16), jnp.uint16).astype(jnp.uint32)
    words = bits[:, :D_MODEL // 2] | (bits[:, D_MODEL // 2:] << 16)
    return words.reshape(e * (SUBLANES // 2), LANES)


def _gather_tiles(row_ref, tab_ref, t):
    tiles = []
    for k in range(PEER_SEL):
        start = pl.multiple_of(row_ref[t, k], SUBLANES)
        tiles.append(pltpu.bitcast(tab_ref[pl.ds(start, SUBLANES), :], BF16))
    return tiles


def _peer_dot_kernel(row_ref, tab_ref, h_ref, gate_ref, idxv_ref, sel_ref, ce_ref, co_ref, ze_scr, zo_scr, *, tt):
    ones = jnp.ones((SUBLANES, LANES), BF16)
    half_rows = PAIRS_PER_KTILE

    def token(t):
        hp = pltpu.bitcast(h_ref[t], BF16)
        prods = [tile * hp for tile in _gather_tiles(row_ref, tab_ref, t)]
        ye, yo = [], []
        for kt in range(PEER_SEL // PAIRS_PER_KTILE):
            stack = jnp.concatenate(prods[kt * PAIRS_PER_KTILE:(kt + 1) * PAIRS_PER_KTILE], axis=0)
            y = jnp.dot(sel_ref[...], stack, preferred_element_type=F32)
            ye.append(y[:half_rows].astype(BF16))
            yo.append(y[half_rows:].astype(BF16))
        ze = lax.dot_general(ones, jnp.concatenate(ye, axis=0), NT_DIMS, preferred_element_type=F32)
        zo = lax.dot_general(ones, jnp.concatenate(yo, axis=0), NT_DIMS, preferred_element_type=F32)
        ze_scr[pl.ds(t, 1), :] = ze[0:1]
        zo_scr[pl.ds(t, 1), :] = zo[0:1]

    def trip(i, carry):
        for u in range(TOK_UNROLL):
            token(i * TOK_UNROLL + u)
        return carry

    lax.fori_loop(0, tt // TOK_UNROLL, trip, 0)
    even = (idxv_ref[...] & 1) == 0
    a = jnp.where(even, ze_scr[...], zo_scr[...])
    c = 0.5 * a * (1.0 + lax.erf(a * (2.0 ** -0.5))) * gate_ref[...]
    ce_ref[...] = jnp.where(even, c, 0.0)
    co_ref[...] = jnp.where(even, 0.0, c)


def _peer_dot(rows2, tab, hpk, gate, idx2, sel, tt):
    n = hpk.shape[0]
    kern = functools.partial(_peer_dot_kernel, tt=tt)
    rows = pl.BlockSpec((tt, PEER_SEL), lambda i: (i, 0))
    return pl.pallas_call(
        kern,
        grid=(n // tt,),
        in_specs=[pl.BlockSpec((tt, PEER_SEL), lambda i: (i, 0), memory_space=pltpu.SMEM),
                  pl.BlockSpec(memory_space=pltpu.VMEM),
                  pl.BlockSpec((tt, SUBLANES, LANES), lambda i: (i, 0, 0)),
                  rows, rows, pl.BlockSpec(sel.shape, lambda i: (0, 0))],
        out_specs=[rows, rows],
        out_shape=[jax.ShapeDtypeStruct((n, PEER_SEL), F32), jax.ShapeDtypeStruct((n, PEER_SEL), F32)],
        scratch_shapes=[pltpu.VMEM((tt, PEER_SEL), F32), pltpu.VMEM((tt, PEER_SEL), F32)],
        compiler_params=pltpu.CompilerParams(dimension_semantics=("arbitrary",), vmem_limit_bytes=VMEM_LIMIT),
        name="peer_dot",
    )(rows2, tab, hpk, gate, idx2, sel)


def _peer_sum_kernel(row_ref, ce_ref, co_ref, spread_ref, tab_ref, o_ref, m1_scr, m2_scr, *, tt):
    width = PEER_SEL * TILE_ROWS
    cc = jnp.concatenate([ce_ref[...], co_ref[...]], axis=1)
    c1 = cc.astype(BF16)
    c2 = (cc - c1.astype(F32)).astype(BF16)
    m1_scr[...] = jnp.dot(c1, spread_ref[...], preferred_element_type=F32)
    m2_scr[...] = jnp.dot(c2, spread_ref[...], preferred_element_type=F32)
    row = lax.broadcasted_iota(I32, (SUBLANES, width), 0)
    lane = lax.broadcasted_iota(I32, (SUBLANES, width), 1)
    half = SUBLANES // 2
    on_row = (lane & (SUBLANES - 1)) == 2 * (row % half) + row // half

    def token(t):
        w = jnp.concatenate(_gather_tiles(row_ref, tab_ref, t), axis=0)
        lhs = []
        for m_scr in (m1_scr, m2_scr):
            coef = jnp.broadcast_to(m_scr[pl.ds(t, 1), :], (SUBLANES, width))
            lhs.append(jnp.where(on_row, coef, 0.0).astype(BF16))
        res = jnp.dot(jnp.concatenate(lhs, axis=0), w, preferred_element_type=F32)
        o_ref[t] = res[:SUBLANES] + res[SUBLANES:]

    def trip(i, carry):
        for u in range(TOK_UNROLL):
            token(i * TOK_UNROLL + u)
        return carry

    lax.fori_loop(0, tt // TOK_UNROLL, trip, 0)


def _peer_sum(rows2, ce, co, spread, tab, tt):
    n = ce.shape[0]
    kern = functools.partial(_peer_sum_kernel, tt=tt)
    rows = pl.BlockSpec((tt, PEER_SEL), lambda i: (i, 0))
    width = PEER_SEL * TILE_ROWS
    return pl.pallas_call(
        kern,
        grid=(n // tt,),
        in_specs=[pl.BlockSpec((tt, PEER_SEL), lambda i: (i, 0), memory_space=pltpu.SMEM), rows, rows,
                  pl.BlockSpec(spread.shape, lambda i: (0, 0)), pl.BlockSpec(memory_space=pltpu.VMEM)],
        out_specs=pl.BlockSpec((tt, SUBLANES, LANES), lambda i: (i, 0, 0)),
        out_shape=jax.ShapeDtypeStruct((n, SUBLANES, LANES), F32),
        scratch_shapes=[pltpu.VMEM((tt, width), F32), pltpu.VMEM((tt, width), F32)],
        compiler_params=pltpu.CompilerParams(dimension_semantics=("arbitrary",), vmem_limit_bytes=VMEM_LIMIT),
        name="peer_sum",
    )(rows2, ce, co, spread, tab)


def _final_kernel(x_ref, p_ref, g_ref, o_ref):
    peer = jnp.concatenate([p_ref[:, s, :] for s in range(SUBLANES)], axis=1)
    o_ref[...] = _rms(x_ref[...] + peer, g_ref[...])


def _final(x2, peer3, g, tm):
    n = x2.shape[0]
    return pl.pallas_call(
        _final_kernel,
        grid=(n // tm,),
        in_specs=[pl.BlockSpec((tm, D_MODEL), lambda i: (i, 0)),
                  pl.BlockSpec((tm, SUBLANES, LANES), lambda i: (i, 0, 0)),
                  pl.BlockSpec(g.shape, lambda i: (0, 0))],
        out_specs=pl.BlockSpec((tm, D_MODEL), lambda i: (i, 0)),
        out_shape=jax.ShapeDtypeStruct((n, D_MODEL), F32),
        compiler_params=pltpu.CompilerParams(dimension_semantics=("arbitrary",), vmem_limit_bytes=VMEM_LIMIT),
        name="final_norm",
    )(x2, peer3, g)


def _tile(n, pref):
    t = pref
    while n % t:
        t //= 2
    return t


def _layer(l, x2, mem2, pos, invf, b, s_len, m_tok, norm_mix_g, w_in, da_lambda, da_subln_g, w_branch_a,
           w_branch_b, gate_bias, w_out, norm_mem_g, mem_kv_norm_g, w_mem_q, w_mem_kv, w_mem_o, norm_ffn_g,
           peer_w_q, peer_sub_keys, peer_u, peer_v):
    n = b * s_len
    row2 = lambda v: v.reshape(1, -1)
    splits = [0]
    for c in COL_SIZES:
        splits.append(splits[-1] + c)
    da_q, da_k, da_v, ds_q, ds_k, ds_v, ix_q, ix_k, ix_w, gates = (
        w_in[l][:, splits[j]:splits[j + 1]] for j in range(len(COL_SIZES)))
    k0, k1 = ds_k[:, :DS_HEAD_DIM], ds_k[:, DS_HEAD_DIM:]
    w_rope = jnp.concatenate([da_q, da_k, ds_q, ix_q, k0, k0, k1, k1, ix_k, ix_k], axis=1).astype(BF16)
    w_plain = jnp.concatenate([da_v, ds_v], axis=1).astype(BF16)
    w_idx = jnp.pad(ix_w, ((0, 0), (0, LANES - IDX_HEADS))).astype(BF16)
    pr, pp, gate, iw = _in_proj(x2, row2(norm_mix_g[l]), pos, invf, row2(gate_bias[l]), w_rope, w_plain,
                                gates.astype(BF16), w_idx, _tile(n, 256))

    lam_init = 0.8 - 0.6 * math.exp(-0.3 * l)
    t_att = _tile(s_len, 256)
    o_a = _diff_attention(da_lambda[l], pr, pp, row2(da_subln_g[l]), b, s_len, t_att, t_att, lam_init)
    o_b = _dsa_attention(pr, pp, iw, b, s_len, t_att)
    x2 = _merge(o_a, o_b, gate, x2, w_branch_a[l].astype(BF16), w_branch_b[l].astype(BF16),
                w_out[l].astype(BF16), _tile(n, 256))

    kv = _norm_matmul(mem2, row2(mem_kv_norm_g[l]), w_mem_kv[l].astype(BF16), _tile(mem2.shape[0], 256))
    x2 = _mem_attn(x2, row2(norm_mem_g[l]), w_mem_q[l].astype(BF16), kv, w_mem_o[l].astype(BF16),
                   b, s_len, m_tok, _tile(s_len, 256))

    sk = peer_sub_keys[l].reshape(N_SUB, PEER_N_KEYS, PEER_HALF)
    z = jnp.zeros_like(sk)
    keys_p = jnp.where((jnp.arange(N_SUB) % 2 == 0)[:, None, None],
                       jnp.concatenate([sk, z], axis=2), jnp.concatenate([z, sk], axis=2)).astype(BF16)
    hpk, idx_t, row_t, gate_t = _route(x2, row2(norm_ffn_g[l]), peer_w_q[l].astype(BF16), keys_p, _tile(n, 256))
    idx2 = idx_t.reshape(PEER_SEL, n).T
    rows2 = row_t.reshape(PEER_SEL, n).T
    gate2 = gate_t.reshape(PEER_SEL, n).T
    tt = _tile(n, 128)
    r32 = jnp.arange(2 * PAIRS_PER_KTILE)[:, None]
    c256 = jnp.arange(PAIRS_PER_KTILE * TILE_ROWS)[None, :]
    sel = ((c256 // TILE_ROWS == r32 % PAIRS_PER_KTILE)
           & ((c256 % TILE_ROWS) // SUBLANES == r32 // PAIRS_PER_KTILE)).astype(BF16)
    k256 = jnp.arange(2 * PEER_SEL)[:, None]
    c2048 = jnp.arange(PEER_SEL * TILE_ROWS)[None, :]
    spread = ((c2048 // TILE_ROWS == k256 % PEER_SEL)
              & ((c2048 % TILE_ROWS) // SUBLANES == k256 // PEER_SEL)).astype(BF16)
    ce2, co2 = _peer_dot(rows2, _pack_table(peer_u[l]), hpk, gate2, idx2, sel, tt)
    peer3 = _peer_sum(rows2, ce2, co2, spread, _pack_table(peer_v[l]), tt)
    return x2, peer3


def kernel(x, mem, positions, norm_mix_g, w_in, da_lambda, da_subln_g, w_branch_a, w_branch_b, gate_bias, w_out, norm_mem_g, mem_kv_norm_g, w_mem_q, w_mem_kv, w_mem_o, norm_ffn_g, peer_w_q, peer_sub_keys, peer_u, peer_v, final_norm_g):
    b, s_len, d = x.shape
    m_tok = mem.shape[1]
    n = b * s_len
    depth = w_in.shape[0]
    x2 = x.reshape(n, d)
    mem2 = mem.reshape(b * m_tok, d)
    pos = positions.astype(F32).reshape(n, 1)
    inv_freq = ROPE_THETA ** (-(jnp.arange(ROPE_HALF, dtype=F32) * 2.0) / ROPE_DIM)
    invf = jnp.tile(inv_freq, LANES // ROPE_HALF).reshape(1, LANES)
    peer3 = None
    for l in range(depth):
        if peer3 is not None:
            x2 = x2 + peer3.reshape(n, d)
        x2, peer3 = _layer(l, x2, mem2, pos, invf, b, s_len, m_tok, norm_mix_g, w_in, da_lambda, da_subln_g,
                           w_branch_a, w_branch_b, gate_bias, w_out, norm_mem_g, mem_kv_norm_g, w_mem_q, w_mem_kv,
                           w_mem_o, norm_ffn_g, peer_w_q, peer_sub_keys, peer_u, peer_v)
    out = _final(x2, peer3, final_norm_g.reshape(1, d), _tile(n, 256))
    return out.reshape(b, s_len, d)
```

```python
import functools
import math

import jax
import jax.numpy as jnp
from jax import lax
from jax.experimental import pallas as pl
from jax.experimental.pallas import tpu as pltpu

F32 = jnp.float32
BF16 = jnp.bfloat16
I32 = jnp.int32

D_MODEL = 1024
EPS = 1e-6
ROPE_THETA = 500000.0
ROT_HEAD_DIM = 64
ROPE_DIM = 16
ROPE_HALF = ROPE_DIM // 2

DA_HEADS = 4
DA_QK_DIM = 64
DA_V_DIM = 128
DA_WIDTH = 512
DS_HEADS = 8
DS_KV_HEADS = 2
DS_HEAD_DIM = 64
DS_WIDTH = 512
IDX_HEADS = 8
IDX_DIM = 64
TOPK_MAX = 256
MEM_HEADS = 4
MEM_HEAD_DIM = 128
MEM_WIDTH = 512
PEER_HEADS = 8
PEER_N_KEYS = 128
PEER_HALF = 64
PEER_TOPK = 16
PEER_SEL = PEER_HEADS * PEER_TOPK

COL_SIZES = (512, 512, 512, 512, 128, 128, 512, 64, 8, 2048)

LANES = 128
SUBLANES = 8
NEG_BIG = -1e30
INT_MIN = -(2 ** 31)
VMEM_LIMIT = 56 * 1024 * 1024

NT_DIMS = (((1,), (1,)), ((), ()))


def _rms(x, g):
    var = jnp.mean(x * x, axis=-1, keepdims=True)
    return x * lax.rsqrt(var + EPS) * g


def _in_proj_kernel(x_ref, g_ref, pos_ref, invf_ref, bias_ref, wr_ref, wp_ref, wg_ref, ww_ref,
                    pr_ref, pp_ref, gate_ref, iw_ref):
    h = _rms(x_ref[...], g_ref[...]).astype(BF16)
    tm = h.shape[0]
    ang = pos_ref[...] * invf_ref[...]
    cos = jnp.cos(ang)
    sin = jnp.sin(ang)
    lane = lax.broadcasted_iota(I32, (tm, LANES), 1) % ROT_HEAD_DIM
    c_t = jnp.where(lane < ROPE_DIM, cos, 1.0)
    s_lo = jnp.where(lane < ROPE_HALF, -sin, 0.0)
    s_hi = jnp.where((lane >= ROPE_HALF) & (lane < ROPE_DIM), sin, 0.0)
    n_r = wr_ref.shape[1]
    for c0 in range(0, n_r, 512):
        c1 = min(c0 + 512, n_r)
        w = c1 - c0
        y = jnp.dot(h, wr_ref[:, c0:c1], preferred_element_type=F32)
        reps = w // LANES
        ct = jnp.concatenate([c_t] * reps, axis=1)
        sl = jnp.concatenate([s_lo] * reps, axis=1)
        sh = jnp.concatenate([s_hi] * reps, axis=1)
        y = y * ct + pltpu.roll(y, w - ROPE_HALF, 1) * sl + pltpu.roll(y, ROPE_HALF, 1) * sh
        pr_ref[:, c0:c1] = y.astype(pr_ref.dtype)
    pp_ref[...] = jnp.dot(h, wp_ref[...], preferred_element_type=F32).astype(pp_ref.dtype)
    n_g = wg_ref.shape[1]
    for c0 in range(0, n_g, 512):
        y = jnp.dot(h, wg_ref[:, c0:c0 + 512], preferred_element_type=F32)
        gate_ref[:, c0:c0 + 512] = jax.nn.sigmoid(y + bias_ref[:, c0:c0 + 512])
    iw_ref[...] = jnp.dot(h, ww_ref[...], preferred_element_type=F32)


def _in_proj(x2, g, pos, invf, bias, wr, wp, wg, ww, tm):
    n = x2.shape[0]
    full = lambda a: pl.BlockSpec(a.shape, lambda i: (0, 0))
    rows = lambda w: pl.BlockSpec((tm, w), lambda i: (i, 0))
    return pl.pallas_call(
        _in_proj_kernel,
        grid=(n // tm,),
        in_specs=[rows(D_MODEL), full(g), rows(1), full(invf), full(bias), full(wr), full(wp), full(wg), full(ww)],
        out_specs=[rows(wr.shape[1]), rows(wp.shape[1]), rows(wg.shape[1]), rows(ww.shape[1])],
        out_shape=[jax.ShapeDtypeStruct((n, wr.shape[1]), BF16), jax.ShapeDtypeStruct((n, wp.shape[1]), BF16),
                   jax.ShapeDtypeStruct((n, wg.shape[1]), F32), jax.ShapeDtypeStruct((n, ww.shape[1]), F32)],
        compiler_params=pltpu.CompilerParams(dimension_semantics=("arbitrary",), vmem_limit_bytes=VMEM_LIMIT),
        name="in_proj",
    )(x2, g, pos, invf, bias, wr, wp, wg, ww)


def _da_kernel(lam_ref, q_ref, k_ref, v_ref, g_ref, o_ref, *, tq, tk, lam_init):
    i = pl.program_id(2)
    q = q_ref[...]
    lane = lax.broadcasted_iota(I32, q.shape, 1)
    qs = q * jnp.asarray(DA_QK_DIM ** -0.5, q.dtype)
    zero = jnp.zeros_like(qs)
    q_maps = (jnp.where(lane < DA_QK_DIM, qs, zero), jnp.where(lane >= DA_QK_DIM, qs, zero))
    row = i * tq + lax.broadcasted_iota(I32, (tq, tk), 0)
    col0 = lax.broadcasted_iota(I32, (tq, tk), 1)

    def step(j, carry, masked):
        start = pl.multiple_of(j * tk, tk)
        k = k_ref[pl.ds(start, tk), :]
        v = v_ref[pl.ds(start, tk), :]
        out = []
        for mp in range(2):
            m, l, a = carry[3 * mp:3 * mp + 3]
            s = lax.dot_general(q_maps[mp], k, NT_DIMS, preferred_element_type=F32)
            if masked:
                s = jnp.where(col0 + j * tk <= row, s, NEG_BIG)
            m_new = jnp.maximum(m, jnp.max(s, axis=1, keepdims=True))
            alpha = jnp.exp(m - m_new)
            p = jnp.exp(s - m_new)
            l = alpha * l + jnp.sum(p, axis=1, keepdims=True)
            a = alpha * a + jnp.dot(p.astype(v.dtype), v, preferred_element_type=F32)
            out += [m_new, l, a]
        return tuple(out)

    init = []
    for _ in range(2):
        init += [jnp.full((tq, 1), NEG_BIG, F32), jnp.zeros((tq, 1), F32), jnp.zeros((tq, DA_V_DIM), F32)]
    n_full = (i * tq + 1) // tk
    n_tot = ((i + 1) * tq + tk - 1) // tk
    carry = lax.fori_loop(0, n_full, lambda j, c: step(j, c, False), tuple(init))
    carry = lax.fori_loop(n_full, n_tot, lambda j, c: step(j, c, True), carry)
    _, l1, a1, _, l2, a2 = carry

    lp = lam_ref[...]
    lam = (jnp.exp(jnp.sum(lp[0:1] * lp[1:2], axis=1, keepdims=True))
           - jnp.exp(jnp.sum(lp[2:3] * lp[3:4], axis=1, keepdims=True)) + lam_init)
    o = a1 / l1 - lam * (a2 / l2)
    o = _rms(o, g_ref[...]) * (1.0 - lam_init)
    o_ref[...] = o.astype(o_ref.dtype)


def _diff_attention(lam_p, pr, pp, subln_g, b, s_len, tq, tk, lam_init):
    nq = s_len // tq
    n = b * s_len
    kern = functools.partial(_da_kernel, tq=tq, tk=tk, lam_init=lam_init)
    return pl.pallas_call(
        kern,
        grid=(b, DA_HEADS, nq),
        in_specs=[
            pl.BlockSpec(lam_p.shape, lambda bi, h, i: (0, 0)),
            pl.BlockSpec((tq, LANES), lambda bi, h, i: (bi * nq + i, h)),
            pl.BlockSpec((s_len, LANES), lambda bi, h, i: (bi, DA_HEADS + h)),
            pl.BlockSpec((s_len, LANES), lambda bi, h, i: (bi, h)),
            pl.BlockSpec(subln_g.shape, lambda bi, h, i: (0, 0)),
        ],
        out_specs=pl.BlockSpec((tq, LANES), lambda bi, h, i: (bi * nq + i, h)),
        out_shape=jax.ShapeDtypeStruct((n, DA_WIDTH), BF16),
        compiler_params=pltpu.CompilerParams(dimension_semantics=("arbitrary",) * 3, vmem_limit_bytes=VMEM_LIMIT),
        name="diff_attn",
    )(lam_p, pr, pr, pp, subln_g)


PR_DSQ = 8
PR_IXQ = 12
PR_DSK = 16
PR_IXK = 18
PP_DSV = 4


def _dsa_kernel(qi_ref, ki_ref, w_ref, q_ref, k0_ref, k1_ref, v_ref, o_ref, keys_scr, *, tq, n_sel, n_bits):
    tk = tq
    i = pl.program_id(1)
    nck = i + 1
    lane = lax.broadcasted_iota(I32, (tq, LANES), 1)
    low_half = lane < DS_HEAD_DIM
    row = i * tq + lax.broadcasted_iota(I32, (tq, tk), 0)
    col0 = lax.broadcasted_iota(I32, (tq, tk), 1)

    def half_masked(ref, h):
        blk = ref[:, (h // 2) * LANES:(h // 2 + 1) * LANES]
        keep = low_half if h % 2 == 0 else jnp.logical_not(low_half)
        return jnp.where(keep, blk, jnp.zeros_like(blk))

    w = w_ref[...]

    def score_chunk(c, carry):
        kc = ki_ref[pl.ds(pl.multiple_of(c * tk, tk), tk), :]
        acc = jnp.zeros((tq, tk), F32)
        for h in range(IDX_HEADS):
            s = lax.dot_general(half_masked(qi_ref, h), kc, NT_DIMS, preferred_element_type=F32)
            acc = acc + w[:, h:h + 1] * jnp.maximum(s, 0.0)
        bits = pltpu.bitcast(acc, I32)
        key = bits ^ ((bits >> 31) & 0x7FFFFFFF)
        key = jnp.where(acc == 0.0, 0, key)
        key = jnp.where(col0 + c * tk <= row, key, INT_MIN)
        keys_scr[c] = key
        return carry

    lax.fori_loop(0, nck, score_chunk, 0)

    def count(pred):
        def body(c, acc):
            hit = pred(keys_scr[c], col0 + c * tk)
            return acc + hit[:, :LANES] + hit[:, LANES:] if tk == 2 * LANES else acc + hit
        width = LANES if tk == 2 * LANES else tk
        acc = lax.fori_loop(0, nck, body, jnp.zeros((tq, width), I32))
        return jnp.sum(acc, axis=1, keepdims=True)

    def thr_bit(t, ans_u):
        cand_u = ans_u | jnp.left_shift(jnp.int32(1), 31 - t)
        cand = cand_u ^ INT_MIN
        cnt = count(lambda k, col: jnp.where(k >= cand, 1, 0))
        return jnp.where(cnt >= n_sel, cand_u, ans_u)

    thr = lax.fori_loop(0, 32, thr_bit, jnp.zeros((tq, 1), I32)) ^ INT_MIN
    n_gt = count(lambda k, col: jnp.where(k > thr, 1, 0))
    need = n_sel - n_gt

    def tie_bit(t, j_cut):
        cand = j_cut | jnp.left_shift(jnp.int32(1), n_bits - 1 - t)
        cnt = count(lambda k, col: jnp.where(k == thr, jnp.where(col < cand, 1, 0), 0))
        return jnp.where(cnt < need, cand, j_cut)

    j_cut = lax.fori_loop(0, n_bits, tie_bit, jnp.zeros((tq, 1), I32))
    j_cut = jnp.where(thr == INT_MIN, -1, j_cut)

    def bias_chunk(c, carry):
        k = keys_scr[c]
        sel = (k > thr) | ((k == thr) & (col0 + c * tk <= j_cut))
        keys_scr[c] = pltpu.bitcast(jnp.where(sel, 0.0, NEG_BIG).astype(F32), I32)
        return carry

    lax.fori_loop(0, nck, bias_chunk, 0)

    scale = jnp.asarray(DS_HEAD_DIM ** -0.5, BF16)
    outs = []
    for h in range(DS_HEADS):
        grp = h // (DS_HEADS // DS_KV_HEADS)
        k_ref = k0_ref if grp == 0 else k1_ref
        qh = half_masked(q_ref, h) * scale

        def att_chunk(c, carry, k_ref=k_ref, qh=qh):
            m, l, a = carry
            start = pl.multiple_of(c * tk, tk)
            s = lax.dot_general(qh, k_ref[pl.ds(start, tk), :], NT_DIMS, preferred_element_type=F32)
            s = s + pltpu.bitcast(keys_scr[c], F32)
            m_new = jnp.maximum(m, jnp.max(s, axis=1, keepdims=True))
            alpha = jnp.exp(m - m_new)
            p = jnp.exp(s - m_new)
            l = alpha * l + jnp.sum(p, axis=1, keepdims=True)
            v = v_ref[pl.ds(start, tk), :]
            a = alpha * a + jnp.dot(p.astype(v.dtype), v, preferred_element_type=F32)
            return m_new, l, a

        init = (jnp.full((tq, 1), NEG_BIG, F32), jnp.zeros((tq, 1), F32), jnp.zeros((tq, LANES), F32))
        _, l, a = lax.fori_loop(0, nck, att_chunk, init)
        o = a / l
        want_low = h % 2 == 0
        have_low = grp == 0
        outs.append(o if want_low == have_low else pltpu.roll(o, DS_HEAD_DIM, 1))
    for pair in range(DS_HEADS // 2):
        blk = jnp.where(low_half, outs[2 * pair], outs[2 * pair + 1])
        o_ref[:, pair * LANES:(pair + 1) * LANES] = blk.astype(o_ref.dtype)


def _dsa_attention(pr, pp, iw, b, s_len, tq):
    nq = s_len // tq
    n = b * s_len
    n_sel = min(TOPK_MAX, s_len // 4)
    n_bits = max(1, (s_len - 1).bit_length())
    kern = functools.partial(_dsa_kernel, tq=tq, n_sel=n_sel, n_bits=n_bits)
    seq = lambda blk: pl.BlockSpec((s_len, LANES), lambda bi, i: (bi, blk))
    return pl.pallas_call(
        kern,
        grid=(b, nq),
        in_specs=[
            pl.BlockSpec((tq, 4 * LANES), lambda bi, i: (bi * nq + i, PR_IXQ // 4)),
            seq(PR_IXK),
            pl.BlockSpec((tq, LANES), lambda bi, i: (bi * nq + i, 0)),
            pl.BlockSpec((tq, 4 * LANES), lambda bi, i: (bi * nq + i, PR_DSQ // 4)),
            seq(PR_DSK), seq(PR_DSK + 1),
            seq(PP_DSV),
        ],
        out_specs=pl.BlockSpec((tq, DS_WIDTH), lambda bi, i: (bi * nq + i, 0)),
        out_shape=jax.ShapeDtypeStruct((n, DS_WIDTH), BF16),
        scratch_shapes=[pltpu.VMEM((nq, tq, tq), I32)],
        compiler_params=pltpu.CompilerParams(dimension_semantics=("arbitrary",) * 2, vmem_limit_bytes=VMEM_LIMIT),
        name="dsa_attn",
    )(pr, pr, iw, pr, pr, pr, pp)


def _merge_kernel(oa_ref, ob_ref, gate_ref, x_ref, wa_ref, wb_ref, wo_ref, o_ref):
    ya = jnp.dot(oa_ref[...], wa_ref[...], preferred_element_type=F32)
    yb = jnp.dot(ob_ref[...], wb_ref[...], preferred_element_type=F32)
    mix = gate_ref[:, :D_MODEL] * ya + gate_ref[:, D_MODEL:] * yb
    o_ref[...] = x_ref[...] + jnp.dot(mix.astype(BF16), wo_ref[...], preferred_element_type=F32)


def _merge(oa, ob, gate, x2, wa, wb, wo, tm):
    n = x2.shape[0]
    full = lambda a: pl.BlockSpec(a.shape, lambda i: (0, 0))
    rows = lambda w: pl.BlockSpec((tm, w), lambda i: (i, 0))
    return pl.pallas_call(
        _merge_kernel,
        grid=(n // tm,),
        in_specs=[rows(DA_WIDTH), rows(DS_WIDTH), rows(2 * D_MODEL), rows(D_MODEL), full(wa), full(wb), full(wo)],
        out_specs=rows(D_MODEL),
        out_shape=jax.ShapeDtypeStruct((n, D_MODEL), F32),
        compiler_params=pltpu.CompilerParams(dimension_semantics=("arbitrary",), vmem_limit_bytes=VMEM_LIMIT),
        name="merge",
    )(oa, ob, gate, x2, wa, wb, wo)


def _norm_matmul_kernel(x_ref, g_ref, w_ref, o_ref):
    h = _rms(x_ref[...], g_ref[...]).astype(BF16)
    o_ref[...] = jnp.dot(h, w_ref[...], preferred_element_type=F32).astype(o_ref.dtype)


def _norm_matmul(x2, g, w, tm):
    n = x2.shape[0]
    return pl.pallas_call(
        _norm_matmul_kernel,
        grid=(n // tm,),
        in_specs=[pl.BlockSpec((tm, x2.shape[1]), lambda i: (i, 0)), pl.BlockSpec(g.shape, lambda i: (0, 0)),
                  pl.BlockSpec(w.shape, lambda i: (0, 0))],
        out_specs=pl.BlockSpec((tm, w.shape[1]), lambda i: (i, 0)),
        out_shape=jax.ShapeDtypeStruct((n, w.shape[1]), BF16),
        compiler_params=pltpu.CompilerParams(dimension_semantics=("arbitrary",), vmem_limit_bytes=VMEM_LIMIT),
        name="mem_kv_proj",
    )(x2, g, w)


def _mem_attn_kernel(x_ref, g_ref, wq_ref, kv_ref, wo_ref, o_ref):
    x = x_ref[...]
    hn = _rms(x, g_ref[...]).astype(BF16)
    q = jnp.dot(hn, wq_ref[...], preferred_element_type=F32).astype(BF16)
    scale = MEM_HEAD_DIM ** -0.5
    heads = []
    for h in range(MEM_HEADS):
        qh = q[:, h * MEM_HEAD_DIM:(h + 1) * MEM_HEAD_DIM]
        kh = kv_ref[:, h * MEM_HEAD_DIM:(h + 1) * MEM_HEAD_DIM]
        vh = kv_ref[:, MEM_WIDTH + h * MEM_HEAD_DIM:MEM_WIDTH + (h + 1) * MEM_HEAD_DIM]
        s = lax.dot_general(qh, kh, NT_DIMS, preferred_element_type=F32) * scale
        m = jnp.max(s, axis=1, keepdims=True)
        p = jnp.exp(s - m)
        l = jnp.sum(p, axis=1, keepdims=True)
        heads.append(jnp.dot(p.astype(BF16), vh, preferred_element_type=F32) / l)
    o = jnp.concatenate(heads, axis=1).astype(BF16)
    o_ref[...] = x + jnp.dot(o, wo_ref[...], preferred_element_type=F32)


def _mem_attn(x2, g, wq, kv, wo, b, s_len, m_tok, tm):
    nb = s_len // tm
    n = x2.shape[0]
    full = lambda a: pl.BlockSpec(a.shape, lambda bi, i: (0, 0))
    return pl.pallas_call(
        _mem_attn_kernel,
        grid=(b, nb),
        in_specs=[pl.BlockSpec((tm, D_MODEL), lambda bi, i: (bi * nb + i, 0)), full(g), full(wq),
                  pl.BlockSpec((m_tok, 2 * MEM_WIDTH), lambda bi, i: (bi, 0)), full(wo)],
        out_specs=pl.BlockSpec((tm, D_MODEL), lambda bi, i: (bi * nb + i, 0)),
        out_shape=jax.ShapeDtypeStruct((n, D_MODEL), F32),
        compiler_params=pltpu.CompilerParams(dimension_semantics=("arbitrary",) * 2, vmem_limit_bytes=VMEM_LIMIT),
        name="mem_attn",
    )(x2, g, wq, kv, wo)


N_SUB = 2 * PEER_HEADS
EXPERT_BITS = 14


def _route_kernel(x_ref, g_ref, wq_ref, keys_ref, hf_ref, idx_ref, row_ref, gate_ref, sc_scr, ts_scr, ti_scr):
    hf = _rms(x_ref[...], g_ref[...])
    tm = hf.shape[0]
    hb = hf.astype(BF16)
    bits = pltpu.bitcast(hb.astype(F32), jnp.uint32)
    half_d = D_MODEL // 2
    for s in range(SUBLANES // 2):
        lo = bits[:, s * LANES:(s + 1) * LANES] >> 16
        hi = bits[:, half_d + s * LANES:half_d + (s + 1) * LANES] & jnp.uint32(0xFFFF0000)
        hf_ref[:, s, :] = lo | hi
        hf_ref[:, s + SUBLANES // 2, :] = lo | hi
    q = jnp.dot(hb, wq_ref[...], preferred_element_type=F32).astype(BF16)
    for g in range(N_SUB):
        blk = q[:, (g // 2) * LANES:(g // 2 + 1) * LANES]
        sc_scr[g] = lax.dot_general(keys_ref[g], blk, NT_DIMS, preferred_element_type=F32)

    key_id = lax.broadcasted_iota(I32, (PEER_N_KEYS, tm), 0)

    def sub_topk(g, carry):
        x = sc_scr[g]
        vals, ids = [], []
        for _ in range(PEER_TOPK):
            m = jnp.max(x, axis=0, keepdims=True)
            idx = jnp.min(jnp.where(x == m, key_id, PEER_N_KEYS), axis=0, keepdims=True)
            vals.append(m)
            ids.append(idx)
            x = jnp.where(key_id == idx, -jnp.inf, x)
        ts_scr[g] = jnp.concatenate(vals, axis=0)
        ti_scr[g] = jnp.concatenate(ids, axis=0)
        return carry

    lax.fori_loop(0, N_SUB, sub_topk, 0)

    n_cand = PEER_TOPK * PEER_TOPK
    pos = lax.broadcasted_iota(I32, (n_cand, tm), 0)

    def head_topk(h, carry):
        s0, s1 = ts_scr[2 * h], ts_scr[2 * h + 1]
        i0, i1 = ti_scr[2 * h], ti_scr[2 * h + 1]
        cand = jnp.concatenate([s0[a:a + 1] + s1 for a in range(PEER_TOPK)], axis=0)
        cid = jnp.concatenate([i0[a:a + 1] * PEER_N_KEYS + i1 for a in range(PEER_TOPK)], axis=0)
        code = (pos << EXPERT_BITS) | cid
        big = jnp.int32(2 ** 30)
        vals, ids = [], []
        for _ in range(PEER_TOPK):
            m = jnp.max(cand, axis=0, keepdims=True)
            best = jnp.min(jnp.where(cand == m, code, big), axis=0, keepdims=True)
            vals.append(m)
            ids.append(best & (2 ** EXPERT_BITS - 1))
            cand = jnp.where(code == best, -jnp.inf, cand)
        best_s = jnp.concatenate(vals, axis=0)
        e = jnp.exp(best_s - best_s[0:1])
        gate_ref[h] = e / jnp.sum(e, axis=0, keepdims=True)
        best_i = jnp.concatenate(ids, axis=0)
        idx_ref[h] = best_i
        row_ref[h] = (best_i >> 1) * SUBLANES
        return carry

    lax.fori_loop(0, PEER_HEADS, head_topk, 0)


def _route(x2, g, wq, keys_p, tm):
    n = x2.shape[0]
    return pl.pallas_call(
        _route_kernel,
        grid=(n // tm,),
        in_specs=[pl.BlockSpec((tm, D_MODEL), lambda i: (i, 0)), pl.BlockSpec(g.shape, lambda i: (0, 0)),
                  pl.BlockSpec(wq.shape, lambda i: (0, 0)), pl.BlockSpec(keys_p.shape, lambda i: (0, 0, 0))],
        out_specs=[pl.BlockSpec((tm, SUBLANES, LANES), lambda i: (i, 0, 0))]
        + [pl.BlockSpec((PEER_HEADS, PEER_TOPK, tm), lambda i: (0, 0, i))] * 3,
        out_shape=[jax.ShapeDtypeStruct((n, SUBLANES, LANES), jnp.uint32),
                   jax.ShapeDtypeStruct((PEER_HEADS, PEER_TOPK, n), I32),
                   jax.ShapeDtypeStruct((PEER_HEADS, PEER_TOPK, n), I32),
                   jax.ShapeDtypeStruct((PEER_HEADS, PEER_TOPK, n), F32)],
        scratch_shapes=[pltpu.VMEM((N_SUB, PEER_N_KEYS, tm), F32), pltpu.VMEM((N_SUB, PEER_TOPK, tm), F32),
                        pltpu.VMEM((N_SUB, PEER_TOPK, tm), I32)],
        compiler_params=pltpu.CompilerParams(dimension_semantics=("arbitrary",), vmem_limit_bytes=VMEM_LIMIT),
        name="peer_route",
    )(x2, g, wq, keys_p)


TILE_ROWS = 2 * SUBLANES
PAIRS_PER_KTILE = 16
TOK_UNROLL = 4


def _pack_table(t):
    e = t.shape[0]
    bits = lax.bitcast_convert_type(t.astype(BF16), jnp.uint16).astype(jnp.uint32)
    words = bits[:, :D_MODEL // 2] | (bits[:, D_MODEL // 2:] << 16)
    return words.reshape(e * (SUBLANES // 2), LANES)


def _gather_tiles(row_ref, tab_ref, t):
    tiles = []
    for k in range(PEER_SEL):
        start = pl.multiple_of(row_ref[t, k], SUBLANES)
        tiles.append(pltpu.bitcast(tab_ref[pl.ds(start, SUBLANES), :], BF16))
    return tiles


def _peer_dot_kernel(row_ref, tab_ref, h_ref, gate_ref, idxv_ref, sel_ref, ce_ref, co_ref,
                     ye_scr, yo_scr, ze_scr, zo_scr, *, tt):
    ones = jnp.ones((SUBLANES, LANES), BF16)
    half_rows = PAIRS_PER_KTILE
    n_trip = tt // TOK_UNROLL

    def row_sums(i, slot):
        for u in range(TOK_UNROLL):
            t = i * TOK_UNROLL + u
            hp = pltpu.bitcast(h_ref[t], BF16)
            prods = [tile * hp for tile in _gather_tiles(row_ref, tab_ref, t)]
            for kt in range(PEER_SEL // PAIRS_PER_KTILE):
                stack = jnp.concatenate(prods[kt * PAIRS_PER_KTILE:(kt + 1) * PAIRS_PER_KTILE], axis=0)
                y = jnp.dot(sel_ref[...], stack, preferred_element_type=F32)
                r0 = u * PEER_SEL + kt * half_rows
                ye_scr[slot, r0:r0 + half_rows, :] = y[:half_rows].astype(BF16)
                yo_scr[slot, r0:r0 + half_rows, :] = y[half_rows:].astype(BF16)

    def lane_sums(i, slot):
        ze = lax.dot_general(ones, ye_scr[slot], NT_DIMS, preferred_element_type=F32)
        zo = lax.dot_general(ones, yo_scr[slot], NT_DIMS, preferred_element_type=F32)
        for u in range(TOK_UNROLL):
            t = i * TOK_UNROLL + u
            ze_scr[pl.ds(t, 1), :] = ze[0:1, u * PEER_SEL:(u + 1) * PEER_SEL]
            zo_scr[pl.ds(t, 1), :] = zo[0:1, u * PEER_SEL:(u + 1) * PEER_SEL]

    def trip(i, carry):
        slot = i & 1
        lane_sums(i - 1, 1 - slot)
        row_sums(i, slot)
        return carry

    row_sums(0, 0)
    lax.fori_loop(1, n_trip, trip, 0)
    lane_sums(n_trip - 1, (n_trip - 1) & 1)
    even = (idxv_ref[...] & 1) == 0
    a = jnp.where(even, ze_scr[...], zo_scr[...])
    c = 0.5 * a * (1.0 + lax.erf(a * (2.0 ** -0.5))) * gate_ref[...]
    ce_ref[...] = jnp.where(even, c, 0.0)
    co_ref[...] = jnp.where(even, 0.0, c)


def _peer_dot(rows2, tab, hpk, gate, idx2, sel, tt):
    n = hpk.shape[0]
    kern = functools.partial(_peer_dot_kernel, tt=tt)
    rows = pl.BlockSpec((tt, PEER_SEL), lambda i: (i, 0))
    return pl.pallas_call(
        kern,
        grid=(n // tt,),
        in_specs=[pl.BlockSpec((tt, PEER_SEL), lambda i: (i, 0), memory_space=pltpu.SMEM),
                  pl.BlockSpec(memory_space=pltpu.VMEM),
                  pl.BlockSpec((tt, SUBLANES, LANES), lambda i: (i, 0, 0)),
                  rows, rows, pl.BlockSpec(sel.shape, lambda i: (0, 0))],
        out_specs=[rows, rows],
        out_shape=[jax.ShapeDtypeStruct((n, PEER_SEL), F32), jax.ShapeDtypeStruct((n, PEER_SEL), F32)],
        scratch_shapes=[pltpu.VMEM((2, TOK_UNROLL * PEER_SEL, LANES), BF16)] * 2
        + [pltpu.VMEM((tt, PEER_SEL), F32)] * 2,
        compiler_params=pltpu.CompilerParams(dimension_semantics=("arbitrary",), vmem_limit_bytes=VMEM_LIMIT),
        name="peer_dot",
    )(rows2, tab, hpk, gate, idx2, sel)


def _peer_sum_kernel(row_ref, ce_ref, co_ref, spread_ref, tab_ref, o_ref, m1_scr, m2_scr, *, tt):
    width = PEER_SEL * TILE_ROWS
    cc = jnp.concatenate([ce_ref[...], co_ref[...]], axis=1)
    c1 = cc.astype(BF16)
    c2 = (cc - c1.astype(F32)).astype(BF16)
    m1_scr[...] = jnp.dot(c1, spread_ref[...], preferred_element_type=F32)
    m2_scr[...] = jnp.dot(c2, spread_ref[...], preferred_element_type=F32)
    row = lax.broadcasted_iota(I32, (SUBLANES, width), 0)
    lane = lax.broadcasted_iota(I32, (SUBLANES, width), 1)
    half = SUBLANES // 2
    on_row = (lane & (SUBLANES - 1)) == 2 * (row % half) + row // half

    def token(t):
        w = jnp.concatenate(_gather_tiles(row_ref, tab_ref, t), axis=0)
        lhs = []
        for m_scr in (m1_scr, m2_scr):
            coef = jnp.broadcast_to(m_scr[pl.ds(t, 1), :], (SUBLANES, width))
            lhs.append(jnp.where(on_row, coef, 0.0).astype(BF16))
        res = jnp.dot(jnp.concatenate(lhs, axis=0), w, preferred_element_type=F32)
        o_ref[t] = res[:SUBLANES] + res[SUBLANES:]

    def trip(i, carry):
        for u in range(TOK_UNROLL):
            token(i * TOK_UNROLL + u)
        return carry

    lax.fori_loop(0, tt // TOK_UNROLL, trip, 0)


def _peer_sum(rows2, ce, co, spread, tab, tt):
    n = ce.shape[0]
    kern = functools.partial(_peer_sum_kernel, tt=tt)
    rows = pl.BlockSpec((tt, PEER_SEL), lambda i: (i, 0))
    width = PEER_SEL * TILE_ROWS
    return pl.pallas_call(
        kern,
        grid=(n // tt,),
        in_specs=[pl.BlockSpec((tt, PEER_SEL), lambda i: (i, 0), memory_space=pltpu.SMEM), rows, rows,
                  pl.BlockSpec(spread.shape, lambda i: (0, 0)), pl.BlockSpec(memory_space=pltpu.VMEM)],
        out_specs=pl.BlockSpec((tt, SUBLANES, LANES), lambda i: (i, 0, 0)),
        out_shape=jax.ShapeDtypeStruct((n, SUBLANES, LANES), F32),
        scratch_shapes=[pltpu.VMEM((tt, width), F32), pltpu.VMEM((tt, width), F32)],
        compiler_params=pltpu.CompilerParams(dimension_semantics=("arbitrary",), vmem_limit_bytes=VMEM_LIMIT),
        name="peer_sum",
    )(rows2, ce, co, spread, tab)


def _final_kernel(x_ref, p_ref, g_ref, o_ref):
    peer = jnp.concatenate([p_ref[:, s, :] for s in range(SUBLANES)], axis=1)
    o_ref[...] = _rms(x_ref[...] + peer, g_ref[...])


def _final(x2, peer3, g, tm):
    n = x2.shape[0]
    return pl.pallas_call(
        _final_kernel,
        grid=(n // tm,),
        in_specs=[pl.BlockSpec((tm, D_MODEL), lambda i: (i, 0)),
                  pl.BlockSpec((tm, SUBLANES, LANES), lambda i: (i, 0, 0)),
                  pl.BlockSpec(g.shape, lambda i: (0, 0))],
        out_specs=pl.BlockSpec((tm, D_MODEL), lambda i: (i, 0)),
        out_shape=jax.ShapeDtypeStruct((n, D_MODEL), F32),
        compiler_params=pltpu.CompilerParams(dimension_semantics=("arbitrary",), vmem_limit_bytes=VMEM_LIMIT),
        name="final_norm",
    )(x2, peer3, g)


def _tile(n, pref):
    t = pref
    while n % t:
        t //= 2
    return t


def _layer(l, x2, mem2, pos, invf, b, s_len, m_tok, norm_mix_g, w_in, da_lambda, da_subln_g, w_branch_a,
           w_branch_b, gate_bias, w_out, norm_mem_g, mem_kv_norm_g, w_mem_q, w_mem_kv, w_mem_o, norm_ffn_g,
           peer_w_q, peer_sub_keys, peer_u, peer_v):
    n = b * s_len
    row2 = lambda v: v.reshape(1, -1)
    splits = [0]
    for c in COL_SIZES:
        splits.append(splits[-1] + c)
    da_q, da_k, da_v, ds_q, ds_k, ds_v, ix_q, ix_k, ix_w, gates = (
        w_in[l][:, splits[j]:splits[j + 1]] for j in range(len(COL_SIZES)))
    k0, k1 = ds_k[:, :DS_HEAD_DIM], ds_k[:, DS_HEAD_DIM:]
    w_rope = jnp.concatenate([da_q, da_k, ds_q, ix_q, k0, k0, k1, k1, ix_k, ix_k], axis=1).astype(BF16)
    w_plain = jnp.concatenate([da_v, ds_v], axis=1).astype(BF16)
    w_idx = jnp.pad(ix_w, ((0, 0), (0, LANES - IDX_HEADS))).astype(BF16)
    pr, pp, gate, iw = _in_proj(x2, row2(norm_mix_g[l]), pos, invf, row2(gate_bias[l]), w_rope, w_plain,
                                gates.astype(BF16), w_idx, _tile(n, 256))

    lam_init = 0.8 - 0.6 * math.exp(-0.3 * l)
    t_att = _tile(s_len, 256)
    o_a = _diff_attention(da_lambda[l], pr, pp, row2(da_subln_g[l]), b, s_len, t_att, t_att, lam_init)
    o_b = _dsa_attention(pr, pp, iw, b, s_len, t_att)
    x2 = _merge(o_a, o_b, gate, x2, w_branch_a[l].astype(BF16), w_branch_b[l].astype(BF16),
                w_out[l].astype(BF16), _tile(n, 256))

    kv = _norm_matmul(mem2, row2(mem_kv_norm_g[l]), w_mem_kv[l].astype(BF16), _tile(mem2.shape[0], 256))
    x2 = _mem_attn(x2, row2(norm_mem_g[l]), w_mem_q[l].astype(BF16), kv, w_mem_o[l].astype(BF16),
                   b, s_len, m_tok, _tile(s_len, 256))

    sk = peer_sub_keys[l].reshape(N_SUB, PEER_N_KEYS, PEER_HALF)
    z = jnp.zeros_like(sk)
    keys_p = jnp.where((jnp.arange(N_SUB) % 2 == 0)[:, None, None],
                       jnp.concatenate([sk, z], axis=2), jnp.concatenate([z, sk], axis=2)).astype(BF16)
    hpk, idx_t, row_t, gate_t = _route(x2, row2(norm_ffn_g[l]), peer_w_q[l].astype(BF16), keys_p, _tile(n, 256))
    idx2 = idx_t.reshape(PEER_SEL, n).T
    rows2 = row_t.reshape(PEER_SEL, n).T
    gate2 = gate_t.reshape(PEER_SEL, n).T
    tt = _tile(n, 128)
    r32 = jnp.arange(2 * PAIRS_PER_KTILE)[:, None]
    c256 = jnp.arange(PAIRS_PER_KTILE * TILE_ROWS)[None, :]
    sel = ((c256 // TILE_ROWS == r32 % PAIRS_PER_KTILE)
           & ((c256 % TILE_ROWS) // SUBLANES == r32 // PAIRS_PER_KTILE)).astype(BF16)
    k256 = jnp.arange(2 * PEER_SEL)[:, None]
    c2048 = jnp.arange(PEER_SEL * TILE_ROWS)[None, :]
    spread = ((c2048 // TILE_ROWS == k256 % PEER_SEL)
              & ((c2048 % TILE_ROWS) // SUBLANES == k256 // PEER_SEL)).astype(BF16)
    ce2, co2 = _peer_dot(rows2, _pack_table(peer_u[l]), hpk, gate2, idx2, sel, tt)
    peer3 = _peer_sum(rows2, ce2, co2, spread, _pack_table(peer_v[l]), tt)
    return x2, peer3


def kernel(x, mem, positions, norm_mix_g, w_in, da_lambda, da_subln_g, w_branch_a, w_branch_b, gate_bias, w_out, norm_mem_g, mem_kv_norm_g, w_mem_q, w_mem_kv, w_mem_o, norm_ffn_g, peer_w_q, peer_sub_keys, peer_u, peer_v, final_norm_g):
    b, s_len, d = x.shape
    m_tok = mem.shape[1]
    n = b * s_len
    depth = w_in.shape[0]
    x2 = x.reshape(n, d)
    mem2 = mem.reshape(b * m_tok, d)
    pos = positions.astype(F32).reshape(n, 1)
    inv_freq = ROPE_THETA ** (-(jnp.arange(ROPE_HALF, dtype=F32) * 2.0) / ROPE_DIM)
    invf = jnp.tile(inv_freq, LANES // ROPE_HALF).reshape(1, LANES)
    peer3 = None
    for l in range(depth):
        if peer3 is not None:
            x2 = x2 + peer3.reshape(n, d)
        x2, peer3 = _layer(l, x2, mem2, pos, invf, b, s_len, m_tok, norm_mix_g, w_in, da_lambda, da_subln_g,
                           w_branch_a, w_branch_b, gate_bias, w_out, norm_mem_g, mem_kv_norm_g, w_mem_q, w_mem_kv,
                           w_mem_o, norm_ffn_g, peer_w_q, peer_sub_keys, peer_u, peer_v)
    out = _final(x2, peer3, final_norm_g.reshape(1, d), _tile(n, 256))
    return out.reshape(b, s_len, d)
```

```python
import functools
import math

import jax
import jax.numpy as jnp
from jax import lax
from jax.experimental import pallas as pl
from jax.experimental.pallas import tpu as pltpu

F32 = jnp.float32
BF16 = jnp.bfloat16
I32 = jnp.int32

D_MODEL = 1024
EPS = 1e-6
ROPE_THETA = 500000.0
ROT_HEAD_DIM = 64
ROPE_DIM = 16
ROPE_HALF = ROPE_DIM // 2

DA_HEADS = 4
DA_QK_DIM = 64
DA_V_DIM = 128
DA_WIDTH = 512
DS_HEADS = 8
DS_KV_HEADS = 2
DS_HEAD_DIM = 64
DS_WIDTH = 512
IDX_HEADS = 8
IDX_DIM = 64
TOPK_MAX = 256
MEM_HEADS = 4
MEM_HEAD_DIM = 128
MEM_WIDTH = 512
PEER_HEADS = 8
PEER_N_KEYS = 128
PEER_HALF = 64
PEER_TOPK = 16
PEER_SEL = PEER_HEADS * PEER_TOPK

COL_SIZES = (512, 512, 512, 512, 128, 128, 512, 64, 8, 2048)

LANES = 128
SUBLANES = 8
NEG_BIG = -1e30
INT_MIN = -(2 ** 31)
VMEM_LIMIT = 56 * 1024 * 1024

NT_DIMS = (((1,), (1,)), ((), ()))


def _rms(x, g):
    var = jnp.mean(x * x, axis=-1, keepdims=True)
    return x * lax.rsqrt(var + EPS) * g


def _rope_rows(y, cos8, sin8):
    pieces = []
    for r0 in range(0, y.shape[0], ROT_HEAD_DIM):
        t1 = y[r0:r0 + ROPE_HALF]
        t2 = y[r0 + ROPE_HALF:r0 + ROPE_DIM]
        pieces += [t1 * cos8 - t2 * sin8, t2 * cos8 + t1 * sin8, y[r0 + ROPE_DIM:r0 + ROT_HEAD_DIM]]
    return jnp.concatenate(pieces, axis=0)


def _in_proj_kernel(x_ref, g_ref, pos_ref, invf_ref, pos_t_ref, invf8_ref, bias_ref, wr_ref, wg_ref,
                    wqt_ref, wvt_ref, wwt_ref, pr_ref, gate_ref, qt_ref, vt_ref, iwt_ref):
    h = _rms(x_ref[...], g_ref[...]).astype(BF16)
    tm = h.shape[0]
    ang_t = invf8_ref[...] * pos_t_ref[...]
    cos8 = jnp.cos(ang_t)
    sin8 = jnp.sin(ang_t)
    row_chunk = 2 * LANES
    for r0 in range(0, wqt_ref.shape[0], row_chunk):
        y = lax.dot_general(wqt_ref[r0:r0 + row_chunk, :], h, NT_DIMS, preferred_element_type=F32)
        qt_ref[r0:r0 + row_chunk, :] = _rope_rows(y, cos8, sin8).astype(qt_ref.dtype)
    for r0 in range(0, wvt_ref.shape[0], row_chunk):
        r1 = min(r0 + row_chunk, wvt_ref.shape[0])
        y = lax.dot_general(wvt_ref[r0:r1, :], h, NT_DIMS, preferred_element_type=F32)
        vt_ref[0, r0:r1, :] = y.astype(vt_ref.dtype)
    iwt_ref[...] = lax.dot_general(wwt_ref[...], h, NT_DIMS, preferred_element_type=F32)
    ang = pos_ref[...] * invf_ref[...]
    cos = jnp.cos(ang)
    sin = jnp.sin(ang)
    lane = lax.broadcasted_iota(I32, (tm, LANES), 1) % ROT_HEAD_DIM
    c_t = jnp.where(lane < ROPE_DIM, cos, 1.0)
    s_lo = jnp.where(lane < ROPE_HALF, -sin, 0.0)
    s_hi = jnp.where((lane >= ROPE_HALF) & (lane < ROPE_DIM), sin, 0.0)
    n_r = wr_ref.shape[1]
    for c0 in range(0, n_r, 512):
        c1 = min(c0 + 512, n_r)
        w = c1 - c0
        y = jnp.dot(h, wr_ref[:, c0:c1], preferred_element_type=F32)
        reps = w // LANES
        ct = jnp.concatenate([c_t] * reps, axis=1)
        sl = jnp.concatenate([s_lo] * reps, axis=1)
        sh = jnp.concatenate([s_hi] * reps, axis=1)
        y = y * ct + pltpu.roll(y, w - ROPE_HALF, 1) * sl + pltpu.roll(y, ROPE_HALF, 1) * sh
        pr_ref[:, c0:c1] = y.astype(pr_ref.dtype)
    n_g = wg_ref.shape[1]
    for c0 in range(0, n_g, 512):
        y = jnp.dot(h, wg_ref[:, c0:c0 + 512], preferred_element_type=F32)
        gate_ref[:, c0:c0 + 512] = jax.nn.sigmoid(y + bias_ref[:, c0:c0 + 512])


def _in_proj(x2, g, pos, invf, pos_t, invf8, bias, wr, wg, wqt, wvt, wwt, tm):
    n = x2.shape[0]
    full = lambda a: pl.BlockSpec(a.shape, lambda i: (0, 0))
    rows = lambda w: pl.BlockSpec((tm, w), lambda i: (i, 0))
    cols = lambda r: pl.BlockSpec((r, tm), lambda i: (0, i))
    return pl.pallas_call(
        _in_proj_kernel,
        grid=(n // tm,),
        in_specs=[rows(D_MODEL), full(g), rows(1), full(invf), cols(1), full(invf8), full(bias), full(wr), full(wg),
                  full(wqt), full(wvt), full(wwt)],
        out_specs=[rows(wr.shape[1]), rows(wg.shape[1]), cols(wqt.shape[0]),
                   pl.BlockSpec((1, wvt.shape[0], tm), lambda i: (i, 0, 0)), cols(wwt.shape[0])],
        out_shape=[jax.ShapeDtypeStruct((n, wr.shape[1]), BF16), jax.ShapeDtypeStruct((n, wg.shape[1]), F32),
                   jax.ShapeDtypeStruct((wqt.shape[0], n), BF16),
                   jax.ShapeDtypeStruct((n // tm, wvt.shape[0], tm), BF16),
                   jax.ShapeDtypeStruct((wwt.shape[0], n), F32)],
        compiler_params=pltpu.CompilerParams(dimension_semantics=("arbitrary",), vmem_limit_bytes=VMEM_LIMIT),
        name="in_proj",
    )(x2, g, pos, invf, pos_t, invf8, bias, wr, wg, wqt, wvt, wwt)


QT_DA = 0
QT_DS = 4
QT_IX = 8
VT_DA = 0
VT_DS = 4
PR_DAK = 0
PR_DSK = 4
PR_IXK = 6


def _online_softmax_step(s, m, l):
    m_new = jnp.maximum(m, jnp.max(s, axis=0, keepdims=True))
    alpha = jnp.exp(m - m_new)
    p = jnp.exp(s - m_new)
    return m_new, alpha * l + jnp.sum(p, axis=0, keepdims=True), alpha, p


def _da_kernel(lam_ref, qt_ref, k_ref, vt_ref, g_ref, o_ref, *, tq, tk, lam_init):
    i = pl.program_id(2)
    qt = qt_ref[...] * jnp.asarray(DA_QK_DIM ** -0.5, qt_ref.dtype)
    frow = lax.broadcasted_iota(I32, qt.shape, 0)
    zero = jnp.zeros_like(qt)
    q_maps = (jnp.where(frow < DA_QK_DIM, qt, zero), jnp.where(frow >= DA_QK_DIM, qt, zero))
    kpos0 = lax.broadcasted_iota(I32, (tk, tq), 0)
    qpos = i * tq + lax.broadcasted_iota(I32, (tk, tq), 1)

    def step(j, carry, masked):
        k = k_ref[pl.ds(pl.multiple_of(j * tk, tk), tk), :]
        vt = vt_ref[j]
        out = []
        for mp in range(2):
            m, l, a = carry[3 * mp:3 * mp + 3]
            s = jnp.dot(k, q_maps[mp], preferred_element_type=F32)
            if masked:
                s = jnp.where(kpos0 + j * tk <= qpos, s, NEG_BIG)
            m, l, alpha, p = _online_softmax_step(s, m, l)
            a = alpha * a + jnp.dot(vt, p.astype(vt.dtype), preferred_element_type=F32)
            out += [m, l, a]
        return tuple(out)

    init = []
    for _ in range(2):
        init += [jnp.full((1, tq), NEG_BIG, F32), jnp.zeros((1, tq), F32), jnp.zeros((DA_V_DIM, tq), F32)]
    n_full = (i * tq + 1) // tk
    n_tot = ((i + 1) * tq + tk - 1) // tk
    carry = lax.fori_loop(0, n_full, lambda j, c: step(j, c, False), tuple(init))
    carry = lax.fori_loop(n_full, n_tot, lambda j, c: step(j, c, True), carry)
    _, l1, a1, _, l2, a2 = carry

    lp = lam_ref[...]
    lam = (jnp.exp(jnp.sum(lp[0:1] * lp[1:2], axis=1, keepdims=True))
           - jnp.exp(jnp.sum(lp[2:3] * lp[3:4], axis=1, keepdims=True)) + lam_init)
    o = a1 / l1 - lam * (a2 / l2)
    var = jnp.mean(o * o, axis=0, keepdims=True)
    o = o * lax.rsqrt(var + EPS) * g_ref[...] * (1.0 - lam_init)
    o_ref[...] = o.T.astype(o_ref.dtype)


def _diff_attention(lam_p, qt, pr, vt3, subln_g, b, s_len, tq, tk, lam_init):
    nq = s_len // tq
    n = b * s_len
    kern = functools.partial(_da_kernel, tq=tq, tk=tk, lam_init=lam_init)
    return pl.pallas_call(
        kern,
        grid=(b, DA_HEADS, nq),
        in_specs=[
            pl.BlockSpec(lam_p.shape, lambda bi, h, i: (0, 0)),
            pl.BlockSpec((LANES, tq), lambda bi, h, i: (QT_DA + h, bi * nq + i)),
            pl.BlockSpec((s_len, LANES), lambda bi, h, i: (bi, PR_DAK + h)),
            pl.BlockSpec((s_len // tk, LANES, tk), lambda bi, h, i: (bi, VT_DA + h, 0)),
            pl.BlockSpec(subln_g.shape, lambda bi, h, i: (0, 0)),
        ],
        out_specs=pl.BlockSpec((tq, LANES), lambda bi, h, i: (bi * nq + i, h)),
        out_shape=jax.ShapeDtypeStruct((n, DA_WIDTH), BF16),
        compiler_params=pltpu.CompilerParams(dimension_semantics=("arbitrary",) * 3, vmem_limit_bytes=VMEM_LIMIT),
        name="diff_attn",
    )(lam_p, qt, pr, vt3, subln_g)


def _dsa_kernel(qit_ref, ki_ref, wt_ref, qt_ref, k0_ref, k1_ref, vt_ref, o_ref,
                keys_scr, qi_scr, q_scr, acc_scr, ml_scr, cut_scr, *, tq, n_sel, n_bits):
    tk = tq
    i = pl.program_id(1)
    nck = i + 1
    kpos0 = lax.broadcasted_iota(I32, (tk, tq), 0)
    qpos = i * tq + lax.broadcasted_iota(I32, (tk, tq), 1)
    frow = lax.broadcasted_iota(I32, (LANES, tq), 0)
    per_grp = DS_HEADS // DS_KV_HEADS

    scale = jnp.asarray(DS_HEAD_DIM ** -0.5, BF16)
    for h in range(DS_HEADS):
        keep = (frow < DS_HEAD_DIM) if h % 2 == 0 else (frow >= DS_HEAD_DIM)
        rows = slice((h // 2) * LANES, (h // 2 + 1) * LANES)
        blk = qit_ref[rows, :]
        qi_scr[h] = jnp.where(keep, blk, jnp.zeros_like(blk))
        blk = qt_ref[rows, :] * scale
        q_scr[h] = jnp.where(keep, blk, jnp.zeros_like(blk))

    def score_chunk(c, carry):
        kc = ki_ref[pl.ds(pl.multiple_of(c * tk, tk), tk), :]
        acc = jnp.zeros((tk, tq), F32)
        for h in range(IDX_HEADS):
            s = jnp.dot(kc, qi_scr[h], preferred_element_type=F32)
            acc = acc + wt_ref[h:h + 1, :] * jnp.maximum(s, 0.0)
        bits = pltpu.bitcast(acc, I32)
        key = bits ^ ((bits >> 31) & 0x7FFFFFFF)
        key = jnp.where(acc == 0.0, 0, key)
        keys_scr[c] = jnp.where(kpos0 + c * tk <= qpos, key, INT_MIN)
        return carry

    lax.fori_loop(0, nck, score_chunk, 0)

    def count(pred):
        def body(c, acc):
            hit = pred(keys_scr[c], kpos0 + c * tk)
            return acc + jnp.sum(hit.reshape(tk // SUBLANES, SUBLANES, tq), axis=0)
        acc = lax.fori_loop(0, nck, body, jnp.zeros((SUBLANES, tq), I32))
        return jnp.sum(acc, axis=0, keepdims=True)

    def thr_bit(t, ans_u):
        cand_u = ans_u | jnp.left_shift(jnp.int32(1), 31 - t)
        cand = cand_u ^ INT_MIN
        cnt = count(lambda k, kpos: jnp.where(k >= cand, 1, 0))
        return jnp.where(cnt >= n_sel, cand_u, ans_u)

    thr = lax.fori_loop(0, 32, thr_bit, jnp.zeros((1, tq), I32)) ^ INT_MIN
    n_ge = count(lambda k, kpos: jnp.where(k >= thr, 1, 0))
    below_all = thr == INT_MIN
    cut_scr[...] = jnp.where(below_all, -1, (1 << n_bits) - 1)
    excess = jnp.max(jnp.where(below_all, 0, n_ge - n_sel))

    @pl.when(excess > 0)
    def _():
        n_gt = count(lambda k, kpos: jnp.where(k > thr, 1, 0))
        need = n_sel - n_gt

        def tie_bit(t, cut):
            cand = cut | jnp.left_shift(jnp.int32(1), n_bits - 1 - t)
            cnt = count(lambda k, kpos: jnp.where(k == thr, jnp.where(kpos < cand, 1, 0), 0))
            return jnp.where(cnt < need, cand, cut)

        cut = lax.fori_loop(0, n_bits, tie_bit, jnp.zeros((1, tq), I32))
        cut_scr[...] = jnp.where(below_all, -1, cut)

    cut = cut_scr[...]

    def bias_chunk(c, carry):
        k = keys_scr[c]
        sel = (k > thr) | ((k == thr) & (kpos0 + c * tk <= cut))
        keys_scr[c] = pltpu.bitcast(jnp.where(sel, 0.0, NEG_BIG).astype(F32), I32)
        return carry

    lax.fori_loop(0, nck, bias_chunk, 0)

    acc_scr[...] = jnp.zeros(acc_scr.shape, F32)
    ml_scr[0:DS_HEADS, :] = jnp.full((DS_HEADS, tq), NEG_BIG, F32)
    ml_scr[DS_HEADS:, :] = jnp.zeros((DS_HEADS, tq), F32)

    def att_chunk(c, carry):
        start = pl.multiple_of(c * tk, tk)
        bias = pltpu.bitcast(keys_scr[c], F32)
        vt = vt_ref[c]
        for h in range(DS_HEADS):
            grp = h // per_grp
            k_ref = k0_ref if grp == 0 else k1_ref
            s = jnp.dot(k_ref[pl.ds(start, tk), :], q_scr[h], preferred_element_type=F32) + bias
            m, l, alpha, p = _online_softmax_step(s, ml_scr[h:h + 1, :], ml_scr[DS_HEADS + h:DS_HEADS + h + 1, :])
            ml_scr[h:h + 1, :] = m
            ml_scr[DS_HEADS + h:DS_HEADS + h + 1, :] = l
            vg = vt[grp * DS_HEAD_DIM:(grp + 1) * DS_HEAD_DIM, :]
            acc_scr[h] = alpha * acc_scr[h] + jnp.dot(vg, p.astype(vg.dtype), preferred_element_type=F32)
        return carry

    lax.fori_loop(0, nck, att_chunk, 0)
    outs = [acc_scr[h] / ml_scr[DS_HEADS + h:DS_HEADS + h + 1, :] for h in range(DS_HEADS)]
    o_ref[...] = jnp.concatenate(outs, axis=0).T.astype(o_ref.dtype)


def _dsa_attention(qt, pr, vt3, iwt, b, s_len, tq):
    nq = s_len // tq
    n = b * s_len
    n_sel = min(TOPK_MAX, s_len // 4)
    n_bits = max(1, (s_len - 1).bit_length())
    kern = functools.partial(_dsa_kernel, tq=tq, n_sel=n_sel, n_bits=n_bits)
    seq = lambda blk: pl.BlockSpec((s_len, LANES), lambda bi, i: (bi, blk))
    qblk = lambda blk: pl.BlockSpec((4 * LANES, tq), lambda bi, i: (blk // 4, bi * nq + i))
    return pl.pallas_call(
        kern,
        grid=(b, nq),
        in_specs=[
            qblk(QT_IX), seq(PR_IXK),
            pl.BlockSpec((IDX_HEADS, tq), lambda bi, i: (0, bi * nq + i)),
            qblk(QT_DS), seq(PR_DSK), seq(PR_DSK + 1),
            pl.BlockSpec((s_len // tq, LANES, tq), lambda bi, i: (bi, VT_DS, 0)),
        ],
        out_specs=pl.BlockSpec((tq, DS_WIDTH), lambda bi, i: (bi * nq + i, 0)),
        out_shape=jax.ShapeDtypeStruct((n, DS_WIDTH), BF16),
        scratch_shapes=[pltpu.VMEM((nq, tq, tq), I32),
                        pltpu.VMEM((IDX_HEADS, LANES, tq), BF16), pltpu.VMEM((DS_HEADS, LANES, tq), BF16),
                        pltpu.VMEM((DS_HEADS, DS_HEAD_DIM, tq), F32), pltpu.VMEM((2 * DS_HEADS, tq), F32),
                        pltpu.VMEM((1, tq), I32)],
        compiler_params=pltpu.CompilerParams(dimension_semantics=("arbitrary",) * 2, vmem_limit_bytes=VMEM_LIMIT),
        name="dsa_attn",
    )(qt, pr, iwt, qt, pr, pr, vt3)


def _merge_kernel(oa_ref, ob_ref, gate_ref, x_ref, wa_ref, wb_ref, wo_ref, o_ref):
    ya = jnp.dot(oa_ref[...], wa_ref[...], preferred_element_type=F32)
    yb = jnp.dot(ob_ref[...], wb_ref[...], preferred_element_type=F32)
    mix = gate_ref[:, :D_MODEL] * ya + gate_ref[:, D_MODEL:] * yb
    o_ref[...] = x_ref[...] + jnp.dot(mix.astype(BF16), wo_ref[...], preferred_element_type=F32)


def _merge(oa, ob, gate, x2, wa, wb, wo, tm):
    n = x2.shape[0]
    full = lambda a: pl.BlockSpec(a.shape, lambda i: (0, 0))
    rows = lambda w: pl.BlockSpec((tm, w), lambda i: (i, 0))
    return pl.pallas_call(
        _merge_kernel,
        grid=(n // tm,),
        in_specs=[rows(DA_WIDTH), rows(DS_WIDTH), rows(2 * D_MODEL), rows(D_MODEL), full(wa), full(wb), full(wo)],
        out_specs=rows(D_MODEL),
        out_shape=jax.ShapeDtypeStruct((n, D_MODEL), F32),
        compiler_params=pltpu.CompilerParams(dimension_semantics=("arbitrary",), vmem_limit_bytes=VMEM_LIMIT),
        name="merge",
    )(oa, ob, gate, x2, wa, wb, wo)


def _norm_matmul_kernel(x_ref, g_ref, w_ref, o_ref):
    h = _rms(x_ref[...], g_ref[...]).astype(BF16)
    o_ref[...] = jnp.dot(h, w_ref[...], preferred_element_type=F32).astype(o_ref.dtype)


def _norm_matmul(x2, g, w, tm):
    n = x2.shape[0]
    return pl.pallas_call(
        _norm_matmul_kernel,
        grid=(n // tm,),
        in_specs=[pl.BlockSpec((tm, x2.shape[1]), lambda i: (i, 0)), pl.BlockSpec(g.shape, lambda i: (0, 0)),
                  pl.BlockSpec(w.shape, lambda i: (0, 0))],
        out_specs=pl.BlockSpec((tm, w.shape[1]), lambda i: (i, 0)),
        out_shape=jax.ShapeDtypeStruct((n, w.shape[1]), BF16),
        compiler_params=pltpu.CompilerParams(dimension_semantics=("arbitrary",), vmem_limit_bytes=VMEM_LIMIT),
        name="mem_kv_proj",
    )(x2, g, w)


def _mem_attn_kernel(x_ref, g_ref, wq_ref, kv_ref, wo_ref, o_ref):
    x = x_ref[...]
    hn = _rms(x, g_ref[...]).astype(BF16)
    q = jnp.dot(hn, wq_ref[...], preferred_element_type=F32).astype(BF16)
    scale = MEM_HEAD_DIM ** -0.5
    heads = []
    for h in range(MEM_HEADS):
        qh = q[:, h * MEM_HEAD_DIM:(h + 1) * MEM_HEAD_DIM]
        kh = kv_ref[:, h * MEM_HEAD_DIM:(h + 1) * MEM_HEAD_DIM]
        vh = kv_ref[:, MEM_WIDTH + h * MEM_HEAD_DIM:MEM_WIDTH + (h + 1) * MEM_HEAD_DIM]
        s = lax.dot_general(qh, kh, NT_DIMS, preferred_element_type=F32) * scale
        m = jnp.max(s, axis=1, keepdims=True)
        p = jnp.exp(s - m)
        l = jnp.sum(p, axis=1, keepdims=True)
        heads.append(jnp.dot(p.astype(BF16), vh, preferred_element_type=F32) / l)
    o = jnp.concatenate(heads, axis=1).astype(BF16)
    o_ref[...] = x + jnp.dot(o, wo_ref[...], preferred_element_type=F32)


def _mem_attn(x2, g, wq, kv, wo, b, s_len, m_tok, tm):
    nb = s_len // tm
    n = x2.shape[0]
    full = lambda a: pl.BlockSpec(a.shape, lambda bi, i: (0, 0))
    return pl.pallas_call(
        _mem_attn_kernel,
        grid=(b, nb),
        in_specs=[pl.BlockSpec((tm, D_MODEL), lambda bi, i: (bi * nb + i, 0)), full(g), full(wq),
                  pl.BlockSpec((m_tok, 2 * MEM_WIDTH), lambda bi, i: (bi, 0)), full(wo)],
        out_specs=pl.BlockSpec((tm, D_MODEL), lambda bi, i: (bi * nb + i, 0)),
        out_shape=jax.ShapeDtypeStruct((n, D_MODEL), F32),
        compiler_params=pltpu.CompilerParams(dimension_semantics=("arbitrary",) * 2, vmem_limit_bytes=VMEM_LIMIT),
        name="mem_attn",
    )(x2, g, wq, kv, wo)


N_SUB = 2 * PEER_HEADS
EXPERT_BITS = 14


def _route_kernel(x_ref, g_ref, wq_ref, keys_ref, hf_ref, idx_ref, row_ref, gate_ref, sc_scr, ts_scr, ti_scr):
    hf = _rms(x_ref[...], g_ref[...])
    tm = hf.shape[0]
    hb = hf.astype(BF16)
    bits = pltpu.bitcast(hb.astype(F32), jnp.uint32)
    half_d = D_MODEL // 2
    for s in range(SUBLANES // 2):
        lo = bits[:, s * LANES:(s + 1) * LANES] >> 16
        hi = bits[:, half_d + s * LANES:half_d + (s + 1) * LANES] & jnp.uint32(0xFFFF0000)
        hf_ref[:, s, :] = lo | hi
        hf_ref[:, s + SUBLANES // 2, :] = lo | hi
    q = jnp.dot(hb, wq_ref[...], preferred_element_type=F32).astype(BF16)
    for g in range(N_SUB):
        blk = q[:, (g // 2) * LANES:(g // 2 + 1) * LANES]
        sc_scr[g] = lax.dot_general(keys_ref[g], blk, NT_DIMS, preferred_element_type=F32)

    key_id = lax.broadcasted_iota(I32, (PEER_N_KEYS, tm), 0)

    def sub_topk(g, carry):
        x = sc_scr[g]
        vals, ids = [], []
        for _ in range(PEER_TOPK):
            m = jnp.max(x, axis=0, keepdims=True)
            idx = jnp.min(jnp.where(x == m, key_id, PEER_N_KEYS), axis=0, keepdims=True)
            vals.append(m)
            ids.append(idx)
            x = jnp.where(key_id == idx, -jnp.inf, x)
        ts_scr[g] = jnp.concatenate(vals, axis=0)
        ti_scr[g] = jnp.concatenate(ids, axis=0)
        return carry

    lax.fori_loop(0, N_SUB, sub_topk, 0)

    n_cand = PEER_TOPK * PEER_TOPK
    pos = lax.broadcasted_iota(I32, (n_cand, tm), 0)

    def head_topk(h, carry):
        s0, s1 = ts_scr[2 * h], ts_scr[2 * h + 1]
        i0, i1 = ti_scr[2 * h], ti_scr[2 * h + 1]
        cand = jnp.concatenate([s0[a:a + 1] + s1 for a in range(PEER_TOPK)], axis=0)
        cid = jnp.concatenate([i0[a:a + 1] * PEER_N_KEYS + i1 for a in range(PEER_TOPK)], axis=0)
        code = (pos << EXPERT_BITS) | cid
        big = jnp.int32(2 ** 30)
        vals, ids = [], []
        for _ in range(PEER_TOPK):
            m = jnp.max(cand, axis=0, keepdims=True)
            best = jnp.min(jnp.where(cand == m, code, big), axis=0, keepdims=True)
            vals.append(m)
            ids.append(best & (2 ** EXPERT_BITS - 1))
            cand = jnp.where(code == best, -jnp.inf, cand)
        best_s = jnp.concatenate(vals, axis=0)
        e = jnp.exp(best_s - best_s[0:1])
        gate_ref[h] = e / jnp.sum(e, axis=0, keepdims=True)
        best_i = jnp.concatenate(ids, axis=0)
        idx_ref[h] = best_i
        row_ref[h] = (best_i >> 1) * SUBLANES
        return carry

    lax.fori_loop(0, PEER_HEADS, head_topk, 0)


def _route(x2, g, wq, keys_p, tm):
    n = x2.shape[0]
    return pl.pallas_call(
        _route_kernel,
        grid=(n // tm,),
        in_specs=[pl.BlockSpec((tm, D_MODEL), lambda i: (i, 0)), pl.BlockSpec(g.shape, lambda i: (0, 0)),
                  pl.BlockSpec(wq.shape, lambda i: (0, 0)), pl.BlockSpec(keys_p.shape, lambda i: (0, 0, 0))],
        out_specs=[pl.BlockSpec((tm, SUBLANES, LANES), lambda i: (i, 0, 0))]
        + [pl.BlockSpec((PEER_HEADS, PEER_TOPK, tm), lambda i: (0, 0, i))] * 3,
        out_shape=[jax.ShapeDtypeStruct((n, SUBLANES, LANES), jnp.uint32),
                   jax.ShapeDtypeStruct((PEER_HEADS, PEER_TOPK, n), I32),
                   jax.ShapeDtypeStruct((PEER_HEADS, PEER_TOPK, n), I32),
                   jax.ShapeDtypeStruct((PEER_HEADS, PEER_TOPK, n), F32)],
        scratch_shapes=[pltpu.VMEM((N_SUB, PEER_N_KEYS, tm), F32), pltpu.VMEM((N_SUB, PEER_TOPK, tm), F32),
                        pltpu.VMEM((N_SUB, PEER_TOPK, tm), I32)],
        compiler_params=pltpu.CompilerParams(dimension_semantics=("arbitrary",), vmem_limit_bytes=VMEM_LIMIT),
        name="peer_route",
    )(x2, g, wq, keys_p)


TILE_ROWS = 2 * SUBLANES
PAIRS_PER_KTILE = 16
TOK_UNROLL = 4


def _pack_table(t):
    e = t.shape[0]
    bits = lax.bitcast_convert_type(t.astype(BF16), jnp.uint16).astype(jnp.uint32)
    words = bits[:, :D_MODEL // 2] | (bits[:, D_MODEL // 2:] << 16)
    return words.reshape(e * (SUBLANES // 2), LANES)


def _gather_tiles(row_ref, tab_ref, t):
    tiles = []
    for k in range(PEER_SEL):
        start = pl.multiple_of(row_ref[t, k], SUBLANES)
        tiles.append(pltpu.bitcast(tab_ref[pl.ds(start, SUBLANES), :], BF16))
    return tiles


def _peer_dot_kernel(row_ref, tab_ref, h_ref, gate_ref, idxv_ref, sel_ref, ce_ref, co_ref,
                     ye_scr, yo_scr, ze_scr, zo_scr, *, tt):
    ones = jnp.ones((SUBLANES, LANES), BF16)
    half_rows = PAIRS_PER_KTILE
    n_trip = tt // TOK_UNROLL

    def row_sums(i, slot):
        for u in range(TOK_UNROLL):
            t = i * TOK_UNROLL + u
            hp = pltpu.bitcast(h_ref[t], BF16)
            prods = [tile * hp for tile in _gather_tiles(row_ref, tab_ref, t)]
            for kt in range(PEER_SEL // PAIRS_PER_KTILE):
                stack = jnp.concatenate(prods[kt * PAIRS_PER_KTILE:(kt + 1) * PAIRS_PER_KTILE], axis=0)
                y = jnp.dot(sel_ref[...], stack, preferred_element_type=F32)
                r0 = u * PEER_SEL + kt * half_rows
                ye_scr[slot, r0:r0 + half_rows, :] = y[:half_rows].astype(BF16)
                yo_scr[slot, r0:r0 + half_rows, :] = y[half_rows:].astype(BF16)

    def lane_sums(i, slot):
        ze = lax.dot_general(ones, ye_scr[slot], NT_DIMS, preferred_element_type=F32)
        zo = lax.dot_general(ones, yo_scr[slot], NT_DIMS, preferred_element_type=F32)
        for u in range(TOK_UNROLL):
            t = i * TOK_UNROLL + u
            ze_scr[pl.ds(t, 1), :] = ze[0:1, u * PEER_SEL:(u + 1) * PEER_SEL]
            zo_scr[pl.ds(t, 1), :] = zo[0:1, u * PEER_SEL:(u + 1) * PEER_SEL]

    def trip(i, carry):
        slot = i & 1
        lane_sums(i - 1, 1 - slot)
        row_sums(i, slot)
        return carry

    row_sums(0, 0)
    lax.fori_loop(1, n_trip, trip, 0)
    lane_sums(n_trip - 1, (n_trip - 1) & 1)
    even = (idxv_ref[...] & 1) == 0
    a = jnp.where(even, ze_scr[...], zo_scr[...])
    c = 0.5 * a * (1.0 + lax.erf(a * (2.0 ** -0.5))) * gate_ref[...]
    ce_ref[...] = jnp.where(even, c, 0.0)
    co_ref[...] = jnp.where(even, 0.0, c)


def _peer_dot(rows2, tab, hpk, gate, idx2, sel, tt):
    n = hpk.shape[0]
    kern = functools.partial(_peer_dot_kernel, tt=tt)
    rows = pl.BlockSpec((tt, PEER_SEL), lambda i: (i, 0))
    return pl.pallas_call(
        kern,
        grid=(n // tt,),
        in_specs=[pl.BlockSpec((tt, PEER_SEL), lambda i: (i, 0), memory_space=pltpu.SMEM),
                  pl.BlockSpec(memory_space=pltpu.VMEM),
                  pl.BlockSpec((tt, SUBLANES, LANES), lambda i: (i, 0, 0)),
                  rows, rows, pl.BlockSpec(sel.shape, lambda i: (0, 0))],
        out_specs=[rows, rows],
        out_shape=[jax.ShapeDtypeStruct((n, PEER_SEL), F32), jax.ShapeDtypeStruct((n, PEER_SEL), F32)],
        scratch_shapes=[pltpu.VMEM((2, TOK_UNROLL * PEER_SEL, LANES), BF16)] * 2
        + [pltpu.VMEM((tt, PEER_SEL), F32)] * 2,
        compiler_params=pltpu.CompilerParams(dimension_semantics=("arbitrary",), vmem_limit_bytes=VMEM_LIMIT),
        name="peer_dot",
    )(rows2, tab, hpk, gate, idx2, sel)


def _peer_sum_kernel(row_ref, ce_ref, co_ref, spread_ref, tab_ref, o_ref, m1_scr, m2_scr, *, tt):
    width = PEER_SEL * TILE_ROWS
    cc = jnp.concatenate([ce_ref[...], co_ref[...]], axis=1)
    c1 = cc.astype(BF16)
    c2 = (cc - c1.astype(F32)).astype(BF16)
    m1_scr[...] = jnp.dot(c1, spread_ref[...], preferred_element_type=F32)
    m2_scr[...] = jnp.dot(c2, spread_ref[...], preferred_element_type=F32)
    row = lax.broadcasted_iota(I32, (SUBLANES, width), 0)
    lane = lax.broadcasted_iota(I32, (SUBLANES, width), 1)
    half = SUBLANES // 2
    on_row = (lane & (SUBLANES - 1)) == 2 * (row % half) + row // half

    def token(t):
        w = jnp.concatenate(_gather_tiles(row_ref, tab_ref, t), axis=0)
        lhs = []
        for m_scr in (m1_scr, m2_scr):
            coef = jnp.broadcast_to(m_scr[pl.ds(t, 1), :], (SUBLANES, width))
            lhs.append(jnp.where(on_row, coef, 0.0).astype(BF16))
        res = jnp.dot(jnp.concatenate(lhs, axis=0), w, preferred_element_type=F32)
        o_ref[t] = res[:SUBLANES] + res[SUBLANES:]

    def trip(i, carry):
        for u in range(TOK_UNROLL):
            token(i * TOK_UNROLL + u)
        return carry

    lax.fori_loop(0, tt // TOK_UNROLL, trip, 0)


def _peer_sum(rows2, ce, co, spread, tab, tt):
    n = ce.shape[0]
    kern = functools.partial(_peer_sum_kernel, tt=tt)
    rows = pl.BlockSpec((tt, PEER_SEL), lambda i: (i, 0))
    width = PEER_SEL * TILE_ROWS
    return pl.pallas_call(
        kern,
        grid=(n // tt,),
        in_specs=[pl.BlockSpec((tt, PEER_SEL), lambda i: (i, 0), memory_space=pltpu.SMEM), rows, rows,
                  pl.BlockSpec(spread.shape, lambda i: (0, 0)), pl.BlockSpec(memory_space=pltpu.VMEM)],
        out_specs=pl.BlockSpec((tt, SUBLANES, LANES), lambda i: (i, 0, 0)),
        out_shape=jax.ShapeDtypeStruct((n, SUBLANES, LANES), F32),
        scratch_shapes=[pltpu.VMEM((tt, width), F32), pltpu.VMEM((tt, width), F32)],
        compiler_params=pltpu.CompilerParams(dimension_semantics=("arbitrary",), vmem_limit_bytes=VMEM_LIMIT),
        name="peer_sum",
    )(rows2, ce, co, spread, tab)


def _final_kernel(x_ref, p_ref, g_ref, o_ref):
    peer = jnp.concatenate([p_ref[:, s, :] for s in range(SUBLANES)], axis=1)
    o_ref[...] = _rms(x_ref[...] + peer, g_ref[...])


def _final(x2, peer3, g, tm):
    n = x2.shape[0]
    return pl.pallas_call(
        _final_kernel,
        grid=(n // tm,),
        in_specs=[pl.BlockSpec((tm, D_MODEL), lambda i: (i, 0)),
                  pl.BlockSpec((tm, SUBLANES, LANES), lambda i: (i, 0, 0)),
                  pl.BlockSpec(g.shape, lambda i: (0, 0))],
        out_specs=pl.BlockSpec((tm, D_MODEL), lambda i: (i, 0)),
        out_shape=jax.ShapeDtypeStruct((n, D_MODEL), F32),
        compiler_params=pltpu.CompilerParams(dimension_semantics=("arbitrary",), vmem_limit_bytes=VMEM_LIMIT),
        name="final_norm",
    )(x2, peer3, g)


def _tile(n, pref):
    t = pref
    while n % t:
        t //= 2
    return t


def _layer(l, x2, mem2, pos, invf, b, s_len, m_tok, norm_mix_g, w_in, da_lambda, da_subln_g, w_branch_a,
           w_branch_b, gate_bias, w_out, norm_mem_g, mem_kv_norm_g, w_mem_q, w_mem_kv, w_mem_o, norm_ffn_g,
           peer_w_q, peer_sub_keys, peer_u, peer_v):
    n = b * s_len
    row2 = lambda v: v.reshape(1, -1)
    splits = [0]
    for c in COL_SIZES:
        splits.append(splits[-1] + c)
    da_q, da_k, da_v, ds_q, ds_k, ds_v, ix_q, ix_k, ix_w, gates = (
        w_in[l][:, splits[j]:splits[j + 1]] for j in range(len(COL_SIZES)))
    k0, k1 = ds_k[:, :DS_HEAD_DIM], ds_k[:, DS_HEAD_DIM:]
    w_keys = jnp.concatenate([da_k, k0, k0, k1, k1, ix_k, ix_k], axis=1).astype(BF16)
    w_q_t = jnp.concatenate([da_q, ds_q, ix_q], axis=1).T.astype(BF16)
    w_v_t = jnp.concatenate([da_v, ds_v], axis=1).T.astype(BF16)
    t_att = _tile(s_len, 256)
    pr, gate, qt, vt3, iwt = _in_proj(x2, row2(norm_mix_g[l]), pos, invf, pos.reshape(1, n),
                                      invf[:, :ROPE_HALF].reshape(ROPE_HALF, 1), row2(gate_bias[l]), w_keys,
                                      gates.astype(BF16), w_q_t, w_v_t, ix_w.T.astype(BF16), t_att)

    lam_init = 0.8 - 0.6 * math.exp(-0.3 * l)
    o_a = _diff_attention(da_lambda[l], qt, pr, vt3, da_subln_g[l].reshape(-1, 1), b, s_len, t_att, t_att, lam_init)
    o_b = _dsa_attention(qt, pr, vt3, iwt, b, s_len, t_att)
    x2 = _merge(o_a, o_b, gate, x2, w_branch_a[l].astype(BF16), w_branch_b[l].astype(BF16),
                w_out[l].astype(BF16), _tile(n, 256))

    kv = _norm_matmul(mem2, row2(mem_kv_norm_g[l]), w_mem_kv[l].astype(BF16), _tile(mem2.shape[0], 256))
    x2 = _mem_attn(x2, row2(norm_mem_g[l]), w_mem_q[l].astype(BF16), kv, w_mem_o[l].astype(BF16),
                   b, s_len, m_tok, _tile(s_len, 256))

    sk = peer_sub_keys[l].reshape(N_SUB, PEER_N_KEYS, PEER_HALF)
    z = jnp.zeros_like(sk)
    keys_p = jnp.where((jnp.arange(N_SUB) % 2 == 0)[:, None, None],
                       jnp.concatenate([sk, z], axis=2), jnp.concatenate([z, sk], axis=2)).astype(BF16)
    hpk, idx_t, row_t, gate_t = _route(x2, row2(norm_ffn_g[l]), peer_w_q[l].astype(BF16), keys_p, _tile(n, 256))
    idx2 = idx_t.reshape(PEER_SEL, n).T
    rows2 = row_t.reshape(PEER_SEL, n).T
    gate2 = gate_t.reshape(PEER_SEL, n).T
    tt = _tile(n, 128)
    r32 = jnp.arange(2 * PAIRS_PER_KTILE)[:, None]
    c256 = jnp.arange(PAIRS_PER_KTILE * TILE_ROWS)[None, :]
    sel = ((c256 // TILE_ROWS == r32 % PAIRS_PER_KTILE)
           & ((c256 % TILE_ROWS) // SUBLANES == r32 // PAIRS_PER_KTILE)).astype(BF16)
    k256 = jnp.arange(2 * PEER_SEL)[:, None]
    c2048 = jnp.arange(PEER_SEL * TILE_ROWS)[None, :]
    spread = ((c2048 // TILE_ROWS == k256 % PEER_SEL)
              & ((c2048 % TILE_ROWS) // SUBLANES == k256 // PEER_SEL)).astype(BF16)
    ce2, co2 = _peer_dot(rows2, _pack_table(peer_u[l]), hpk, gate2, idx2, sel, tt)
    peer3 = _peer_sum(rows2, ce2, co2, spread, _pack_table(peer_v[l]), tt)
    return x2, peer3


def kernel(x, mem, positions, norm_mix_g, w_in, da_lambda, da_subln_g, w_branch_a, w_branch_b, gate_bias, w_out, norm_mem_g, mem_kv_norm_g, w_mem_q, w_mem_kv, w_mem_o, norm_ffn_g, peer_w_q, peer_sub_keys, peer_u, peer_v, final_norm_g):
    b, s_len, d = x.shape
    m_tok = mem.shape[1]
    n = b * s_len
    depth = w_in.shape[0]
    x2 = x.reshape(n, d)
    mem2 = mem.reshape(b * m_tok, d)
    pos = positions.astype(F32).reshape(n, 1)
    inv_freq = ROPE_THETA ** (-(jnp.arange(ROPE_HALF, dtype=F32) * 2.0) / ROPE_DIM)
    invf = jnp.tile(inv_freq, LANES // ROPE_HALF).reshape(1, LANES)
    peer3 = None
    for l in range(depth):
        if peer3 is not None:
            x2 = x2 + peer3.reshape(n, d)
        x2, peer3 = _layer(l, x2, mem2, pos, invf, b, s_len, m_tok, norm_mix_g, w_in, da_lambda, da_subln_g,
                           w_branch_a, w_branch_b, gate_bias, w_out, norm_mem_g, mem_kv_norm_g, w_mem_q, w_mem_kv,
                           w_mem_o, norm_ffn_g, peer_w_q, peer_sub_keys, peer_u, peer_v)
    out = _final(x2, peer3, final_norm_g.reshape(1, d), _tile(n, 256))
    return out.reshape(b, s_len, d)
```

```python
import functools
import math

import jax
import jax.numpy as jnp
from jax import lax
from jax.experimental import pallas as pl
from jax.experimental.pallas import tpu as pltpu

F32 = jnp.float32
BF16 = jnp.bfloat16
I32 = jnp.int32

D_MODEL = 1024
EPS = 1e-6
ROPE_THETA = 500000.0
ROT_HEAD_DIM = 64
ROPE_DIM = 16
ROPE_HALF = ROPE_DIM // 2

DA_HEADS = 4
DA_QK_DIM = 64
DA_V_DIM = 128
DA_WIDTH = 512
DS_HEADS = 8
DS_KV_HEADS = 2
DS_HEAD_DIM = 64
DS_WIDTH = 512
IDX_HEADS = 8
IDX_DIM = 64
TOPK_MAX = 256
MEM_HEADS = 4
MEM_HEAD_DIM = 128
MEM_WIDTH = 512
PEER_HEADS = 8
PEER_N_KEYS = 128
PEER_HALF = 64
PEER_TOPK = 16
PEER_SEL = PEER_HEADS * PEER_TOPK

COL_SIZES = (512, 512, 512, 512, 128, 128, 512, 64, 8, 2048)

LANES = 128
SUBLANES = 8
NEG_BIG = -1e30
INT_MIN = -(2 ** 31)
VMEM_LIMIT = 56 * 1024 * 1024

NT_DIMS = (((1,), (1,)), ((), ()))


def _rms(x, g):
    var = jnp.mean(x * x, axis=-1, keepdims=True)
    return x * lax.rsqrt(var + EPS) * g


def _rope_rows(y, cos8, sin8):
    pieces = []
    for r0 in range(0, y.shape[0], ROT_HEAD_DIM):
        t1 = y[r0:r0 + ROPE_HALF]
        t2 = y[r0 + ROPE_HALF:r0 + ROPE_DIM]
        pieces += [t1 * cos8 - t2 * sin8, t2 * cos8 + t1 * sin8, y[r0 + ROPE_DIM:r0 + ROT_HEAD_DIM]]
    return jnp.concatenate(pieces, axis=0)


def _in_proj_kernel(x_ref, g_ref, pos_ref, invf_ref, pos_t_ref, invf8_ref, bias_ref, wr_ref, wg_ref,
                    wqt_ref, wvt_ref, wwt_ref, pr_ref, gate_ref, qt_ref, vt_ref, iwt_ref):
    h = _rms(x_ref[...], g_ref[...]).astype(BF16)
    tm = h.shape[0]
    ang_t = invf8_ref[...] * pos_t_ref[...]
    cos8 = jnp.cos(ang_t)
    sin8 = jnp.sin(ang_t)
    row_chunk = 2 * LANES
    for r0 in range(0, wqt_ref.shape[0], row_chunk):
        y = lax.dot_general(wqt_ref[r0:r0 + row_chunk, :], h, NT_DIMS, preferred_element_type=F32)
        qt_ref[r0:r0 + row_chunk, :] = _rope_rows(y, cos8, sin8).astype(qt_ref.dtype)
    for r0 in range(0, wvt_ref.shape[0], row_chunk):
        r1 = min(r0 + row_chunk, wvt_ref.shape[0])
        y = lax.dot_general(wvt_ref[r0:r1, :], h, NT_DIMS, preferred_element_type=F32)
        vt_ref[0, r0:r1, :] = y.astype(vt_ref.dtype)
    iwt_ref[...] = lax.dot_general(wwt_ref[...], h, NT_DIMS, preferred_element_type=F32)
    ang = pos_ref[...] * invf_ref[...]
    cos = jnp.cos(ang)
    sin = jnp.sin(ang)
    lane = lax.broadcasted_iota(I32, (tm, LANES), 1) % ROT_HEAD_DIM
    c_t = jnp.where(lane < ROPE_DIM, cos, 1.0)
    s_lo = jnp.where(lane < ROPE_HALF, -sin, 0.0)
    s_hi = jnp.where((lane >= ROPE_HALF) & (lane < ROPE_DIM), sin, 0.0)
    n_r = wr_ref.shape[1]
    for c0 in range(0, n_r, 512):
        c1 = min(c0 + 512, n_r)
        w = c1 - c0
        y = jnp.dot(h, wr_ref[:, c0:c1], preferred_element_type=F32)
        reps = w // LANES
        ct = jnp.concatenate([c_t] * reps, axis=1)
        sl = jnp.concatenate([s_lo] * reps, axis=1)
        sh = jnp.concatenate([s_hi] * reps, axis=1)
        y = y * ct + pltpu.roll(y, w - ROPE_HALF, 1) * sl + pltpu.roll(y, ROPE_HALF, 1) * sh
        pr_ref[:, c0:c1] = y.astype(pr_ref.dtype)
    n_g = wg_ref.shape[1]
    for c0 in range(0, n_g, 512):
        y = jnp.dot(h, wg_ref[:, c0:c0 + 512], preferred_element_type=F32)
        gate_ref[:, c0:c0 + 512] = jax.nn.sigmoid(y + bias_ref[:, c0:c0 + 512])


def _in_proj(x2, g, pos, invf, pos_t, invf8, bias, wr, wg, wqt, wvt, wwt, tm):
    n = x2.shape[0]
    full = lambda a: pl.BlockSpec(a.shape, lambda i: (0, 0))
    rows = lambda w: pl.BlockSpec((tm, w), lambda i: (i, 0))
    cols = lambda r: pl.BlockSpec((r, tm), lambda i: (0, i))
    return pl.pallas_call(
        _in_proj_kernel,
        grid=(n // tm,),
        in_specs=[rows(D_MODEL), full(g), rows(1), full(invf), cols(1), full(invf8), full(bias), full(wr), full(wg),
                  full(wqt), full(wvt), full(wwt)],
        out_specs=[rows(wr.shape[1]), rows(wg.shape[1]), cols(wqt.shape[0]),
                   pl.BlockSpec((1, wvt.shape[0], tm), lambda i: (i, 0, 0)), cols(wwt.shape[0])],
        out_shape=[jax.ShapeDtypeStruct((n, wr.shape[1]), BF16), jax.ShapeDtypeStruct((n, wg.shape[1]), F32),
                   jax.ShapeDtypeStruct((wqt.shape[0], n), BF16),
                   jax.ShapeDtypeStruct((n // tm, wvt.shape[0], tm), BF16),
                   jax.ShapeDtypeStruct((wwt.shape[0], n), F32)],
        compiler_params=pltpu.CompilerParams(dimension_semantics=("arbitrary",), vmem_limit_bytes=VMEM_LIMIT),
        name="in_proj",
    )(x2, g, pos, invf, pos_t, invf8, bias, wr, wg, wqt, wvt, wwt)


QT_DA = 0
QT_DS = 4
QT_IX = 8
VT_DA = 0
VT_DS = 4
PR_DAK = 0
PR_DSK = 4
PR_IXK = 6


SUB_KEYS = 128
QK_AHEAD = 2


def _zero_after(x):
    bits = pltpu.bitcast(x, jnp.uint32)
    return pltpu.bitcast((bits >> 16) >> 16, F32)


def _online_softmax_step(s, m, l):
    m_new = jnp.maximum(m, jnp.max(s, axis=0, keepdims=True))
    alpha = jnp.exp(m - m_new)
    p = jnp.exp(s - m_new)
    return m_new, alpha * l + jnp.sum(p, axis=0, keepdims=True), alpha, p


def _da_kernel(lam_ref, qt_ref, k_ref, vt_ref, g_ref, o_ref, *, tq, tk, lam_init):
    i = pl.program_id(2)
    qt = qt_ref[...] * jnp.asarray(DA_QK_DIM ** -0.5, qt_ref.dtype)
    frow = lax.broadcasted_iota(I32, qt.shape, 0)
    zero = jnp.zeros_like(qt)
    q_maps = (jnp.where(frow < DA_QK_DIM, qt, zero), jnp.where(frow >= DA_QK_DIM, qt, zero))
    kpos0 = lax.broadcasted_iota(I32, (SUB_KEYS, tq), 0)
    qpos = i * tq + lax.broadcasted_iota(I32, (SUB_KEYS, tq), 1)

    def step(j, carry, masked):
        start = pl.multiple_of(j * tk, tk)
        out = []
        for mp in range(2):
            m, l, a = carry[3 * mp:3 * mp + 3]
            subs = []
            for r0 in range(0, tk, SUB_KEYS):
                k = k_ref[pl.ds(start + r0, SUB_KEYS), :]
                s = jnp.dot(k, q_maps[mp], preferred_element_type=F32)
                if masked:
                    s = jnp.where(kpos0 + (j * tk + r0) <= qpos, s, NEG_BIG)
                subs.append(s)
            m_new = m
            for s in subs:
                m_new = jnp.maximum(m_new, jnp.max(s, axis=0, keepdims=True))
            alpha = jnp.exp(m - m_new)
            l = alpha * l
            a = alpha * a
            for r0, s in zip(range(0, tk, SUB_KEYS), subs):
                p = jnp.exp(s - m_new)
                l = l + jnp.sum(p, axis=0, keepdims=True)
                vt = vt_ref[j, :, r0:r0 + SUB_KEYS]
                a = a + jnp.dot(vt, p.astype(vt.dtype), preferred_element_type=F32)
            out += [m_new, l, a]
        return tuple(out)

    init = []
    for _ in range(2):
        init += [jnp.full((1, tq), NEG_BIG, F32), jnp.zeros((1, tq), F32), jnp.zeros((DA_V_DIM, tq), F32)]
    n_full = (i * tq + 1) // tk
    n_tot = ((i + 1) * tq + tk - 1) // tk
    carry = lax.fori_loop(0, n_full, lambda j, c: step(j, c, False), tuple(init))
    carry = lax.fori_loop(n_full, n_tot, lambda j, c: step(j, c, True), carry)
    _, l1, a1, _, l2, a2 = carry

    lp = lam_ref[...]
    lam = (jnp.exp(jnp.sum(lp[0:1] * lp[1:2], axis=1, keepdims=True))
           - jnp.exp(jnp.sum(lp[2:3] * lp[3:4], axis=1, keepdims=True)) + lam_init)
    o = a1 / l1 - lam * (a2 / l2)
    var = jnp.mean(o * o, axis=0, keepdims=True)
    o = o * lax.rsqrt(var + EPS) * g_ref[...] * (1.0 - lam_init)
    o_ref[...] = o.T.astype(o_ref.dtype)


def _diff_attention(lam_p, qt, pr, vt3, subln_g, b, s_len, tq, tk, lam_init):
    nq = s_len // tq
    n = b * s_len
    kern = functools.partial(_da_kernel, tq=tq, tk=tk, lam_init=lam_init)
    return pl.pallas_call(
        kern,
        grid=(b, DA_HEADS, nq),
        in_specs=[
            pl.BlockSpec(lam_p.shape, lambda bi, h, i: (0, 0)),
            pl.BlockSpec((LANES, tq), lambda bi, h, i: (QT_DA + h, bi * nq + i)),
            pl.BlockSpec((s_len, LANES), lambda bi, h, i: (bi, PR_DAK + h)),
            pl.BlockSpec((s_len // tk, LANES, tk), lambda bi, h, i: (bi, VT_DA + h, 0)),
            pl.BlockSpec(subln_g.shape, lambda bi, h, i: (0, 0)),
        ],
        out_specs=pl.BlockSpec((tq, LANES), lambda bi, h, i: (bi * nq + i, h)),
        out_shape=jax.ShapeDtypeStruct((n, DA_WIDTH), BF16),
        compiler_params=pltpu.CompilerParams(dimension_semantics=("arbitrary",) * 3, vmem_limit_bytes=VMEM_LIMIT),
        name="diff_attn",
    )(lam_p, qt, pr, vt3, subln_g)


def _dsa_kernel(qit_ref, ki_ref, wt_ref, qt_ref, k0_ref, k1_ref, vt_ref, o_ref,
                keys_scr, qi_scr, q_scr, acc_scr, m_scr, l_scr, cut_scr, *, tq, n_sel, n_bits):
    tk = tq
    i = pl.program_id(1)
    nck = i + 1
    kpos0 = lax.broadcasted_iota(I32, (tk, tq), 0)
    qpos = i * tq + lax.broadcasted_iota(I32, (tk, tq), 1)
    frow = lax.broadcasted_iota(I32, (LANES, tq), 0)
    per_grp = DS_HEADS // DS_KV_HEADS

    scale = jnp.asarray(DS_HEAD_DIM ** -0.5, BF16)
    for h in range(DS_HEADS):
        keep = (frow < DS_HEAD_DIM) if h % 2 == 0 else (frow >= DS_HEAD_DIM)
        rows = slice((h // 2) * LANES, (h // 2 + 1) * LANES)
        blk = qit_ref[rows, :]
        qi_scr[h] = jnp.where(keep, blk, jnp.zeros_like(blk))
        blk = qt_ref[rows, :] * scale
        q_scr[h] = jnp.where(keep, blk, jnp.zeros_like(blk))

    def score_chunk(c, carry):
        kc = ki_ref[pl.ds(pl.multiple_of(c * tk, tk), tk), :]
        acc = jnp.zeros((tk, tq), F32)
        for h in range(IDX_HEADS):
            s = jnp.dot(kc, qi_scr[h], preferred_element_type=F32)
            acc = acc + wt_ref[h:h + 1, :] * jnp.maximum(s, 0.0)
        bits = pltpu.bitcast(acc, I32)
        key = bits ^ ((bits >> 31) & 0x7FFFFFFF)
        key = jnp.where(acc == 0.0, 0, key)
        keys_scr[c] = jnp.where(kpos0 + c * tk <= qpos, key, INT_MIN)
        return carry

    lax.fori_loop(0, nck, score_chunk, 0)

    def count(pred):
        def body(c, acc):
            hit = pred(keys_scr[c], kpos0 + c * tk)
            return acc + jnp.sum(hit.reshape(tk // SUBLANES, SUBLANES, tq), axis=0)
        acc = lax.fori_loop(0, nck, body, jnp.zeros((SUBLANES, tq), I32))
        return jnp.sum(acc, axis=0, keepdims=True)

    def thr_bit(t, ans_u):
        cand_u = ans_u | jnp.left_shift(jnp.int32(1), 31 - t)
        cand = cand_u ^ INT_MIN
        cnt = count(lambda k, kpos: jnp.where(k >= cand, 1, 0))
        return jnp.where(cnt >= n_sel, cand_u, ans_u)

    thr = lax.fori_loop(0, 32, thr_bit, jnp.zeros((1, tq), I32)) ^ INT_MIN
    n_ge = count(lambda k, kpos: jnp.where(k >= thr, 1, 0))
    below_all = thr == INT_MIN
    cut_scr[...] = jnp.where(below_all, -1, (1 << n_bits) - 1)
    excess = jnp.max(jnp.where(below_all, 0, n_ge - n_sel))

    @pl.when(excess > 0)
    def _():
        n_gt = count(lambda k, kpos: jnp.where(k > thr, 1, 0))
        need = n_sel - n_gt

        def tie_bit(t, cut):
            cand = cut | jnp.left_shift(jnp.int32(1), n_bits - 1 - t)
            cnt = count(lambda k, kpos: jnp.where(k == thr, jnp.where(kpos < cand, 1, 0), 0))
            return jnp.where(cnt < need, cand, cut)

        cut = lax.fori_loop(0, n_bits, tie_bit, jnp.zeros((1, tq), I32))
        cut_scr[...] = jnp.where(below_all, -1, cut)

    cut = cut_scr[...]

    def bias_chunk(c, carry):
        k = keys_scr[c]
        sel = (k > thr) | ((k == thr) & (kpos0 + c * tk <= cut))
        keys_scr[c] = pltpu.bitcast(jnp.where(sel, 0.0, NEG_BIG).astype(F32), I32)
        return carry

    lax.fori_loop(0, nck, bias_chunk, 0)

    acc_scr[...] = jnp.zeros(acc_scr.shape, F32)
    m_scr[...] = jnp.full(m_scr.shape, NEG_BIG, F32)
    l_scr[...] = jnp.zeros(l_scr.shape, F32)

    def att_chunk(c, carry):
        start = pl.multiple_of(c * tk, tk)
        subs = range(0, tk, SUB_KEYS)

        def head_scores(h):
            k_ref = k0_ref if h // per_grp == 0 else k1_ref
            out = []
            for r0 in subs:
                s = jnp.dot(k_ref[pl.ds(start + r0, SUB_KEYS), :], q_scr[h], preferred_element_type=F32)
                out.append(s + pltpu.bitcast(keys_scr[c, r0:r0 + SUB_KEYS, :], F32))
            return out

        pending = [head_scores(h) for h in range(QK_AHEAD)]
        for h in range(DS_HEADS):
            grp = h // per_grp
            scores = pending.pop(0)
            m = m_scr[h][0:1]
            if h + QK_AHEAD < DS_HEADS:
                pending.append(head_scores(h + QK_AHEAD))
                m = m + _zero_after(pending[-1][0][0:1])
            m_new = m
            for s in scores:
                m_new = jnp.maximum(m_new, jnp.max(s, axis=0, keepdims=True))
            alpha = jnp.exp(m - m_new)
            l = alpha * l_scr[h][0:1]
            a = alpha * acc_scr[h]
            for r0, s in zip(subs, scores):
                p = jnp.exp(s - m_new)
                l = l + jnp.sum(p, axis=0, keepdims=True)
                vg = vt_ref[c, grp * DS_HEAD_DIM:(grp + 1) * DS_HEAD_DIM, r0:r0 + SUB_KEYS]
                a = a + jnp.dot(vg, p.astype(vg.dtype), preferred_element_type=F32)
            m_scr[h] = jnp.broadcast_to(m_new, (SUBLANES, tq))
            l_scr[h] = jnp.broadcast_to(l, (SUBLANES, tq))
            acc_scr[h] = a
        return carry

    lax.fori_loop(0, nck, att_chunk, 0)
    outs = [acc_scr[h] / l_scr[h][0:1] for h in range(DS_HEADS)]
    o_ref[...] = jnp.concatenate(outs, axis=0).T.astype(o_ref.dtype)


def _dsa_attention(qt, pr, vt3, iwt, b, s_len, tq):
    nq = s_len // tq
    n = b * s_len
    n_sel = min(TOPK_MAX, s_len // 4)
    n_bits = max(1, (s_len - 1).bit_length())
    kern = functools.partial(_dsa_kernel, tq=tq, n_sel=n_sel, n_bits=n_bits)
    seq = lambda blk: pl.BlockSpec((s_len, LANES), lambda bi, i: (bi, blk))
    qblk = lambda blk: pl.BlockSpec((4 * LANES, tq), lambda bi, i: (blk // 4, bi * nq + i))
    return pl.pallas_call(
        kern,
        grid=(b, nq),
        in_specs=[
            qblk(QT_IX), seq(PR_IXK),
            pl.BlockSpec((IDX_HEADS, tq), lambda bi, i: (0, bi * nq + i)),
            qblk(QT_DS), seq(PR_DSK), seq(PR_DSK + 1),
            pl.BlockSpec((s_len // tq, LANES, tq), lambda bi, i: (bi, VT_DS, 0)),
        ],
        out_specs=pl.BlockSpec((tq, DS_WIDTH), lambda bi, i: (bi * nq + i, 0)),
        out_shape=jax.ShapeDtypeStruct((n, DS_WIDTH), BF16),
        scratch_shapes=[pltpu.VMEM((nq, tq, tq), I32),
                        pltpu.VMEM((IDX_HEADS, LANES, tq), BF16), pltpu.VMEM((DS_HEADS, LANES, tq), BF16),
                        pltpu.VMEM((DS_HEADS, DS_HEAD_DIM, tq), F32),
                        pltpu.VMEM((DS_HEADS, SUBLANES, tq), F32), pltpu.VMEM((DS_HEADS, SUBLANES, tq), F32),
                        pltpu.VMEM((1, tq), I32)],
        compiler_params=pltpu.CompilerParams(dimension_semantics=("arbitrary",) * 2, vmem_limit_bytes=VMEM_LIMIT),
        name="dsa_attn",
    )(qt, pr, iwt, qt, pr, pr, vt3)


def _merge_kernel(oa_ref, ob_ref, gate_ref, x_ref, wa_ref, wb_ref, wo_ref, o_ref):
    ya = jnp.dot(oa_ref[...], wa_ref[...], preferred_element_type=F32)
    yb = jnp.dot(ob_ref[...], wb_ref[...], preferred_element_type=F32)
    mix = gate_ref[:, :D_MODEL] * ya + gate_ref[:, D_MODEL:] * yb
    o_ref[...] = x_ref[...] + jnp.dot(mix.astype(BF16), wo_ref[...], preferred_element_type=F32)


def _merge(oa, ob, gate, x2, wa, wb, wo, tm):
    n = x2.shape[0]
    full = lambda a: pl.BlockSpec(a.shape, lambda i: (0, 0))
    rows = lambda w: pl.BlockSpec((tm, w), lambda i: (i, 0))
    return pl.pallas_call(
        _merge_kernel,
        grid=(n // tm,),
        in_specs=[rows(DA_WIDTH), rows(DS_WIDTH), rows(2 * D_MODEL), rows(D_MODEL), full(wa), full(wb), full(wo)],
        out_specs=rows(D_MODEL),
        out_shape=jax.ShapeDtypeStruct((n, D_MODEL), F32),
        compiler_params=pltpu.CompilerParams(dimension_semantics=("arbitrary",), vmem_limit_bytes=VMEM_LIMIT),
        name="merge",
    )(oa, ob, gate, x2, wa, wb, wo)


def _norm_matmul_kernel(x_ref, g_ref, w_ref, o_ref):
    h = _rms(x_ref[...], g_ref[...]).astype(BF16)
    o_ref[...] = jnp.dot(h, w_ref[...], preferred_element_type=F32).astype(o_ref.dtype)


def _norm_matmul(x2, g, w, tm):
    n = x2.shape[0]
    return pl.pallas_call(
        _norm_matmul_kernel,
        grid=(n // tm,),
        in_specs=[pl.BlockSpec((tm, x2.shape[1]), lambda i: (i, 0)), pl.BlockSpec(g.shape, lambda i: (0, 0)),
                  pl.BlockSpec(w.shape, lambda i: (0, 0))],
        out_specs=pl.BlockSpec((tm, w.shape[1]), lambda i: (i, 0)),
        out_shape=jax.ShapeDtypeStruct((n, w.shape[1]), BF16),
        compiler_params=pltpu.CompilerParams(dimension_semantics=("arbitrary",), vmem_limit_bytes=VMEM_LIMIT),
        name="mem_kv_proj",
    )(x2, g, w)


def _mem_attn_kernel(x_ref, g_ref, wq_ref, kv_ref, wo_ref, o_ref):
    x = x_ref[...]
    hn = _rms(x, g_ref[...]).astype(BF16)
    q = jnp.dot(hn, wq_ref[...], preferred_element_type=F32).astype(BF16)
    scale = MEM_HEAD_DIM ** -0.5
    heads = []
    for h in range(MEM_HEADS):
        qh = q[:, h * MEM_HEAD_DIM:(h + 1) * MEM_HEAD_DIM]
        kh = kv_ref[:, h * MEM_HEAD_DIM:(h + 1) * MEM_HEAD_DIM]
        vh = kv_ref[:, MEM_WIDTH + h * MEM_HEAD_DIM:MEM_WIDTH + (h + 1) * MEM_HEAD_DIM]
        s = lax.dot_general(qh, kh, NT_DIMS, preferred_element_type=F32) * scale
        m = jnp.max(s, axis=1, keepdims=True)
        p = jnp.exp(s - m)
        l = jnp.sum(p, axis=1, keepdims=True)
        heads.append(jnp.dot(p.astype(BF16), vh, preferred_element_type=F32) / l)
    o = jnp.concatenate(heads, axis=1).astype(BF16)
    o_ref[...] = x + jnp.dot(o, wo_ref[...], preferred_element_type=F32)


def _mem_attn(x2, g, wq, kv, wo, b, s_len, m_tok, tm):
    nb = s_len // tm
    n = x2.shape[0]
    full = lambda a: pl.BlockSpec(a.shape, lambda bi, i: (0, 0))
    return pl.pallas_call(
        _mem_attn_kernel,
        grid=(b, nb),
        in_specs=[pl.BlockSpec((tm, D_MODEL), lambda bi, i: (bi * nb + i, 0)), full(g), full(wq),
                  pl.BlockSpec((m_tok, 2 * MEM_WIDTH), lambda bi, i: (bi, 0)), full(wo)],
        out_specs=pl.BlockSpec((tm, D_MODEL), lambda bi, i: (bi * nb + i, 0)),
        out_shape=jax.ShapeDtypeStruct((n, D_MODEL), F32),
        compiler_params=pltpu.CompilerParams(dimension_semantics=("arbitrary",) * 2, vmem_limit_bytes=VMEM_LIMIT),
        name="mem_attn",
    )(x2, g, wq, kv, wo)


N_SUB = 2 * PEER_HEADS
EXPERT_BITS = 14


def _route_kernel(x_ref, g_ref, wq_ref, keys_ref, hf_ref, idx_ref, row_ref, gate_ref, sc_scr, ts_scr, ti_scr):
    hf = _rms(x_ref[...], g_ref[...])
    tm = hf.shape[0]
    hb = hf.astype(BF16)
    bits = pltpu.bitcast(hb.astype(F32), jnp.uint32)
    half_d = D_MODEL // 2
    for s in range(SUBLANES // 2):
        lo = bits[:, s * LANES:(s + 1) * LANES] >> 16
        hi = bits[:, half_d + s * LANES:half_d + (s + 1) * LANES] & jnp.uint32(0xFFFF0000)
        hf_ref[:, s, :] = lo | hi
        hf_ref[:, s + SUBLANES // 2, :] = lo | hi
    q = jnp.dot(hb, wq_ref[...], preferred_element_type=F32).astype(BF16)
    for g in range(N_SUB):
        blk = q[:, (g // 2) * LANES:(g // 2 + 1) * LANES]
        sc_scr[g] = lax.dot_general(keys_ref[g], blk, NT_DIMS, preferred_element_type=F32)

    key_id = lax.broadcasted_iota(I32, (PEER_N_KEYS, tm), 0)

    def sub_topk(g, carry):
        x = sc_scr[g]
        vals, ids = [], []
        for _ in range(PEER_TOPK):
            m = jnp.max(x, axis=0, keepdims=True)
            idx = jnp.min(jnp.where(x == m, key_id, PEER_N_KEYS), axis=0, keepdims=True)
            vals.append(m)
            ids.append(idx)
            x = jnp.where(key_id == idx, -jnp.inf, x)
        ts_scr[g] = jnp.concatenate(vals, axis=0)
        ti_scr[g] = jnp.concatenate(ids, axis=0)
        return carry

    lax.fori_loop(0, N_SUB, sub_topk, 0)

    iota16 = lax.broadcasted_iota(I32, (PEER_TOPK, tm), 0)
    iota8 = lax.broadcasted_iota(I32, (SUBLANES, tm), 0)
    lead_rows = SUBLANES

    def head_topk(h, carry):
        s0, s1 = ts_scr[2 * h], ts_scr[2 * h + 1]
        i0, i1 = ti_scr[2 * h], ti_scr[2 * h + 1]
        cands, codes = [], []
        for a in range(lead_rows):
            n_j = PEER_TOPK if a == 0 else SUBLANES
            cands.append(s0[a:a + 1] + s1[:n_j])
            j_iota = iota16 if a == 0 else iota8
            codes.append(((a * PEER_TOPK + j_iota) << EXPERT_BITS) | (i0[a:a + 1] * PEER_N_KEYS + i1[:n_j]))
        cands.append(s0[lead_rows:] + s1[0:1])
        codes.append((((iota8 + lead_rows) * PEER_TOPK) << EXPERT_BITS) | (i0[lead_rows:] * PEER_N_KEYS + i1[0:1]))
        cand = jnp.concatenate(cands, axis=0)
        code = jnp.concatenate(codes, axis=0)
        big = jnp.int32(2 ** 30)
        vals, ids = [], []
        for _ in range(PEER_TOPK):
            m = jnp.max(cand, axis=0, keepdims=True)
            best = jnp.min(jnp.where(cand == m, code, big), axis=0, keepdims=True)
            vals.append(m)
            ids.append(best & (2 ** EXPERT_BITS - 1))
            cand = jnp.where(code == best, -jnp.inf, cand)
        best_s = jnp.concatenate(vals, axis=0)
        e = jnp.exp(best_s - best_s[0:1])
        gate_ref[h] = e / jnp.sum(e, axis=0, keepdims=True)
        best_i = jnp.concatenate(ids, axis=0)
        idx_ref[h] = best_i
        row_ref[h] = (best_i >> 1) * SUBLANES
        return carry

    lax.fori_loop(0, PEER_HEADS, head_topk, 0)


def _route(x2, g, wq, keys_p, tm):
    n = x2.shape[0]
    return pl.pallas_call(
        _route_kernel,
        grid=(n // tm,),
        in_specs=[pl.BlockSpec((tm, D_MODEL), lambda i: (i, 0)), pl.BlockSpec(g.shape, lambda i: (0, 0)),
                  pl.BlockSpec(wq.shape, lambda i: (0, 0)), pl.BlockSpec(keys_p.shape, lambda i: (0, 0, 0))],
        out_specs=[pl.BlockSpec((tm, SUBLANES, LANES), lambda i: (i, 0, 0))]
        + [pl.BlockSpec((PEER_HEADS, PEER_TOPK, tm), lambda i: (0, 0, i))] * 3,
        out_shape=[jax.ShapeDtypeStruct((n, SUBLANES, LANES), jnp.uint32),
                   jax.ShapeDtypeStruct((PEER_HEADS, PEER_TOPK, n), I32),
                   jax.ShapeDtypeStruct((PEER_HEADS, PEER_TOPK, n), I32),
                   jax.ShapeDtypeStruct((PEER_HEADS, PEER_TOPK, n), F32)],
        scratch_shapes=[pltpu.VMEM((N_SUB, PEER_N_KEYS, tm), F32), pltpu.VMEM((N_SUB, PEER_TOPK, tm), F32),
                        pltpu.VMEM((N_SUB, PEER_TOPK, tm), I32)],
        compiler_params=pltpu.CompilerParams(dimension_semantics=("arbitrary",), vmem_limit_bytes=VMEM_LIMIT),
        name="peer_route",
    )(x2, g, wq, keys_p)


TILE_ROWS = 2 * SUBLANES
PAIRS_PER_KTILE = 16
TOK_UNROLL = 4


def _pack_table(t):
    e = t.shape[0]
    bits = lax.bitcast_convert_type(t.astype(BF16), jnp.uint16).astype(jnp.uint32)
    words = bits[:, :D_MODEL // 2] | (bits[:, D_MODEL // 2:] << 16)
    return words.reshape(e * (SUBLANES // 2), LANES)


def _gather_tiles(row_ref, tab_ref, t):
    tiles = []
    for k in range(PEER_SEL):
        start = pl.multiple_of(row_ref[t, k], SUBLANES)
        tiles.append(pltpu.bitcast(tab_ref[pl.ds(start, SUBLANES), :], BF16))
    return tiles


def _peer_dot_kernel(row_ref, tab_ref, h_ref, gate_ref, idxv_ref, sel_ref, ce_ref, co_ref,
                     ye_scr, yo_scr, ze_scr, zo_scr, *, tt):
    ones = jnp.ones((SUBLANES, LANES), BF16)
    half_rows = PAIRS_PER_KTILE
    n_trip = tt // TOK_UNROLL

    def row_sums(i, slot):
        for u in range(TOK_UNROLL):
            t = i * TOK_UNROLL + u
            hp = pltpu.bitcast(h_ref[t], BF16)
            prods = [tile * hp for tile in _gather_tiles(row_ref, tab_ref, t)]
            for kt in range(PEER_SEL // PAIRS_PER_KTILE):
                stack = jnp.concatenate(prods[kt * PAIRS_PER_KTILE:(kt + 1) * PAIRS_PER_KTILE], axis=0)
                y = jnp.dot(sel_ref[...], stack, preferred_element_type=F32)
                r0 = u * PEER_SEL + kt * half_rows
                ye_scr[slot, r0:r0 + half_rows, :] = y[:half_rows].astype(BF16)
                yo_scr[slot, r0:r0 + half_rows, :] = y[half_rows:].astype(BF16)

    def lane_sums(i, slot):
        ze = lax.dot_general(ones, ye_scr[slot], NT_DIMS, preferred_element_type=F32)
        zo = lax.dot_general(ones, yo_scr[slot], NT_DIMS, preferred_element_type=F32)
        for u in range(TOK_UNROLL):
            t = i * TOK_UNROLL + u
            ze_scr[pl.ds(t, 1), :] = ze[0:1, u * PEER_SEL:(u + 1) * PEER_SEL]
            zo_scr[pl.ds(t, 1), :] = zo[0:1, u * PEER_SEL:(u + 1) * PEER_SEL]

    def trip(i, carry):
        slot = i & 1
        lane_sums(i - 1, 1 - slot)
        row_sums(i, slot)
        return carry

    row_sums(0, 0)
    lax.fori_loop(1, n_trip, trip, 0)
    lane_sums(n_trip - 1, (n_trip - 1) & 1)
    even = (idxv_ref[...] & 1) == 0
    a = jnp.where(even, ze_scr[...], zo_scr[...])
    c = 0.5 * a * (1.0 + lax.erf(a * (2.0 ** -0.5))) * gate_ref[...]
    ce_ref[...] = jnp.where(even, c, 0.0)
    co_ref[...] = jnp.where(even, 0.0, c)


def _peer_dot(rows2, tab, hpk, gate, idx2, sel, tt):
    n = hpk.shape[0]
    kern = functools.partial(_peer_dot_kernel, tt=tt)
    rows = pl.BlockSpec((tt, PEER_SEL), lambda i: (i, 0))
    return pl.pallas_call(
        kern,
        grid=(n // tt,),
        in_specs=[pl.BlockSpec((tt, PEER_SEL), lambda i: (i, 0), memory_space=pltpu.SMEM),
                  pl.BlockSpec(memory_space=pltpu.VMEM),
                  pl.BlockSpec((tt, SUBLANES, LANES), lambda i: (i, 0, 0)),
                  rows, rows, pl.BlockSpec(sel.shape, lambda i: (0, 0))],
        out_specs=[rows, rows],
        out_shape=[jax.ShapeDtypeStruct((n, PEER_SEL), F32), jax.ShapeDtypeStruct((n, PEER_SEL), F32)],
        scratch_shapes=[pltpu.VMEM((2, TOK_UNROLL * PEER_SEL, LANES), BF16)] * 2
        + [pltpu.VMEM((tt, PEER_SEL), F32)] * 2,
        compiler_params=pltpu.CompilerParams(dimension_semantics=("arbitrary",), vmem_limit_bytes=VMEM_LIMIT),
        name="peer_dot",
    )(rows2, tab, hpk, gate, idx2, sel)


def _peer_sum_kernel(row_ref, ce_ref, co_ref, spread_ref, tab_ref, o_ref, m1_scr, m2_scr, *, tt):
    width = PEER_SEL * TILE_ROWS
    cc = jnp.concatenate([ce_ref[...], co_ref[...]], axis=1)
    c1 = cc.astype(BF16)
    c2 = (cc - c1.astype(F32)).astype(BF16)
    m1_scr[...] = jnp.dot(c1, spread_ref[...], preferred_element_type=F32)
    m2_scr[...] = jnp.dot(c2, spread_ref[...], preferred_element_type=F32)
    row = lax.broadcasted_iota(I32, (SUBLANES, width), 0)
    lane = lax.broadcasted_iota(I32, (SUBLANES, width), 1)
    half = SUBLANES // 2
    on_row = (lane & (SUBLANES - 1)) == 2 * (row % half) + row // half

    def token(t):
        w = jnp.concatenate(_gather_tiles(row_ref, tab_ref, t), axis=0)
        lhs = []
        for m_scr in (m1_scr, m2_scr):
            coef = jnp.broadcast_to(m_scr[pl.ds(t, 1), :], (SUBLANES, width))
            lhs.append(jnp.where(on_row, coef, 0.0).astype(BF16))
        res = jnp.dot(jnp.concatenate(lhs, axis=0), w, preferred_element_type=F32)
        o_ref[t] = res[:SUBLANES] + res[SUBLANES:]

    def trip(i, carry):
        for u in range(TOK_UNROLL):
            token(i * TOK_UNROLL + u)
        return carry

    lax.fori_loop(0, tt // TOK_UNROLL, trip, 0)


def _peer_sum(rows2, ce, co, spread, tab, tt):
    n = ce.shape[0]
    kern = functools.partial(_peer_sum_kernel, tt=tt)
    rows = pl.BlockSpec((tt, PEER_SEL), lambda i: (i, 0))
    width = PEER_SEL * TILE_ROWS
    return pl.pallas_call(
        kern,
        grid=(n // tt,),
        in_specs=[pl.BlockSpec((tt, PEER_SEL), lambda i: (i, 0), memory_space=pltpu.SMEM), rows, rows,
                  pl.BlockSpec(spread.shape, lambda i: (0, 0)), pl.BlockSpec(memory_space=pltpu.VMEM)],
        out_specs=pl.BlockSpec((tt, SUBLANES, LANES), lambda i: (i, 0, 0)),
        out_shape=jax.ShapeDtypeStruct((n, SUBLANES, LANES), F32),
        scratch_shapes=[pltpu.VMEM((tt, width), F32), pltpu.VMEM((tt, width), F32)],
        compiler_params=pltpu.CompilerParams(dimension_semantics=("arbitrary",), vmem_limit_bytes=VMEM_LIMIT),
        name="peer_sum",
    )(rows2, ce, co, spread, tab)


def _final_kernel(x_ref, p_ref, g_ref, o_ref):
    peer = jnp.concatenate([p_ref[:, s, :] for s in range(SUBLANES)], axis=1)
    o_ref[...] = _rms(x_ref[...] + peer, g_ref[...])


def _final(x2, peer3, g, tm):
    n = x2.shape[0]
    return pl.pallas_call(
        _final_kernel,
        grid=(n // tm,),
        in_specs=[pl.BlockSpec((tm, D_MODEL), lambda i: (i, 0)),
                  pl.BlockSpec((tm, SUBLANES, LANES), lambda i: (i, 0, 0)),
                  pl.BlockSpec(g.shape, lambda i: (0, 0))],
        out_specs=pl.BlockSpec((tm, D_MODEL), lambda i: (i, 0)),
        out_shape=jax.ShapeDtypeStruct((n, D_MODEL), F32),
        compiler_params=pltpu.CompilerParams(dimension_semantics=("arbitrary",), vmem_limit_bytes=VMEM_LIMIT),
        name="final_norm",
    )(x2, peer3, g)


def _tile(n, pref):
    t = pref
    while n % t:
        t //= 2
    return t


def _layer(l, x2, mem2, pos, invf, b, s_len, m_tok, norm_mix_g, w_in, da_lambda, da_subln_g, w_branch_a,
           w_branch_b, gate_bias, w_out, norm_mem_g, mem_kv_norm_g, w_mem_q, w_mem_kv, w_mem_o, norm_ffn_g,
           peer_w_q, peer_sub_keys, peer_u, peer_v):
    n = b * s_len
    row2 = lambda v: v.reshape(1, -1)
    splits = [0]
    for c in COL_SIZES:
        splits.append(splits[-1] + c)
    da_q, da_k, da_v, ds_q, ds_k, ds_v, ix_q, ix_k, ix_w, gates = (
        w_in[l][:, splits[j]:splits[j + 1]] for j in range(len(COL_SIZES)))
    k0, k1 = ds_k[:, :DS_HEAD_DIM], ds_k[:, DS_HEAD_DIM:]
    w_keys = jnp.concatenate([da_k, k0, k0, k1, k1, ix_k, ix_k], axis=1).astype(BF16)
    w_q_t = jnp.concatenate([da_q, ds_q, ix_q], axis=1).T.astype(BF16)
    w_v_t = jnp.concatenate([da_v, ds_v], axis=1).T.astype(BF16)
    t_att = _tile(s_len, 256)
    pr, gate, qt, vt3, iwt = _in_proj(x2, row2(norm_mix_g[l]), pos, invf, pos.reshape(1, n),
                                      invf[:, :ROPE_HALF].reshape(ROPE_HALF, 1), row2(gate_bias[l]), w_keys,
                                      gates.astype(BF16), w_q_t, w_v_t, ix_w.T.astype(BF16), t_att)

    lam_init = 0.8 - 0.6 * math.exp(-0.3 * l)
    o_a = _diff_attention(da_lambda[l], qt, pr, vt3, da_subln_g[l].reshape(-1, 1), b, s_len, t_att, t_att, lam_init)
    o_b = _dsa_attention(qt, pr, vt3, iwt, b, s_len, t_att)
    x2 = _merge(o_a, o_b, gate, x2, w_branch_a[l].astype(BF16), w_branch_b[l].astype(BF16),
                w_out[l].astype(BF16), _tile(n, 256))

    kv = _norm_matmul(mem2, row2(mem_kv_norm_g[l]), w_mem_kv[l].astype(BF16), _tile(mem2.shape[0], 256))
    x2 = _mem_attn(x2, row2(norm_mem_g[l]), w_mem_q[l].astype(BF16), kv, w_mem_o[l].astype(BF16),
                   b, s_len, m_tok, _tile(s_len, 256))

    sk = peer_sub_keys[l].reshape(N_SUB, PEER_N_KEYS, PEER_HALF)
    z = jnp.zeros_like(sk)
    keys_p = jnp.where((jnp.arange(N_SUB) % 2 == 0)[:, None, None],
                       jnp.concatenate([sk, z], axis=2), jnp.concatenate([z, sk], axis=2)).astype(BF16)
    hpk, idx_t, row_t, gate_t = _route(x2, row2(norm_ffn_g[l]), peer_w_q[l].astype(BF16), keys_p, _tile(n, 256))
    idx2 = idx_t.reshape(PEER_SEL, n).T
    rows2 = row_t.reshape(PEER_SEL, n).T
    gate2 = gate_t.reshape(PEER_SEL, n).T
    tt = _tile(n, 128)
    r32 = jnp.arange(2 * PAIRS_PER_KTILE)[:, None]
    c256 = jnp.arange(PAIRS_PER_KTILE * TILE_ROWS)[None, :]
    sel = ((c256 // TILE_ROWS == r32 % PAIRS_PER_KTILE)
           & ((c256 % TILE_ROWS) // SUBLANES == r32 // PAIRS_PER_KTILE)).astype(BF16)
    k256 = jnp.arange(2 * PEER_SEL)[:, None]
    c2048 = jnp.arange(PEER_SEL * TILE_ROWS)[None, :]
    spread = ((c2048 // TILE_ROWS == k256 % PEER_SEL)
              & ((c2048 % TILE_ROWS) // SUBLANES == k256 // PEER_SEL)).astype(BF16)
    ce2, co2 = _peer_dot(rows2, _pack_table(peer_u[l]), hpk, gate2, idx2, sel, tt)
    peer3 = _peer_sum(rows2, ce2, co2, spread, _pack_table(peer_v[l]), tt)
    return x2, peer3


def kernel(x, mem, positions, norm_mix_g, w_in, da_lambda, da_subln_g, w_branch_a, w_branch_b, gate_bias, w_out, norm_mem_g, mem_kv_norm_g, w_mem_q, w_mem_kv, w_mem_o, norm_ffn_g, peer_w_q, peer_sub_keys, peer_u, peer_v, final_norm_g):
    b, s_len, d = x.shape
    m_tok = mem.shape[1]
    n = b * s_len
    depth = w_in.shape[0]
    x2 = x.reshape(n, d)
    mem2 = mem.reshape(b * m_tok, d)
    pos = positions.astype(F32).reshape(n, 1)
    inv_freq = ROPE_THETA ** (-(jnp.arange(ROPE_HALF, dtype=F32) * 2.0) / ROPE_DIM)
    invf = jnp.tile(inv_freq, LANES // ROPE_HALF).reshape(1, LANES)
    peer3 = None
    for l in range(depth):
        if peer3 is not None:
            x2 = x2 + peer3.reshape(n, d)
        x2, peer3 = _layer(l, x2, mem2, pos, invf, b, s_len, m_tok, norm_mix_g, w_in, da_lambda, da_subln_g,
                           w_branch_a, w_branch_b, gate_bias, w_out, norm_mem_g, mem_kv_norm_g, w_mem_q, w_mem_kv,
                           w_mem_o, norm_ffn_g, peer_w_q, peer_sub_keys, peer_u, peer_v)
    out = _final(x2, peer3, final_norm_g.reshape(1, d), _tile(n, 256))
    return out.reshape(b, s_len, d)
```

```python
import functools
import math

import jax
import jax.numpy as jnp
from jax import lax
from jax.experimental import pallas as pl
from jax.experimental.pallas import tpu as pltpu

F32 = jnp.float32
BF16 = jnp.bfloat16
I32 = jnp.int32

D_MODEL = 1024
EPS = 1e-6
ROPE_THETA = 500000.0
ROT_HEAD_DIM = 64
ROPE_DIM = 16
ROPE_HALF = ROPE_DIM // 2

DA_HEADS = 4
DA_QK_DIM = 64
DA_V_DIM = 128
DA_WIDTH = 512
DS_HEADS = 8
DS_KV_HEADS = 2
DS_HEAD_DIM = 64
DS_WIDTH = 512
IDX_HEADS = 8
IDX_DIM = 64
TOPK_MAX = 256
MEM_HEADS = 4
MEM_HEAD_DIM = 128
MEM_WIDTH = 512
PEER_HEADS = 8
PEER_N_KEYS = 128
PEER_HALF = 64
PEER_TOPK = 16
PEER_SEL = PEER_HEADS * PEER_TOPK

COL_SIZES = (512, 512, 512, 512, 128, 128, 512, 64, 8, 2048)

LANES = 128
SUBLANES = 8
NEG_BIG = -1e30
INT_MIN = -(2 ** 31)
VMEM_LIMIT = 56 * 1024 * 1024

NT_DIMS = (((1,), (1,)), ((), ()))


def _rms(x, g):
    var = jnp.mean(x * x, axis=-1, keepdims=True)
    return x * lax.rsqrt(var + EPS) * g


def _rope_rows(y, cos8, sin8):
    pieces = []
    for r0 in range(0, y.shape[0], ROT_HEAD_DIM):
        t1 = y[r0:r0 + ROPE_HALF]
        t2 = y[r0 + ROPE_HALF:r0 + ROPE_DIM]
        pieces += [t1 * cos8 - t2 * sin8, t2 * cos8 + t1 * sin8, y[r0 + ROPE_DIM:r0 + ROT_HEAD_DIM]]
    return jnp.concatenate(pieces, axis=0)


def _in_proj_kernel(x_ref, g_ref, pos_ref, invf_ref, pos_t_ref, invf8_ref, bias_ref, wr_ref, wg_ref,
                    wqt_ref, wvt_ref, wwt_ref, pr_ref, gate_ref, qt_ref, vt_ref, iwt_ref):
    h = _rms(x_ref[...], g_ref[...]).astype(BF16)
    tm = h.shape[0]
    ang_t = invf8_ref[...] * pos_t_ref[...]
    cos8 = jnp.cos(ang_t)
    sin8 = jnp.sin(ang_t)
    row_chunk = 2 * LANES
    for r0 in range(0, wqt_ref.shape[0], row_chunk):
        y = lax.dot_general(wqt_ref[r0:r0 + row_chunk, :], h, NT_DIMS, preferred_element_type=F32)
        qt_ref[r0:r0 + row_chunk, :] = _rope_rows(y, cos8, sin8).astype(qt_ref.dtype)
    for r0 in range(0, wvt_ref.shape[0], row_chunk):
        r1 = min(r0 + row_chunk, wvt_ref.shape[0])
        y = lax.dot_general(wvt_ref[r0:r1, :], h, NT_DIMS, preferred_element_type=F32)
        vt_ref[0, r0:r1, :] = y.astype(vt_ref.dtype)
    iwt_ref[...] = lax.dot_general(wwt_ref[...], h, NT_DIMS, preferred_element_type=F32)
    ang = pos_ref[...] * invf_ref[...]
    cos = jnp.cos(ang)
    sin = jnp.sin(ang)
    lane = lax.broadcasted_iota(I32, (tm, LANES), 1) % ROT_HEAD_DIM
    c_t = jnp.where(lane < ROPE_DIM, cos, 1.0)
    s_lo = jnp.where(lane < ROPE_HALF, -sin, 0.0)
    s_hi = jnp.where((lane >= ROPE_HALF) & (lane < ROPE_DIM), sin, 0.0)
    n_r = wr_ref.shape[1]
    for c0 in range(0, n_r, 512):
        c1 = min(c0 + 512, n_r)
        w = c1 - c0
        y = jnp.dot(h, wr_ref[:, c0:c1], preferred_element_type=F32)
        reps = w // LANES
        ct = jnp.concatenate([c_t] * reps, axis=1)
        sl = jnp.concatenate([s_lo] * reps, axis=1)
        sh = jnp.concatenate([s_hi] * reps, axis=1)
        y = y * ct + pltpu.roll(y, w - ROPE_HALF, 1) * sl + pltpu.roll(y, ROPE_HALF, 1) * sh
        pr_ref[:, c0:c1] = y.astype(pr_ref.dtype)
    n_g = wg_ref.shape[1]
    for c0 in range(0, n_g, 512):
        y = jnp.dot(h, wg_ref[:, c0:c0 + 512], preferred_element_type=F32)
        gate_ref[:, c0:c0 + 512] = jax.nn.sigmoid(y + bias_ref[:, c0:c0 + 512])


def _in_proj(x2, g, pos, invf, pos_t, invf8, bias, wr, wg, wqt, wvt, wwt, tm):
    n = x2.shape[0]
    full = lambda a: pl.BlockSpec(a.shape, lambda i: (0, 0))
    rows = lambda w: pl.BlockSpec((tm, w), lambda i: (i, 0))
    cols = lambda r: pl.BlockSpec((r, tm), lambda i: (0, i))
    return pl.pallas_call(
        _in_proj_kernel,
        grid=(n // tm,),
        in_specs=[rows(D_MODEL), full(g), rows(1), full(invf), cols(1), full(invf8), full(bias), full(wr), full(wg),
                  full(wqt), full(wvt), full(wwt)],
        out_specs=[rows(wr.shape[1]), rows(wg.shape[1]), cols(wqt.shape[0]),
                   pl.BlockSpec((1, wvt.shape[0], tm), lambda i: (i, 0, 0)), cols(wwt.shape[0])],
        out_shape=[jax.ShapeDtypeStruct((n, wr.shape[1]), BF16), jax.ShapeDtypeStruct((n, wg.shape[1]), F32),
                   jax.ShapeDtypeStruct((wqt.shape[0], n), BF16),
                   jax.ShapeDtypeStruct((n // tm, wvt.shape[0], tm), BF16),
                   jax.ShapeDtypeStruct((wwt.shape[0], n), F32)],
        compiler_params=pltpu.CompilerParams(dimension_semantics=("arbitrary",), vmem_limit_bytes=VMEM_LIMIT),
        name="in_proj",
    )(x2, g, pos, invf, pos_t, invf8, bias, wr, wg, wqt, wvt, wwt)


QT_DA = 0
QT_DS = 4
QT_IX = 8
VT_DA = 0
VT_DS = 4
PR_DAK = 0
PR_DSK = 4
PR_IXK = 6


SUB_KEYS = 128
QK_AHEAD = 2


def _zero_after(x):
    bits = pltpu.bitcast(x, jnp.uint32)
    return pltpu.bitcast((bits >> 16) >> 16, F32)


def _flash_chains(n_chains, score_fn, value_fn, subs, m_scr, l_scr, acc_scr):
    pending = [score_fn(c) for c in range(QK_AHEAD)]
    for c in range(n_chains):
        scores = pending.pop(0)
        m = m_scr[c][0:1]
        if c + QK_AHEAD < n_chains:
            pending.append(score_fn(c + QK_AHEAD))
            m = m + _zero_after(pending[-1][0][0:1])
        m_new = m
        for s in scores:
            m_new = jnp.maximum(m_new, jnp.max(s, axis=0, keepdims=True))
        alpha = jnp.exp(m - m_new)
        l = alpha * l_scr[c][0:1]
        a = alpha * acc_scr[c]
        for r0, s in zip(subs, scores):
            p = jnp.exp(s - m_new)
            l = l + jnp.sum(p, axis=0, keepdims=True)
            v = value_fn(c, r0)
            a = a + jnp.dot(v, p.astype(v.dtype), preferred_element_type=F32)
        m_scr[c] = jnp.broadcast_to(m_new, m_scr.shape[1:])
        l_scr[c] = jnp.broadcast_to(l, l_scr.shape[1:])
        acc_scr[c] = a


def _da_kernel(lam_ref, qt_ref, k_ref, vt_ref, g_ref, o_ref, q_scr, acc_scr, m_scr, l_scr, *, tq, lam_init):
    tk = tq
    i = pl.program_id(1)
    n_chains = 2 * DA_HEADS
    frow = lax.broadcasted_iota(I32, (LANES, tq), 0)
    scale = jnp.asarray(DA_QK_DIM ** -0.5, BF16)
    for h in range(DA_HEADS):
        blk = qt_ref[h * LANES:(h + 1) * LANES, :] * scale
        zero = jnp.zeros_like(blk)
        q_scr[2 * h] = jnp.where(frow < DA_QK_DIM, blk, zero)
        q_scr[2 * h + 1] = jnp.where(frow >= DA_QK_DIM, blk, zero)
    acc_scr[...] = jnp.zeros(acc_scr.shape, F32)
    m_scr[...] = jnp.full(m_scr.shape, NEG_BIG, F32)
    l_scr[...] = jnp.zeros(l_scr.shape, F32)
    kpos0 = lax.broadcasted_iota(I32, (SUB_KEYS, tq), 0)
    qpos = i * tq + lax.broadcasted_iota(I32, (SUB_KEYS, tq), 1)
    subs = range(0, tk, SUB_KEYS)

    def chunk(j, masked):
        start = pl.multiple_of(j * tk, tk)

        def scores(c):
            head = c // 2
            out = []
            for r0 in subs:
                k = k_ref[pl.ds(start + r0, SUB_KEYS), head * LANES:(head + 1) * LANES]
                s = jnp.dot(k, q_scr[c], preferred_element_type=F32)
                if masked:
                    s = jnp.where(kpos0 + (j * tk + r0) <= qpos, s, NEG_BIG)
                out.append(s)
            return out

        def values(c, r0):
            head = c // 2
            return vt_ref[j, head * LANES:(head + 1) * LANES, r0:r0 + SUB_KEYS]

        _flash_chains(n_chains, scores, values, subs, m_scr, l_scr, acc_scr)

    def full_chunk(j, carry):
        chunk(j, False)
        return carry

    lax.fori_loop(0, i, full_chunk, 0)
    chunk(i, True)

    lp = lam_ref[...]
    lam = (jnp.exp(jnp.sum(lp[0:1] * lp[1:2], axis=1, keepdims=True))
           - jnp.exp(jnp.sum(lp[2:3] * lp[3:4], axis=1, keepdims=True)) + lam_init)
    for h in range(DA_HEADS):
        o = acc_scr[2 * h] / l_scr[2 * h][0:1] - lam * (acc_scr[2 * h + 1] / l_scr[2 * h + 1][0:1])
        var = jnp.mean(o * o, axis=0, keepdims=True)
        o = o * lax.rsqrt(var + EPS) * g_ref[...] * (1.0 - lam_init)
        o_ref[:, h * LANES:(h + 1) * LANES] = o.T.astype(o_ref.dtype)


def _diff_attention(lam_p, qt, pr, vt3, subln_g, b, s_len, tq, lam_init):
    nq = s_len // tq
    n = b * s_len
    n_chains = 2 * DA_HEADS
    kern = functools.partial(_da_kernel, tq=tq, lam_init=lam_init)
    return pl.pallas_call(
        kern,
        grid=(b, nq),
        in_specs=[
            pl.BlockSpec(lam_p.shape, lambda bi, i: (0, 0)),
            pl.BlockSpec((DA_HEADS * LANES, tq), lambda bi, i: (QT_DA // DA_HEADS, bi * nq + i)),
            pl.BlockSpec((s_len, DA_HEADS * LANES), lambda bi, i: (bi, PR_DAK // DA_HEADS)),
            pl.BlockSpec((s_len // tq, DA_HEADS * LANES, tq), lambda bi, i: (bi, VT_DA // DA_HEADS, 0)),
            pl.BlockSpec(subln_g.shape, lambda bi, i: (0, 0)),
        ],
        out_specs=pl.BlockSpec((tq, DA_WIDTH), lambda bi, i: (bi * nq + i, 0)),
        out_shape=jax.ShapeDtypeStruct((n, DA_WIDTH), BF16),
        scratch_shapes=[pltpu.VMEM((n_chains, LANES, tq), BF16), pltpu.VMEM((n_chains, DA_V_DIM, tq), F32),
                        pltpu.VMEM((n_chains, SUBLANES, tq), F32), pltpu.VMEM((n_chains, SUBLANES, tq), F32)],
        compiler_params=pltpu.CompilerParams(dimension_semantics=("arbitrary",) * 2, vmem_limit_bytes=VMEM_LIMIT),
        name="diff_attn",
    )(lam_p, qt, pr, vt3, subln_g)


def _dsa_kernel(qit_ref, ki_ref, wt_ref, qt_ref, k0_ref, k1_ref, vt_ref, o_ref,
                keys_scr, qi_scr, q_scr, acc_scr, m_scr, l_scr, cut_scr, *, tq, n_sel, n_bits):
    tk = tq
    i = pl.program_id(1)
    nck = i + 1
    kpos0 = lax.broadcasted_iota(I32, (tk, tq), 0)
    qpos = i * tq + lax.broadcasted_iota(I32, (tk, tq), 1)
    frow = lax.broadcasted_iota(I32, (LANES, tq), 0)
    per_grp = DS_HEADS // DS_KV_HEADS

    scale = jnp.asarray(DS_HEAD_DIM ** -0.5, BF16)
    for h in range(DS_HEADS):
        keep = (frow < DS_HEAD_DIM) if h % 2 == 0 else (frow >= DS_HEAD_DIM)
        rows = slice((h // 2) * LANES, (h // 2 + 1) * LANES)
        blk = qit_ref[rows, :]
        qi_scr[h] = jnp.where(keep, blk, jnp.zeros_like(blk))
        blk = qt_ref[rows, :] * scale
        q_scr[h] = jnp.where(keep, blk, jnp.zeros_like(blk))

    def score_chunk(c, carry):
        kc = ki_ref[pl.ds(pl.multiple_of(c * tk, tk), tk), :]
        acc = jnp.zeros((tk, tq), F32)
        for h in range(IDX_HEADS):
            s = jnp.dot(kc, qi_scr[h], preferred_element_type=F32)
            acc = acc + wt_ref[h:h + 1, :] * jnp.maximum(s, 0.0)
        bits = pltpu.bitcast(acc, I32)
        key = bits ^ ((bits >> 31) & 0x7FFFFFFF)
        key = jnp.where(acc == 0.0, 0, key)
        keys_scr[c] = jnp.where(kpos0 + c * tk <= qpos, key, INT_MIN)
        return carry

    lax.fori_loop(0, nck, score_chunk, 0)

    def count(pred):
        def body(c, acc):
            hit = pred(keys_scr[c], kpos0 + c * tk)
            return acc + jnp.sum(hit.reshape(tk // SUBLANES, SUBLANES, tq), axis=0)
        acc = lax.fori_loop(0, nck, body, jnp.zeros((SUBLANES, tq), I32))
        return jnp.sum(acc, axis=0, keepdims=True)

    def thr_bit(t, ans_u):
        cand_u = ans_u | jnp.left_shift(jnp.int32(1), 31 - t)
        cand = cand_u ^ INT_MIN
        cnt = count(lambda k, kpos: jnp.where(k >= cand, 1, 0))
        return jnp.where(cnt >= n_sel, cand_u, ans_u)

    thr = lax.fori_loop(0, 32, thr_bit, jnp.zeros((1, tq), I32)) ^ INT_MIN
    n_ge = count(lambda k, kpos: jnp.where(k >= thr, 1, 0))
    below_all = thr == INT_MIN
    cut_scr[...] = jnp.where(below_all, -1, (1 << n_bits) - 1)
    excess = jnp.max(jnp.where(below_all, 0, n_ge - n_sel))

    @pl.when(excess > 0)
    def _():
        n_gt = count(lambda k, kpos: jnp.where(k > thr, 1, 0))
        need = n_sel - n_gt

        def tie_bit(t, cut):
            cand = cut | jnp.left_shift(jnp.int32(1), n_bits - 1 - t)
            cnt = count(lambda k, kpos: jnp.where(k == thr, jnp.where(kpos < cand, 1, 0), 0))
            return jnp.where(cnt < need, cand, cut)

        cut = lax.fori_loop(0, n_bits, tie_bit, jnp.zeros((1, tq), I32))
        cut_scr[...] = jnp.where(below_all, -1, cut)

    cut = cut_scr[...]

    def bias_chunk(c, carry):
        k = keys_scr[c]
        sel = (k > thr) | ((k == thr) & (kpos0 + c * tk <= cut))
        keys_scr[c] = pltpu.bitcast(jnp.where(sel, 0.0, NEG_BIG).astype(F32), I32)
        return carry

    lax.fori_loop(0, nck, bias_chunk, 0)

    acc_scr[...] = jnp.zeros(acc_scr.shape, F32)
    m_scr[...] = jnp.full(m_scr.shape, NEG_BIG, F32)
    l_scr[...] = jnp.zeros(l_scr.shape, F32)

    def att_chunk(c, carry):
        start = pl.multiple_of(c * tk, tk)
        subs = range(0, tk, SUB_KEYS)

        def head_scores(h):
            k_ref = k0_ref if h // per_grp == 0 else k1_ref
            out = []
            for r0 in subs:
                s = jnp.dot(k_ref[pl.ds(start + r0, SUB_KEYS), :], q_scr[h], preferred_element_type=F32)
                out.append(s + pltpu.bitcast(keys_scr[c, r0:r0 + SUB_KEYS, :], F32))
            return out

        def head_values(h, r0):
            grp = h // per_grp
            return vt_ref[c, grp * DS_HEAD_DIM:(grp + 1) * DS_HEAD_DIM, r0:r0 + SUB_KEYS]

        _flash_chains(DS_HEADS, head_scores, head_values, subs, m_scr, l_scr, acc_scr)
        return carry

    lax.fori_loop(0, nck, att_chunk, 0)
    outs = [acc_scr[h] / l_scr[h][0:1] for h in range(DS_HEADS)]
    o_ref[...] = jnp.concatenate(outs, axis=0).T.astype(o_ref.dtype)


def _dsa_attention(qt, pr, vt3, iwt, b, s_len, tq):
    nq = s_len // tq
    n = b * s_len
    n_sel = min(TOPK_MAX, s_len // 4)
    n_bits = max(1, (s_len - 1).bit_length())
    kern = functools.partial(_dsa_kernel, tq=tq, n_sel=n_sel, n_bits=n_bits)
    seq = lambda blk: pl.BlockSpec((s_len, LANES), lambda bi, i: (bi, blk))
    qblk = lambda blk: pl.BlockSpec((4 * LANES, tq), lambda bi, i: (blk // 4, bi * nq + i))
    return pl.pallas_call(
        kern,
        grid=(b, nq),
        in_specs=[
            qblk(QT_IX), seq(PR_IXK),
            pl.BlockSpec((IDX_HEADS, tq), lambda bi, i: (0, bi * nq + i)),
            qblk(QT_DS), seq(PR_DSK), seq(PR_DSK + 1),
            pl.BlockSpec((s_len // tq, LANES, tq), lambda bi, i: (bi, VT_DS, 0)),
        ],
        out_specs=pl.BlockSpec((tq, DS_WIDTH), lambda bi, i: (bi * nq + i, 0)),
        out_shape=jax.ShapeDtypeStruct((n, DS_WIDTH), BF16),
        scratch_shapes=[pltpu.VMEM((nq, tq, tq), I32),
                        pltpu.VMEM((IDX_HEADS, LANES, tq), BF16), pltpu.VMEM((DS_HEADS, LANES, tq), BF16),
                        pltpu.VMEM((DS_HEADS, DS_HEAD_DIM, tq), F32),
                        pltpu.VMEM((DS_HEADS, SUBLANES, tq), F32), pltpu.VMEM((DS_HEADS, SUBLANES, tq), F32),
                        pltpu.VMEM((1, tq), I32)],
        compiler_params=pltpu.CompilerParams(dimension_semantics=("arbitrary",) * 2, vmem_limit_bytes=VMEM_LIMIT),
        name="dsa_attn",
    )(qt, pr, iwt, qt, pr, pr, vt3)


def _merge_kernel(oa_ref, ob_ref, gate_ref, x_ref, wa_ref, wb_ref, wo_ref, o_ref):
    ya = jnp.dot(oa_ref[...], wa_ref[...], preferred_element_type=F32)
    yb = jnp.dot(ob_ref[...], wb_ref[...], preferred_element_type=F32)
    mix = gate_ref[:, :D_MODEL] * ya + gate_ref[:, D_MODEL:] * yb
    o_ref[...] = x_ref[...] + jnp.dot(mix.astype(BF16), wo_ref[...], preferred_element_type=F32)


def _merge(oa, ob, gate, x2, wa, wb, wo, tm):
    n = x2.shape[0]
    full = lambda a: pl.BlockSpec(a.shape, lambda i: (0, 0))
    rows = lambda w: pl.BlockSpec((tm, w), lambda i: (i, 0))
    return pl.pallas_call(
        _merge_kernel,
        grid=(n // tm,),
        in_specs=[rows(DA_WIDTH), rows(DS_WIDTH), rows(2 * D_MODEL), rows(D_MODEL), full(wa), full(wb), full(wo)],
        out_specs=rows(D_MODEL),
        out_shape=jax.ShapeDtypeStruct((n, D_MODEL), F32),
        compiler_params=pltpu.CompilerParams(dimension_semantics=("arbitrary",), vmem_limit_bytes=VMEM_LIMIT),
        name="merge",
    )(oa, ob, gate, x2, wa, wb, wo)


def _norm_matmul_kernel(x_ref, g_ref, w_ref, o_ref):
    h = _rms(x_ref[...], g_ref[...]).astype(BF16)
    o_ref[...] = jnp.dot(h, w_ref[...], preferred_element_type=F32).astype(o_ref.dtype)


def _norm_matmul(x2, g, w, tm):
    n = x2.shape[0]
    return pl.pallas_call(
        _norm_matmul_kernel,
        grid=(n // tm,),
        in_specs=[pl.BlockSpec((tm, x2.shape[1]), lambda i: (i, 0)), pl.BlockSpec(g.shape, lambda i: (0, 0)),
                  pl.BlockSpec(w.shape, lambda i: (0, 0))],
        out_specs=pl.BlockSpec((tm, w.shape[1]), lambda i: (i, 0)),
        out_shape=jax.ShapeDtypeStruct((n, w.shape[1]), BF16),
        compiler_params=pltpu.CompilerParams(dimension_semantics=("arbitrary",), vmem_limit_bytes=VMEM_LIMIT),
        name="mem_kv_proj",
    )(x2, g, w)


def _mem_attn_kernel(x_ref, g_ref, wq_ref, kv_ref, wo_ref, o_ref):
    x = x_ref[...]
    hn = _rms(x, g_ref[...]).astype(BF16)
    q = jnp.dot(hn, wq_ref[...], preferred_element_type=F32).astype(BF16)
    scale = MEM_HEAD_DIM ** -0.5
    heads = []
    for h in range(MEM_HEADS):
        qh = q[:, h * MEM_HEAD_DIM:(h + 1) * MEM_HEAD_DIM]
        kh = kv_ref[:, h * MEM_HEAD_DIM:(h + 1) * MEM_HEAD_DIM]
        vh = kv_ref[:, MEM_WIDTH + h * MEM_HEAD_DIM:MEM_WIDTH + (h + 1) * MEM_HEAD_DIM]
        s = lax.dot_general(qh, kh, NT_DIMS, preferred_element_type=F32) * scale
        m = jnp.max(s, axis=1, keepdims=True)
        p = jnp.exp(s - m)
        l = jnp.sum(p, axis=1, keepdims=True)
        heads.append(jnp.dot(p.astype(BF16), vh, preferred_element_type=F32) / l)
    o = jnp.concatenate(heads, axis=1).astype(BF16)
    o_ref[...] = x + jnp.dot(o, wo_ref[...], preferred_element_type=F32)


def _mem_attn(x2, g, wq, kv, wo, b, s_len, m_tok, tm):
    nb = s_len // tm
    n = x2.shape[0]
    full = lambda a: pl.BlockSpec(a.shape, lambda bi, i: (0, 0))
    return pl.pallas_call(
        _mem_attn_kernel,
        grid=(b, nb),
        in_specs=[pl.BlockSpec((tm, D_MODEL), lambda bi, i: (bi * nb + i, 0)), full(g), full(wq),
                  pl.BlockSpec((m_tok, 2 * MEM_WIDTH), lambda bi, i: (bi, 0)), full(wo)],
        out_specs=pl.BlockSpec((tm, D_MODEL), lambda bi, i: (bi * nb + i, 0)),
        out_shape=jax.ShapeDtypeStruct((n, D_MODEL), F32),
        compiler_params=pltpu.CompilerParams(dimension_semantics=("arbitrary",) * 2, vmem_limit_bytes=VMEM_LIMIT),
        name="mem_attn",
    )(x2, g, wq, kv, wo)


N_SUB = 2 * PEER_HEADS
EXPERT_BITS = 14


def _route_kernel(x_ref, g_ref, wq_ref, keys_ref, hf_ref, idx_ref, row_ref, gate_ref, sc_scr, ts_scr, ti_scr):
    hf = _rms(x_ref[...], g_ref[...])
    tm = hf.shape[0]
    hb = hf.astype(BF16)
    bits = pltpu.bitcast(hb.astype(F32), jnp.uint32)
    half_d = D_MODEL // 2
    for s in range(SUBLANES // 2):
        lo = bits[:, s * LANES:(s + 1) * LANES] >> 16
        hi = bits[:, half_d + s * LANES:half_d + (s + 1) * LANES] & jnp.uint32(0xFFFF0000)
        hf_ref[:, s, :] = lo | hi
        hf_ref[:, s + SUBLANES // 2, :] = lo | hi
    q = jnp.dot(hb, wq_ref[...], preferred_element_type=F32).astype(BF16)
    for g in range(N_SUB):
        blk = q[:, (g // 2) * LANES:(g // 2 + 1) * LANES]
        sc_scr[g] = lax.dot_general(keys_ref[g], blk, NT_DIMS, preferred_element_type=F32)

    key_id = lax.broadcasted_iota(I32, (PEER_N_KEYS, tm), 0)

    def sub_topk(g, carry):
        x = sc_scr[g]
        vals, ids = [], []
        for _ in range(PEER_TOPK):
            m = jnp.max(x, axis=0, keepdims=True)
            idx = jnp.min(jnp.where(x == m, key_id, PEER_N_KEYS), axis=0, keepdims=True)
            vals.append(m)
            ids.append(idx)
            x = jnp.where(key_id == idx, -jnp.inf, x)
        ts_scr[g] = jnp.concatenate(vals, axis=0)
        ti_scr[g] = jnp.concatenate(ids, axis=0)
        return carry

    lax.fori_loop(0, N_SUB, sub_topk, 0)

    iota16 = lax.broadcasted_iota(I32, (PEER_TOPK, tm), 0)
    iota8 = lax.broadcasted_iota(I32, (SUBLANES, tm), 0)
    lead_rows = SUBLANES

    def head_topk(h, carry):
        s0, s1 = ts_scr[2 * h], ts_scr[2 * h + 1]
        i0, i1 = ti_scr[2 * h], ti_scr[2 * h + 1]
        cands, codes = [], []
        for a in range(lead_rows):
            n_j = PEER_TOPK if a == 0 else SUBLANES
            cands.append(s0[a:a + 1] + s1[:n_j])
            j_iota = iota16 if a == 0 else iota8
            codes.append(((a * PEER_TOPK + j_iota) << EXPERT_BITS) | (i0[a:a + 1] * PEER_N_KEYS + i1[:n_j]))
        cands.append(s0[lead_rows:] + s1[0:1])
        codes.append((((iota8 + lead_rows) * PEER_TOPK) << EXPERT_BITS) | (i0[lead_rows:] * PEER_N_KEYS + i1[0:1]))
        cand = jnp.concatenate(cands, axis=0)
        code = jnp.concatenate(codes, axis=0)
        big = jnp.int32(2 ** 30)
        vals, ids = [], []
        for _ in range(PEER_TOPK):
            m = jnp.max(cand, axis=0, keepdims=True)
            best = jnp.min(jnp.where(cand == m, code, big), axis=0, keepdims=True)
            vals.append(m)
            ids.append(best & (2 ** EXPERT_BITS - 1))
            cand = jnp.where(code == best, -jnp.inf, cand)
        best_s = jnp.concatenate(vals, axis=0)
        e = jnp.exp(best_s - best_s[0:1])
        gate_ref[h] = e / jnp.sum(e, axis=0, keepdims=True)
        best_i = jnp.concatenate(ids, axis=0)
        idx_ref[h] = best_i
        row_ref[h] = (best_i >> 1) * SUBLANES
        return carry

    lax.fori_loop(0, PEER_HEADS, head_topk, 0)


def _route(x2, g, wq, keys_p, tm):
    n = x2.shape[0]
    return pl.pallas_call(
        _route_kernel,
        grid=(n // tm,),
        in_specs=[pl.BlockSpec((tm, D_MODEL), lambda i: (i, 0)), pl.BlockSpec(g.shape, lambda i: (0, 0)),
                  pl.BlockSpec(wq.shape, lambda i: (0, 0)), pl.BlockSpec(keys_p.shape, lambda i: (0, 0, 0))],
        out_specs=[pl.BlockSpec((tm, SUBLANES, LANES), lambda i: (i, 0, 0))]
        + [pl.BlockSpec((PEER_HEADS, PEER_TOPK, tm), lambda i: (0, 0, i))] * 3,
        out_shape=[jax.ShapeDtypeStruct((n, SUBLANES, LANES), jnp.uint32),
                   jax.ShapeDtypeStruct((PEER_HEADS, PEER_TOPK, n), I32),
                   jax.ShapeDtypeStruct((PEER_HEADS, PEER_TOPK, n), I32),
                   jax.ShapeDtypeStruct((PEER_HEADS, PEER_TOPK, n), F32)],
        scratch_shapes=[pltpu.VMEM((N_SUB, PEER_N_KEYS, tm), F32), pltpu.VMEM((N_SUB, PEER_TOPK, tm), F32),
                        pltpu.VMEM((N_SUB, PEER_TOPK, tm), I32)],
        compiler_params=pltpu.CompilerParams(dimension_semantics=("arbitrary",), vmem_limit_bytes=VMEM_LIMIT),
        name="peer_route",
    )(x2, g, wq, keys_p)


TILE_ROWS = 2 * SUBLANES
PAIRS_PER_KTILE = 16
TOK_UNROLL = 4


def _pack_table(t):
    e = t.shape[0]
    bits = lax.bitcast_convert_type(t.astype(BF16), jnp.uint16).astype(jnp.uint32)
    words = bits[:, :D_MODEL // 2] | (bits[:, D_MODEL // 2:] << 16)
    return words.reshape(e * (SUBLANES // 2), LANES)


def _gather_tiles(row_ref, tab_ref, t):
    tiles = []
    for k in range(PEER_SEL):
        start = pl.multiple_of(row_ref[t, k], SUBLANES)
        tiles.append(pltpu.bitcast(tab_ref[pl.ds(start, SUBLANES), :], BF16))
    return tiles


def _peer_dot_kernel(row_ref, tab_ref, h_ref, gate_ref, idxv_ref, sel_ref, ce_ref, co_ref,
                     ye_scr, yo_scr, ze_scr, zo_scr, *, tt):
    ones = jnp.ones((SUBLANES, LANES), BF16)
    half_rows = PAIRS_PER_KTILE
    n_trip = tt // TOK_UNROLL

    def row_sums(i, slot):
        for u in range(TOK_UNROLL):
            t = i * TOK_UNROLL + u
            hp = pltpu.bitcast(h_ref[t], BF16)
            prods = [tile * hp for tile in _gather_tiles(row_ref, tab_ref, t)]
            for kt in range(PEER_SEL // PAIRS_PER_KTILE):
                stack = jnp.concatenate(prods[kt * PAIRS_PER_KTILE:(kt + 1) * PAIRS_PER_KTILE], axis=0)
                y = jnp.dot(sel_ref[...], stack, preferred_element_type=F32)
                r0 = u * PEER_SEL + kt * half_rows
                ye_scr[slot, r0:r0 + half_rows, :] = y[:half_rows].astype(BF16)
                yo_scr[slot, r0:r0 + half_rows, :] = y[half_rows:].astype(BF16)

    def lane_sums(i, slot):
        ze = lax.dot_general(ones, ye_scr[slot], NT_DIMS, preferred_element_type=F32)
        zo = lax.dot_general(ones, yo_scr[slot], NT_DIMS, preferred_element_type=F32)
        for u in range(TOK_UNROLL):
            t = i * TOK_UNROLL + u
            ze_scr[pl.ds(t, 1), :] = ze[0:1, u * PEER_SEL:(u + 1) * PEER_SEL]
            zo_scr[pl.ds(t, 1), :] = zo[0:1, u * PEER_SEL:(u + 1) * PEER_SEL]

    def trip(i, carry):
        slot = i & 1
        lane_sums(i - 1, 1 - slot)
        row_sums(i, slot)
        return carry

    row_sums(0, 0)
    lax.fori_loop(1, n_trip, trip, 0)
    lane_sums(n_trip - 1, (n_trip - 1) & 1)
    even = (idxv_ref[...] & 1) == 0
    a = jnp.where(even, ze_scr[...], zo_scr[...])
    c = 0.5 * a * (1.0 + lax.erf(a * (2.0 ** -0.5))) * gate_ref[...]
    ce_ref[...] = jnp.where(even, c, 0.0)
    co_ref[...] = jnp.where(even, 0.0, c)


def _peer_dot(rows2, tab, hpk, gate, idx2, sel, tt):
    n = hpk.shape[0]
    kern = functools.partial(_peer_dot_kernel, tt=tt)
    rows = pl.BlockSpec((tt, PEER_SEL), lambda i: (i, 0))
    return pl.pallas_call(
        kern,
        grid=(n // tt,),
        in_specs=[pl.BlockSpec((tt, PEER_SEL), lambda i: (i, 0), memory_space=pltpu.SMEM),
                  pl.BlockSpec(memory_space=pltpu.VMEM),
                  pl.BlockSpec((tt, SUBLANES, LANES), lambda i: (i, 0, 0)),
                  rows, rows, pl.BlockSpec(sel.shape, lambda i: (0, 0))],
        out_specs=[rows, rows],
        out_shape=[jax.ShapeDtypeStruct((n, PEER_SEL), F32), jax.ShapeDtypeStruct((n, PEER_SEL), F32)],
        scratch_shapes=[pltpu.VMEM((2, TOK_UNROLL * PEER_SEL, LANES), BF16)] * 2
        + [pltpu.VMEM((tt, PEER_SEL), F32)] * 2,
        compiler_params=pltpu.CompilerParams(dimension_semantics=("arbitrary",), vmem_limit_bytes=VMEM_LIMIT),
        name="peer_dot",
    )(rows2, tab, hpk, gate, idx2, sel)


def _peer_sum_kernel(row_ref, ce_ref, co_ref, spread_ref, tab_ref, o_ref, m1_scr, m2_scr, *, tt):
    width = PEER_SEL * TILE_ROWS
    cc = jnp.concatenate([ce_ref[...], co_ref[...]], axis=1)
    c1 = cc.astype(BF16)
    c2 = (cc - c1.astype(F32)).astype(BF16)
    m1_scr[...] = jnp.dot(c1, spread_ref[...], preferred_element_type=F32)
    m2_scr[...] = jnp.dot(c2, spread_ref[...], preferred_element_type=F32)
    row = lax.broadcasted_iota(I32, (SUBLANES, width), 0)
    lane = lax.broadcasted_iota(I32, (SUBLANES, width), 1)
    half = SUBLANES // 2
    on_row = (lane & (SUBLANES - 1)) == 2 * (row % half) + row // half

    def token(t):
        w = jnp.concatenate(_gather_tiles(row_ref, tab_ref, t), axis=0)
        lhs = []
        for m_scr in (m1_scr, m2_scr):
            coef = jnp.broadcast_to(m_scr[pl.ds(t, 1), :], (SUBLANES, width))
            lhs.append(jnp.where(on_row, coef, 0.0).astype(BF16))
        res = jnp.dot(jnp.concatenate(lhs, axis=0), w, preferred_element_type=F32)
        o_ref[t] = res[:SUBLANES] + res[SUBLANES:]

    def trip(i, carry):
        for u in range(TOK_UNROLL):
            token(i * TOK_UNROLL + u)
        return carry

    lax.fori_loop(0, tt // TOK_UNROLL, trip, 0)


def _peer_sum(rows2, ce, co, spread, tab, tt):
    n = ce.shape[0]
    kern = functools.partial(_peer_sum_kernel, tt=tt)
    rows = pl.BlockSpec((tt, PEER_SEL), lambda i: (i, 0))
    width = PEER_SEL * TILE_ROWS
    return pl.pallas_call(
        kern,
        grid=(n // tt,),
        in_specs=[pl.BlockSpec((tt, PEER_SEL), lambda i: (i, 0), memory_space=pltpu.SMEM), rows, rows,
                  pl.BlockSpec(spread.shape, lambda i: (0, 0)), pl.BlockSpec(memory_space=pltpu.VMEM)],
        out_specs=pl.BlockSpec((tt, SUBLANES, LANES), lambda i: (i, 0, 0)),
        out_shape=jax.ShapeDtypeStruct((n, SUBLANES, LANES), F32),
        scratch_shapes=[pltpu.VMEM((tt, width), F32), pltpu.VMEM((tt, width), F32)],
        compiler_params=pltpu.CompilerParams(dimension_semantics=("arbitrary",), vmem_limit_bytes=VMEM_LIMIT),
        name="peer_sum",
    )(rows2, ce, co, spread, tab)


def _final_kernel(x_ref, p_ref, g_ref, o_ref):
    peer = jnp.concatenate([p_ref[:, s, :] for s in range(SUBLANES)], axis=1)
    o_ref[...] = _rms(x_ref[...] + peer, g_ref[...])


def _final(x2, peer3, g, tm):
    n = x2.shape[0]
    return pl.pallas_call(
        _final_kernel,
        grid=(n // tm,),
        in_specs=[pl.BlockSpec((tm, D_MODEL), lambda i: (i, 0)),
                  pl.BlockSpec((tm, SUBLANES, LANES), lambda i: (i, 0, 0)),
                  pl.BlockSpec(g.shape, lambda i: (0, 0))],
        out_specs=pl.BlockSpec((tm, D_MODEL), lambda i: (i, 0)),
        out_shape=jax.ShapeDtypeStruct((n, D_MODEL), F32),
        compiler_params=pltpu.CompilerParams(dimension_semantics=("arbitrary",), vmem_limit_bytes=VMEM_LIMIT),
        name="final_norm",
    )(x2, peer3, g)


def _tile(n, pref):
    t = pref
    while n % t:
        t //= 2
    return t


def _layer(l, x2, mem2, pos, invf, b, s_len, m_tok, norm_mix_g, w_in, da_lambda, da_subln_g, w_branch_a,
           w_branch_b, gate_bias, w_out, norm_mem_g, mem_kv_norm_g, w_mem_q, w_mem_kv, w_mem_o, norm_ffn_g,
           peer_w_q, peer_sub_keys, peer_u, peer_v):
    n = b * s_len
    row2 = lambda v: v.reshape(1, -1)
    splits = [0]
    for c in COL_SIZES:
        splits.append(splits[-1] + c)
    da_q, da_k, da_v, ds_q, ds_k, ds_v, ix_q, ix_k, ix_w, gates = (
        w_in[l][:, splits[j]:splits[j + 1]] for j in range(len(COL_SIZES)))
    k0, k1 = ds_k[:, :DS_HEAD_DIM], ds_k[:, DS_HEAD_DIM:]
    w_keys = jnp.concatenate([da_k, k0, k0, k1, k1, ix_k, ix_k], axis=1).astype(BF16)
    w_q_t = jnp.concatenate([da_q, ds_q, ix_q], axis=1).T.astype(BF16)
    w_v_t = jnp.concatenate([da_v, ds_v], axis=1).T.astype(BF16)
    t_att = _tile(s_len, 256)
    pr, gate, qt, vt3, iwt = _in_proj(x2, row2(norm_mix_g[l]), pos, invf, pos.reshape(1, n),
                                      invf[:, :ROPE_HALF].reshape(ROPE_HALF, 1), row2(gate_bias[l]), w_keys,
                                      gates.astype(BF16), w_q_t, w_v_t, ix_w.T.astype(BF16), t_att)

    lam_init = 0.8 - 0.6 * math.exp(-0.3 * l)
    o_a = _diff_attention(da_lambda[l], qt, pr, vt3, da_subln_g[l].reshape(-1, 1), b, s_len, t_att, lam_init)
    o_b = _dsa_attention(qt, pr, vt3, iwt, b, s_len, t_att)
    x2 = _merge(o_a, o_b, gate, x2, w_branch_a[l].astype(BF16), w_branch_b[l].astype(BF16),
                w_out[l].astype(BF16), _tile(n, 256))

    kv = _norm_matmul(mem2, row2(mem_kv_norm_g[l]), w_mem_kv[l].astype(BF16), _tile(mem2.shape[0], 256))
    x2 = _mem_attn(x2, row2(norm_mem_g[l]), w_mem_q[l].astype(BF16), kv, w_mem_o[l].astype(BF16),
                   b, s_len, m_tok, _tile(s_len, 256))

    sk = peer_sub_keys[l].reshape(N_SUB, PEER_N_KEYS, PEER_HALF)
    z = jnp.zeros_like(sk)
    keys_p = jnp.where((jnp.arange(N_SUB) % 2 == 0)[:, None, None],
                       jnp.concatenate([sk, z], axis=2), jnp.concatenate([z, sk], axis=2)).astype(BF16)
    hpk, idx_t, row_t, gate_t = _route(x2, row2(norm_ffn_g[l]), peer_w_q[l].astype(BF16), keys_p, _tile(n, 256))
    idx2 = idx_t.reshape(PEER_SEL, n).T
    rows2 = row_t.reshape(PEER_SEL, n).T
    gate2 = gate_t.reshape(PEER_SEL, n).T
    tt = _tile(n, 128)
    r32 = jnp.arange(2 * PAIRS_PER_KTILE)[:, None]
    c256 = jnp.arange(PAIRS_PER_KTILE * TILE_ROWS)[None, :]
    sel = ((c256 // TILE_ROWS == r32 % PAIRS_PER_KTILE)
           & ((c256 % TILE_ROWS) // SUBLANES == r32 // PAIRS_PER_KTILE)).astype(BF16)
    k256 = jnp.arange(2 * PEER_SEL)[:, None]
    c2048 = jnp.arange(PEER_SEL * TILE_ROWS)[None, :]
    spread = ((c2048 // TILE_ROWS == k256 % PEER_SEL)
              & ((c2048 % TILE_ROWS) // SUBLANES == k256 // PEER_SEL)).astype(BF16)
    ce2, co2 = _peer_dot(rows2, _pack_table(peer_u[l]), hpk, gate2, idx2, sel, tt)
    peer3 = _peer_sum(rows2, ce2, co2, spread, _pack_table(peer_v[l]), tt)
    return x2, peer3


def kernel(x, mem, positions, norm_mix_g, w_in, da_lambda, da_subln_g, w_branch_a, w_branch_b, gate_bias, w_out, norm_mem_g, mem_kv_norm_g, w_mem_q, w_mem_kv, w_mem_o, norm_ffn_g, peer_w_q, peer_sub_keys, peer_u, peer_v, final_norm_g):
    b, s_len, d = x.shape
    m_tok = mem.shape[1]
    n = b * s_len
    depth = w_in.shape[0]
    x2 = x.reshape(n, d)
    mem2 = mem.reshape(b * m_tok, d)
    pos = positions.astype(F32).reshape(n, 1)
    inv_freq = ROPE_THETA ** (-(jnp.arange(ROPE_HALF, dtype=F32) * 2.0) / ROPE_DIM)
    invf = jnp.tile(inv_freq, LANES // ROPE_HALF).reshape(1, LANES)
    peer3 = None
    for l in range(depth):
        if peer3 is not None:
            x2 = x2 + peer3.reshape(n, d)
        x2, peer3 = _layer(l, x2, mem2, pos, invf, b, s_len, m_tok, norm_mix_g, w_in, da_lambda, da_subln_g,
                           w_branch_a, w_branch_b, gate_bias, w_out, norm_mem_g, mem_kv_norm_g, w_mem_q, w_mem_kv,
                           w_mem_o, norm_ffn_g, peer_w_q, peer_sub_keys, peer_u, peer_v)
    out = _final(x2, peer3, final_norm_g.reshape(1, d), _tile(n, 256))
    return out.reshape(b, s_len, d)
```

```python
import functools
import math

import jax
import jax.numpy as jnp
from jax import lax
from jax.experimental import pallas as pl
from jax.experimental.pallas import tpu as pltpu

F32 = jnp.float32
BF16 = jnp.bfloat16
I32 = jnp.int32

D_MODEL = 1024
EPS = 1e-6
ROPE_THETA = 500000.0
ROT_HEAD_DIM = 64
ROPE_DIM = 16
ROPE_HALF = ROPE_DIM // 2

DA_HEADS = 4
DA_QK_DIM = 64
DA_V_DIM = 128
DA_WIDTH = 512
DS_HEADS = 8
DS_KV_HEADS = 2
DS_HEAD_DIM = 64
DS_WIDTH = 512
IDX_HEADS = 8
IDX_DIM = 64
TOPK_MAX = 256
MEM_HEADS = 4
MEM_HEAD_DIM = 128
MEM_WIDTH = 512
PEER_HEADS = 8
PEER_N_KEYS = 128
PEER_HALF = 64
PEER_TOPK = 16
PEER_SEL = PEER_HEADS * PEER_TOPK

COL_SIZES = (512, 512, 512, 512, 128, 128, 512, 64, 8, 2048)

LANES = 128
SUBLANES = 8
NEG_BIG = -1e30
INT_MIN = -(2 ** 31)
VMEM_LIMIT = 56 * 1024 * 1024

NT_DIMS = (((1,), (1,)), ((), ()))


def _rms(x, g):
    var = jnp.mean(x * x, axis=-1, keepdims=True)
    return x * lax.rsqrt(var + EPS) * g


def _rope_rows(y, cos8, sin8):
    pieces = []
    for r0 in range(0, y.shape[0], ROT_HEAD_DIM):
        t1 = y[r0:r0 + ROPE_HALF]
        t2 = y[r0 + ROPE_HALF:r0 + ROPE_DIM]
        pieces += [t1 * cos8 - t2 * sin8, t2 * cos8 + t1 * sin8, y[r0 + ROPE_DIM:r0 + ROT_HEAD_DIM]]
    return jnp.concatenate(pieces, axis=0)


def _in_proj_kernel(x_ref, g_ref, pos_ref, invf_ref, pos_t_ref, invf8_ref, bias_ref, wr_ref, wg_ref,
                    wqt_ref, wvt_ref, wwt_ref, pr_ref, gate_ref, qt_ref, vt_ref, iwt_ref):
    h = _rms(x_ref[...], g_ref[...]).astype(BF16)
    tm = h.shape[0]
    ang_t = invf8_ref[...] * pos_t_ref[...]
    cos8 = jnp.cos(ang_t)
    sin8 = jnp.sin(ang_t)
    row_chunk = 2 * LANES
    for r0 in range(0, wqt_ref.shape[0], row_chunk):
        y = lax.dot_general(wqt_ref[r0:r0 + row_chunk, :], h, NT_DIMS, preferred_element_type=F32)
        qt_ref[r0:r0 + row_chunk, :] = _rope_rows(y, cos8, sin8).astype(qt_ref.dtype)
    for r0 in range(0, wvt_ref.shape[0], row_chunk):
        r1 = min(r0 + row_chunk, wvt_ref.shape[0])
        y = lax.dot_general(wvt_ref[r0:r1, :], h, NT_DIMS, preferred_element_type=F32)
        vt_ref[0, r0:r1, :] = y.astype(vt_ref.dtype)
    iwt_ref[...] = lax.dot_general(wwt_ref[...], h, NT_DIMS, preferred_element_type=F32)
    ang = pos_ref[...] * invf_ref[...]
    cos = jnp.cos(ang)
    sin = jnp.sin(ang)
    lane = lax.broadcasted_iota(I32, (tm, LANES), 1) % ROT_HEAD_DIM
    c_t = jnp.where(lane < ROPE_DIM, cos, 1.0)
    s_lo = jnp.where(lane < ROPE_HALF, -sin, 0.0)
    s_hi = jnp.where((lane >= ROPE_HALF) & (lane < ROPE_DIM), sin, 0.0)
    n_r = wr_ref.shape[1]
    for c0 in range(0, n_r, 512):
        c1 = min(c0 + 512, n_r)
        w = c1 - c0
        y = jnp.dot(h, wr_ref[:, c0:c1], preferred_element_type=F32)
        reps = w // LANES
        ct = jnp.concatenate([c_t] * reps, axis=1)
        sl = jnp.concatenate([s_lo] * reps, axis=1)
        sh = jnp.concatenate([s_hi] * reps, axis=1)
        y = y * ct + pltpu.roll(y, w - ROPE_HALF, 1) * sl + pltpu.roll(y, ROPE_HALF, 1) * sh
        pr_ref[:, c0:c1] = y.astype(pr_ref.dtype)
    n_g = wg_ref.shape[1]
    for c0 in range(0, n_g, 512):
        y = jnp.dot(h, wg_ref[:, c0:c0 + 512], preferred_element_type=F32)
        gate_ref[:, c0:c0 + 512] = jax.nn.sigmoid(y + bias_ref[:, c0:c0 + 512])


def _in_proj(x2, g, pos, invf, pos_t, invf8, bias, wr, wg, wqt, wvt, wwt, tm):
    n = x2.shape[0]
    full = lambda a: pl.BlockSpec(a.shape, lambda i: (0, 0))
    rows = lambda w: pl.BlockSpec((tm, w), lambda i: (i, 0))
    cols = lambda r: pl.BlockSpec((r, tm), lambda i: (0, i))
    return pl.pallas_call(
        _in_proj_kernel,
        grid=(n // tm,),
        in_specs=[rows(D_MODEL), full(g), rows(1), full(invf), cols(1), full(invf8), full(bias), full(wr), full(wg),
                  full(wqt), full(wvt), full(wwt)],
        out_specs=[rows(wr.shape[1]), rows(wg.shape[1]), cols(wqt.shape[0]),
                   pl.BlockSpec((1, wvt.shape[0], tm), lambda i: (i, 0, 0)), cols(wwt.shape[0])],
        out_shape=[jax.ShapeDtypeStruct((n, wr.shape[1]), BF16), jax.ShapeDtypeStruct((n, wg.shape[1]), F32),
                   jax.ShapeDtypeStruct((wqt.shape[0], n), BF16),
                   jax.ShapeDtypeStruct((n // tm, wvt.shape[0], tm), BF16),
                   jax.ShapeDtypeStruct((wwt.shape[0], n), F32)],
        compiler_params=pltpu.CompilerParams(dimension_semantics=("arbitrary",), vmem_limit_bytes=VMEM_LIMIT),
        name="in_proj",
    )(x2, g, pos, invf, pos_t, invf8, bias, wr, wg, wqt, wvt, wwt)


QT_DA = 0
QT_DS = 4
QT_IX = 8
VT_DA = 0
VT_DS = 4
PR_DAK = 0
PR_DSK = 4
PR_IXK = 6


SUB_KEYS = 128
QK_AHEAD = 2


def _zero_after(x):
    bits = pltpu.bitcast(x, jnp.uint32)
    return pltpu.bitcast((bits >> 16) >> 16, F32)


def _flash_chains(n_chains, score_fn, value_fn, subs, m_scr, l_scr, acc_scr):
    pending = [score_fn(c) for c in range(QK_AHEAD)]
    for c in range(n_chains):
        scores = pending.pop(0)
        m = m_scr[c][0:1]
        if c + QK_AHEAD < n_chains:
            pending.append(score_fn(c + QK_AHEAD))
            m = m + _zero_after(pending[-1][0][0:1])
        m_new = m
        for s in scores:
            m_new = jnp.maximum(m_new, jnp.max(s, axis=0, keepdims=True))
        alpha = jnp.exp(m - m_new)
        l = alpha * l_scr[c][0:1]
        a = alpha * acc_scr[c]
        for r0, s in zip(subs, scores):
            p = jnp.exp(s - m_new)
            l = l + jnp.sum(p, axis=0, keepdims=True)
            v = value_fn(c, r0)
            a = a + jnp.dot(v, p.astype(v.dtype), preferred_element_type=F32)
        m_scr[c] = jnp.broadcast_to(m_new, m_scr.shape[1:])
        l_scr[c] = jnp.broadcast_to(l, l_scr.shape[1:])
        acc_scr[c] = a


def _da_kernel(lam_ref, qt_ref, k_ref, vt_ref, g_ref, o_ref, q_scr, acc_scr, m_scr, l_scr, *, tq, lam_init):
    tk = tq
    i = pl.program_id(1)
    n_chains = 2 * DA_HEADS
    frow = lax.broadcasted_iota(I32, (LANES, tq), 0)
    scale = jnp.asarray(DA_QK_DIM ** -0.5, BF16)
    for h in range(DA_HEADS):
        blk = qt_ref[h * LANES:(h + 1) * LANES, :] * scale
        zero = jnp.zeros_like(blk)
        q_scr[2 * h] = jnp.where(frow < DA_QK_DIM, blk, zero)
        q_scr[2 * h + 1] = jnp.where(frow >= DA_QK_DIM, blk, zero)
    acc_scr[...] = jnp.zeros(acc_scr.shape, F32)
    m_scr[...] = jnp.full(m_scr.shape, NEG_BIG, F32)
    l_scr[...] = jnp.zeros(l_scr.shape, F32)
    kpos0 = lax.broadcasted_iota(I32, (SUB_KEYS, tq), 0)
    qpos = i * tq + lax.broadcasted_iota(I32, (SUB_KEYS, tq), 1)
    subs = range(0, tk, SUB_KEYS)

    def chunk(j, masked):
        start = pl.multiple_of(j * tk, tk)

        def scores(c):
            head = c // 2
            out = []
            for r0 in subs:
                k = k_ref[pl.ds(start + r0, SUB_KEYS), head * LANES:(head + 1) * LANES]
                s = jnp.dot(k, q_scr[c], preferred_element_type=F32)
                if masked:
                    s = jnp.where(kpos0 + (j * tk + r0) <= qpos, s, NEG_BIG)
                out.append(s)
            return out

        def values(c, r0):
            head = c // 2
            return vt_ref[j, head * LANES:(head + 1) * LANES, r0:r0 + SUB_KEYS]

        _flash_chains(n_chains, scores, values, subs, m_scr, l_scr, acc_scr)

    def full_chunk(j, carry):
        chunk(j, False)
        return carry

    lax.fori_loop(0, i, full_chunk, 0)
    chunk(i, True)

    lp = lam_ref[...]
    lam = (jnp.exp(jnp.sum(lp[0:1] * lp[1:2], axis=1, keepdims=True))
           - jnp.exp(jnp.sum(lp[2:3] * lp[3:4], axis=1, keepdims=True)) + lam_init)
    for h in range(DA_HEADS):
        o = acc_scr[2 * h] / l_scr[2 * h][0:1] - lam * (acc_scr[2 * h + 1] / l_scr[2 * h + 1][0:1])
        var = jnp.mean(o * o, axis=0, keepdims=True)
        o = o * lax.rsqrt(var + EPS) * g_ref[...] * (1.0 - lam_init)
        o_ref[:, h * LANES:(h + 1) * LANES] = o.T.astype(o_ref.dtype)


def _diff_attention(lam_p, qt, pr, vt3, subln_g, b, s_len, tq, lam_init):
    nq = s_len // tq
    n = b * s_len
    n_chains = 2 * DA_HEADS
    kern = functools.partial(_da_kernel, tq=tq, lam_init=lam_init)
    return pl.pallas_call(
        kern,
        grid=(b, nq),
        in_specs=[
            pl.BlockSpec(lam_p.shape, lambda bi, i: (0, 0)),
            pl.BlockSpec((DA_HEADS * LANES, tq), lambda bi, i: (QT_DA // DA_HEADS, bi * nq + i)),
            pl.BlockSpec((s_len, DA_HEADS * LANES), lambda bi, i: (bi, PR_DAK // DA_HEADS)),
            pl.BlockSpec((s_len // tq, DA_HEADS * LANES, tq), lambda bi, i: (bi, VT_DA // DA_HEADS, 0)),
            pl.BlockSpec(subln_g.shape, lambda bi, i: (0, 0)),
        ],
        out_specs=pl.BlockSpec((tq, DA_WIDTH), lambda bi, i: (bi * nq + i, 0)),
        out_shape=jax.ShapeDtypeStruct((n, DA_WIDTH), BF16),
        scratch_shapes=[pltpu.VMEM((n_chains, LANES, tq), BF16), pltpu.VMEM((n_chains, DA_V_DIM, tq), F32),
                        pltpu.VMEM((n_chains, SUBLANES, tq), F32), pltpu.VMEM((n_chains, SUBLANES, tq), F32)],
        compiler_params=pltpu.CompilerParams(dimension_semantics=("arbitrary",) * 2, vmem_limit_bytes=VMEM_LIMIT),
        name="diff_attn",
    )(lam_p, qt, pr, vt3, subln_g)


def _dsa_kernel(qit_ref, ki_ref, wt_ref, qt_ref, k0_ref, k1_ref, vt_ref, o_ref,
                keys_scr, qi_scr, q_scr, acc_scr, m_scr, l_scr, cut_scr, *, tq, n_sel, n_bits):
    tk = tq
    i = pl.program_id(1)
    nck = i + 1
    kpos0 = lax.broadcasted_iota(I32, (tk, tq), 0)
    qpos = i * tq + lax.broadcasted_iota(I32, (tk, tq), 1)
    frow = lax.broadcasted_iota(I32, (LANES, tq), 0)
    per_grp = DS_HEADS // DS_KV_HEADS

    scale = jnp.asarray(DS_HEAD_DIM ** -0.5, BF16)
    for h in range(DS_HEADS):
        keep = (frow < DS_HEAD_DIM) if h % 2 == 0 else (frow >= DS_HEAD_DIM)
        rows = slice((h // 2) * LANES, (h // 2 + 1) * LANES)
        blk = qit_ref[rows, :]
        qi_scr[h] = jnp.where(keep, blk, jnp.zeros_like(blk))
        blk = qt_ref[rows, :] * scale
        q_scr[h] = jnp.where(keep, blk, jnp.zeros_like(blk))

    def score_chunk(c, carry):
        kc = ki_ref[pl.ds(pl.multiple_of(c * tk, tk), tk), :]
        acc = jnp.zeros((tk, tq), F32)
        for h in range(IDX_HEADS):
            s = jnp.dot(kc, qi_scr[h], preferred_element_type=F32)
            acc = acc + wt_ref[h:h + 1, :] * jnp.maximum(s, 0.0)
        bits = pltpu.bitcast(acc, I32)
        key = bits ^ ((bits >> 31) & 0x7FFFFFFF)
        key = jnp.where(acc == 0.0, 0, key)
        keys_scr[c] = jnp.where(kpos0 + c * tk <= qpos, key, INT_MIN)
        return carry

    lax.fori_loop(0, nck, score_chunk, 0)

    def count(pred):
        def body(c, acc):
            hit = pred(keys_scr[c], kpos0 + c * tk)
            return acc + jnp.sum(hit.reshape(tk // SUBLANES, SUBLANES, tq), axis=0)
        acc = lax.fori_loop(0, nck, body, jnp.zeros((SUBLANES, tq), I32))
        return jnp.sum(acc, axis=0, keepdims=True)

    def thr_bit(t, ans_u):
        cand_u = ans_u | jnp.left_shift(jnp.int32(1), 31 - t)
        cand = cand_u ^ INT_MIN
        cnt = count(lambda k, kpos: jnp.where(k >= cand, 1, 0))
        return jnp.where(cnt >= n_sel, cand_u, ans_u)

    thr = lax.fori_loop(0, 32, thr_bit, jnp.zeros((1, tq), I32)) ^ INT_MIN
    n_ge = count(lambda k, kpos: jnp.where(k >= thr, 1, 0))
    below_all = thr == INT_MIN
    cut_scr[...] = jnp.where(below_all, -1, (1 << n_bits) - 1)
    excess = jnp.max(jnp.where(below_all, 0, n_ge - n_sel))

    @pl.when(excess > 0)
    def _():
        n_gt = count(lambda k, kpos: jnp.where(k > thr, 1, 0))
        need = n_sel - n_gt

        def tie_bit(t, cut):
            cand = cut | jnp.left_shift(jnp.int32(1), n_bits - 1 - t)
            cnt = count(lambda k, kpos: jnp.where(k == thr, jnp.where(kpos < cand, 1, 0), 0))
            return jnp.where(cnt < need, cand, cut)

        cut = lax.fori_loop(0, n_bits, tie_bit, jnp.zeros((1, tq), I32))
        cut_scr[...] = jnp.where(below_all, -1, cut)

    cut = cut_scr[...]

    def bias_chunk(c, carry):
        k = keys_scr[c]
        sel = (k > thr) | ((k == thr) & (kpos0 + c * tk <= cut))
        keys_scr[c] = pltpu.bitcast(jnp.where(sel, 0.0, NEG_BIG).astype(F32), I32)
        return carry

    lax.fori_loop(0, nck, bias_chunk, 0)

    acc_scr[...] = jnp.zeros(acc_scr.shape, F32)
    m_scr[...] = jnp.full(m_scr.shape, NEG_BIG, F32)
    l_scr[...] = jnp.zeros(l_scr.shape, F32)

    def att_chunk(c, carry):
        start = pl.multiple_of(c * tk, tk)
        subs = range(0, tk, SUB_KEYS)

        def head_scores(h):
            k_ref = k0_ref if h // per_grp == 0 else k1_ref
            out = []
            for r0 in subs:
                s = jnp.dot(k_ref[pl.ds(start + r0, SUB_KEYS), :], q_scr[h], preferred_element_type=F32)
                out.append(s + pltpu.bitcast(keys_scr[c, r0:r0 + SUB_KEYS, :], F32))
            return out

        def head_values(h, r0):
            grp = h // per_grp
            return vt_ref[c, grp * DS_HEAD_DIM:(grp + 1) * DS_HEAD_DIM, r0:r0 + SUB_KEYS]

        _flash_chains(DS_HEADS, head_scores, head_values, subs, m_scr, l_scr, acc_scr)
        return carry

    lax.fori_loop(0, nck, att_chunk, 0)
    outs = [acc_scr[h] / l_scr[h][0:1] for h in range(DS_HEADS)]
    o_ref[...] = jnp.concatenate(outs, axis=0).T.astype(o_ref.dtype)


def _dsa_attention(qt, pr, vt3, iwt, b, s_len, tq):
    nq = s_len // tq
    n = b * s_len
    n_sel = min(TOPK_MAX, s_len // 4)
    n_bits = max(1, (s_len - 1).bit_length())
    kern = functools.partial(_dsa_kernel, tq=tq, n_sel=n_sel, n_bits=n_bits)
    seq = lambda blk: pl.BlockSpec((s_len, LANES), lambda bi, i: (bi, blk))
    qblk = lambda blk: pl.BlockSpec((4 * LANES, tq), lambda bi, i: (blk // 4, bi * nq + i))
    return pl.pallas_call(
        kern,
        grid=(b, nq),
        in_specs=[
            qblk(QT_IX), seq(PR_IXK),
            pl.BlockSpec((IDX_HEADS, tq), lambda bi, i: (0, bi * nq + i)),
            qblk(QT_DS), seq(PR_DSK), seq(PR_DSK + 1),
            pl.BlockSpec((s_len // tq, LANES, tq), lambda bi, i: (bi, VT_DS, 0)),
        ],
        out_specs=pl.BlockSpec((tq, DS_WIDTH), lambda bi, i: (bi * nq + i, 0)),
        out_shape=jax.ShapeDtypeStruct((n, DS_WIDTH), BF16),
        scratch_shapes=[pltpu.VMEM((nq, tq, tq), I32),
                        pltpu.VMEM((IDX_HEADS, LANES, tq), BF16), pltpu.VMEM((DS_HEADS, LANES, tq), BF16),
                        pltpu.VMEM((DS_HEADS, DS_HEAD_DIM, tq), F32),
                        pltpu.VMEM((DS_HEADS, SUBLANES, tq), F32), pltpu.VMEM((DS_HEADS, SUBLANES, tq), F32),
                        pltpu.VMEM((1, tq), I32)],
        compiler_params=pltpu.CompilerParams(dimension_semantics=("arbitrary",) * 2, vmem_limit_bytes=VMEM_LIMIT),
        name="dsa_attn",
    )(qt, pr, iwt, qt, pr, pr, vt3)


def _merge_kernel(oa_ref, ob_ref, gate_ref, x_ref, wa_ref, wb_ref, wo_ref, o_ref):
    ya = jnp.dot(oa_ref[...], wa_ref[...], preferred_element_type=F32)
    yb = jnp.dot(ob_ref[...], wb_ref[...], preferred_element_type=F32)
    mix = gate_ref[:, :D_MODEL] * ya + gate_ref[:, D_MODEL:] * yb
    o_ref[...] = x_ref[...] + jnp.dot(mix.astype(BF16), wo_ref[...], preferred_element_type=F32)


def _merge(oa, ob, gate, x2, wa, wb, wo, tm):
    n = x2.shape[0]
    full = lambda a: pl.BlockSpec(a.shape, lambda i: (0, 0))
    rows = lambda w: pl.BlockSpec((tm, w), lambda i: (i, 0))
    return pl.pallas_call(
        _merge_kernel,
        grid=(n // tm,),
        in_specs=[rows(DA_WIDTH), rows(DS_WIDTH), rows(2 * D_MODEL), rows(D_MODEL), full(wa), full(wb), full(wo)],
        out_specs=rows(D_MODEL),
        out_shape=jax.ShapeDtypeStruct((n, D_MODEL), F32),
        compiler_params=pltpu.CompilerParams(dimension_semantics=("arbitrary",), vmem_limit_bytes=VMEM_LIMIT),
        name="merge",
    )(oa, ob, gate, x2, wa, wb, wo)


def _norm_matmul_kernel(x_ref, g_ref, w_ref, o_ref):
    h = _rms(x_ref[...], g_ref[...]).astype(BF16)
    o_ref[...] = jnp.dot(h, w_ref[...], preferred_element_type=F32).astype(o_ref.dtype)


def _norm_matmul(x2, g, w, tm):
    n = x2.shape[0]
    return pl.pallas_call(
        _norm_matmul_kernel,
        grid=(n // tm,),
        in_specs=[pl.BlockSpec((tm, x2.shape[1]), lambda i: (i, 0)), pl.BlockSpec(g.shape, lambda i: (0, 0)),
                  pl.BlockSpec(w.shape, lambda i: (0, 0))],
        out_specs=pl.BlockSpec((tm, w.shape[1]), lambda i: (i, 0)),
        out_shape=jax.ShapeDtypeStruct((n, w.shape[1]), BF16),
        compiler_params=pltpu.CompilerParams(dimension_semantics=("arbitrary",), vmem_limit_bytes=VMEM_LIMIT),
        name="mem_kv_proj",
    )(x2, g, w)


def _mem_attn_kernel(x_ref, g_ref, wq_ref, kv_ref, wo_ref, o_ref):
    x = x_ref[...]
    hn = _rms(x, g_ref[...]).astype(BF16)
    q = jnp.dot(hn, wq_ref[...], preferred_element_type=F32).astype(BF16)
    scale = MEM_HEAD_DIM ** -0.5
    heads = []
    for h in range(MEM_HEADS):
        qh = q[:, h * MEM_HEAD_DIM:(h + 1) * MEM_HEAD_DIM]
        kh = kv_ref[:, h * MEM_HEAD_DIM:(h + 1) * MEM_HEAD_DIM]
        vh = kv_ref[:, MEM_WIDTH + h * MEM_HEAD_DIM:MEM_WIDTH + (h + 1) * MEM_HEAD_DIM]
        s = lax.dot_general(qh, kh, NT_DIMS, preferred_element_type=F32) * scale
        m = jnp.max(s, axis=1, keepdims=True)
        p = jnp.exp(s - m)
        l = jnp.sum(p, axis=1, keepdims=True)
        heads.append(jnp.dot(p.astype(BF16), vh, preferred_element_type=F32) / l)
    o = jnp.concatenate(heads, axis=1).astype(BF16)
    o_ref[...] = x + jnp.dot(o, wo_ref[...], preferred_element_type=F32)


def _mem_attn(x2, g, wq, kv, wo, b, s_len, m_tok, tm):
    nb = s_len // tm
    n = x2.shape[0]
    full = lambda a: pl.BlockSpec(a.shape, lambda bi, i: (0, 0))
    return pl.pallas_call(
        _mem_attn_kernel,
        grid=(b, nb),
        in_specs=[pl.BlockSpec((tm, D_MODEL), lambda bi, i: (bi * nb + i, 0)), full(g), full(wq),
                  pl.BlockSpec((m_tok, 2 * MEM_WIDTH), lambda bi, i: (bi, 0)), full(wo)],
        out_specs=pl.BlockSpec((tm, D_MODEL), lambda bi, i: (bi * nb + i, 0)),
        out_shape=jax.ShapeDtypeStruct((n, D_MODEL), F32),
        compiler_params=pltpu.CompilerParams(dimension_semantics=("arbitrary",) * 2, vmem_limit_bytes=VMEM_LIMIT),
        name="mem_attn",
    )(x2, g, wq, kv, wo)


N_SUB = 2 * PEER_HEADS
EXPERT_BITS = 14


def _route_kernel(x_ref, g_ref, wq_ref, keys_ref, hf_ref, idx_ref, row_ref, gate_ref, sc_scr, ts_scr, ti_scr):
    hf = _rms(x_ref[...], g_ref[...])
    tm = hf.shape[0]
    hb = hf.astype(BF16)
    bits = pltpu.bitcast(hb.astype(F32), jnp.uint32)
    half_d = D_MODEL // 2
    for s in range(SUBLANES // 2):
        lo = bits[:, s * LANES:(s + 1) * LANES] >> 16
        hi = bits[:, half_d + s * LANES:half_d + (s + 1) * LANES] & jnp.uint32(0xFFFF0000)
        hf_ref[:, s, :] = lo | hi
        hf_ref[:, s + SUBLANES // 2, :] = lo | hi
    q = jnp.dot(hb, wq_ref[...], preferred_element_type=F32).astype(BF16)
    for g in range(N_SUB):
        blk = q[:, (g // 2) * LANES:(g // 2 + 1) * LANES]
        sc_scr[g] = lax.dot_general(keys_ref[g], blk, NT_DIMS, preferred_element_type=F32)

    key_id = lax.broadcasted_iota(I32, (PEER_N_KEYS, tm), 0)

    def sub_topk(g, carry):
        x = sc_scr[g]
        vals, ids = [], []
        for _ in range(PEER_TOPK):
            m = jnp.max(x, axis=0, keepdims=True)
            idx = jnp.min(jnp.where(x == m, key_id, PEER_N_KEYS), axis=0, keepdims=True)
            vals.append(m)
            ids.append(idx)
            x = jnp.where(key_id == idx, -jnp.inf, x)
        ts_scr[g] = jnp.concatenate(vals, axis=0)
        ti_scr[g] = jnp.concatenate(ids, axis=0)
        return carry

    lax.fori_loop(0, N_SUB, sub_topk, 0)

    iota16 = lax.broadcasted_iota(I32, (PEER_TOPK, tm), 0)
    iota8 = lax.broadcasted_iota(I32, (SUBLANES, tm), 0)
    lead_rows = SUBLANES

    def head_topk(h, carry):
        s0, s1 = ts_scr[2 * h], ts_scr[2 * h + 1]
        i0, i1 = ti_scr[2 * h], ti_scr[2 * h + 1]
        cands, codes = [], []
        for a in range(lead_rows):
            n_j = PEER_TOPK if a == 0 else SUBLANES
            cands.append(s0[a:a + 1] + s1[:n_j])
            j_iota = iota16 if a == 0 else iota8
            codes.append(((a * PEER_TOPK + j_iota) << EXPERT_BITS) | (i0[a:a + 1] * PEER_N_KEYS + i1[:n_j]))
        cands.append(s0[lead_rows:] + s1[0:1])
        codes.append((((iota8 + lead_rows) * PEER_TOPK) << EXPERT_BITS) | (i0[lead_rows:] * PEER_N_KEYS + i1[0:1]))
        cand = jnp.concatenate(cands, axis=0)
        code = jnp.concatenate(codes, axis=0)
        big = jnp.int32(2 ** 30)
        vals, ids = [], []
        for _ in range(PEER_TOPK):
            m = jnp.max(cand, axis=0, keepdims=True)
            best = jnp.min(jnp.where(cand == m, code, big), axis=0, keepdims=True)
            vals.append(m)
            ids.append(best & (2 ** EXPERT_BITS - 1))
            cand = jnp.where(code == best, -jnp.inf, cand)
        best_s = jnp.concatenate(vals, axis=0)
        e = jnp.exp(best_s - best_s[0:1])
        gate_ref[h] = e / jnp.sum(e, axis=0, keepdims=True)
        best_i = jnp.concatenate(ids, axis=0)
        idx_ref[h] = best_i
        row_ref[h] = (best_i >> 1) * SUBLANES
        return carry

    lax.fori_loop(0, PEER_HEADS, head_topk, 0)


def _route(x2, g, wq, keys_p, tm):
    n = x2.shape[0]
    return pl.pallas_call(
        _route_kernel,
        grid=(n // tm,),
        in_specs=[pl.BlockSpec((tm, D_MODEL), lambda i: (i, 0)), pl.BlockSpec(g.shape, lambda i: (0, 0)),
                  pl.BlockSpec(wq.shape, lambda i: (0, 0)), pl.BlockSpec(keys_p.shape, lambda i: (0, 0, 0))],
        out_specs=[pl.BlockSpec((tm, SUBLANES, LANES), lambda i: (i, 0, 0))]
        + [pl.BlockSpec((PEER_HEADS, PEER_TOPK, tm), lambda i: (0, 0, i))] * 3,
        out_shape=[jax.ShapeDtypeStruct((n, SUBLANES, LANES), jnp.uint32),
                   jax.ShapeDtypeStruct((PEER_HEADS, PEER_TOPK, n), I32),
                   jax.ShapeDtypeStruct((PEER_HEADS, PEER_TOPK, n), I32),
                   jax.ShapeDtypeStruct((PEER_HEADS, PEER_TOPK, n), F32)],
        scratch_shapes=[pltpu.VMEM((N_SUB, PEER_N_KEYS, tm), F32), pltpu.VMEM((N_SUB, PEER_TOPK, tm), F32),
                        pltpu.VMEM((N_SUB, PEER_TOPK, tm), I32)],
        compiler_params=pltpu.CompilerParams(dimension_semantics=("arbitrary",), vmem_limit_bytes=VMEM_LIMIT),
        name="peer_route",
    )(x2, g, wq, keys_p)


TILE_ROWS = 2 * SUBLANES
PAIRS_PER_KTILE = 16
TOK_UNROLL = 8
SUM_UNROLL = 16


def _pack_table(t):
    e = t.shape[0]
    bits = lax.bitcast_convert_type(t.astype(BF16), jnp.uint16).astype(jnp.uint32)
    words = bits[:, :D_MODEL // 2] | (bits[:, D_MODEL // 2:] << 16)
    return words.reshape(e * (SUBLANES // 2), LANES)


def _gather_tiles(row_ref, tab_ref, t):
    tiles = []
    for k in range(PEER_SEL):
        start = pl.multiple_of(row_ref[t, k], SUBLANES)
        tiles.append(pltpu.bitcast(tab_ref[pl.ds(start, SUBLANES), :], BF16))
    return tiles


def _peer_dot_kernel(row_ref, tab_ref, h_ref, gate_ref, idxv_ref, sel_ref, ce_ref, co_ref,
                     ye_scr, yo_scr, ze_scr, zo_scr, *, tt):
    ones = jnp.ones((SUBLANES, LANES), BF16)
    half_rows = PAIRS_PER_KTILE
    n_trip = tt // TOK_UNROLL

    def row_sums(i, slot):
        for u in range(TOK_UNROLL):
            t = i * TOK_UNROLL + u
            hp = pltpu.bitcast(h_ref[t], BF16)
            prods = [tile * hp for tile in _gather_tiles(row_ref, tab_ref, t)]
            for kt in range(PEER_SEL // PAIRS_PER_KTILE):
                stack = jnp.concatenate(prods[kt * PAIRS_PER_KTILE:(kt + 1) * PAIRS_PER_KTILE], axis=0)
                y = jnp.dot(sel_ref[...], stack, preferred_element_type=F32)
                r0 = u * PEER_SEL + kt * half_rows
                ye_scr[slot, r0:r0 + half_rows, :] = y[:half_rows].astype(BF16)
                yo_scr[slot, r0:r0 + half_rows, :] = y[half_rows:].astype(BF16)

    def lane_sums(i, slot):
        ze = lax.dot_general(ones, ye_scr[slot], NT_DIMS, preferred_element_type=F32)
        zo = lax.dot_general(ones, yo_scr[slot], NT_DIMS, preferred_element_type=F32)
        for u in range(TOK_UNROLL):
            t = i * TOK_UNROLL + u
            ze_scr[pl.ds(t, 1), :] = ze[0:1, u * PEER_SEL:(u + 1) * PEER_SEL]
            zo_scr[pl.ds(t, 1), :] = zo[0:1, u * PEER_SEL:(u + 1) * PEER_SEL]

    def trip(i, carry):
        slot = i & 1
        lane_sums(i - 1, 1 - slot)
        row_sums(i, slot)
        return carry

    row_sums(0, 0)
    lax.fori_loop(1, n_trip, trip, 0)
    lane_sums(n_trip - 1, (n_trip - 1) & 1)
    even = (idxv_ref[...] & 1) == 0
    a = jnp.where(even, ze_scr[...], zo_scr[...])
    c = 0.5 * a * (1.0 + lax.erf(a * (2.0 ** -0.5))) * gate_ref[...]
    ce_ref[...] = jnp.where(even, c, 0.0)
    co_ref[...] = jnp.where(even, 0.0, c)


def _peer_dot(rows2, tab, hpk, gate, idx2, sel, tt):
    n = hpk.shape[0]
    kern = functools.partial(_peer_dot_kernel, tt=tt)
    rows = pl.BlockSpec((tt, PEER_SEL), lambda i: (i, 0))
    return pl.pallas_call(
        kern,
        grid=(n // tt,),
        in_specs=[pl.BlockSpec((tt, PEER_SEL), lambda i: (i, 0), memory_space=pltpu.SMEM),
                  pl.BlockSpec(memory_space=pltpu.VMEM),
                  pl.BlockSpec((tt, SUBLANES, LANES), lambda i: (i, 0, 0)),
                  rows, rows, pl.BlockSpec(sel.shape, lambda i: (0, 0))],
        out_specs=[rows, rows],
        out_shape=[jax.ShapeDtypeStruct((n, PEER_SEL), F32), jax.ShapeDtypeStruct((n, PEER_SEL), F32)],
        scratch_shapes=[pltpu.VMEM((2, TOK_UNROLL * PEER_SEL, LANES), BF16)] * 2
        + [pltpu.VMEM((tt, PEER_SEL), F32)] * 2,
        compiler_params=pltpu.CompilerParams(dimension_semantics=("arbitrary",), vmem_limit_bytes=VMEM_LIMIT),
        name="peer_dot",
    )(rows2, tab, hpk, gate, idx2, sel)


def _peer_sum_kernel(row_ref, ce_ref, co_ref, spread_ref, tab_ref, o_ref, m1_scr, m2_scr, *, tt):
    width = PEER_SEL * TILE_ROWS
    cc = jnp.concatenate([ce_ref[...], co_ref[...]], axis=1)
    c1 = cc.astype(BF16)
    c2 = (cc - c1.astype(F32)).astype(BF16)
    m1_scr[...] = jnp.dot(c1, spread_ref[...], preferred_element_type=F32)
    m2_scr[...] = jnp.dot(c2, spread_ref[...], preferred_element_type=F32)
    row = lax.broadcasted_iota(I32, (SUBLANES, width), 0)
    lane = lax.broadcasted_iota(I32, (SUBLANES, width), 1)
    half = SUBLANES // 2
    on_row = (lane & (SUBLANES - 1)) == 2 * (row % half) + row // half

    def token(t):
        w = jnp.concatenate(_gather_tiles(row_ref, tab_ref, t), axis=0)
        lhs = []
        for m_scr in (m1_scr, m2_scr):
            coef = jnp.broadcast_to(m_scr[pl.ds(t, 1), :], (SUBLANES, width))
            lhs.append(jnp.where(on_row, coef, 0.0).astype(BF16))
        res = jnp.dot(jnp.concatenate(lhs, axis=0), w, preferred_element_type=F32)
        o_ref[t] = res[:SUBLANES] + res[SUBLANES:]

    def trip(i, carry):
        for u in range(SUM_UNROLL):
            token(i * SUM_UNROLL + u)
        return carry

    lax.fori_loop(0, tt // SUM_UNROLL, trip, 0)


def _peer_sum(rows2, ce, co, spread, tab, tt):
    n = ce.shape[0]
    kern = functools.partial(_peer_sum_kernel, tt=tt)
    rows = pl.BlockSpec((tt, PEER_SEL), lambda i: (i, 0))
    width = PEER_SEL * TILE_ROWS
    return pl.pallas_call(
        kern,
        grid=(n // tt,),
        in_specs=[pl.BlockSpec((tt, PEER_SEL), lambda i: (i, 0), memory_space=pltpu.SMEM), rows, rows,
                  pl.BlockSpec(spread.shape, lambda i: (0, 0)), pl.BlockSpec(memory_space=pltpu.VMEM)],
        out_specs=pl.BlockSpec((tt, SUBLANES, LANES), lambda i: (i, 0, 0)),
        out_shape=jax.ShapeDtypeStruct((n, SUBLANES, LANES), F32),
        scratch_shapes=[pltpu.VMEM((tt, width), F32), pltpu.VMEM((tt, width), F32)],
        compiler_params=pltpu.CompilerParams(dimension_semantics=("arbitrary",), vmem_limit_bytes=VMEM_LIMIT),
        name="peer_sum",
    )(rows2, ce, co, spread, tab)


def _final_kernel(x_ref, p_ref, g_ref, o_ref):
    peer = jnp.concatenate([p_ref[:, s, :] for s in range(SUBLANES)], axis=1)
    o_ref[...] = _rms(x_ref[...] + peer, g_ref[...])


def _final(x2, peer3, g, tm):
    n = x2.shape[0]
    return pl.pallas_call(
        _final_kernel,
        grid=(n // tm,),
        in_specs=[pl.BlockSpec((tm, D_MODEL), lambda i: (i, 0)),
                  pl.BlockSpec((tm, SUBLANES, LANES), lambda i: (i, 0, 0)),
                  pl.BlockSpec(g.shape, lambda i: (0, 0))],
        out_specs=pl.BlockSpec((tm, D_MODEL), lambda i: (i, 0)),
        out_shape=jax.ShapeDtypeStruct((n, D_MODEL), F32),
        compiler_params=pltpu.CompilerParams(dimension_semantics=("arbitrary",), vmem_limit_bytes=VMEM_LIMIT),
        name="final_norm",
    )(x2, peer3, g)


def _tile(n, pref):
    t = pref
    while n % t:
        t //= 2
    return t


def _layer(l, x2, mem2, pos, invf, b, s_len, m_tok, norm_mix_g, w_in, da_lambda, da_subln_g, w_branch_a,
           w_branch_b, gate_bias, w_out, norm_mem_g, mem_kv_norm_g, w_mem_q, w_mem_kv, w_mem_o, norm_ffn_g,
           peer_w_q, peer_sub_keys, peer_u, peer_v):
    n = b * s_len
    row2 = lambda v: v.reshape(1, -1)
    splits = [0]
    for c in COL_SIZES:
        splits.append(splits[-1] + c)
    da_q, da_k, da_v, ds_q, ds_k, ds_v, ix_q, ix_k, ix_w, gates = (
        w_in[l][:, splits[j]:splits[j + 1]] for j in range(len(COL_SIZES)))
    k0, k1 = ds_k[:, :DS_HEAD_DIM], ds_k[:, DS_HEAD_DIM:]
    w_keys = jnp.concatenate([da_k, k0, k0, k1, k1, ix_k, ix_k], axis=1).astype(BF16)
    w_q_t = jnp.concatenate([da_q, ds_q, ix_q], axis=1).T.astype(BF16)
    w_v_t = jnp.concatenate([da_v, ds_v], axis=1).T.astype(BF16)
    t_att = _tile(s_len, 256)
    pr, gate, qt, vt3, iwt = _in_proj(x2, row2(norm_mix_g[l]), pos, invf, pos.reshape(1, n),
                                      invf[:, :ROPE_HALF].reshape(ROPE_HALF, 1), row2(gate_bias[l]), w_keys,
                                      gates.astype(BF16), w_q_t, w_v_t, ix_w.T.astype(BF16), t_att)

    lam_init = 0.8 - 0.6 * math.exp(-0.3 * l)
    o_a = _diff_attention(da_lambda[l], qt, pr, vt3, da_subln_g[l].reshape(-1, 1), b, s_len, t_att, lam_init)
    o_b = _dsa_attention(qt, pr, vt3, iwt, b, s_len, t_att)
    x2 = _merge(o_a, o_b, gate, x2, w_branch_a[l].astype(BF16), w_branch_b[l].astype(BF16),
                w_out[l].astype(BF16), _tile(n, 256))

    kv = _norm_matmul(mem2, row2(mem_kv_norm_g[l]), w_mem_kv[l].astype(BF16), _tile(mem2.shape[0], 256))
    x2 = _mem_attn(x2, row2(norm_mem_g[l]), w_mem_q[l].astype(BF16), kv, w_mem_o[l].astype(BF16),
                   b, s_len, m_tok, _tile(s_len, 256))

    sk = peer_sub_keys[l].reshape(N_SUB, PEER_N_KEYS, PEER_HALF)
    z = jnp.zeros_like(sk)
    keys_p = jnp.where((jnp.arange(N_SUB) % 2 == 0)[:, None, None],
                       jnp.concatenate([sk, z], axis=2), jnp.concatenate([z, sk], axis=2)).astype(BF16)
    hpk, idx_t, row_t, gate_t = _route(x2, row2(norm_ffn_g[l]), peer_w_q[l].astype(BF16), keys_p, _tile(n, 256))
    idx2 = idx_t.reshape(PEER_SEL, n).T
    rows2 = row_t.reshape(PEER_SEL, n).T
    gate2 = gate_t.reshape(PEER_SEL, n).T
    tt = _tile(n, 256)
    r32 = jnp.arange(2 * PAIRS_PER_KTILE)[:, None]
    c256 = jnp.arange(PAIRS_PER_KTILE * TILE_ROWS)[None, :]
    sel = ((c256 // TILE_ROWS == r32 % PAIRS_PER_KTILE)
           & ((c256 % TILE_ROWS) // SUBLANES == r32 // PAIRS_PER_KTILE)).astype(BF16)
    k256 = jnp.arange(2 * PEER_SEL)[:, None]
    c2048 = jnp.arange(PEER_SEL * TILE_ROWS)[None, :]
    spread = ((c2048 // TILE_ROWS == k256 % PEER_SEL)
              & ((c2048 % TILE_ROWS) // SUBLANES == k256 // PEER_SEL)).astype(BF16)
    ce2, co2 = _peer_dot(rows2, _pack_table(peer_u[l]), hpk, gate2, idx2, sel, tt)
    peer3 = _peer_sum(rows2, ce2, co2, spread, _pack_table(peer_v[l]), tt)
    return x2, peer3


def kernel(x, mem, positions, norm_mix_g, w_in, da_lambda, da_subln_g, w_branch_a, w_branch_b, gate_bias, w_out, norm_mem_g, mem_kv_norm_g, w_mem_q, w_mem_kv, w_mem_o, norm_ffn_g, peer_w_q, peer_sub_keys, peer_u, peer_v, final_norm_g):
    b, s_len, d = x.shape
    m_tok = mem.shape[1]
    n = b * s_len
    depth = w_in.shape[0]
    x2 = x.reshape(n, d)
    mem2 = mem.reshape(b * m_tok, d)
    pos = positions.astype(F32).reshape(n, 1)
    inv_freq = ROPE_THETA ** (-(jnp.arange(ROPE_HALF, dtype=F32) * 2.0) / ROPE_DIM)
    invf = jnp.tile(inv_freq, LANES // ROPE_HALF).reshape(1, LANES)
    peer3 = None
    for l in range(depth):
        if peer3 is not None:
            x2 = x2 + peer3.reshape(n, d)
        x2, peer3 = _layer(l, x2, mem2, pos, invf, b, s_len, m_tok, norm_mix_g, w_in, da_lambda, da_subln_g,
                           w_branch_a, w_branch_b, gate_bias, w_out, norm_mem_g, mem_kv_norm_g, w_mem_q, w_mem_kv,
                           w_mem_o, norm_ffn_g, peer_w_q, peer_sub_keys, peer_u, peer_v)
    out = _final(x2, peer3, final_norm_g.reshape(1, d), _tile(n, 256))
    return out.reshape(b, s_len, d)
```

```python
import functools
import math

import jax
import jax.numpy as jnp
from jax import lax
from jax.experimental import pallas as pl
from jax.experimental.pallas import tpu as pltpu

F32 = jnp.float32
BF16 = jnp.bfloat16
I32 = jnp.int32

D_MODEL = 1024
EPS = 1e-6
ROPE_THETA = 500000.0
ROT_HEAD_DIM = 64
ROPE_DIM = 16
ROPE_HALF = ROPE_DIM // 2

DA_HEADS = 4
DA_QK_DIM = 64
DA_V_DIM = 128
DA_WIDTH = 512
DS_HEADS = 8
DS_KV_HEADS = 2
DS_HEAD_DIM = 64
DS_WIDTH = 512
IDX_HEADS = 8
IDX_DIM = 64
TOPK_MAX = 256
MEM_HEADS = 4
MEM_HEAD_DIM = 128
MEM_WIDTH = 512
PEER_HEADS = 8
PEER_N_KEYS = 128
PEER_HALF = 64
PEER_TOPK = 16
PEER_SEL = PEER_HEADS * PEER_TOPK

COL_SIZES = (512, 512, 512, 512, 128, 128, 512, 64, 8, 2048)

LANES = 128
SUBLANES = 8
NEG_BIG = -1e30
INT_MIN = -(2 ** 31)
VMEM_LIMIT = 56 * 1024 * 1024

NT_DIMS = (((1,), (1,)), ((), ()))


def _rms(x, g):
    var = jnp.mean(x * x, axis=-1, keepdims=True)
    return x * lax.rsqrt(var + EPS) * g


def _rope_rows(y, cos8, sin8):
    pieces = []
    for r0 in range(0, y.shape[0], ROT_HEAD_DIM):
        t1 = y[r0:r0 + ROPE_HALF]
        t2 = y[r0 + ROPE_HALF:r0 + ROPE_DIM]
        pieces += [t1 * cos8 - t2 * sin8, t2 * cos8 + t1 * sin8, y[r0 + ROPE_DIM:r0 + ROT_HEAD_DIM]]
    return jnp.concatenate(pieces, axis=0)


def _in_proj_kernel(x_ref, g_ref, pos_ref, invf_ref, pos_t_ref, invf8_ref, bias_ref, wr_ref, wg_ref,
                    wqt_ref, wvt_ref, wwt_ref, pr_ref, gate_ref, qt_ref, vt_ref, iwt_ref):
    h = _rms(x_ref[...], g_ref[...]).astype(BF16)
    tm = h.shape[0]
    ang_t = invf8_ref[...] * pos_t_ref[...]
    cos8 = jnp.cos(ang_t)
    sin8 = jnp.sin(ang_t)
    row_chunk = 2 * LANES
    for r0 in range(0, wqt_ref.shape[0], row_chunk):
        y = lax.dot_general(wqt_ref[r0:r0 + row_chunk, :], h, NT_DIMS, preferred_element_type=F32)
        qt_ref[r0:r0 + row_chunk, :] = _rope_rows(y, cos8, sin8).astype(qt_ref.dtype)
    for r0 in range(0, wvt_ref.shape[0], row_chunk):
        r1 = min(r0 + row_chunk, wvt_ref.shape[0])
        y = lax.dot_general(wvt_ref[r0:r1, :], h, NT_DIMS, preferred_element_type=F32)
        vt_ref[0, r0:r1, :] = y.astype(vt_ref.dtype)
    iwt_ref[...] = lax.dot_general(wwt_ref[...], h, NT_DIMS, preferred_element_type=F32)
    ang = pos_ref[...] * invf_ref[...]
    cos = jnp.cos(ang)
    sin = jnp.sin(ang)
    lane = lax.broadcasted_iota(I32, (tm, LANES), 1) % ROT_HEAD_DIM
    c_t = jnp.where(lane < ROPE_DIM, cos, 1.0)
    s_lo = jnp.where(lane < ROPE_HALF, -sin, 0.0)
    s_hi = jnp.where((lane >= ROPE_HALF) & (lane < ROPE_DIM), sin, 0.0)
    n_r = wr_ref.shape[1]
    for c0 in range(0, n_r, 512):
        c1 = min(c0 + 512, n_r)
        w = c1 - c0
        y = jnp.dot(h, wr_ref[:, c0:c1], preferred_element_type=F32)
        reps = w // LANES
        ct = jnp.concatenate([c_t] * reps, axis=1)
        sl = jnp.concatenate([s_lo] * reps, axis=1)
        sh = jnp.concatenate([s_hi] * reps, axis=1)
        y = y * ct + pltpu.roll(y, w - ROPE_HALF, 1) * sl + pltpu.roll(y, ROPE_HALF, 1) * sh
        pr_ref[:, c0:c1] = y.astype(pr_ref.dtype)
    n_g = wg_ref.shape[1]
    for c0 in range(0, n_g, 512):
        y = jnp.dot(h, wg_ref[:, c0:c0 + 512], preferred_element_type=F32)
        gate_ref[:, c0:c0 + 512] = jax.nn.sigmoid(y + bias_ref[:, c0:c0 + 512])


def _in_proj(x2, g, pos, invf, pos_t, invf8, bias, wr, wg, wqt, wvt, wwt, tm):
    n = x2.shape[0]
    full = lambda a: pl.BlockSpec(a.shape, lambda i: (0, 0))
    rows = lambda w: pl.BlockSpec((tm, w), lambda i: (i, 0))
    cols = lambda r: pl.BlockSpec((r, tm), lambda i: (0, i))
    return pl.pallas_call(
        _in_proj_kernel,
        grid=(n // tm,),
        in_specs=[rows(D_MODEL), full(g), rows(1), full(invf), cols(1), full(invf8), full(bias), full(wr), full(wg),
                  full(wqt), full(wvt), full(wwt)],
        out_specs=[rows(wr.shape[1]), rows(wg.shape[1]), cols(wqt.shape[0]),
                   pl.BlockSpec((1, wvt.shape[0], tm), lambda i: (i, 0, 0)), cols(wwt.shape[0])],
        out_shape=[jax.ShapeDtypeStruct((n, wr.shape[1]), BF16), jax.ShapeDtypeStruct((n, wg.shape[1]), F32),
                   jax.ShapeDtypeStruct((wqt.shape[0], n), BF16),
                   jax.ShapeDtypeStruct((n // tm, wvt.shape[0], tm), BF16),
                   jax.ShapeDtypeStruct((wwt.shape[0], n), F32)],
        compiler_params=pltpu.CompilerParams(dimension_semantics=("arbitrary",), vmem_limit_bytes=VMEM_LIMIT),
        name="in_proj",
    )(x2, g, pos, invf, pos_t, invf8, bias, wr, wg, wqt, wvt, wwt)


QT_DA = 0
QT_DS = 4
QT_IX = 8
VT_DA = 0
VT_DS = 4
PR_DAK = 0
PR_DSK = 4
PR_IXK = 6


SUB_KEYS = 128
QK_AHEAD = 2


def _zero_after(x):
    bits = pltpu.bitcast(x, jnp.uint32)
    return pltpu.bitcast((bits >> 16) >> 16, F32)


def _flash_chains(n_chains, score_fn, value_fn, subs, m_scr, l_scr, acc_scr):
    pending = [score_fn(c) for c in range(QK_AHEAD)]
    for c in range(n_chains):
        scores = pending.pop(0)
        m = m_scr[c][0:1]
        if c + QK_AHEAD < n_chains:
            pending.append(score_fn(c + QK_AHEAD))
            m = m + _zero_after(pending[-1][0][0:1])
        m_new = m
        for s in scores:
            m_new = jnp.maximum(m_new, jnp.max(s, axis=0, keepdims=True))
        alpha = jnp.exp(m - m_new)
        l = alpha * l_scr[c][0:1]
        a = alpha * acc_scr[c]
        for r0, s in zip(subs, scores):
            p = jnp.exp(s - m_new)
            l = l + jnp.sum(p, axis=0, keepdims=True)
            v = value_fn(c, r0)
            a = a + jnp.dot(v, p.astype(v.dtype), preferred_element_type=F32)
        m_scr[c] = jnp.broadcast_to(m_new, m_scr.shape[1:])
        l_scr[c] = jnp.broadcast_to(l, l_scr.shape[1:])
        acc_scr[c] = a


def _da_kernel(lam_ref, qt_ref, k_ref, vt_ref, g_ref, o_ref, q_scr, acc_scr, m_scr, l_scr, *, tq, lam_init):
    tk = tq
    i = pl.program_id(1)
    n_chains = 2 * DA_HEADS
    frow = lax.broadcasted_iota(I32, (LANES, tq), 0)
    scale = jnp.asarray(DA_QK_DIM ** -0.5, BF16)
    for h in range(DA_HEADS):
        blk = qt_ref[h * LANES:(h + 1) * LANES, :] * scale
        zero = jnp.zeros_like(blk)
        q_scr[2 * h] = jnp.where(frow < DA_QK_DIM, blk, zero)
        q_scr[2 * h + 1] = jnp.where(frow >= DA_QK_DIM, blk, zero)
    acc_scr[...] = jnp.zeros(acc_scr.shape, F32)
    m_scr[...] = jnp.full(m_scr.shape, NEG_BIG, F32)
    l_scr[...] = jnp.zeros(l_scr.shape, F32)
    kpos0 = lax.broadcasted_iota(I32, (SUB_KEYS, tq), 0)
    qpos = i * tq + lax.broadcasted_iota(I32, (SUB_KEYS, tq), 1)
    subs = range(0, tk, SUB_KEYS)

    def chunk(j, masked):
        start = pl.multiple_of(j * tk, tk)

        def scores(c):
            head = c // 2
            out = []
            for r0 in subs:
                k = k_ref[pl.ds(start + r0, SUB_KEYS), head * LANES:(head + 1) * LANES]
                s = jnp.dot(k, q_scr[c], preferred_element_type=F32)
                if masked:
                    s = jnp.where(kpos0 + (j * tk + r0) <= qpos, s, NEG_BIG)
                out.append(s)
            return out

        def values(c, r0):
            head = c // 2
            return vt_ref[j, head * LANES:(head + 1) * LANES, r0:r0 + SUB_KEYS]

        _flash_chains(n_chains, scores, values, subs, m_scr, l_scr, acc_scr)

    def full_chunk(j, carry):
        chunk(j, False)
        return carry

    lax.fori_loop(0, i, full_chunk, 0)
    chunk(i, True)

    lp = lam_ref[...]
    lam = (jnp.exp(jnp.sum(lp[0:1] * lp[1:2], axis=1, keepdims=True))
           - jnp.exp(jnp.sum(lp[2:3] * lp[3:4], axis=1, keepdims=True)) + lam_init)
    for h in range(DA_HEADS):
        o = acc_scr[2 * h] / l_scr[2 * h][0:1] - lam * (acc_scr[2 * h + 1] / l_scr[2 * h + 1][0:1])
        var = jnp.mean(o * o, axis=0, keepdims=True)
        o = o * lax.rsqrt(var + EPS) * g_ref[...] * (1.0 - lam_init)
        o_ref[:, h * LANES:(h + 1) * LANES] = o.T.astype(o_ref.dtype)


def _diff_attention(lam_p, qt, pr, vt3, subln_g, b, s_len, tq, lam_init):
    nq = s_len // tq
    n = b * s_len
    n_chains = 2 * DA_HEADS
    kern = functools.partial(_da_kernel, tq=tq, lam_init=lam_init)
    return pl.pallas_call(
        kern,
        grid=(b, nq),
        in_specs=[
            pl.BlockSpec(lam_p.shape, lambda bi, i: (0, 0)),
            pl.BlockSpec((DA_HEADS * LANES, tq), lambda bi, i: (QT_DA // DA_HEADS, bi * nq + i)),
            pl.BlockSpec((s_len, DA_HEADS * LANES), lambda bi, i: (bi, PR_DAK // DA_HEADS)),
            pl.BlockSpec((s_len // tq, DA_HEADS * LANES, tq), lambda bi, i: (bi, VT_DA // DA_HEADS, 0)),
            pl.BlockSpec(subln_g.shape, lambda bi, i: (0, 0)),
        ],
        out_specs=pl.BlockSpec((tq, DA_WIDTH), lambda bi, i: (bi * nq + i, 0)),
        out_shape=jax.ShapeDtypeStruct((n, DA_WIDTH), BF16),
        scratch_shapes=[pltpu.VMEM((n_chains, LANES, tq), BF16), pltpu.VMEM((n_chains, DA_V_DIM, tq), F32),
                        pltpu.VMEM((n_chains, SUBLANES, tq), F32), pltpu.VMEM((n_chains, SUBLANES, tq), F32)],
        compiler_params=pltpu.CompilerParams(dimension_semantics=("arbitrary",) * 2, vmem_limit_bytes=VMEM_LIMIT),
        name="diff_attn",
    )(lam_p, qt, pr, vt3, subln_g)


def _dsa_kernel(qit_ref, ki_ref, wt_ref, qt_ref, k0_ref, k1_ref, vt_ref, o_ref,
                keys_scr, qi_scr, q_scr, acc_scr, m_scr, l_scr, cut_scr, *, tq, n_sel, n_bits):
    tk = tq
    i = pl.program_id(1)
    nck = i + 1
    kpos0 = lax.broadcasted_iota(I32, (tk, tq), 0)
    qpos = i * tq + lax.broadcasted_iota(I32, (tk, tq), 1)
    frow = lax.broadcasted_iota(I32, (LANES, tq), 0)
    per_grp = DS_HEADS // DS_KV_HEADS

    scale = jnp.asarray(DS_HEAD_DIM ** -0.5, BF16)
    for h in range(DS_HEADS):
        keep = (frow < DS_HEAD_DIM) if h % 2 == 0 else (frow >= DS_HEAD_DIM)
        rows = slice((h // 2) * LANES, (h // 2 + 1) * LANES)
        blk = qit_ref[rows, :]
        qi_scr[h] = jnp.where(keep, blk, jnp.zeros_like(blk))
        blk = qt_ref[rows, :] * scale
        q_scr[h] = jnp.where(keep, blk, jnp.zeros_like(blk))

    def score_chunk(c, carry):
        kc = ki_ref[pl.ds(pl.multiple_of(c * tk, tk), tk), :]
        acc = jnp.zeros((tk, tq), F32)
        for h in range(IDX_HEADS):
            s = jnp.dot(kc, qi_scr[h], preferred_element_type=F32)
            acc = acc + wt_ref[h:h + 1, :] * jnp.maximum(s, 0.0)
        bits = pltpu.bitcast(acc, I32)
        key = bits ^ ((bits >> 31) & 0x7FFFFFFF)
        key = jnp.where(acc == 0.0, 0, key)
        keys_scr[c] = jnp.where(kpos0 + c * tk <= qpos, key, INT_MIN)
        return carry

    lax.fori_loop(0, nck, score_chunk, 0)

    def count(pred):
        def body(c, acc):
            hit = pred(keys_scr[c], kpos0 + c * tk)
            return acc + jnp.sum(hit.reshape(tk // SUBLANES, SUBLANES, tq), axis=0)
        acc = lax.fori_loop(0, nck, body, jnp.zeros((SUBLANES, tq), I32))
        return jnp.sum(acc, axis=0, keepdims=True)

    def thr_bit(t, ans_u):
        cand_u = ans_u | jnp.left_shift(jnp.int32(1), 31 - t)
        cand = cand_u ^ INT_MIN
        cnt = count(lambda k, kpos: jnp.where(k >= cand, 1, 0))
        return jnp.where(cnt >= n_sel, cand_u, ans_u)

    thr = lax.fori_loop(0, 32, thr_bit, jnp.zeros((1, tq), I32)) ^ INT_MIN
    n_ge = count(lambda k, kpos: jnp.where(k >= thr, 1, 0))
    below_all = thr == INT_MIN
    cut_scr[...] = jnp.where(below_all, -1, (1 << n_bits) - 1)
    excess = jnp.max(jnp.where(below_all, 0, n_ge - n_sel))

    @pl.when(excess > 0)
    def _():
        n_gt = count(lambda k, kpos: jnp.where(k > thr, 1, 0))
        need = n_sel - n_gt

        def tie_bit(t, cut):
            cand = cut | jnp.left_shift(jnp.int32(1), n_bits - 1 - t)
            cnt = count(lambda k, kpos: jnp.where(k == thr, jnp.where(kpos < cand, 1, 0), 0))
            return jnp.where(cnt < need, cand, cut)

        cut = lax.fori_loop(0, n_bits, tie_bit, jnp.zeros((1, tq), I32))
        cut_scr[...] = jnp.where(below_all, -1, cut)

    cut = cut_scr[...]

    def bias_chunk(c, carry):
        k = keys_scr[c]
        sel = (k > thr) | ((k == thr) & (kpos0 + c * tk <= cut))
        keys_scr[c] = pltpu.bitcast(jnp.where(sel, 0.0, NEG_BIG).astype(F32), I32)
        return carry

    lax.fori_loop(0, nck, bias_chunk, 0)

    acc_scr[...] = jnp.zeros(acc_scr.shape, F32)
    m_scr[...] = jnp.full(m_scr.shape, NEG_BIG, F32)
    l_scr[...] = jnp.zeros(l_scr.shape, F32)

    def att_chunk(c, carry):
        start = pl.multiple_of(c * tk, tk)
        subs = range(0, tk, SUB_KEYS)

        def head_scores(h):
            k_ref = k0_ref if h // per_grp == 0 else k1_ref
            out = []
            for r0 in subs:
                s = jnp.dot(k_ref[pl.ds(start + r0, SUB_KEYS), :], q_scr[h], preferred_element_type=F32)
                out.append(s + pltpu.bitcast(keys_scr[c, r0:r0 + SUB_KEYS, :], F32))
            return out

        def head_values(h, r0):
            grp = h // per_grp
            return vt_ref[c, grp * DS_HEAD_DIM:(grp + 1) * DS_HEAD_DIM, r0:r0 + SUB_KEYS]

        _flash_chains(DS_HEADS, head_scores, head_values, subs, m_scr, l_scr, acc_scr)
        return carry

    lax.fori_loop(0, nck, att_chunk, 0)
    outs = [acc_scr[h] / l_scr[h][0:1] for h in range(DS_HEADS)]
    o_ref[...] = jnp.concatenate(outs, axis=0).T.astype(o_ref.dtype)


def _dsa_attention(qt, pr, vt3, iwt, b, s_len, tq):
    nq = s_len // tq
    n = b * s_len
    n_sel = min(TOPK_MAX, s_len // 4)
    n_bits = max(1, (s_len - 1).bit_length())
    kern = functools.partial(_dsa_kernel, tq=tq, n_sel=n_sel, n_bits=n_bits)
    seq = lambda blk: pl.BlockSpec((s_len, LANES), lambda bi, i: (bi, blk))
    qblk = lambda blk: pl.BlockSpec((4 * LANES, tq), lambda bi, i: (blk // 4, bi * nq + i))
    return pl.pallas_call(
        kern,
        grid=(b, nq),
        in_specs=[
            qblk(QT_IX), seq(PR_IXK),
            pl.BlockSpec((IDX_HEADS, tq), lambda bi, i: (0, bi * nq + i)),
            qblk(QT_DS), seq(PR_DSK), seq(PR_DSK + 1),
            pl.BlockSpec((s_len // tq, LANES, tq), lambda bi, i: (bi, VT_DS, 0)),
        ],
        out_specs=pl.BlockSpec((tq, DS_WIDTH), lambda bi, i: (bi * nq + i, 0)),
        out_shape=jax.ShapeDtypeStruct((n, DS_WIDTH), BF16),
        scratch_shapes=[pltpu.VMEM((nq, tq, tq), I32),
                        pltpu.VMEM((IDX_HEADS, LANES, tq), BF16), pltpu.VMEM((DS_HEADS, LANES, tq), BF16),
                        pltpu.VMEM((DS_HEADS, DS_HEAD_DIM, tq), F32),
                        pltpu.VMEM((DS_HEADS, SUBLANES, tq), F32), pltpu.VMEM((DS_HEADS, SUBLANES, tq), F32),
                        pltpu.VMEM((1, tq), I32)],
        compiler_params=pltpu.CompilerParams(dimension_semantics=("arbitrary",) * 2, vmem_limit_bytes=VMEM_LIMIT),
        name="dsa_attn",
    )(qt, pr, iwt, qt, pr, pr, vt3)


def _merge_kernel(oa_ref, ob_ref, gate_ref, x_ref, wa_ref, wb_ref, wo_ref, o_ref):
    ya = jnp.dot(oa_ref[...], wa_ref[...], preferred_element_type=F32)
    yb = jnp.dot(ob_ref[...], wb_ref[...], preferred_element_type=F32)
    mix = gate_ref[:, :D_MODEL] * ya + gate_ref[:, D_MODEL:] * yb
    o_ref[...] = x_ref[...] + jnp.dot(mix.astype(BF16), wo_ref[...], preferred_element_type=F32)


def _merge(oa, ob, gate, x2, wa, wb, wo, tm):
    n = x2.shape[0]
    full = lambda a: pl.BlockSpec(a.shape, lambda i: (0, 0))
    rows = lambda w: pl.BlockSpec((tm, w), lambda i: (i, 0))
    return pl.pallas_call(
        _merge_kernel,
        grid=(n // tm,),
        in_specs=[rows(DA_WIDTH), rows(DS_WIDTH), rows(2 * D_MODEL), rows(D_MODEL), full(wa), full(wb), full(wo)],
        out_specs=rows(D_MODEL),
        out_shape=jax.ShapeDtypeStruct((n, D_MODEL), F32),
        compiler_params=pltpu.CompilerParams(dimension_semantics=("arbitrary",), vmem_limit_bytes=VMEM_LIMIT),
        name="merge",
    )(oa, ob, gate, x2, wa, wb, wo)


def _norm_matmul_kernel(x_ref, g_ref, w_ref, o_ref):
    h = _rms(x_ref[...], g_ref[...]).astype(BF16)
    o_ref[...] = jnp.dot(h, w_ref[...], preferred_element_type=F32).astype(o_ref.dtype)


def _norm_matmul(x2, g, w, tm):
    n = x2.shape[0]
    return pl.pallas_call(
        _norm_matmul_kernel,
        grid=(n // tm,),
        in_specs=[pl.BlockSpec((tm, x2.shape[1]), lambda i: (i, 0)), pl.BlockSpec(g.shape, lambda i: (0, 0)),
                  pl.BlockSpec(w.shape, lambda i: (0, 0))],
        out_specs=pl.BlockSpec((tm, w.shape[1]), lambda i: (i, 0)),
        out_shape=jax.ShapeDtypeStruct((n, w.shape[1]), BF16),
        compiler_params=pltpu.CompilerParams(dimension_semantics=("arbitrary",), vmem_limit_bytes=VMEM_LIMIT),
        name="mem_kv_proj",
    )(x2, g, w)


def _mem_attn_kernel(x_ref, g_ref, wq_ref, kv_ref, wo_ref, o_ref):
    x = x_ref[...]
    hn = _rms(x, g_ref[...]).astype(BF16)
    q = jnp.dot(hn, wq_ref[...], preferred_element_type=F32).astype(BF16)
    scale = MEM_HEAD_DIM ** -0.5
    heads = []
    for h in range(MEM_HEADS):
        qh = q[:, h * MEM_HEAD_DIM:(h + 1) * MEM_HEAD_DIM]
        kh = kv_ref[:, h * MEM_HEAD_DIM:(h + 1) * MEM_HEAD_DIM]
        vh = kv_ref[:, MEM_WIDTH + h * MEM_HEAD_DIM:MEM_WIDTH + (h + 1) * MEM_HEAD_DIM]
        s = lax.dot_general(qh, kh, NT_DIMS, preferred_element_type=F32) * scale
        m = jnp.max(s, axis=1, keepdims=True)
        p = jnp.exp(s - m)
        l = jnp.sum(p, axis=1, keepdims=True)
        heads.append(jnp.dot(p.astype(BF16), vh, preferred_element_type=F32) / l)
    o = jnp.concatenate(heads, axis=1).astype(BF16)
    o_ref[...] = x + jnp.dot(o, wo_ref[...], preferred_element_type=F32)


def _mem_attn(x2, g, wq, kv, wo, b, s_len, m_tok, tm):
    nb = s_len // tm
    n = x2.shape[0]
    full = lambda a: pl.BlockSpec(a.shape, lambda bi, i: (0, 0))
    return pl.pallas_call(
        _mem_attn_kernel,
        grid=(b, nb),
        in_specs=[pl.BlockSpec((tm, D_MODEL), lambda bi, i: (bi * nb + i, 0)), full(g), full(wq),
                  pl.BlockSpec((m_tok, 2 * MEM_WIDTH), lambda bi, i: (bi, 0)), full(wo)],
        out_specs=pl.BlockSpec((tm, D_MODEL), lambda bi, i: (bi * nb + i, 0)),
        out_shape=jax.ShapeDtypeStruct((n, D_MODEL), F32),
        compiler_params=pltpu.CompilerParams(dimension_semantics=("arbitrary",) * 2, vmem_limit_bytes=VMEM_LIMIT),
        name="mem_attn",
    )(x2, g, wq, kv, wo)


N_SUB = 2 * PEER_HEADS
EXPERT_BITS = 14


def _route_kernel(x_ref, g_ref, wq_ref, keys_ref, hf_ref, idx_ref, row_ref, gate_ref, sc_scr, ts_scr, ti_scr):
    hf = _rms(x_ref[...], g_ref[...])
    tm = hf.shape[0]
    hb = hf.astype(BF16)
    bits = pltpu.bitcast(hb.astype(F32), jnp.uint32)
    half_d = D_MODEL // 2
    for s in range(SUBLANES // 2):
        lo = bits[:, s * LANES:(s + 1) * LANES] >> 16
        hi = bits[:, half_d + s * LANES:half_d + (s + 1) * LANES] & jnp.uint32(0xFFFF0000)
        hf_ref[:, s, :] = lo | hi
        hf_ref[:, s + SUBLANES // 2, :] = lo | hi
    q = jnp.dot(hb, wq_ref[...], preferred_element_type=F32).astype(BF16)
    for g in range(N_SUB):
        blk = q[:, (g // 2) * LANES:(g // 2 + 1) * LANES]
        sc_scr[g] = lax.dot_general(keys_ref[g], blk, NT_DIMS, preferred_element_type=F32)

    key_id = lax.broadcasted_iota(I32, (PEER_N_KEYS, tm), 0)

    def sub_topk(g, carry):
        x = sc_scr[g]
        vals, ids = [], []
        for _ in range(PEER_TOPK):
            m = jnp.max(x, axis=0, keepdims=True)
            idx = jnp.min(jnp.where(x == m, key_id, PEER_N_KEYS), axis=0, keepdims=True)
            vals.append(m)
            ids.append(idx)
            x = jnp.where(key_id == idx, -jnp.inf, x)
        ts_scr[g] = jnp.concatenate(vals, axis=0)
        ti_scr[g] = jnp.concatenate(ids, axis=0)
        return carry

    lax.fori_loop(0, N_SUB, sub_topk, 0)

    iota16 = lax.broadcasted_iota(I32, (PEER_TOPK, tm), 0)
    iota8 = lax.broadcasted_iota(I32, (SUBLANES, tm), 0)
    lead_rows = SUBLANES

    def head_topk(h, carry):
        s0, s1 = ts_scr[2 * h], ts_scr[2 * h + 1]
        i0, i1 = ti_scr[2 * h], ti_scr[2 * h + 1]
        cands, codes = [], []
        for a in range(lead_rows):
            n_j = PEER_TOPK if a == 0 else SUBLANES
            cands.append(s0[a:a + 1] + s1[:n_j])
            j_iota = iota16 if a == 0 else iota8
            codes.append(((a * PEER_TOPK + j_iota) << EXPERT_BITS) | (i0[a:a + 1] * PEER_N_KEYS + i1[:n_j]))
        cands.append(s0[lead_rows:] + s1[0:1])
        codes.append((((iota8 + lead_rows) * PEER_TOPK) << EXPERT_BITS) | (i0[lead_rows:] * PEER_N_KEYS + i1[0:1]))
        cand = jnp.concatenate(cands, axis=0)
        code = jnp.concatenate(codes, axis=0)
        big = jnp.int32(2 ** 30)
        vals, ids = [], []
        for _ in range(PEER_TOPK):
            m = jnp.max(cand, axis=0, keepdims=True)
            best = jnp.min(jnp.where(cand == m, code, big), axis=0, keepdims=True)
            vals.append(m)
            ids.append(best & (2 ** EXPERT_BITS - 1))
            cand = jnp.where(code == best, -jnp.inf, cand)
        best_s = jnp.concatenate(vals, axis=0)
        e = jnp.exp(best_s - best_s[0:1])
        gate_ref[h] = e / jnp.sum(e, axis=0, keepdims=True)
        best_i = jnp.concatenate(ids, axis=0)
        idx_ref[h] = best_i
        row_ref[h] = (best_i >> 1) * SUBLANES
        return carry

    lax.fori_loop(0, PEER_HEADS, head_topk, 0)


def _route(x2, g, wq, keys_p, tm):
    n = x2.shape[0]
    return pl.pallas_call(
        _route_kernel,
        grid=(n // tm,),
        in_specs=[pl.BlockSpec((tm, D_MODEL), lambda i: (i, 0)), pl.BlockSpec(g.shape, lambda i: (0, 0)),
                  pl.BlockSpec(wq.shape, lambda i: (0, 0)), pl.BlockSpec(keys_p.shape, lambda i: (0, 0, 0))],
        out_specs=[pl.BlockSpec((tm, SUBLANES, LANES), lambda i: (i, 0, 0))]
        + [pl.BlockSpec((PEER_HEADS, PEER_TOPK, tm), lambda i: (0, 0, i))] * 3,
        out_shape=[jax.ShapeDtypeStruct((n, SUBLANES, LANES), jnp.uint32),
                   jax.ShapeDtypeStruct((PEER_HEADS, PEER_TOPK, n), I32),
                   jax.ShapeDtypeStruct((PEER_HEADS, PEER_TOPK, n), I32),
                   jax.ShapeDtypeStruct((PEER_HEADS, PEER_TOPK, n), F32)],
        scratch_shapes=[pltpu.VMEM((N_SUB, PEER_N_KEYS, tm), F32), pltpu.VMEM((N_SUB, PEER_TOPK, tm), F32),
                        pltpu.VMEM((N_SUB, PEER_TOPK, tm), I32)],
        compiler_params=pltpu.CompilerParams(dimension_semantics=("arbitrary",), vmem_limit_bytes=VMEM_LIMIT),
        name="peer_route",
    )(x2, g, wq, keys_p)


TILE_ROWS = 2 * SUBLANES
PAIRS_PER_KTILE = 16
TOK_UNROLL = 8
SUM_UNROLL = 16


def _pack_table(t):
    e = t.shape[0]
    bits = lax.bitcast_convert_type(t.astype(BF16), jnp.uint16).astype(jnp.uint32)
    words = bits[:, :D_MODEL // 2] | (bits[:, D_MODEL // 2:] << 16)
    return words.reshape(e * (SUBLANES // 2), LANES)


def _gather_tiles(off_ref, tab_ref, u):
    tiles = []
    for k in range(PEER_SEL):
        start = pl.multiple_of(off_ref[u * PEER_SEL + k], SUBLANES)
        tiles.append(pltpu.bitcast(tab_ref[pl.ds(start, SUBLANES), :], BF16))
    return tiles


def _for_trips(rows_hbm, bufs, sems, tok0, unroll, n_trip, body):
    assert n_trip % 2 == 0
    words = unroll * PEER_SEL

    def fetch(trip, par):
        src = rows_hbm.at[pl.ds(pl.multiple_of((tok0 + trip * unroll) * PEER_SEL, words), words)]
        return pltpu.make_async_copy(src, bufs[par], sems.at[par])

    fetch(0, 0).start()
    fetch(1, 1).start()

    def pair(p, carry):
        for par in range(2):
            trip = 2 * p + par
            fetch(trip, par).wait()
            body(trip, bufs[par], par)

            @pl.when(trip + 2 < n_trip)
            def _():
                fetch(trip + 2, par).start()
        return carry

    lax.fori_loop(0, n_trip // 2, pair, 0)


def _peer_dot_kernel(rows_hbm, tab_ref, h_ref, gate_ref, idxv_ref, sel_ref, ce_ref, co_ref,
                     ye_scr, yo_scr, ze_scr, zo_scr, off_a, off_b, sems, *, tt):
    ones = jnp.ones((SUBLANES, LANES), BF16)
    half_rows = PAIRS_PER_KTILE
    n_trip = tt // TOK_UNROLL

    def row_sums(i, slot, off_ref):
        for u in range(TOK_UNROLL):
            t = i * TOK_UNROLL + u
            hp = pltpu.bitcast(h_ref[t], BF16)
            prods = [tile * hp for tile in _gather_tiles(off_ref, tab_ref, u)]
            for kt in range(PEER_SEL // PAIRS_PER_KTILE):
                stack = jnp.concatenate(prods[kt * PAIRS_PER_KTILE:(kt + 1) * PAIRS_PER_KTILE], axis=0)
                y = jnp.dot(sel_ref[...], stack, preferred_element_type=F32)
                r0 = u * PEER_SEL + kt * half_rows
                ye_scr[slot, r0:r0 + half_rows, :] = y[:half_rows].astype(BF16)
                yo_scr[slot, r0:r0 + half_rows, :] = y[half_rows:].astype(BF16)

    def lane_sums(i, slot):
        ze = lax.dot_general(ones, ye_scr[slot], NT_DIMS, preferred_element_type=F32)
        zo = lax.dot_general(ones, yo_scr[slot], NT_DIMS, preferred_element_type=F32)
        for u in range(TOK_UNROLL):
            t = i * TOK_UNROLL + u
            ze_scr[pl.ds(t, 1), :] = ze[0:1, u * PEER_SEL:(u + 1) * PEER_SEL]
            zo_scr[pl.ds(t, 1), :] = zo[0:1, u * PEER_SEL:(u + 1) * PEER_SEL]

    def trip(i, off_ref, par):
        @pl.when(i > 0)
        def _():
            lane_sums(i - 1, 1 - par)

        row_sums(i, par, off_ref)

    _for_trips(rows_hbm, (off_a, off_b), sems, pl.program_id(0) * tt, TOK_UNROLL, n_trip, trip)
    lane_sums(n_trip - 1, (n_trip - 1) & 1)
    even = (idxv_ref[...] & 1) == 0
    a = jnp.where(even, ze_scr[...], zo_scr[...])
    c = 0.5 * a * (1.0 + lax.erf(a * (2.0 ** -0.5))) * gate_ref[...]
    ce_ref[...] = jnp.where(even, c, 0.0)
    co_ref[...] = jnp.where(even, 0.0, c)


def _peer_dot(rows_flat, tab, hpk, gate, idx2, sel, tt):
    n = hpk.shape[0]
    kern = functools.partial(_peer_dot_kernel, tt=tt)
    rows = pl.BlockSpec((tt, PEER_SEL), lambda i: (i, 0))
    return pl.pallas_call(
        kern,
        grid=(n // tt,),
        in_specs=[pl.BlockSpec(memory_space=pl.ANY),
                  pl.BlockSpec(memory_space=pltpu.VMEM),
                  pl.BlockSpec((tt, SUBLANES, LANES), lambda i: (i, 0, 0)),
                  rows, rows, pl.BlockSpec(sel.shape, lambda i: (0, 0))],
        out_specs=[rows, rows],
        out_shape=[jax.ShapeDtypeStruct((n, PEER_SEL), F32), jax.ShapeDtypeStruct((n, PEER_SEL), F32)],
        scratch_shapes=[pltpu.VMEM((2, TOK_UNROLL * PEER_SEL, LANES), BF16)] * 2
        + [pltpu.VMEM((tt, PEER_SEL), F32)] * 2
        + [pltpu.SMEM((TOK_UNROLL * PEER_SEL,), I32)] * 2 + [pltpu.SemaphoreType.DMA((2,))],
        compiler_params=pltpu.CompilerParams(dimension_semantics=("arbitrary",), vmem_limit_bytes=VMEM_LIMIT),
        name="peer_dot",
    )(rows_flat, tab, hpk, gate, idx2, sel)


def _peer_sum_kernel(rows_hbm, ce_ref, co_ref, spread_ref, tab_ref, o_ref, m1_scr, m2_scr, off_a, off_b, sems,
                     *, tt):
    width = PEER_SEL * TILE_ROWS
    cc = jnp.concatenate([ce_ref[...], co_ref[...]], axis=1)
    c1 = cc.astype(BF16)
    c2 = (cc - c1.astype(F32)).astype(BF16)
    m1_scr[...] = jnp.dot(c1, spread_ref[...], preferred_element_type=F32)
    m2_scr[...] = jnp.dot(c2, spread_ref[...], preferred_element_type=F32)
    row = lax.broadcasted_iota(I32, (SUBLANES, width), 0)
    lane = lax.broadcasted_iota(I32, (SUBLANES, width), 1)
    half = SUBLANES // 2
    on_row = (lane & (SUBLANES - 1)) == 2 * (row % half) + row // half

    def token(t, off_ref, u):
        w = jnp.concatenate(_gather_tiles(off_ref, tab_ref, u), axis=0)
        lhs = []
        for m_scr in (m1_scr, m2_scr):
            coef = jnp.broadcast_to(m_scr[pl.ds(t, 1), :], (SUBLANES, width))
            lhs.append(jnp.where(on_row, coef, 0.0).astype(BF16))
        res = jnp.dot(jnp.concatenate(lhs, axis=0), w, preferred_element_type=F32)
        o_ref[t] = res[:SUBLANES] + res[SUBLANES:]

    def trip(i, off_ref, par):
        for u in range(SUM_UNROLL):
            token(i * SUM_UNROLL + u, off_ref, u)

    _for_trips(rows_hbm, (off_a, off_b), sems, pl.program_id(0) * tt, SUM_UNROLL, tt // SUM_UNROLL, trip)


def _peer_sum(rows_flat, ce, co, spread, tab, tt):
    n = ce.shape[0]
    kern = functools.partial(_peer_sum_kernel, tt=tt)
    rows = pl.BlockSpec((tt, PEER_SEL), lambda i: (i, 0))
    width = PEER_SEL * TILE_ROWS
    return pl.pallas_call(
        kern,
        grid=(n // tt,),
        in_specs=[pl.BlockSpec(memory_space=pl.ANY), rows, rows,
                  pl.BlockSpec(spread.shape, lambda i: (0, 0)), pl.BlockSpec(memory_space=pltpu.VMEM)],
        out_specs=pl.BlockSpec((tt, SUBLANES, LANES), lambda i: (i, 0, 0)),
        out_shape=jax.ShapeDtypeStruct((n, SUBLANES, LANES), F32),
        scratch_shapes=[pltpu.VMEM((tt, width), F32), pltpu.VMEM((tt, width), F32)]
        + [pltpu.SMEM((SUM_UNROLL * PEER_SEL,), I32)] * 2 + [pltpu.SemaphoreType.DMA((2,))],
        compiler_params=pltpu.CompilerParams(dimension_semantics=("arbitrary",), vmem_limit_bytes=VMEM_LIMIT),
        name="peer_sum",
    )(rows_flat, ce, co, spread, tab)


def _final_kernel(x_ref, p_ref, g_ref, o_ref):
    peer = jnp.concatenate([p_ref[:, s, :] for s in range(SUBLANES)], axis=1)
    o_ref[...] = _rms(x_ref[...] + peer, g_ref[...])


def _final(x2, peer3, g, tm):
    n = x2.shape[0]
    return pl.pallas_call(
        _final_kernel,
        grid=(n // tm,),
        in_specs=[pl.BlockSpec((tm, D_MODEL), lambda i: (i, 0)),
                  pl.BlockSpec((tm, SUBLANES, LANES), lambda i: (i, 0, 0)),
                  pl.BlockSpec(g.shape, lambda i: (0, 0))],
        out_specs=pl.BlockSpec((tm, D_MODEL), lambda i: (i, 0)),
        out_shape=jax.ShapeDtypeStruct((n, D_MODEL), F32),
        compiler_params=pltpu.CompilerParams(dimension_semantics=("arbitrary",), vmem_limit_bytes=VMEM_LIMIT),
        name="final_norm",
    )(x2, peer3, g)


def _tile(n, pref):
    t = pref
    while n % t:
        t //= 2
    return t


def _layer(l, x2, mem2, pos, invf, b, s_len, m_tok, norm_mix_g, w_in, da_lambda, da_subln_g, w_branch_a,
           w_branch_b, gate_bias, w_out, norm_mem_g, mem_kv_norm_g, w_mem_q, w_mem_kv, w_mem_o, norm_ffn_g,
           peer_w_q, peer_sub_keys, peer_u, peer_v):
    n = b * s_len
    row2 = lambda v: v.reshape(1, -1)
    splits = [0]
    for c in COL_SIZES:
        splits.append(splits[-1] + c)
    da_q, da_k, da_v, ds_q, ds_k, ds_v, ix_q, ix_k, ix_w, gates = (
        w_in[l][:, splits[j]:splits[j + 1]] for j in range(len(COL_SIZES)))
    k0, k1 = ds_k[:, :DS_HEAD_DIM], ds_k[:, DS_HEAD_DIM:]
    w_keys = jnp.concatenate([da_k, k0, k0, k1, k1, ix_k, ix_k], axis=1).astype(BF16)
    w_q_t = jnp.concatenate([da_q, ds_q, ix_q], axis=1).T.astype(BF16)
    w_v_t = jnp.concatenate([da_v, ds_v], axis=1).T.astype(BF16)
    t_att = _tile(s_len, 256)
    pr, gate, qt, vt3, iwt = _in_proj(x2, row2(norm_mix_g[l]), pos, invf, pos.reshape(1, n),
                                      invf[:, :ROPE_HALF].reshape(ROPE_HALF, 1), row2(gate_bias[l]), w_keys,
                                      gates.astype(BF16), w_q_t, w_v_t, ix_w.T.astype(BF16), t_att)

    lam_init = 0.8 - 0.6 * math.exp(-0.3 * l)
    o_a = _diff_attention(da_lambda[l], qt, pr, vt3, da_subln_g[l].reshape(-1, 1), b, s_len, t_att, lam_init)
    o_b = _dsa_attention(qt, pr, vt3, iwt, b, s_len, t_att)
    x2 = _merge(o_a, o_b, gate, x2, w_branch_a[l].astype(BF16), w_branch_b[l].astype(BF16),
                w_out[l].astype(BF16), _tile(n, 256))

    kv = _norm_matmul(mem2, row2(mem_kv_norm_g[l]), w_mem_kv[l].astype(BF16), _tile(mem2.shape[0], 256))
    x2 = _mem_attn(x2, row2(norm_mem_g[l]), w_mem_q[l].astype(BF16), kv, w_mem_o[l].astype(BF16),
                   b, s_len, m_tok, _tile(s_len, 256))

    sk = peer_sub_keys[l].reshape(N_SUB, PEER_N_KEYS, PEER_HALF)
    z = jnp.zeros_like(sk)
    keys_p = jnp.where((jnp.arange(N_SUB) % 2 == 0)[:, None, None],
                       jnp.concatenate([sk, z], axis=2), jnp.concatenate([z, sk], axis=2)).astype(BF16)
    hpk, idx_t, row_t, gate_t = _route(x2, row2(norm_ffn_g[l]), peer_w_q[l].astype(BF16), keys_p, _tile(n, 256))
    idx2 = idx_t.reshape(PEER_SEL, n).T
    rows2 = row_t.reshape(PEER_SEL, n).T
    gate2 = gate_t.reshape(PEER_SEL, n).T
    tt = _tile(n, 256)
    r32 = jnp.arange(2 * PAIRS_PER_KTILE)[:, None]
    c256 = jnp.arange(PAIRS_PER_KTILE * TILE_ROWS)[None, :]
    sel = ((c256 // TILE_ROWS == r32 % PAIRS_PER_KTILE)
           & ((c256 % TILE_ROWS) // SUBLANES == r32 // PAIRS_PER_KTILE)).astype(BF16)
    k256 = jnp.arange(2 * PEER_SEL)[:, None]
    c2048 = jnp.arange(PEER_SEL * TILE_ROWS)[None, :]
    spread = ((c2048 // TILE_ROWS == k256 % PEER_SEL)
              & ((c2048 % TILE_ROWS) // SUBLANES == k256 // PEER_SEL)).astype(BF16)
    rows_flat = rows2.reshape(-1)
    ce2, co2 = _peer_dot(rows_flat, _pack_table(peer_u[l]), hpk, gate2, idx2, sel, tt)
    peer3 = _peer_sum(rows_flat, ce2, co2, spread, _pack_table(peer_v[l]), tt)
    return x2, peer3


def kernel(x, mem, positions, norm_mix_g, w_in, da_lambda, da_subln_g, w_branch_a, w_branch_b, gate_bias, w_out, norm_mem_g, mem_kv_norm_g, w_mem_q, w_mem_kv, w_mem_o, norm_ffn_g, peer_w_q, peer_sub_keys, peer_u, peer_v, final_norm_g):
    b, s_len, d = x.shape
    m_tok = mem.shape[1]
    n = b * s_len
    depth = w_in.shape[0]
    x2 = x.reshape(n, d)
    mem2 = mem.reshape(b * m_tok, d)
    pos = positions.astype(F32).reshape(n, 1)
    inv_freq = ROPE_THETA ** (-(jnp.arange(ROPE_HALF, dtype=F32) * 2.0) / ROPE_DIM)
    invf = jnp.tile(inv_freq, LANES // ROPE_HALF).reshape(1, LANES)
    peer3 = None
    for l in range(depth):
        if peer3 is not None:
            x2 = x2 + peer3.reshape(n, d)
        x2, peer3 = _layer(l, x2, mem2, pos, invf, b, s_len, m_tok, norm_mix_g, w_in, da_lambda, da_subln_g,
                           w_branch_a, w_branch_b, gate_bias, w_out, norm_mem_g, mem_kv_norm_g, w_mem_q, w_mem_kv,
                           w_mem_o, norm_ffn_g, peer_w_q, peer_sub_keys, peer_u, peer_v)
    out = _final(x2, peer3, final_norm_g.reshape(1, d), _tile(n, 256))
    return out.reshape(b, s_len, d)
```

```python
import functools
import math

import jax
import jax.numpy as jnp
from jax import lax
from jax.experimental import pallas as pl
from jax.experimental.pallas import tpu as pltpu

F32 = jnp.float32
BF16 = jnp.bfloat16
I32 = jnp.int32

D_MODEL = 1024
EPS = 1e-6
ROPE_THETA = 500000.0
ROT_HEAD_DIM = 64
ROPE_DIM = 16
ROPE_HALF = ROPE_DIM // 2

DA_HEADS = 4
DA_QK_DIM = 64
DA_V_DIM = 128
DA_WIDTH = 512
DS_HEADS = 8
DS_KV_HEADS = 2
DS_HEAD_DIM = 64
DS_WIDTH = 512
IDX_HEADS = 8
IDX_DIM = 64
TOPK_MAX = 256
MEM_HEADS = 4
MEM_HEAD_DIM = 128
MEM_WIDTH = 512
PEER_HEADS = 8
PEER_N_KEYS = 128
PEER_HALF = 64
PEER_TOPK = 16
PEER_SEL = PEER_HEADS * PEER_TOPK

COL_SIZES = (512, 512, 512, 512, 128, 128, 512, 64, 8, 2048)

LANES = 128
SUBLANES = 8
NEG_BIG = -1e30
INT_MIN = -(2 ** 31)
VMEM_LIMIT = 56 * 1024 * 1024

NT_DIMS = (((1,), (1,)), ((), ()))


def _rms(x, g):
    var = jnp.mean(x * x, axis=-1, keepdims=True)
    return x * lax.rsqrt(var + EPS) * g


def _rope_rows(y, cos8, sin8):
    pieces = []
    for r0 in range(0, y.shape[0], ROT_HEAD_DIM):
        t1 = y[r0:r0 + ROPE_HALF]
        t2 = y[r0 + ROPE_HALF:r0 + ROPE_DIM]
        pieces += [t1 * cos8 - t2 * sin8, t2 * cos8 + t1 * sin8, y[r0 + ROPE_DIM:r0 + ROT_HEAD_DIM]]
    return jnp.concatenate(pieces, axis=0)


def _in_proj_kernel(x_ref, g_ref, pos_ref, invf_ref, pos_t_ref, invf8_ref, bias_ref, wr_ref, wg_ref,
                    wqt_ref, wvt_ref, wwt_ref, pr_ref, gate_ref, qt_ref, vt_ref, iwt_ref):
    h = _rms(x_ref[...], g_ref[...]).astype(BF16)
    tm = h.shape[0]
    ang_t = invf8_ref[...] * pos_t_ref[...]
    cos8 = jnp.cos(ang_t)
    sin8 = jnp.sin(ang_t)
    row_chunk = 2 * LANES
    for r0 in range(0, wqt_ref.shape[0], row_chunk):
        y = lax.dot_general(wqt_ref[r0:r0 + row_chunk, :], h, NT_DIMS, preferred_element_type=F32)
        qt_ref[r0:r0 + row_chunk, :] = _rope_rows(y, cos8, sin8).astype(qt_ref.dtype)
    for r0 in range(0, wvt_ref.shape[0], row_chunk):
        r1 = min(r0 + row_chunk, wvt_ref.shape[0])
        y = lax.dot_general(wvt_ref[r0:r1, :], h, NT_DIMS, preferred_element_type=F32)
        vt_ref[0, r0:r1, :] = y.astype(vt_ref.dtype)
    iwt_ref[...] = lax.dot_general(wwt_ref[...], h, NT_DIMS, preferred_element_type=F32)
    ang = pos_ref[...] * invf_ref[...]
    cos = jnp.cos(ang)
    sin = jnp.sin(ang)
    lane = lax.broadcasted_iota(I32, (tm, LANES), 1) % ROT_HEAD_DIM
    c_t = jnp.where(lane < ROPE_DIM, cos, 1.0)
    s_lo = jnp.where(lane < ROPE_HALF, -sin, 0.0)
    s_hi = jnp.where((lane >= ROPE_HALF) & (lane < ROPE_DIM), sin, 0.0)
    n_r = wr_ref.shape[1]
    for c0 in range(0, n_r, 512):
        c1 = min(c0 + 512, n_r)
        w = c1 - c0
        y = jnp.dot(h, wr_ref[:, c0:c1], preferred_element_type=F32)
        reps = w // LANES
        ct = jnp.concatenate([c_t] * reps, axis=1)
        sl = jnp.concatenate([s_lo] * reps, axis=1)
        sh = jnp.concatenate([s_hi] * reps, axis=1)
        y = y * ct + pltpu.roll(y, w - ROPE_HALF, 1) * sl + pltpu.roll(y, ROPE_HALF, 1) * sh
        pr_ref[:, c0:c1] = y.astype(pr_ref.dtype)
    n_g = wg_ref.shape[1]
    for c0 in range(0, n_g, 512):
        y = jnp.dot(h, wg_ref[:, c0:c0 + 512], preferred_element_type=F32)
        gate_ref[:, c0:c0 + 512] = jax.nn.sigmoid(y + bias_ref[:, c0:c0 + 512])


def _in_proj(x2, g, pos, invf, pos_t, invf8, bias, wr, wg, wqt, wvt, wwt, tm):
    n = x2.shape[0]
    full = lambda a: pl.BlockSpec(a.shape, lambda i: (0, 0))
    rows = lambda w: pl.BlockSpec((tm, w), lambda i: (i, 0))
    cols = lambda r: pl.BlockSpec((r, tm), lambda i: (0, i))
    return pl.pallas_call(
        _in_proj_kernel,
        grid=(n // tm,),
        in_specs=[rows(D_MODEL), full(g), rows(1), full(invf), cols(1), full(invf8), full(bias), full(wr), full(wg),
                  full(wqt), full(wvt), full(wwt)],
        out_specs=[rows(wr.shape[1]), rows(wg.shape[1]), cols(wqt.shape[0]),
                   pl.BlockSpec((1, wvt.shape[0], tm), lambda i: (i, 0, 0)), cols(wwt.shape[0])],
        out_shape=[jax.ShapeDtypeStruct((n, wr.shape[1]), BF16), jax.ShapeDtypeStruct((n, wg.shape[1]), F32),
                   jax.ShapeDtypeStruct((wqt.shape[0], n), BF16),
                   jax.ShapeDtypeStruct((n // tm, wvt.shape[0], tm), BF16),
                   jax.ShapeDtypeStruct((wwt.shape[0], n), F32)],
        compiler_params=pltpu.CompilerParams(dimension_semantics=("arbitrary",), vmem_limit_bytes=VMEM_LIMIT),
        name="in_proj",
    )(x2, g, pos, invf, pos_t, invf8, bias, wr, wg, wqt, wvt, wwt)


QT_DA = 0
QT_DS = 4
QT_IX = 8
VT_DA = 0
VT_DS = 4
PR_DAK = 0
PR_DSK = 4
PR_IXK = 6


SUB_KEYS = 128
QK_AHEAD = 2


def _zero_after(x):
    bits = pltpu.bitcast(x, jnp.uint32)
    return pltpu.bitcast((bits >> 16) >> 16, F32)


def _flash_chains(n_chains, score_fn, value_fn, subs, m_scr, l_scr, acc_scr):
    pending = [score_fn(c) for c in range(QK_AHEAD)]
    for c in range(n_chains):
        scores = pending.pop(0)
        m = m_scr[c][0:1]
        if c + QK_AHEAD < n_chains:
            pending.append(score_fn(c + QK_AHEAD))
            m = m + _zero_after(pending[-1][0][0:1])
        m_new = m
        for s in scores:
            m_new = jnp.maximum(m_new, jnp.max(s, axis=0, keepdims=True))
        alpha = jnp.exp(m - m_new)
        l = alpha * l_scr[c][0:1]
        a = alpha * acc_scr[c]
        for r0, s in zip(subs, scores):
            p = jnp.exp(s - m_new)
            l = l + jnp.sum(p, axis=0, keepdims=True)
            v = value_fn(c, r0)
            a = a + jnp.dot(v, p.astype(v.dtype), preferred_element_type=F32)
        m_scr[c] = jnp.broadcast_to(m_new, m_scr.shape[1:])
        l_scr[c] = jnp.broadcast_to(l, l_scr.shape[1:])
        acc_scr[c] = a


def _da_kernel(lam_ref, qt_ref, k_ref, vt_ref, g_ref, o_ref, q_scr, acc_scr, m_scr, l_scr, *, tq, lam_init):
    tk = tq
    i = pl.program_id(1)
    n_chains = 2 * DA_HEADS
    frow = lax.broadcasted_iota(I32, (LANES, tq), 0)
    scale = jnp.asarray(DA_QK_DIM ** -0.5, BF16)
    for h in range(DA_HEADS):
        blk = qt_ref[h * LANES:(h + 1) * LANES, :] * scale
        zero = jnp.zeros_like(blk)
        q_scr[2 * h] = jnp.where(frow < DA_QK_DIM, blk, zero)
        q_scr[2 * h + 1] = jnp.where(frow >= DA_QK_DIM, blk, zero)
    acc_scr[...] = jnp.zeros(acc_scr.shape, F32)
    m_scr[...] = jnp.full(m_scr.shape, NEG_BIG, F32)
    l_scr[...] = jnp.zeros(l_scr.shape, F32)
    kpos0 = lax.broadcasted_iota(I32, (SUB_KEYS, tq), 0)
    qpos = i * tq + lax.broadcasted_iota(I32, (SUB_KEYS, tq), 1)
    subs = range(0, tk, SUB_KEYS)

    def chunk(j, masked):
        start = pl.multiple_of(j * tk, tk)

        def scores(c):
            head = c // 2
            out = []
            for r0 in subs:
                k = k_ref[pl.ds(start + r0, SUB_KEYS), head * LANES:(head + 1) * LANES]
                s = jnp.dot(k, q_scr[c], preferred_element_type=F32)
                if masked:
                    s = jnp.where(kpos0 + (j * tk + r0) <= qpos, s, NEG_BIG)
                out.append(s)
            return out

        def values(c, r0):
            head = c // 2
            return vt_ref[j, head * LANES:(head + 1) * LANES, r0:r0 + SUB_KEYS]

        _flash_chains(n_chains, scores, values, subs, m_scr, l_scr, acc_scr)

    def full_chunk(j, carry):
        chunk(j, False)
        return carry

    lax.fori_loop(0, i, full_chunk, 0)
    chunk(i, True)

    lp = lam_ref[...]
    lam = (jnp.exp(jnp.sum(lp[0:1] * lp[1:2], axis=1, keepdims=True))
           - jnp.exp(jnp.sum(lp[2:3] * lp[3:4], axis=1, keepdims=True)) + lam_init)
    for h in range(DA_HEADS):
        o = acc_scr[2 * h] / l_scr[2 * h][0:1] - lam * (acc_scr[2 * h + 1] / l_scr[2 * h + 1][0:1])
        var = jnp.mean(o * o, axis=0, keepdims=True)
        o = o * lax.rsqrt(var + EPS) * g_ref[...] * (1.0 - lam_init)
        o_ref[:, h * LANES:(h + 1) * LANES] = o.T.astype(o_ref.dtype)


def _diff_attention(lam_p, qt, pr, vt3, subln_g, b, s_len, tq, lam_init):
    nq = s_len // tq
    n = b * s_len
    n_chains = 2 * DA_HEADS
    kern = functools.partial(_da_kernel, tq=tq, lam_init=lam_init)
    return pl.pallas_call(
        kern,
        grid=(b, nq),
        in_specs=[
            pl.BlockSpec(lam_p.shape, lambda bi, i: (0, 0)),
            pl.BlockSpec((DA_HEADS * LANES, tq), lambda bi, i: (QT_DA // DA_HEADS, bi * nq + i)),
            pl.BlockSpec((s_len, DA_HEADS * LANES), lambda bi, i: (bi, PR_DAK // DA_HEADS)),
            pl.BlockSpec((s_len // tq, DA_HEADS * LANES, tq), lambda bi, i: (bi, VT_DA // DA_HEADS, 0)),
            pl.BlockSpec(subln_g.shape, lambda bi, i: (0, 0)),
        ],
        out_specs=pl.BlockSpec((tq, DA_WIDTH), lambda bi, i: (bi * nq + i, 0)),
        out_shape=jax.ShapeDtypeStruct((n, DA_WIDTH), BF16),
        scratch_shapes=[pltpu.VMEM((n_chains, LANES, tq), BF16), pltpu.VMEM((n_chains, DA_V_DIM, tq), F32),
                        pltpu.VMEM((n_chains, SUBLANES, tq), F32), pltpu.VMEM((n_chains, SUBLANES, tq), F32)],
        compiler_params=pltpu.CompilerParams(dimension_semantics=("arbitrary",) * 2, vmem_limit_bytes=VMEM_LIMIT),
        name="diff_attn",
    )(lam_p, qt, pr, vt3, subln_g)


def _dsa_kernel(qit_ref, ki_ref, wt_ref, qt_ref, k0_ref, k1_ref, vt_ref, o_ref,
                keys_scr, qi_scr, q_scr, acc_scr, m_scr, l_scr, cut_scr, *, tq, n_sel, n_bits):
    tk = tq
    i = pl.program_id(1)
    nck = i + 1
    kpos0 = lax.broadcasted_iota(I32, (tk, tq), 0)
    qpos = i * tq + lax.broadcasted_iota(I32, (tk, tq), 1)
    frow = lax.broadcasted_iota(I32, (LANES, tq), 0)
    per_grp = DS_HEADS // DS_KV_HEADS

    scale = jnp.asarray(DS_HEAD_DIM ** -0.5, BF16)
    for h in range(DS_HEADS):
        keep = (frow < DS_HEAD_DIM) if h % 2 == 0 else (frow >= DS_HEAD_DIM)
        rows = slice((h // 2) * LANES, (h // 2 + 1) * LANES)
        blk = qit_ref[rows, :]
        qi_scr[h] = jnp.where(keep, blk, jnp.zeros_like(blk))
        blk = qt_ref[rows, :] * scale
        q_scr[h] = jnp.where(keep, blk, jnp.zeros_like(blk))

    def score_chunk(c, carry):
        kc = ki_ref[pl.ds(pl.multiple_of(c * tk, tk), tk), :]
        acc = jnp.zeros((tk, tq), F32)
        for h in range(IDX_HEADS):
            s = jnp.dot(kc, qi_scr[h], preferred_element_type=F32)
            acc = acc + wt_ref[h:h + 1, :] * jnp.maximum(s, 0.0)
        bits = pltpu.bitcast(acc, I32)
        key = bits ^ ((bits >> 31) & 0x7FFFFFFF)
        key = jnp.where(acc == 0.0, 0, key)
        keys_scr[c] = jnp.where(kpos0 + c * tk <= qpos, key, INT_MIN)
        return carry

    lax.fori_loop(0, nck, score_chunk, 0)

    def count(pred):
        def body(c, acc):
            hit = pred(keys_scr[c], kpos0 + c * tk)
            return acc + jnp.sum(hit.reshape(tk // SUBLANES, SUBLANES, tq), axis=0)
        acc = lax.fori_loop(0, nck, body, jnp.zeros((SUBLANES, tq), I32))
        return jnp.sum(acc, axis=0, keepdims=True)

    def thr_bit(t, ans_u):
        cand_u = ans_u | jnp.left_shift(jnp.int32(1), 31 - t)
        cand = cand_u ^ INT_MIN
        cnt = count(lambda k, kpos: jnp.where(k >= cand, 1, 0))
        return jnp.where(cnt >= n_sel, cand_u, ans_u)

    thr = lax.fori_loop(0, 32, thr_bit, jnp.zeros((1, tq), I32)) ^ INT_MIN
    n_ge = count(lambda k, kpos: jnp.where(k >= thr, 1, 0))
    below_all = thr == INT_MIN
    cut_scr[...] = jnp.where(below_all, -1, (1 << n_bits) - 1)
    excess = jnp.max(jnp.where(below_all, 0, n_ge - n_sel))

    @pl.when(excess > 0)
    def _():
        n_gt = count(lambda k, kpos: jnp.where(k > thr, 1, 0))
        need = n_sel - n_gt

        def tie_bit(t, cut):
            cand = cut | jnp.left_shift(jnp.int32(1), n_bits - 1 - t)
            cnt = count(lambda k, kpos: jnp.where(k == thr, jnp.where(kpos < cand, 1, 0), 0))
            return jnp.where(cnt < need, cand, cut)

        cut = lax.fori_loop(0, n_bits, tie_bit, jnp.zeros((1, tq), I32))
        cut_scr[...] = jnp.where(below_all, -1, cut)

    cut = cut_scr[...]

    def bias_chunk(c, carry):
        k = keys_scr[c]
        sel = (k > thr) | ((k == thr) & (kpos0 + c * tk <= cut))
        keys_scr[c] = pltpu.bitcast(jnp.where(sel, 0.0, NEG_BIG).astype(F32), I32)
        return carry

    lax.fori_loop(0, nck, bias_chunk, 0)

    acc_scr[...] = jnp.zeros(acc_scr.shape, F32)
    m_scr[...] = jnp.full(m_scr.shape, NEG_BIG, F32)
    l_scr[...] = jnp.zeros(l_scr.shape, F32)

    def att_chunk(c, carry):
        start = pl.multiple_of(c * tk, tk)
        subs = range(0, tk, SUB_KEYS)

        def head_scores(h):
            k_ref = k0_ref if h // per_grp == 0 else k1_ref
            out = []
            for r0 in subs:
                s = jnp.dot(k_ref[pl.ds(start + r0, SUB_KEYS), :], q_scr[h], preferred_element_type=F32)
                out.append(s + pltpu.bitcast(keys_scr[c, r0:r0 + SUB_KEYS, :], F32))
            return out

        def head_values(h, r0):
            grp = h // per_grp
            return vt_ref[c, grp * DS_HEAD_DIM:(grp + 1) * DS_HEAD_DIM, r0:r0 + SUB_KEYS]

        _flash_chains(DS_HEADS, head_scores, head_values, subs, m_scr, l_scr, acc_scr)
        return carry

    lax.fori_loop(0, nck, att_chunk, 0)
    outs = [acc_scr[h] / l_scr[h][0:1] for h in range(DS_HEADS)]
    o_ref[...] = jnp.concatenate(outs, axis=0).T.astype(o_ref.dtype)


def _dsa_attention(qt, pr, vt3, iwt, b, s_len, tq):
    nq = s_len // tq
    n = b * s_len
    n_sel = min(TOPK_MAX, s_len // 4)
    n_bits = max(1, (s_len - 1).bit_length())
    kern = functools.partial(_dsa_kernel, tq=tq, n_sel=n_sel, n_bits=n_bits)
    seq = lambda blk: pl.BlockSpec((s_len, LANES), lambda bi, i: (bi, blk))
    qblk = lambda blk: pl.BlockSpec((4 * LANES, tq), lambda bi, i: (blk // 4, bi * nq + i))
    return pl.pallas_call(
        kern,
        grid=(b, nq),
        in_specs=[
            qblk(QT_IX), seq(PR_IXK),
            pl.BlockSpec((IDX_HEADS, tq), lambda bi, i: (0, bi * nq + i)),
            qblk(QT_DS), seq(PR_DSK), seq(PR_DSK + 1),
            pl.BlockSpec((s_len // tq, LANES, tq), lambda bi, i: (bi, VT_DS, 0)),
        ],
        out_specs=pl.BlockSpec((tq, DS_WIDTH), lambda bi, i: (bi * nq + i, 0)),
        out_shape=jax.ShapeDtypeStruct((n, DS_WIDTH), BF16),
        scratch_shapes=[pltpu.VMEM((nq, tq, tq), I32),
                        pltpu.VMEM((IDX_HEADS, LANES, tq), BF16), pltpu.VMEM((DS_HEADS, LANES, tq), BF16),
                        pltpu.VMEM((DS_HEADS, DS_HEAD_DIM, tq), F32),
                        pltpu.VMEM((DS_HEADS, SUBLANES, tq), F32), pltpu.VMEM((DS_HEADS, SUBLANES, tq), F32),
                        pltpu.VMEM((1, tq), I32)],
        compiler_params=pltpu.CompilerParams(dimension_semantics=("arbitrary",) * 2, vmem_limit_bytes=VMEM_LIMIT),
        name="dsa_attn",
    )(qt, pr, iwt, qt, pr, pr, vt3)


def _merge_kernel(oa_ref, ob_ref, gate_ref, x_ref, wa_ref, wb_ref, wo_ref, o_ref):
    ya = jnp.dot(oa_ref[...], wa_ref[...], preferred_element_type=F32)
    yb = jnp.dot(ob_ref[...], wb_ref[...], preferred_element_type=F32)
    mix = gate_ref[:, :D_MODEL] * ya + gate_ref[:, D_MODEL:] * yb
    o_ref[...] = x_ref[...] + jnp.dot(mix.astype(BF16), wo_ref[...], preferred_element_type=F32)


def _merge(oa, ob, gate, x2, wa, wb, wo, tm):
    n = x2.shape[0]
    full = lambda a: pl.BlockSpec(a.shape, lambda i: (0, 0))
    rows = lambda w: pl.BlockSpec((tm, w), lambda i: (i, 0))
    return pl.pallas_call(
        _merge_kernel,
        grid=(n // tm,),
        in_specs=[rows(DA_WIDTH), rows(DS_WIDTH), rows(2 * D_MODEL), rows(D_MODEL), full(wa), full(wb), full(wo)],
        out_specs=rows(D_MODEL),
        out_shape=jax.ShapeDtypeStruct((n, D_MODEL), F32),
        compiler_params=pltpu.CompilerParams(dimension_semantics=("arbitrary",), vmem_limit_bytes=VMEM_LIMIT),
        name="merge",
    )(oa, ob, gate, x2, wa, wb, wo)


def _norm_matmul_kernel(x_ref, g_ref, w_ref, o_ref):
    h = _rms(x_ref[...], g_ref[...]).astype(BF16)
    o_ref[...] = jnp.dot(h, w_ref[...], preferred_element_type=F32).astype(o_ref.dtype)


def _norm_matmul(x2, g, w, tm):
    n = x2.shape[0]
    return pl.pallas_call(
        _norm_matmul_kernel,
        grid=(n // tm,),
        in_specs=[pl.BlockSpec((tm, x2.shape[1]), lambda i: (i, 0)), pl.BlockSpec(g.shape, lambda i: (0, 0)),
                  pl.BlockSpec(w.shape, lambda i: (0, 0))],
        out_specs=pl.BlockSpec((tm, w.shape[1]), lambda i: (i, 0)),
        out_shape=jax.ShapeDtypeStruct((n, w.shape[1]), BF16),
        compiler_params=pltpu.CompilerParams(dimension_semantics=("arbitrary",), vmem_limit_bytes=VMEM_LIMIT),
        name="mem_kv_proj",
    )(x2, g, w)


def _mem_attn_kernel(x_ref, g_ref, wq_ref, kv_ref, wo_ref, o_ref):
    x = x_ref[...]
    hn = _rms(x, g_ref[...]).astype(BF16)
    q = jnp.dot(hn, wq_ref[...], preferred_element_type=F32).astype(BF16)
    scale = MEM_HEAD_DIM ** -0.5
    heads = []
    for h in range(MEM_HEADS):
        qh = q[:, h * MEM_HEAD_DIM:(h + 1) * MEM_HEAD_DIM]
        kh = kv_ref[:, h * MEM_HEAD_DIM:(h + 1) * MEM_HEAD_DIM]
        vh = kv_ref[:, MEM_WIDTH + h * MEM_HEAD_DIM:MEM_WIDTH + (h + 1) * MEM_HEAD_DIM]
        s = lax.dot_general(qh, kh, NT_DIMS, preferred_element_type=F32) * scale
        m = jnp.max(s, axis=1, keepdims=True)
        p = jnp.exp(s - m)
        l = jnp.sum(p, axis=1, keepdims=True)
        heads.append(jnp.dot(p.astype(BF16), vh, preferred_element_type=F32) / l)
    o = jnp.concatenate(heads, axis=1).astype(BF16)
    o_ref[...] = x + jnp.dot(o, wo_ref[...], preferred_element_type=F32)


def _mem_attn(x2, g, wq, kv, wo, b, s_len, m_tok, tm):
    nb = s_len // tm
    n = x2.shape[0]
    full = lambda a: pl.BlockSpec(a.shape, lambda bi, i: (0, 0))
    return pl.pallas_call(
        _mem_attn_kernel,
        grid=(b, nb),
        in_specs=[pl.BlockSpec((tm, D_MODEL), lambda bi, i: (bi * nb + i, 0)), full(g), full(wq),
                  pl.BlockSpec((m_tok, 2 * MEM_WIDTH), lambda bi, i: (bi, 0)), full(wo)],
        out_specs=pl.BlockSpec((tm, D_MODEL), lambda bi, i: (bi * nb + i, 0)),
        out_shape=jax.ShapeDtypeStruct((n, D_MODEL), F32),
        compiler_params=pltpu.CompilerParams(dimension_semantics=("arbitrary",) * 2, vmem_limit_bytes=VMEM_LIMIT),
        name="mem_attn",
    )(x2, g, wq, kv, wo)


N_SUB = 2 * PEER_HEADS
EXPERT_BITS = 14


def _route_kernel(x_ref, g_ref, wq_ref, keys_ref, hf_ref, idx_ref, row_ref, gate_ref, sc_scr, ts_scr, ti_scr):
    hf = _rms(x_ref[...], g_ref[...])
    tm = hf.shape[0]
    hb = hf.astype(BF16)
    bits = pltpu.bitcast(hb.astype(F32), jnp.uint32)
    half_d = D_MODEL // 2
    for s in range(SUBLANES // 2):
        lo = bits[:, s * LANES:(s + 1) * LANES] >> 16
        hi = bits[:, half_d + s * LANES:half_d + (s + 1) * LANES] & jnp.uint32(0xFFFF0000)
        hf_ref[:, s, :] = lo | hi
        hf_ref[:, s + SUBLANES // 2, :] = lo | hi
    q = jnp.dot(hb, wq_ref[...], preferred_element_type=F32).astype(BF16)
    for g in range(N_SUB):
        blk = q[:, (g // 2) * LANES:(g // 2 + 1) * LANES]
        sc_scr[g] = lax.dot_general(keys_ref[g], blk, NT_DIMS, preferred_element_type=F32)

    key_id = lax.broadcasted_iota(I32, (PEER_N_KEYS, tm), 0)

    def sub_topk(g, carry):
        x = sc_scr[g]
        vals, ids = [], []
        for _ in range(PEER_TOPK):
            m = jnp.max(x, axis=0, keepdims=True)
            idx = jnp.min(jnp.where(x == m, key_id, PEER_N_KEYS), axis=0, keepdims=True)
            vals.append(m)
            ids.append(idx)
            x = jnp.where(key_id == idx, -jnp.inf, x)
        ts_scr[g] = jnp.concatenate(vals, axis=0)
        ti_scr[g] = jnp.concatenate(ids, axis=0)
        return carry

    lax.fori_loop(0, N_SUB, sub_topk, 0)

    iota16 = lax.broadcasted_iota(I32, (PEER_TOPK, tm), 0)
    iota8 = lax.broadcasted_iota(I32, (SUBLANES, tm), 0)
    lead_rows = SUBLANES

    def head_topk(h, carry):
        s0, s1 = ts_scr[2 * h], ts_scr[2 * h + 1]
        i0, i1 = ti_scr[2 * h], ti_scr[2 * h + 1]
        cands, codes = [], []
        for a in range(lead_rows):
            n_j = PEER_TOPK if a == 0 else SUBLANES
            cands.append(s0[a:a + 1] + s1[:n_j])
            j_iota = iota16 if a == 0 else iota8
            codes.append(((a * PEER_TOPK + j_iota) << EXPERT_BITS) | (i0[a:a + 1] * PEER_N_KEYS + i1[:n_j]))
        cands.append(s0[lead_rows:] + s1[0:1])
        codes.append((((iota8 + lead_rows) * PEER_TOPK) << EXPERT_BITS) | (i0[lead_rows:] * PEER_N_KEYS + i1[0:1]))
        cand = jnp.concatenate(cands, axis=0)
        code = jnp.concatenate(codes, axis=0)
        big = jnp.int32(2 ** 30)
        vals, ids = [], []
        for _ in range(PEER_TOPK):
            m = jnp.max(cand, axis=0, keepdims=True)
            best = jnp.min(jnp.where(cand == m, code, big), axis=0, keepdims=True)
            vals.append(m)
            ids.append(best & (2 ** EXPERT_BITS - 1))
            cand = jnp.where(code == best, -jnp.inf, cand)
        best_s = jnp.concatenate(vals, axis=0)
        e = jnp.exp(best_s - best_s[0:1])
        gate_ref[h] = e / jnp.sum(e, axis=0, keepdims=True)
        best_i = jnp.concatenate(ids, axis=0)
        idx_ref[h] = best_i
        row_ref[h] = (best_i >> 1) * SUBLANES
        return carry

    lax.fori_loop(0, PEER_HEADS, head_topk, 0)


def _route(x2, g, wq, keys_p, tm):
    n = x2.shape[0]
    return pl.pallas_call(
        _route_kernel,
        grid=(n // tm,),
        in_specs=[pl.BlockSpec((tm, D_MODEL), lambda i: (i, 0)), pl.BlockSpec(g.shape, lambda i: (0, 0)),
                  pl.BlockSpec(wq.shape, lambda i: (0, 0)), pl.BlockSpec(keys_p.shape, lambda i: (0, 0, 0))],
        out_specs=[pl.BlockSpec((tm, SUBLANES, LANES), lambda i: (i, 0, 0))]
        + [pl.BlockSpec((PEER_HEADS, PEER_TOPK, tm), lambda i: (0, 0, i))] * 3,
        out_shape=[jax.ShapeDtypeStruct((n, SUBLANES, LANES), jnp.uint32),
                   jax.ShapeDtypeStruct((PEER_HEADS, PEER_TOPK, n), I32),
                   jax.ShapeDtypeStruct((PEER_HEADS, PEER_TOPK, n), I32),
                   jax.ShapeDtypeStruct((PEER_HEADS, PEER_TOPK, n), F32)],
        scratch_shapes=[pltpu.VMEM((N_SUB, PEER_N_KEYS, tm), F32), pltpu.VMEM((N_SUB, PEER_TOPK, tm), F32),
                        pltpu.VMEM((N_SUB, PEER_TOPK, tm), I32)],
        compiler_params=pltpu.CompilerParams(dimension_semantics=("arbitrary",), vmem_limit_bytes=VMEM_LIMIT),
        name="peer_route",
    )(x2, g, wq, keys_p)


TILE_ROWS = 2 * SUBLANES
PAIRS_PER_KTILE = 16
TOK_UNROLL = 16
SUM_UNROLL = 16


def _pack_table(t):
    e = t.shape[0]
    bits = lax.bitcast_convert_type(t.astype(BF16), jnp.uint16).astype(jnp.uint32)
    words = bits[:, :D_MODEL // 2] | (bits[:, D_MODEL // 2:] << 16)
    return words.reshape(e * (SUBLANES // 2), LANES)


def _gather_tiles(off_ref, tab_ref, u):
    tiles = []
    for k in range(PEER_SEL):
        start = pl.multiple_of(off_ref[u * PEER_SEL + k], SUBLANES)
        tiles.append(pltpu.bitcast(tab_ref[pl.ds(start, SUBLANES), :], BF16))
    return tiles


def _for_trips(rows_hbm, bufs, sems, tok0, unroll, n_trip, body):
    assert n_trip % 2 == 0
    words = unroll * PEER_SEL

    def fetch(trip, par):
        src = rows_hbm.at[pl.ds(pl.multiple_of((tok0 + trip * unroll) * PEER_SEL, words), words)]
        return pltpu.make_async_copy(src, bufs[par], sems.at[par])

    fetch(0, 0).start()
    fetch(1, 1).start()

    def pair(p, carry):
        for par in range(2):
            trip = 2 * p + par
            fetch(trip, par).wait()
            body(trip, bufs[par], par)
            fetch(jnp.minimum(trip + 2, n_trip - 2 + par), par).start()
        return carry

    lax.fori_loop(0, n_trip // 2, pair, 0)
    for par in range(2):
        fetch(n_trip - 2 + par, par).wait()


def _peer_dot_kernel(rows_hbm, tab_ref, h_ref, gate_ref, idxv_ref, sel_ref, ce_ref, co_ref,
                     ye_scr, yo_scr, ze_scr, zo_scr, off_a, off_b, sems, *, tt):
    ones = jnp.ones((SUBLANES, LANES), BF16)
    half_rows = PAIRS_PER_KTILE
    n_trip = tt // TOK_UNROLL

    def row_sums(i, slot, off_ref):
        for u in range(TOK_UNROLL):
            t = i * TOK_UNROLL + u
            hp = pltpu.bitcast(h_ref[t], BF16)
            prods = [tile * hp for tile in _gather_tiles(off_ref, tab_ref, u)]
            for kt in range(PEER_SEL // PAIRS_PER_KTILE):
                stack = jnp.concatenate(prods[kt * PAIRS_PER_KTILE:(kt + 1) * PAIRS_PER_KTILE], axis=0)
                y = jnp.dot(sel_ref[...], stack, preferred_element_type=F32)
                r0 = u * PEER_SEL + kt * half_rows
                ye_scr[slot, r0:r0 + half_rows, :] = y[:half_rows].astype(BF16)
                yo_scr[slot, r0:r0 + half_rows, :] = y[half_rows:].astype(BF16)

    def lane_sums(i, slot):
        ze = lax.dot_general(ones, ye_scr[slot], NT_DIMS, preferred_element_type=F32)
        zo = lax.dot_general(ones, yo_scr[slot], NT_DIMS, preferred_element_type=F32)
        for u in range(TOK_UNROLL):
            t = i * TOK_UNROLL + u
            ze_scr[pl.ds(t, 1), :] = ze[0:1, u * PEER_SEL:(u + 1) * PEER_SEL]
            zo_scr[pl.ds(t, 1), :] = zo[0:1, u * PEER_SEL:(u + 1) * PEER_SEL]

    def trip(i, off_ref, par):
        lane_sums(jnp.maximum(i - 1, 0), 1 - par)
        row_sums(i, par, off_ref)

    ye_scr[1] = jnp.zeros(ye_scr.shape[1:], ye_scr.dtype)
    yo_scr[1] = jnp.zeros(yo_scr.shape[1:], yo_scr.dtype)
    _for_trips(rows_hbm, (off_a, off_b), sems, pl.program_id(0) * tt, TOK_UNROLL, n_trip, trip)
    lane_sums(n_trip - 1, (n_trip - 1) & 1)
    even = (idxv_ref[...] & 1) == 0
    a = jnp.where(even, ze_scr[...], zo_scr[...])
    c = 0.5 * a * (1.0 + lax.erf(a * (2.0 ** -0.5))) * gate_ref[...]
    ce_ref[...] = jnp.where(even, c, 0.0)
    co_ref[...] = jnp.where(even, 0.0, c)


def _peer_dot(rows_flat, tab, hpk, gate, idx2, sel, tt):
    n = hpk.shape[0]
    kern = functools.partial(_peer_dot_kernel, tt=tt)
    rows = pl.BlockSpec((tt, PEER_SEL), lambda i: (i, 0))
    return pl.pallas_call(
        kern,
        grid=(n // tt,),
        in_specs=[pl.BlockSpec(memory_space=pl.ANY),
                  pl.BlockSpec(memory_space=pltpu.VMEM),
                  pl.BlockSpec((tt, SUBLANES, LANES), lambda i: (i, 0, 0)),
                  rows, rows, pl.BlockSpec(sel.shape, lambda i: (0, 0))],
        out_specs=[rows, rows],
        out_shape=[jax.ShapeDtypeStruct((n, PEER_SEL), F32), jax.ShapeDtypeStruct((n, PEER_SEL), F32)],
        scratch_shapes=[pltpu.VMEM((2, TOK_UNROLL * PEER_SEL, LANES), BF16)] * 2
        + [pltpu.VMEM((tt, PEER_SEL), F32)] * 2
        + [pltpu.SMEM((TOK_UNROLL * PEER_SEL,), I32)] * 2 + [pltpu.SemaphoreType.DMA((2,))],
        compiler_params=pltpu.CompilerParams(dimension_semantics=("arbitrary",), vmem_limit_bytes=VMEM_LIMIT),
        name="peer_dot",
    )(rows_flat, tab, hpk, gate, idx2, sel)


def _peer_sum_kernel(rows_hbm, ce_ref, co_ref, spread_ref, tab_ref, o_ref, m1_scr, m2_scr, off_a, off_b, sems,
                     *, tt):
    width = PEER_SEL * TILE_ROWS
    cc = jnp.concatenate([ce_ref[...], co_ref[...]], axis=1)
    c1 = cc.astype(BF16)
    c2 = (cc - c1.astype(F32)).astype(BF16)
    m1_scr[...] = jnp.dot(c1, spread_ref[...], preferred_element_type=F32)
    m2_scr[...] = jnp.dot(c2, spread_ref[...], preferred_element_type=F32)
    row = lax.broadcasted_iota(I32, (SUBLANES, width), 0)
    lane = lax.broadcasted_iota(I32, (SUBLANES, width), 1)
    half = SUBLANES // 2
    on_row = (lane & (SUBLANES - 1)) == 2 * (row % half) + row // half

    def token(t, off_ref, u):
        w = jnp.concatenate(_gather_tiles(off_ref, tab_ref, u), axis=0)
        lhs = []
        for m_scr in (m1_scr, m2_scr):
            coef = jnp.broadcast_to(m_scr[pl.ds(t, 1), :], (SUBLANES, width))
            lhs.append(jnp.where(on_row, coef, 0.0).astype(BF16))
        res = jnp.dot(jnp.concatenate(lhs, axis=0), w, preferred_element_type=F32)
        o_ref[t] = res[:SUBLANES] + res[SUBLANES:]

    def trip(i, off_ref, par):
        for u in range(SUM_UNROLL):
            token(i * SUM_UNROLL + u, off_ref, u)

    _for_trips(rows_hbm, (off_a, off_b), sems, pl.program_id(0) * tt, SUM_UNROLL, tt // SUM_UNROLL, trip)


def _peer_sum(rows_flat, ce, co, spread, tab, tt):
    n = ce.shape[0]
    kern = functools.partial(_peer_sum_kernel, tt=tt)
    rows = pl.BlockSpec((tt, PEER_SEL), lambda i: (i, 0))
    width = PEER_SEL * TILE_ROWS
    return pl.pallas_call(
        kern,
        grid=(n // tt,),
        in_specs=[pl.BlockSpec(memory_space=pl.ANY), rows, rows,
                  pl.BlockSpec(spread.shape, lambda i: (0, 0)), pl.BlockSpec(memory_space=pltpu.VMEM)],
        out_specs=pl.BlockSpec((tt, SUBLANES, LANES), lambda i: (i, 0, 0)),
        out_shape=jax.ShapeDtypeStruct((n, SUBLANES, LANES), F32),
        scratch_shapes=[pltpu.VMEM((tt, width), F32), pltpu.VMEM((tt, width), F32)]
        + [pltpu.SMEM((SUM_UNROLL * PEER_SEL,), I32)] * 2 + [pltpu.SemaphoreType.DMA((2,))],
        compiler_params=pltpu.CompilerParams(dimension_semantics=("arbitrary",), vmem_limit_bytes=VMEM_LIMIT),
        name="peer_sum",
    )(rows_flat, ce, co, spread, tab)


def _final_kernel(x_ref, p_ref, g_ref, o_ref):
    peer = jnp.concatenate([p_ref[:, s, :] for s in range(SUBLANES)], axis=1)
    o_ref[...] = _rms(x_ref[...] + peer, g_ref[...])


def _final(x2, peer3, g, tm):
    n = x2.shape[0]
    return pl.pallas_call(
        _final_kernel,
        grid=(n // tm,),
        in_specs=[pl.BlockSpec((tm, D_MODEL), lambda i: (i, 0)),
                  pl.BlockSpec((tm, SUBLANES, LANES), lambda i: (i, 0, 0)),
                  pl.BlockSpec(g.shape, lambda i: (0, 0))],
        out_specs=pl.BlockSpec((tm, D_MODEL), lambda i: (i, 0)),
        out_shape=jax.ShapeDtypeStruct((n, D_MODEL), F32),
        compiler_params=pltpu.CompilerParams(dimension_semantics=("arbitrary",), vmem_limit_bytes=VMEM_LIMIT),
        name="final_norm",
    )(x2, peer3, g)


def _tile(n, pref):
    t = pref
    while n % t:
        t //= 2
    return t


def _layer(l, x2, mem2, pos, invf, b, s_len, m_tok, norm_mix_g, w_in, da_lambda, da_subln_g, w_branch_a,
           w_branch_b, gate_bias, w_out, norm_mem_g, mem_kv_norm_g, w_mem_q, w_mem_kv, w_mem_o, norm_ffn_g,
           peer_w_q, peer_sub_keys, peer_u, peer_v):
    n = b * s_len
    row2 = lambda v: v.reshape(1, -1)
    splits = [0]
    for c in COL_SIZES:
        splits.append(splits[-1] + c)
    da_q, da_k, da_v, ds_q, ds_k, ds_v, ix_q, ix_k, ix_w, gates = (
        w_in[l][:, splits[j]:splits[j + 1]] for j in range(len(COL_SIZES)))
    k0, k1 = ds_k[:, :DS_HEAD_DIM], ds_k[:, DS_HEAD_DIM:]
    w_keys = jnp.concatenate([da_k, k0, k0, k1, k1, ix_k, ix_k], axis=1).astype(BF16)
    w_q_t = jnp.concatenate([da_q, ds_q, ix_q], axis=1).T.astype(BF16)
    w_v_t = jnp.concatenate([da_v, ds_v], axis=1).T.astype(BF16)
    t_att = _tile(s_len, 256)
    pr, gate, qt, vt3, iwt = _in_proj(x2, row2(norm_mix_g[l]), pos, invf, pos.reshape(1, n),
                                      invf[:, :ROPE_HALF].reshape(ROPE_HALF, 1), row2(gate_bias[l]), w_keys,
                                      gates.astype(BF16), w_q_t, w_v_t, ix_w.T.astype(BF16), t_att)

    lam_init = 0.8 - 0.6 * math.exp(-0.3 * l)
    o_a = _diff_attention(da_lambda[l], qt, pr, vt3, da_subln_g[l].reshape(-1, 1), b, s_len, t_att, lam_init)
    o_b = _dsa_attention(qt, pr, vt3, iwt, b, s_len, t_att)
    x2 = _merge(o_a, o_b, gate, x2, w_branch_a[l].astype(BF16), w_branch_b[l].astype(BF16),
                w_out[l].astype(BF16), _tile(n, 256))

    kv = _norm_matmul(mem2, row2(mem_kv_norm_g[l]), w_mem_kv[l].astype(BF16), _tile(mem2.shape[0], 256))
    x2 = _mem_attn(x2, row2(norm_mem_g[l]), w_mem_q[l].astype(BF16), kv, w_mem_o[l].astype(BF16),
                   b, s_len, m_tok, _tile(s_len, 256))

    sk = peer_sub_keys[l].reshape(N_SUB, PEER_N_KEYS, PEER_HALF)
    z = jnp.zeros_like(sk)
    keys_p = jnp.where((jnp.arange(N_SUB) % 2 == 0)[:, None, None],
                       jnp.concatenate([sk, z], axis=2), jnp.concatenate([z, sk], axis=2)).astype(BF16)
    hpk, idx_t, row_t, gate_t = _route(x2, row2(norm_ffn_g[l]), peer_w_q[l].astype(BF16), keys_p, _tile(n, 256))
    idx2 = idx_t.reshape(PEER_SEL, n).T
    rows2 = row_t.reshape(PEER_SEL, n).T
    gate2 = gate_t.reshape(PEER_SEL, n).T
    tt = _tile(n, 256)
    r32 = jnp.arange(2 * PAIRS_PER_KTILE)[:, None]
    c256 = jnp.arange(PAIRS_PER_KTILE * TILE_ROWS)[None, :]
    sel = ((c256 // TILE_ROWS == r32 % PAIRS_PER_KTILE)
           & ((c256 % TILE_ROWS) // SUBLANES == r32 // PAIRS_PER_KTILE)).astype(BF16)
    k256 = jnp.arange(2 * PEER_SEL)[:, None]
    c2048 = jnp.arange(PEER_SEL * TILE_ROWS)[None, :]
    spread = ((c2048 // TILE_ROWS == k256 % PEER_SEL)
              & ((c2048 % TILE_ROWS) // SUBLANES == k256 // PEER_SEL)).astype(BF16)
    rows_flat = rows2.reshape(-1)
    ce2, co2 = _peer_dot(rows_flat, _pack_table(peer_u[l]), hpk, gate2, idx2, sel, tt)
    peer3 = _peer_sum(rows_flat, ce2, co2, spread, _pack_table(peer_v[l]), tt)
    return x2, peer3


def kernel(x, mem, positions, norm_mix_g, w_in, da_lambda, da_subln_g, w_branch_a, w_branch_b, gate_bias, w_out, norm_mem_g, mem_kv_norm_g, w_mem_q, w_mem_kv, w_mem_o, norm_ffn_g, peer_w_q, peer_sub_keys, peer_u, peer_v, final_norm_g):
    b, s_len, d = x.shape
    m_tok = mem.shape[1]
    n = b * s_len
    depth = w_in.shape[0]
    x2 = x.reshape(n, d)
    mem2 = mem.reshape(b * m_tok, d)
    pos = positions.astype(F32).reshape(n, 1)
    inv_freq = ROPE_THETA ** (-(jnp.arange(ROPE_HALF, dtype=F32) * 2.0) / ROPE_DIM)
    invf = jnp.tile(inv_freq, LANES // ROPE_HALF).reshape(1, LANES)
    peer3 = None
    for l in range(depth):
        if peer3 is not None:
            x2 = x2 + peer3.reshape(n, d)
        x2, peer3 = _layer(l, x2, mem2, pos, invf, b, s_len, m_tok, norm_mix_g, w_in, da_lambda, da_subln_g,
                           w_branch_a, w_branch_b, gate_bias, w_out, norm_mem_g, mem_kv_norm_g, w_mem_q, w_mem_kv,
                           w_mem_o, norm_ffn_g, peer_w_q, peer_sub_keys, peer_u, peer_v)
    out = _final(x2, peer3, final_norm_g.reshape(1, d), _tile(n, 256))
    return out.reshape(b, s_len, d)
```

```python
import functools
import math

import jax
import jax.numpy as jnp
from jax import lax
from jax.experimental import pallas as pl
from jax.experimental.pallas import tpu as pltpu

F32 = jnp.float32
BF16 = jnp.bfloat16
I32 = jnp.int32
I16 = jnp.int16

D_MODEL = 1024
EPS = 1e-6
ROPE_THETA = 500000.0
ROT_HEAD_DIM = 64
ROPE_DIM = 16
ROPE_HALF = ROPE_DIM // 2

DA_HEADS = 4
DA_QK_DIM = 64
DA_V_DIM = 128
DA_WIDTH = 512
DS_HEADS = 8
DS_KV_HEADS = 2
DS_HEAD_DIM = 64
DS_WIDTH = 512
IDX_HEADS = 8
IDX_DIM = 64
TOPK_MAX = 256
MEM_HEADS = 4
MEM_HEAD_DIM = 128
MEM_WIDTH = 512
PEER_HEADS = 8
PEER_N_KEYS = 128
PEER_HALF = 64
PEER_TOPK = 16
PEER_SEL = PEER_HEADS * PEER_TOPK

COL_SIZES = (512, 512, 512, 512, 128, 128, 512, 64, 8, 2048)

LANES = 128
SUBLANES = 8
NEG_BIG = -1e30
INT_MIN = -(2 ** 31)
HALF_BIAS = 2 ** 15
HALF_ROWS = 2 * SUBLANES
VMEM_LIMIT = 56 * 1024 * 1024

NT_DIMS = (((1,), (1,)), ((), ()))


def _rms(x, g):
    var = jnp.mean(x * x, axis=-1, keepdims=True)
    return x * lax.rsqrt(var + EPS) * g


def _rope_rows(y, cos8, sin8):
    pieces = []
    for r0 in range(0, y.shape[0], ROT_HEAD_DIM):
        t1 = y[r0:r0 + ROPE_HALF]
        t2 = y[r0 + ROPE_HALF:r0 + ROPE_DIM]
        pieces += [t1 * cos8 - t2 * sin8, t2 * cos8 + t1 * sin8, y[r0 + ROPE_DIM:r0 + ROT_HEAD_DIM]]
    return jnp.concatenate(pieces, axis=0)


def _in_proj_kernel(x_ref, g_ref, pos_ref, invf_ref, pos_t_ref, invf8_ref, bias_ref, wr_ref, wg_ref,
                    wqt_ref, wvt_ref, wwt_ref, pr_ref, gate_ref, qt_ref, vt_ref, iwt_ref):
    h = _rms(x_ref[...], g_ref[...]).astype(BF16)
    tm = h.shape[0]
    ang_t = invf8_ref[...] * pos_t_ref[...]
    cos8 = jnp.cos(ang_t)
    sin8 = jnp.sin(ang_t)
    row_chunk = 2 * LANES
    for r0 in range(0, wqt_ref.shape[0], row_chunk):
        y = lax.dot_general(wqt_ref[r0:r0 + row_chunk, :], h, NT_DIMS, preferred_element_type=F32)
        qt_ref[r0:r0 + row_chunk, :] = _rope_rows(y, cos8, sin8).astype(qt_ref.dtype)
    for r0 in range(0, wvt_ref.shape[0], row_chunk):
        r1 = min(r0 + row_chunk, wvt_ref.shape[0])
        y = lax.dot_general(wvt_ref[r0:r1, :], h, NT_DIMS, preferred_element_type=F32)
        vt_ref[0, r0:r1, :] = y.astype(vt_ref.dtype)
    iwt_ref[...] = lax.dot_general(wwt_ref[...], h, NT_DIMS, preferred_element_type=F32)
    ang = pos_ref[...] * invf_ref[...]
    cos = jnp.cos(ang)
    sin = jnp.sin(ang)
    lane = lax.broadcasted_iota(I32, (tm, LANES), 1) % ROT_HEAD_DIM
    c_t = jnp.where(lane < ROPE_DIM, cos, 1.0)
    s_lo = jnp.where(lane < ROPE_HALF, -sin, 0.0)
    s_hi = jnp.where((lane >= ROPE_HALF) & (lane < ROPE_DIM), sin, 0.0)
    n_r = wr_ref.shape[1]
    for c0 in range(0, n_r, 512):
        c1 = min(c0 + 512, n_r)
        w = c1 - c0
        y = jnp.dot(h, wr_ref[:, c0:c1], preferred_element_type=F32)
        reps = w // LANES
        ct = jnp.concatenate([c_t] * reps, axis=1)
        sl = jnp.concatenate([s_lo] * reps, axis=1)
        sh = jnp.concatenate([s_hi] * reps, axis=1)
        y = y * ct + pltpu.roll(y, w - ROPE_HALF, 1) * sl + pltpu.roll(y, ROPE_HALF, 1) * sh
        pr_ref[:, c0:c1] = y.astype(pr_ref.dtype)
    n_g = wg_ref.shape[1]
    for c0 in range(0, n_g, 512):
        y = jnp.dot(h, wg_ref[:, c0:c0 + 512], preferred_element_type=F32)
        gate_ref[:, c0:c0 + 512] = jax.nn.sigmoid(y + bias_ref[:, c0:c0 + 512])


def _in_proj(x2, g, pos, invf, pos_t, invf8, bias, wr, wg, wqt, wvt, wwt, tm):
    n = x2.shape[0]
    full = lambda a: pl.BlockSpec(a.shape, lambda i: (0, 0))
    rows = lambda w: pl.BlockSpec((tm, w), lambda i: (i, 0))
    cols = lambda r: pl.BlockSpec((r, tm), lambda i: (0, i))
    return pl.pallas_call(
        _in_proj_kernel,
        grid=(n // tm,),
        in_specs=[rows(D_MODEL), full(g), rows(1), full(invf), cols(1), full(invf8), full(bias), full(wr), full(wg),
                  full(wqt), full(wvt), full(wwt)],
        out_specs=[rows(wr.shape[1]), rows(wg.shape[1]), cols(wqt.shape[0]),
                   pl.BlockSpec((1, wvt.shape[0], tm), lambda i: (i, 0, 0)), cols(wwt.shape[0])],
        out_shape=[jax.ShapeDtypeStruct((n, wr.shape[1]), BF16), jax.ShapeDtypeStruct((n, wg.shape[1]), F32),
                   jax.ShapeDtypeStruct((wqt.shape[0], n), BF16),
                   jax.ShapeDtypeStruct((n // tm, wvt.shape[0], tm), BF16),
                   jax.ShapeDtypeStruct((wwt.shape[0], n), F32)],
        compiler_params=pltpu.CompilerParams(dimension_semantics=("arbitrary",), vmem_limit_bytes=VMEM_LIMIT),
        name="in_proj",
    )(x2, g, pos, invf, pos_t, invf8, bias, wr, wg, wqt, wvt, wwt)


QT_DA = 0
QT_DS = 4
QT_IX = 8
VT_DA = 0
VT_DS = 4
PR_DAK = 0
PR_DSK = 4
PR_IXK = 6


SUB_KEYS = 128
DA_QK_AHEAD = 1
DS_QK_AHEAD = 3


def _zero_after(x):
    bits = pltpu.bitcast(x, jnp.uint32)
    return pltpu.bitcast((bits >> 16) >> 16, F32)


def _flash_chains(n_chains, score_fn, value_fn, subs, m_scr, l_scr, acc_scr, ahead):
    pending = [score_fn(c) for c in range(ahead)]
    for c in range(n_chains):
        scores = pending.pop(0)
        m = m_scr[c][0:1]
        if c + ahead < n_chains:
            pending.append(score_fn(c + ahead))
            m = m + _zero_after(pending[-1][0][0:1])
        m_new = m
        for s in scores:
            m_new = jnp.maximum(m_new, jnp.max(s, axis=0, keepdims=True))
        alpha = jnp.exp(m - m_new)
        l = alpha * l_scr[c][0:1]
        a = alpha * acc_scr[c]
        for r0, s in zip(subs, scores):
            p = jnp.exp(s - m_new)
            l = l + jnp.sum(p, axis=0, keepdims=True)
            v = value_fn(c, r0)
            a = a + jnp.dot(v, p.astype(v.dtype), preferred_element_type=F32)
        m_scr[c] = jnp.broadcast_to(m_new, m_scr.shape[1:])
        l_scr[c] = jnp.broadcast_to(l, l_scr.shape[1:])
        acc_scr[c] = a


def _da_kernel(lam_ref, qt_ref, k_ref, vt_ref, g_ref, o_ref, q_scr, acc_scr, m_scr, l_scr, *, tq, lam_init):
    tk = tq
    i = pl.program_id(1)
    n_chains = 2 * DA_HEADS
    frow = lax.broadcasted_iota(I32, (LANES, tq), 0)
    scale = jnp.asarray(DA_QK_DIM ** -0.5, BF16)
    for h in range(DA_HEADS):
        blk = qt_ref[h * LANES:(h + 1) * LANES, :] * scale
        zero = jnp.zeros_like(blk)
        q_scr[2 * h] = jnp.where(frow < DA_QK_DIM, blk, zero)
        q_scr[2 * h + 1] = jnp.where(frow >= DA_QK_DIM, blk, zero)
    acc_scr[...] = jnp.zeros(acc_scr.shape, F32)
    m_scr[...] = jnp.full(m_scr.shape, NEG_BIG, F32)
    l_scr[...] = jnp.zeros(l_scr.shape, F32)
    kpos0 = lax.broadcasted_iota(I32, (SUB_KEYS, tq), 0)
    qpos = i * tq + lax.broadcasted_iota(I32, (SUB_KEYS, tq), 1)
    subs = range(0, tk, SUB_KEYS)

    def chunk(j, masked):
        start = pl.multiple_of(j * tk, tk)

        def scores(c):
            head = c // 2
            out = []
            for r0 in subs:
                k = k_ref[pl.ds(start + r0, SUB_KEYS), head * LANES:(head + 1) * LANES]
                s = jnp.dot(k, q_scr[c], preferred_element_type=F32)
                if masked:
                    s = jnp.where(kpos0 + (j * tk + r0) <= qpos, s, NEG_BIG)
                out.append(s)
            return out

        def values(c, r0):
            head = c // 2
            return vt_ref[j, head * LANES:(head + 1) * LANES, r0:r0 + SUB_KEYS]

        _flash_chains(n_chains, scores, values, subs, m_scr, l_scr, acc_scr, DA_QK_AHEAD)

    def full_chunk(j, carry):
        chunk(j, False)
        return carry

    lax.fori_loop(0, i, full_chunk, 0)
    chunk(i, True)

    lp = lam_ref[...]
    lam = (jnp.exp(jnp.sum(lp[0:1] * lp[1:2], axis=1, keepdims=True))
           - jnp.exp(jnp.sum(lp[2:3] * lp[3:4], axis=1, keepdims=True)) + lam_init)
    for h in range(DA_HEADS):
        o = acc_scr[2 * h] / l_scr[2 * h][0:1] - lam * (acc_scr[2 * h + 1] / l_scr[2 * h + 1][0:1])
        var = jnp.mean(o * o, axis=0, keepdims=True)
        o = o * lax.rsqrt(var + EPS) * g_ref[...] * (1.0 - lam_init)
        o_ref[:, h * LANES:(h + 1) * LANES] = o.T.astype(o_ref.dtype)


def _diff_attention(lam_p, qt, pr, vt3, subln_g, b, s_len, tq, lam_init):
    nq = s_len // tq
    n = b * s_len
    n_chains = 2 * DA_HEADS
    kern = functools.partial(_da_kernel, tq=tq, lam_init=lam_init)
    return pl.pallas_call(
        kern,
        grid=(b, nq),
        in_specs=[
            pl.BlockSpec(lam_p.shape, lambda bi, i: (0, 0)),
            pl.BlockSpec((DA_HEADS * LANES, tq), lambda bi, i: (QT_DA // DA_HEADS, bi * nq + i)),
            pl.BlockSpec((s_len, DA_HEADS * LANES), lambda bi, i: (bi, PR_DAK // DA_HEADS)),
            pl.BlockSpec((s_len // tq, DA_HEADS * LANES, tq), lambda bi, i: (bi, VT_DA // DA_HEADS, 0)),
            pl.BlockSpec(subln_g.shape, lambda bi, i: (0, 0)),
        ],
        out_specs=pl.BlockSpec((tq, DA_WIDTH), lambda bi, i: (bi * nq + i, 0)),
        out_shape=jax.ShapeDtypeStruct((n, DA_WIDTH), BF16),
        scratch_shapes=[pltpu.VMEM((n_chains, LANES, tq), BF16), pltpu.VMEM((n_chains, DA_V_DIM, tq), F32),
                        pltpu.VMEM((n_chains, SUBLANES, tq), F32), pltpu.VMEM((n_chains, SUBLANES, tq), F32)],
        compiler_params=pltpu.CompilerParams(dimension_semantics=("arbitrary",) * 2, vmem_limit_bytes=VMEM_LIMIT),
        name="diff_attn",
    )(lam_p, qt, pr, vt3, subln_g)


def _dsa_kernel(qit_ref, ki_ref, wt_ref, qt_ref, k0_ref, k1_ref, vt_ref, o_ref,
                keys_scr, half_scr, qi_scr, q_scr, acc_scr, m_scr, l_scr, cut_scr, *, tq, n_sel, n_bits):
    tk = tq
    i = pl.program_id(1)
    nck = i + 1
    kpos0 = lax.broadcasted_iota(I32, (tk, tq), 0)
    qpos = i * tq + lax.broadcasted_iota(I32, (tk, tq), 1)
    frow = lax.broadcasted_iota(I32, (LANES, tq), 0)
    per_grp = DS_HEADS // DS_KV_HEADS

    scale = jnp.asarray(DS_HEAD_DIM ** -0.5, BF16)
    for h in range(DS_HEADS):
        keep = (frow < DS_HEAD_DIM) if h % 2 == 0 else (frow >= DS_HEAD_DIM)
        rows = slice((h // 2) * LANES, (h // 2 + 1) * LANES)
        blk = qit_ref[rows, :]
        qi_scr[h] = jnp.where(keep, blk, jnp.zeros_like(blk))
        blk = qt_ref[rows, :] * scale
        q_scr[h] = jnp.where(keep, blk, jnp.zeros_like(blk))

    kpos_sub = lax.broadcasted_iota(I32, (SUB_KEYS, tq), 0)
    qpos_sub = i * tq + lax.broadcasted_iota(I32, (SUB_KEYS, tq), 1)

    def score_chunk(c, carry):
        start = pl.multiple_of(c * tk, tk)
        for r0 in range(0, tk, SUB_KEYS):
            kc = ki_ref[pl.ds(start + r0, SUB_KEYS), :]
            acc = jnp.zeros((SUB_KEYS, tq), F32)
            for h in range(IDX_HEADS):
                s = jnp.dot(kc, qi_scr[h], preferred_element_type=F32)
                acc = acc + wt_ref[h:h + 1, :] * jnp.maximum(s, 0.0)
            bits = pltpu.bitcast(acc, I32)
            key = bits ^ ((bits >> 31) & 0x7FFFFFFF)
            key = jnp.where(acc == 0.0, 0, key)
            key = jnp.where(kpos_sub + (c * tk + r0) <= qpos_sub, key, INT_MIN)
            keys_scr[c, r0:r0 + SUB_KEYS, :] = key
            half_scr[c, r0:r0 + SUB_KEYS, :] = (key >> 16).astype(I16)
        return carry

    lax.fori_loop(0, nck, score_chunk, 0)

    def count(pred):
        def body(c, acc):
            hit = pred(keys_scr[c], kpos0 + c * tk)
            return acc + jnp.sum(hit.reshape(tk // SUBLANES, SUBLANES, tq), axis=0)
        acc = lax.fori_loop(0, nck, body, jnp.zeros((SUBLANES, tq), I32))
        return jnp.sum(acc, axis=0, keepdims=True)

    def count_half(pred):
        def body(c, acc):
            hit = pred(half_scr[c])
            for r0 in range(0, tk, HALF_ROWS):
                acc = acc + hit[r0:r0 + HALF_ROWS]
            return acc
        acc = lax.fori_loop(0, nck, body, jnp.zeros((HALF_ROWS, tq), I16))
        return jnp.sum(acc.astype(I32), axis=0, keepdims=True)

    one16, zero16 = jnp.int16(1), jnp.int16(0)

    def bisect_half(need):
        def bit(b, ans_u):
            cand_u = ans_u | jnp.left_shift(jnp.int32(1), 15 - b)
            cand = (cand_u - HALF_BIAS).astype(I16)
            cnt = count_half(lambda x: jnp.where(x >= cand, one16, zero16))
            return jnp.where(cnt >= need, cand_u, ans_u)
        return lax.fori_loop(0, 16, bit, jnp.zeros((1, tq), I32)) - HALF_BIAS

    hi_t = bisect_half(n_sel)
    hi_t16 = hi_t.astype(I16)
    n_hi_gt = count_half(lambda x: jnp.where(x > hi_t16, one16, zero16))

    def low_chunk(c, carry):
        lo = ((keys_scr[c] & 0xFFFF) - HALF_BIAS).astype(I16)
        half_scr[c] = jnp.where(half_scr[c] == hi_t16, lo, jnp.int16(-HALF_BIAS))
        return carry

    lax.fori_loop(0, nck, low_chunk, 0)
    lo_t = bisect_half(n_sel - n_hi_gt)
    thr = hi_t * (2 * HALF_BIAS) + (lo_t + HALF_BIAS)
    n_ge = count(lambda k, kpos: jnp.where(k >= thr, 1, 0))
    below_all = thr == INT_MIN
    cut_scr[...] = jnp.where(below_all, -1, (1 << n_bits) - 1)
    excess = jnp.max(jnp.where(below_all, 0, n_ge - n_sel))

    @pl.when(excess > 0)
    def _():
        n_gt = count(lambda k, kpos: jnp.where(k > thr, 1, 0))
        need = n_sel - n_gt

        def tie_bit(t, cut):
            cand = cut | jnp.left_shift(jnp.int32(1), n_bits - 1 - t)
            cnt = count(lambda k, kpos: jnp.where(k == thr, jnp.where(kpos < cand, 1, 0), 0))
            return jnp.where(cnt < need, cand, cut)

        cut = lax.fori_loop(0, n_bits, tie_bit, jnp.zeros((1, tq), I32))
        cut_scr[...] = jnp.where(below_all, -1, cut)

    cut = cut_scr[...]

    def bias_chunk(c, carry):
        k = keys_scr[c]
        sel = (k > thr) | ((k == thr) & (kpos0 + c * tk <= cut))
        keys_scr[c] = pltpu.bitcast(jnp.where(sel, 0.0, NEG_BIG).astype(F32), I32)
        return carry

    lax.fori_loop(0, nck, bias_chunk, 0)

    acc_scr[...] = jnp.zeros(acc_scr.shape, F32)
    m_scr[...] = jnp.full(m_scr.shape, NEG_BIG, F32)
    l_scr[...] = jnp.zeros(l_scr.shape, F32)

    def att_chunk(c, carry):
        start = pl.multiple_of(c * tk, tk)
        subs = range(0, tk, SUB_KEYS)

        def head_scores(h):
            k_ref = k0_ref if h // per_grp == 0 else k1_ref
            out = []
            for r0 in subs:
                s = jnp.dot(k_ref[pl.ds(start + r0, SUB_KEYS), :], q_scr[h], preferred_element_type=F32)
                out.append(s + pltpu.bitcast(keys_scr[c, r0:r0 + SUB_KEYS, :], F32))
            return out

        def head_values(h, r0):
            grp = h // per_grp
            return vt_ref[c, grp * DS_HEAD_DIM:(grp + 1) * DS_HEAD_DIM, r0:r0 + SUB_KEYS]

        _flash_chains(DS_HEADS, head_scores, head_values, subs, m_scr, l_scr, acc_scr, DS_QK_AHEAD)
        return carry

    lax.fori_loop(0, nck, att_chunk, 0)
    outs = [acc_scr[h] / l_scr[h][0:1] for h in range(DS_HEADS)]
    o_ref[...] = jnp.concatenate(outs, axis=0).T.astype(o_ref.dtype)


def _dsa_attention(qt, pr, vt3, iwt, b, s_len, tq):
    nq = s_len // tq
    n = b * s_len
    n_sel = min(TOPK_MAX, s_len // 4)
    n_bits = max(1, (s_len - 1).bit_length())
    kern = functools.partial(_dsa_kernel, tq=tq, n_sel=n_sel, n_bits=n_bits)
    seq = lambda blk: pl.BlockSpec((s_len, LANES), lambda bi, i: (bi, blk))
    qblk = lambda blk: pl.BlockSpec((4 * LANES, tq), lambda bi, i: (blk // 4, bi * nq + i))
    return pl.pallas_call(
        kern,
        grid=(b, nq),
        in_specs=[
            qblk(QT_IX), seq(PR_IXK),
            pl.BlockSpec((IDX_HEADS, tq), lambda bi, i: (0, bi * nq + i)),
            qblk(QT_DS), seq(PR_DSK), seq(PR_DSK + 1),
            pl.BlockSpec((s_len // tq, LANES, tq), lambda bi, i: (bi, VT_DS, 0)),
        ],
        out_specs=pl.BlockSpec((tq, DS_WIDTH), lambda bi, i: (bi * nq + i, 0)),
        out_shape=jax.ShapeDtypeStruct((n, DS_WIDTH), BF16),
        scratch_shapes=[pltpu.VMEM((nq, tq, tq), I32), pltpu.VMEM((nq, tq, tq), I16),
                        pltpu.VMEM((IDX_HEADS, LANES, tq), BF16), pltpu.VMEM((DS_HEADS, LANES, tq), BF16),
                        pltpu.VMEM((DS_HEADS, DS_HEAD_DIM, tq), F32),
                        pltpu.VMEM((DS_HEADS, SUBLANES, tq), F32), pltpu.VMEM((DS_HEADS, SUBLANES, tq), F32),
                        pltpu.VMEM((1, tq), I32)],
        compiler_params=pltpu.CompilerParams(dimension_semantics=("arbitrary",) * 2, vmem_limit_bytes=VMEM_LIMIT),
        name="dsa_attn",
    )(qt, pr, iwt, qt, pr, pr, vt3)


def _merge_kernel(oa_ref, ob_ref, gate_ref, x_ref, wa_ref, wb_ref, wo_ref, o_ref):
    ya = jnp.dot(oa_ref[...], wa_ref[...], preferred_element_type=F32)
    yb = jnp.dot(ob_ref[...], wb_ref[...], preferred_element_type=F32)
    mix = gate_ref[:, :D_MODEL] * ya + gate_ref[:, D_MODEL:] * yb
    o_ref[...] = x_ref[...] + jnp.dot(mix.astype(BF16), wo_ref[...], preferred_element_type=F32)


def _merge(oa, ob, gate, x2, wa, wb, wo, tm):
    n = x2.shape[0]
    full = lambda a: pl.BlockSpec(a.shape, lambda i: (0, 0))
    rows = lambda w: pl.BlockSpec((tm, w), lambda i: (i, 0))
    return pl.pallas_call(
        _merge_kernel,
        grid=(n // tm,),
        in_specs=[rows(DA_WIDTH), rows(DS_WIDTH), rows(2 * D_MODEL), rows(D_MODEL), full(wa), full(wb), full(wo)],
        out_specs=rows(D_MODEL),
        out_shape=jax.ShapeDtypeStruct((n, D_MODEL), F32),
        compiler_params=pltpu.CompilerParams(dimension_semantics=("arbitrary",), vmem_limit_bytes=VMEM_LIMIT),
        name="merge",
    )(oa, ob, gate, x2, wa, wb, wo)


def _norm_matmul_kernel(x_ref, g_ref, w_ref, o_ref):
    h = _rms(x_ref[...], g_ref[...]).astype(BF16)
    o_ref[...] = jnp.dot(h, w_ref[...], preferred_element_type=F32).astype(o_ref.dtype)


def _norm_matmul(x2, g, w, tm):
    n = x2.shape[0]
    return pl.pallas_call(
        _norm_matmul_kernel,
        grid=(n // tm,),
        in_specs=[pl.BlockSpec((tm, x2.shape[1]), lambda i: (i, 0)), pl.BlockSpec(g.shape, lambda i: (0, 0)),
                  pl.BlockSpec(w.shape, lambda i: (0, 0))],
        out_specs=pl.BlockSpec((tm, w.shape[1]), lambda i: (i, 0)),
        out_shape=jax.ShapeDtypeStruct((n, w.shape[1]), BF16),
        compiler_params=pltpu.CompilerParams(dimension_semantics=("arbitrary",), vmem_limit_bytes=VMEM_LIMIT),
        name="mem_kv_proj",
    )(x2, g, w)


def _mem_attn_kernel(x_ref, g_ref, wq_ref, kv_ref, wo_ref, o_ref):
    x = x_ref[...]
    hn = _rms(x, g_ref[...]).astype(BF16)
    q = jnp.dot(hn, wq_ref[...], preferred_element_type=F32).astype(BF16)
    scale = MEM_HEAD_DIM ** -0.5
    heads = []
    for h in range(MEM_HEADS):
        qh = q[:, h * MEM_HEAD_DIM:(h + 1) * MEM_HEAD_DIM]
        kh = kv_ref[:, h * MEM_HEAD_DIM:(h + 1) * MEM_HEAD_DIM]
        vh = kv_ref[:, MEM_WIDTH + h * MEM_HEAD_DIM:MEM_WIDTH + (h + 1) * MEM_HEAD_DIM]
        s = lax.dot_general(qh, kh, NT_DIMS, preferred_element_type=F32) * scale
        m = jnp.max(s, axis=1, keepdims=True)
        p = jnp.exp(s - m)
        l = jnp.sum(p, axis=1, keepdims=True)
        heads.append(jnp.dot(p.astype(BF16), vh, preferred_element_type=F32) / l)
    o = jnp.concatenate(heads, axis=1).astype(BF16)
    o_ref[...] = x + jnp.dot(o, wo_ref[...], preferred_element_type=F32)


def _mem_attn(x2, g, wq, kv, wo, b, s_len, m_tok, tm):
    nb = s_len // tm
    n = x2.shape[0]
    full = lambda a: pl.BlockSpec(a.shape, lambda bi, i: (0, 0))
    return pl.pallas_call(
        _mem_attn_kernel,
        grid=(b, nb),
        in_specs=[pl.BlockSpec((tm, D_MODEL), lambda bi, i: (bi * nb + i, 0)), full(g), full(wq),
                  pl.BlockSpec((m_tok, 2 * MEM_WIDTH), lambda bi, i: (bi, 0)), full(wo)],
        out_specs=pl.BlockSpec((tm, D_MODEL), lambda bi, i: (bi * nb + i, 0)),
        out_shape=jax.ShapeDtypeStruct((n, D_MODEL), F32),
        compiler_params=pltpu.CompilerParams(dimension_semantics=("arbitrary",) * 2, vmem_limit_bytes=VMEM_LIMIT),
        name="mem_attn",
    )(x2, g, wq, kv, wo)


N_SUB = 2 * PEER_HEADS
EXPERT_BITS = 14


def _route_kernel(x_ref, g_ref, wq_ref, keys_ref, hf_ref, idx_ref, row_ref, gate_ref, sc_scr, ts_scr, ti_scr):
    hf = _rms(x_ref[...], g_ref[...])
    tm = hf.shape[0]
    hb = hf.astype(BF16)
    bits = pltpu.bitcast(hb.astype(F32), jnp.uint32)
    half_d = D_MODEL // 2
    for s in range(SUBLANES // 2):
        lo = bits[:, s * LANES:(s + 1) * LANES] >> 16
        hi = bits[:, half_d + s * LANES:half_d + (s + 1) * LANES] & jnp.uint32(0xFFFF0000)
        hf_ref[:, s, :] = lo | hi
        hf_ref[:, s + SUBLANES // 2, :] = lo | hi
    q = jnp.dot(hb, wq_ref[...], preferred_element_type=F32).astype(BF16)
    for g in range(N_SUB):
        blk = q[:, (g // 2) * LANES:(g // 2 + 1) * LANES]
        sc_scr[g] = lax.dot_general(keys_ref[g], blk, NT_DIMS, preferred_element_type=F32)

    key_id = lax.broadcasted_iota(I32, (PEER_N_KEYS, tm), 0)

    def sub_topk(g, carry):
        x = sc_scr[g]
        vals, ids = [], []
        for _ in range(PEER_TOPK):
            m = jnp.max(x, axis=0, keepdims=True)
            idx = jnp.min(jnp.where(x == m, key_id, PEER_N_KEYS), axis=0, keepdims=True)
            vals.append(m)
            ids.append(idx)
            x = jnp.where(key_id == idx, -jnp.inf, x)
        ts_scr[g] = jnp.concatenate(vals, axis=0)
        ti_scr[g] = jnp.concatenate(ids, axis=0)
        return carry

    lax.fori_loop(0, N_SUB, sub_topk, 0)

    iota16 = lax.broadcasted_iota(I32, (PEER_TOPK, tm), 0)
    iota8 = lax.broadcasted_iota(I32, (SUBLANES, tm), 0)
    lead_rows = SUBLANES

    def head_topk(h, carry):
        s0, s1 = ts_scr[2 * h], ts_scr[2 * h + 1]
        i0, i1 = ti_scr[2 * h], ti_scr[2 * h + 1]
        cands, codes = [], []
        for a in range(lead_rows):
            n_j = PEER_TOPK if a == 0 else SUBLANES
            cands.append(s0[a:a + 1] + s1[:n_j])
            j_iota = iota16 if a == 0 else iota8
            codes.append(((a * PEER_TOPK + j_iota) << EXPERT_BITS) | (i0[a:a + 1] * PEER_N_KEYS + i1[:n_j]))
        cands.append(s0[lead_rows:] + s1[0:1])
        codes.append((((iota8 + lead_rows) * PEER_TOPK) << EXPERT_BITS) | (i0[lead_rows:] * PEER_N_KEYS + i1[0:1]))
        cand = jnp.concatenate(cands, axis=0)
        code = jnp.concatenate(codes, axis=0)
        big = jnp.int32(2 ** 30)
        vals, ids = [], []
        for _ in range(PEER_TOPK):
            m = jnp.max(cand, axis=0, keepdims=True)
            best = jnp.min(jnp.where(cand == m, code, big), axis=0, keepdims=True)
            vals.append(m)
            ids.append(best & (2 ** EXPERT_BITS - 1))
            cand = jnp.where(code == best, -jnp.inf, cand)
        best_s = jnp.concatenate(vals, axis=0)
        e = jnp.exp(best_s - best_s[0:1])
        gate_ref[h] = e / jnp.sum(e, axis=0, keepdims=True)
        best_i = jnp.concatenate(ids, axis=0)
        idx_ref[h] = best_i
        row_ref[h] = (best_i >> 1) * SUBLANES
        return carry

    lax.fori_loop(0, PEER_HEADS, head_topk, 0)


def _route(x2, g, wq, keys_p, tm):
    n = x2.shape[0]
    return pl.pallas_call(
        _route_kernel,
        grid=(n // tm,),
        in_specs=[pl.BlockSpec((tm, D_MODEL), lambda i: (i, 0)), pl.BlockSpec(g.shape, lambda i: (0, 0)),
                  pl.BlockSpec(wq.shape, lambda i: (0, 0)), pl.BlockSpec(keys_p.shape, lambda i: (0, 0, 0))],
        out_specs=[pl.BlockSpec((tm, SUBLANES, LANES), lambda i: (i, 0, 0))]
        + [pl.BlockSpec((PEER_HEADS, PEER_TOPK, tm), lambda i: (0, 0, i))] * 3,
        out_shape=[jax.ShapeDtypeStruct((n, SUBLANES, LANES), jnp.uint32),
                   jax.ShapeDtypeStruct((PEER_HEADS, PEER_TOPK, n), I32),
                   jax.ShapeDtypeStruct((PEER_HEADS, PEER_TOPK, n), I32),
                   jax.ShapeDtypeStruct((PEER_HEADS, PEER_TOPK, n), F32)],
        scratch_shapes=[pltpu.VMEM((N_SUB, PEER_N_KEYS, tm), F32), pltpu.VMEM((N_SUB, PEER_TOPK, tm), F32),
                        pltpu.VMEM((N_SUB, PEER_TOPK, tm), I32)],
        compiler_params=pltpu.CompilerParams(dimension_semantics=("arbitrary",), vmem_limit_bytes=VMEM_LIMIT),
        name="peer_route",
    )(x2, g, wq, keys_p)


TILE_ROWS = 2 * SUBLANES
PAIRS_PER_KTILE = 16
TOK_UNROLL = 16
SUM_UNROLL = 16


def _pack_table(t):
    e = t.shape[0]
    bits = lax.bitcast_convert_type(t.astype(BF16), jnp.uint16).astype(jnp.uint32)
    words = bits[:, :D_MODEL // 2] | (bits[:, D_MODEL // 2:] << 16)
    return words.reshape(e * (SUBLANES // 2), LANES)


def _gather_tiles(off_ref, tab_ref, u):
    tiles = []
    for k in range(PEER_SEL):
        start = pl.multiple_of(off_ref[u * PEER_SEL + k], SUBLANES)
        tiles.append(pltpu.bitcast(tab_ref[pl.ds(start, SUBLANES), :], BF16))
    return tiles


def _for_trips(rows_hbm, bufs, sems, tok0, unroll, n_trip, body):
    assert n_trip % 2 == 0
    words = unroll * PEER_SEL

    def fetch(trip, par):
        src = rows_hbm.at[pl.ds(pl.multiple_of((tok0 + trip * unroll) * PEER_SEL, words), words)]
        return pltpu.make_async_copy(src, bufs[par], sems.at[par])

    fetch(0, 0).start()
    fetch(1, 1).start()

    def pair(p, carry):
        for par in range(2):
            trip = 2 * p + par
            fetch(trip, par).wait()
            body(trip, bufs[par], par)
            fetch(jnp.minimum(trip + 2, n_trip - 2 + par), par).start()
        return carry

    lax.fori_loop(0, n_trip // 2, pair, 0)
    for par in range(2):
        fetch(n_trip - 2 + par, par).wait()


def _peer_dot_kernel(rows_hbm, tab_ref, h_ref, gate_ref, idxv_ref, sel_ref, ce_ref, co_ref,
                     ye_scr, yo_scr, ze_scr, zo_scr, off_a, off_b, sems, *, tt):
    ones = jnp.ones((SUBLANES, LANES), BF16)
    half_rows = PAIRS_PER_KTILE
    n_trip = tt // TOK_UNROLL

    def row_sums(i, slot, off_ref):
        for u in range(TOK_UNROLL):
            t = i * TOK_UNROLL + u
            hp = pltpu.bitcast(h_ref[t], BF16)
            prods = [tile * hp for tile in _gather_tiles(off_ref, tab_ref, u)]
            for kt in range(PEER_SEL // PAIRS_PER_KTILE):
                stack = jnp.concatenate(prods[kt * PAIRS_PER_KTILE:(kt + 1) * PAIRS_PER_KTILE], axis=0)
                y = jnp.dot(sel_ref[...], stack, preferred_element_type=F32)
                r0 = u * PEER_SEL + kt * half_rows
                ye_scr[slot, r0:r0 + half_rows, :] = y[:half_rows].astype(BF16)
                yo_scr[slot, r0:r0 + half_rows, :] = y[half_rows:].astype(BF16)

    def lane_sums(i, slot):
        ze = lax.dot_general(ones, ye_scr[slot], NT_DIMS, preferred_element_type=F32)
        zo = lax.dot_general(ones, yo_scr[slot], NT_DIMS, preferred_element_type=F32)
        for u in range(TOK_UNROLL):
            t = i * TOK_UNROLL + u
            ze_scr[pl.ds(t, 1), :] = ze[0:1, u * PEER_SEL:(u + 1) * PEER_SEL]
            zo_scr[pl.ds(t, 1), :] = zo[0:1, u * PEER_SEL:(u + 1) * PEER_SEL]

    def trip(i, off_ref, par):
        lane_sums(jnp.maximum(i - 1, 0), 1 - par)
        row_sums(i, par, off_ref)

    ye_scr[1] = jnp.zeros(ye_scr.shape[1:], ye_scr.dtype)
    yo_scr[1] = jnp.zeros(yo_scr.shape[1:], yo_scr.dtype)
    _for_trips(rows_hbm, (off_a, off_b), sems, pl.program_id(0) * tt, TOK_UNROLL, n_trip, trip)
    lane_sums(n_trip - 1, (n_trip - 1) & 1)
    even = (idxv_ref[...] & 1) == 0
    a = jnp.where(even, ze_scr[...], zo_scr[...])
    c = 0.5 * a * (1.0 + lax.erf(a * (2.0 ** -0.5))) * gate_ref[...]
    ce_ref[...] = jnp.where(even, c, 0.0)
    co_ref[...] = jnp.where(even, 0.0, c)


def _peer_dot(rows_flat, tab, hpk, gate, idx2, sel, tt):
    n = hpk.shape[0]
    kern = functools.partial(_peer_dot_kernel, tt=tt)
    rows = pl.BlockSpec((tt, PEER_SEL), lambda i: (i, 0))
    return pl.pallas_call(
        kern,
        grid=(n // tt,),
        in_specs=[pl.BlockSpec(memory_space=pl.ANY),
                  pl.BlockSpec(memory_space=pltpu.VMEM),
                  pl.BlockSpec((tt, SUBLANES, LANES), lambda i: (i, 0, 0)),
                  rows, rows, pl.BlockSpec(sel.shape, lambda i: (0, 0))],
        out_specs=[rows, rows],
        out_shape=[jax.ShapeDtypeStruct((n, PEER_SEL), F32), jax.ShapeDtypeStruct((n, PEER_SEL), F32)],
        scratch_shapes=[pltpu.VMEM((2, TOK_UNROLL * PEER_SEL, LANES), BF16)] * 2
        + [pltpu.VMEM((tt, PEER_SEL), F32)] * 2
        + [pltpu.SMEM((TOK_UNROLL * PEER_SEL,), I32)] * 2 + [pltpu.SemaphoreType.DMA((2,))],
        compiler_params=pltpu.CompilerParams(dimension_semantics=("arbitrary",), vmem_limit_bytes=VMEM_LIMIT),
        name="peer_dot",
    )(rows_flat, tab, hpk, gate, idx2, sel)


def _peer_sum_kernel(rows_hbm, ce_ref, co_ref, spread_ref, tab_ref, o_ref, m1_scr, m2_scr, off_a, off_b, sems,
                     *, tt):
    width = PEER_SEL * TILE_ROWS
    cc = jnp.concatenate([ce_ref[...], co_ref[...]], axis=1)
    c1 = cc.astype(BF16)
    c2 = (cc - c1.astype(F32)).astype(BF16)
    m1_scr[...] = jnp.dot(c1, spread_ref[...], preferred_element_type=F32)
    m2_scr[...] = jnp.dot(c2, spread_ref[...], preferred_element_type=F32)
    row = lax.broadcasted_iota(I32, (SUBLANES, width), 0)
    lane = lax.broadcasted_iota(I32, (SUBLANES, width), 1)
    half = SUBLANES // 2
    on_row = (lane & (SUBLANES - 1)) == 2 * (row % half) + row // half

    def token(t, off_ref, u):
        w = jnp.concatenate(_gather_tiles(off_ref, tab_ref, u), axis=0)
        lhs = []
        for m_scr in (m1_scr, m2_scr):
            coef = jnp.broadcast_to(m_scr[pl.ds(t, 1), :], (SUBLANES, width))
            lhs.append(jnp.where(on_row, coef, 0.0).astype(BF16))
        res = jnp.dot(jnp.concatenate(lhs, axis=0), w, preferred_element_type=F32)
        o_ref[t] = res[:SUBLANES] + res[SUBLANES:]

    def trip(i, off_ref, par):
        for u in range(SUM_UNROLL):
            token(i * SUM_UNROLL + u, off_ref, u)

    _for_trips(rows_hbm, (off_a, off_b), sems, pl.program_id(0) * tt, SUM_UNROLL, tt // SUM_UNROLL, trip)


def _peer_sum(rows_flat, ce, co, spread, tab, tt):
    n = ce.shape[0]
    kern = functools.partial(_peer_sum_kernel, tt=tt)
    rows = pl.BlockSpec((tt, PEER_SEL), lambda i: (i, 0))
    width = PEER_SEL * TILE_ROWS
    return pl.pallas_call(
        kern,
        grid=(n // tt,),
        in_specs=[pl.BlockSpec(memory_space=pl.ANY), rows, rows,
                  pl.BlockSpec(spread.shape, lambda i: (0, 0)), pl.BlockSpec(memory_space=pltpu.VMEM)],
        out_specs=pl.BlockSpec((tt, SUBLANES, LANES), lambda i: (i, 0, 0)),
        out_shape=jax.ShapeDtypeStruct((n, SUBLANES, LANES), F32),
        scratch_shapes=[pltpu.VMEM((tt, width), F32), pltpu.VMEM((tt, width), F32)]
        + [pltpu.SMEM((SUM_UNROLL * PEER_SEL,), I32)] * 2 + [pltpu.SemaphoreType.DMA((2,))],
        compiler_params=pltpu.CompilerParams(dimension_semantics=("arbitrary",), vmem_limit_bytes=VMEM_LIMIT),
        name="peer_sum",
    )(rows_flat, ce, co, spread, tab)


def _final_kernel(x_ref, p_ref, g_ref, o_ref):
    peer = jnp.concatenate([p_ref[:, s, :] for s in range(SUBLANES)], axis=1)
    o_ref[...] = _rms(x_ref[...] + peer, g_ref[...])


def _final(x2, peer3, g, tm):
    n = x2.shape[0]
    return pl.pallas_call(
        _final_kernel,
        grid=(n // tm,),
        in_specs=[pl.BlockSpec((tm, D_MODEL), lambda i: (i, 0)),
                  pl.BlockSpec((tm, SUBLANES, LANES), lambda i: (i, 0, 0)),
                  pl.BlockSpec(g.shape, lambda i: (0, 0))],
        out_specs=pl.BlockSpec((tm, D_MODEL), lambda i: (i, 0)),
        out_shape=jax.ShapeDtypeStruct((n, D_MODEL), F32),
        compiler_params=pltpu.CompilerParams(dimension_semantics=("arbitrary",), vmem_limit_bytes=VMEM_LIMIT),
        name="final_norm",
    )(x2, peer3, g)


def _tile(n, pref):
    t = pref
    while n % t:
        t //= 2
    return t


def _layer(l, x2, mem2, pos, invf, b, s_len, m_tok, norm_mix_g, w_in, da_lambda, da_subln_g, w_branch_a,
           w_branch_b, gate_bias, w_out, norm_mem_g, mem_kv_norm_g, w_mem_q, w_mem_kv, w_mem_o, norm_ffn_g,
           peer_w_q, peer_sub_keys, peer_u, peer_v):
    n = b * s_len
    row2 = lambda v: v.reshape(1, -1)
    splits = [0]
    for c in COL_SIZES:
        splits.append(splits[-1] + c)
    da_q, da_k, da_v, ds_q, ds_k, ds_v, ix_q, ix_k, ix_w, gates = (
        w_in[l][:, splits[j]:splits[j + 1]] for j in range(len(COL_SIZES)))
    k0, k1 = ds_k[:, :DS_HEAD_DIM], ds_k[:, DS_HEAD_DIM:]
    w_keys = jnp.concatenate([da_k, k0, k0, k1, k1, ix_k, ix_k], axis=1).astype(BF16)
    w_q_t = jnp.concatenate([da_q, ds_q, ix_q], axis=1).T.astype(BF16)
    w_v_t = jnp.concatenate([da_v, ds_v], axis=1).T.astype(BF16)
    t_att = _tile(s_len, 256)
    pr, gate, qt, vt3, iwt = _in_proj(x2, row2(norm_mix_g[l]), pos, invf, pos.reshape(1, n),
                                      invf[:, :ROPE_HALF].reshape(ROPE_HALF, 1), row2(gate_bias[l]), w_keys,
                                      gates.astype(BF16), w_q_t, w_v_t, ix_w.T.astype(BF16), t_att)

    lam_init = 0.8 - 0.6 * math.exp(-0.3 * l)
    o_a = _diff_attention(da_lambda[l], qt, pr, vt3, da_subln_g[l].reshape(-1, 1), b, s_len, t_att, lam_init)
    o_b = _dsa_attention(qt, pr, vt3, iwt, b, s_len, t_att)
    x2 = _merge(o_a, o_b, gate, x2, w_branch_a[l].astype(BF16), w_branch_b[l].astype(BF16),
                w_out[l].astype(BF16), _tile(n, 256))

    kv = _norm_matmul(mem2, row2(mem_kv_norm_g[l]), w_mem_kv[l].astype(BF16), _tile(mem2.shape[0], 256))
    x2 = _mem_attn(x2, row2(norm_mem_g[l]), w_mem_q[l].astype(BF16), kv, w_mem_o[l].astype(BF16),
                   b, s_len, m_tok, _tile(s_len, 256))

    sk = peer_sub_keys[l].reshape(N_SUB, PEER_N_KEYS, PEER_HALF)
    z = jnp.zeros_like(sk)
    keys_p = jnp.where((jnp.arange(N_SUB) % 2 == 0)[:, None, None],
                       jnp.concatenate([sk, z], axis=2), jnp.concatenate([z, sk], axis=2)).astype(BF16)
    hpk, idx_t, row_t, gate_t = _route(x2, row2(norm_ffn_g[l]), peer_w_q[l].astype(BF16), keys_p, _tile(n, 256))
    idx2 = idx_t.reshape(PEER_SEL, n).T
    rows2 = row_t.reshape(PEER_SEL, n).T
    gate2 = gate_t.reshape(PEER_SEL, n).T
    tt = _tile(n, 256)
    r32 = jnp.arange(2 * PAIRS_PER_KTILE)[:, None]
    c256 = jnp.arange(PAIRS_PER_KTILE * TILE_ROWS)[None, :]
    sel = ((c256 // TILE_ROWS == r32 % PAIRS_PER_KTILE)
           & ((c256 % TILE_ROWS) // SUBLANES == r32 // PAIRS_PER_KTILE)).astype(BF16)
    k256 = jnp.arange(2 * PEER_SEL)[:, None]
    c2048 = jnp.arange(PEER_SEL * TILE_ROWS)[None, :]
    spread = ((c2048 // TILE_ROWS == k256 % PEER_SEL)
              & ((c2048 % TILE_ROWS) // SUBLANES == k256 // PEER_SEL)).astype(BF16)
    rows_flat = rows2.reshape(-1)
    ce2, co2 = _peer_dot(rows_flat, _pack_table(peer_u[l]), hpk, gate2, idx2, sel, tt)
    peer3 = _peer_sum(rows_flat, ce2, co2, spread, _pack_table(peer_v[l]), tt)
    return x2, peer3


def kernel(x, mem, positions, norm_mix_g, w_in, da_lambda, da_subln_g, w_branch_a, w_branch_b, gate_bias, w_out, norm_mem_g, mem_kv_norm_g, w_mem_q, w_mem_kv, w_mem_o, norm_ffn_g, peer_w_q, peer_sub_keys, peer_u, peer_v, final_norm_g):
    b, s_len, d = x.shape
    m_tok = mem.shape[1]
    n = b * s_len
    depth = w_in.shape[0]
    x2 = x.reshape(n, d)
    mem2 = mem.reshape(b * m_tok, d)
    pos = positions.astype(F32).reshape(n, 1)
    inv_freq = ROPE_THETA ** (-(jnp.arange(ROPE_HALF, dtype=F32) * 2.0) / ROPE_DIM)
    invf = jnp.tile(inv_freq, LANES // ROPE_HALF).reshape(1, LANES)
    peer3 = None
    for l in range(depth):
        if peer3 is not None:
            x2 = x2 + peer3.reshape(n, d)
        x2, peer3 = _layer(l, x2, mem2, pos, invf, b, s_len, m_tok, norm_mix_g, w_in, da_lambda, da_subln_g,
                           w_branch_a, w_branch_b, gate_bias, w_out, norm_mem_g, mem_kv_norm_g, w_mem_q, w_mem_kv,
                           w_mem_o, norm_ffn_g, peer_w_q, peer_sub_keys, peer_u, peer_v)
    out = _final(x2, peer3, final_norm_g.reshape(1, d), _tile(n, 256))
    return out.reshape(b, s_len, d)
```

```python
import functools
import math

import jax
import jax.numpy as jnp
from jax import lax
from jax.experimental import pallas as pl
from jax.experimental.pallas import tpu as pltpu

F32 = jnp.float32
BF16 = jnp.bfloat16
I32 = jnp.int32
I16 = jnp.int16

D_MODEL = 1024
EPS = 1e-6
ROPE_THETA = 500000.0
ROT_HEAD_DIM = 64
ROPE_DIM = 16
ROPE_HALF = ROPE_DIM // 2

DA_HEADS = 4
DA_QK_DIM = 64
DA_V_DIM = 128
DA_WIDTH = 512
DS_HEADS = 8
DS_KV_HEADS = 2
DS_HEAD_DIM = 64
DS_WIDTH = 512
IDX_HEADS = 8
IDX_DIM = 64
TOPK_MAX = 256
MEM_HEADS = 4
MEM_HEAD_DIM = 128
MEM_WIDTH = 512
PEER_HEADS = 8
PEER_N_KEYS = 128
PEER_HALF = 64
PEER_TOPK = 16
PEER_SEL = PEER_HEADS * PEER_TOPK

COL_SIZES = (512, 512, 512, 512, 128, 128, 512, 64, 8, 2048)

LANES = 128
SUBLANES = 8
NEG_BIG = -1e30
INT_MIN = -(2 ** 31)
HALF_BIAS = 2 ** 15
HALF_ROWS = 2 * SUBLANES
VMEM_LIMIT = 56 * 1024 * 1024

NT_DIMS = (((1,), (1,)), ((), ()))


def _rms(x, g):
    var = jnp.mean(x * x, axis=-1, keepdims=True)
    return x * lax.rsqrt(var + EPS) * g


def _rope_rows(y, cos8, sin8):
    pieces = []
    for r0 in range(0, y.shape[0], ROT_HEAD_DIM):
        t1 = y[r0:r0 + ROPE_HALF]
        t2 = y[r0 + ROPE_HALF:r0 + ROPE_DIM]
        pieces += [t1 * cos8 - t2 * sin8, t2 * cos8 + t1 * sin8, y[r0 + ROPE_DIM:r0 + ROT_HEAD_DIM]]
    return jnp.concatenate(pieces, axis=0)


def _in_proj_kernel(x_ref, g_ref, pos_ref, invf_ref, pos_t_ref, invf8_ref, bias_ref, wr_ref, wg_ref,
                    wqt_ref, wvt_ref, wwt_ref, pr_ref, gate_ref, qt_ref, vt_ref, iwt_ref):
    h = _rms(x_ref[...], g_ref[...]).astype(BF16)
    tm = h.shape[0]
    ang_t = invf8_ref[...] * pos_t_ref[...]
    cos8 = jnp.cos(ang_t)
    sin8 = jnp.sin(ang_t)
    row_chunk = 2 * LANES
    for r0 in range(0, wqt_ref.shape[0], row_chunk):
        y = lax.dot_general(wqt_ref[r0:r0 + row_chunk, :], h, NT_DIMS, preferred_element_type=F32)
        qt_ref[r0:r0 + row_chunk, :] = _rope_rows(y, cos8, sin8).astype(qt_ref.dtype)
    for r0 in range(0, wvt_ref.shape[0], row_chunk):
        r1 = min(r0 + row_chunk, wvt_ref.shape[0])
        y = lax.dot_general(wvt_ref[r0:r1, :], h, NT_DIMS, preferred_element_type=F32)
        chunk = vt_ref.shape[2]
        for j in range(vt_ref.shape[0]):
            vt_ref[j, r0:r1, :] = y[:, j * chunk:(j + 1) * chunk].astype(vt_ref.dtype)
    iwt_ref[...] = lax.dot_general(wwt_ref[...], h, NT_DIMS, preferred_element_type=F32)
    ang = pos_ref[...] * invf_ref[...]
    cos = jnp.cos(ang)
    sin = jnp.sin(ang)
    lane = lax.broadcasted_iota(I32, (tm, LANES), 1) % ROT_HEAD_DIM
    c_t = jnp.where(lane < ROPE_DIM, cos, 1.0)
    s_lo = jnp.where(lane < ROPE_HALF, -sin, 0.0)
    s_hi = jnp.where((lane >= ROPE_HALF) & (lane < ROPE_DIM), sin, 0.0)
    n_r = wr_ref.shape[1]
    for c0 in range(0, n_r, 512):
        c1 = min(c0 + 512, n_r)
        w = c1 - c0
        y = jnp.dot(h, wr_ref[:, c0:c1], preferred_element_type=F32)
        reps = w // LANES
        ct = jnp.concatenate([c_t] * reps, axis=1)
        sl = jnp.concatenate([s_lo] * reps, axis=1)
        sh = jnp.concatenate([s_hi] * reps, axis=1)
        y = y * ct + pltpu.roll(y, w - ROPE_HALF, 1) * sl + pltpu.roll(y, ROPE_HALF, 1) * sh
        pr_ref[:, c0:c1] = y.astype(pr_ref.dtype)
    n_g = wg_ref.shape[1]
    for c0 in range(0, n_g, 512):
        y = jnp.dot(h, wg_ref[:, c0:c0 + 512], preferred_element_type=F32)
        gate_ref[:, c0:c0 + 512] = jax.nn.sigmoid(y + bias_ref[:, c0:c0 + 512])


def _in_proj(x2, g, pos, invf, pos_t, invf8, bias, wr, wg, wqt, wvt, wwt, tm, chunk):
    n = x2.shape[0]
    per_step = tm // chunk
    full = lambda a: pl.BlockSpec(a.shape, lambda i: (0, 0))
    rows = lambda w: pl.BlockSpec((tm, w), lambda i: (i, 0))
    cols = lambda r: pl.BlockSpec((r, tm), lambda i: (0, i))
    return pl.pallas_call(
        _in_proj_kernel,
        grid=(n // tm,),
        in_specs=[rows(D_MODEL), full(g), rows(1), full(invf), cols(1), full(invf8), full(bias), full(wr), full(wg),
                  full(wqt), full(wvt), full(wwt)],
        out_specs=[rows(wr.shape[1]), rows(wg.shape[1]), cols(wqt.shape[0]),
                   pl.BlockSpec((per_step, wvt.shape[0], chunk), lambda i: (i, 0, 0)), cols(wwt.shape[0])],
        out_shape=[jax.ShapeDtypeStruct((n, wr.shape[1]), BF16), jax.ShapeDtypeStruct((n, wg.shape[1]), F32),
                   jax.ShapeDtypeStruct((wqt.shape[0], n), BF16),
                   jax.ShapeDtypeStruct((n // chunk, wvt.shape[0], chunk), BF16),
                   jax.ShapeDtypeStruct((wwt.shape[0], n), F32)],
        compiler_params=pltpu.CompilerParams(dimension_semantics=("arbitrary",), vmem_limit_bytes=VMEM_LIMIT),
        name="in_proj",
    )(x2, g, pos, invf, pos_t, invf8, bias, wr, wg, wqt, wvt, wwt)


QT_DA = 0
QT_DS = 4
QT_IX = 8
VT_DA = 0
VT_DS = 4
PR_DAK = 0
PR_DSK = 4
PR_IXK = 6


SUB_KEYS = 128
DA_QK_AHEAD = 1
DS_QK_AHEAD = 3


def _zero_after(x):
    bits = pltpu.bitcast(x, jnp.uint32)
    return pltpu.bitcast((bits >> 16) >> 16, F32)


def _flash_chains(n_chains, score_fn, value_fn, subs, m_scr, l_scr, acc_scr, ahead):
    pending = [score_fn(c) for c in range(ahead)]
    for c in range(n_chains):
        scores = pending.pop(0)
        m = m_scr[c][0:1]
        if c + ahead < n_chains:
            pending.append(score_fn(c + ahead))
            m = m + _zero_after(pending[-1][0][0:1])
        m_new = m
        for s in scores:
            m_new = jnp.maximum(m_new, jnp.max(s, axis=0, keepdims=True))
        alpha = jnp.exp(m - m_new)
        l = alpha * l_scr[c][0:1]
        a = alpha * acc_scr[c]
        for r0, s in zip(subs, scores):
            p = jnp.exp(s - m_new)
            l = l + jnp.sum(p, axis=0, keepdims=True)
            v = value_fn(c, r0)
            a = a + jnp.dot(v, p.astype(v.dtype), preferred_element_type=F32)
        m_scr[c] = jnp.broadcast_to(m_new, m_scr.shape[1:])
        l_scr[c] = jnp.broadcast_to(l, l_scr.shape[1:])
        acc_scr[c] = a


def _da_kernel(lam_ref, qt_ref, k_ref, vt_ref, g_ref, o_ref, q_scr, acc_scr, m_scr, l_scr, *, tq, lam_init):
    tk = tq
    i = pl.program_id(1)
    n_chains = 2 * DA_HEADS
    frow = lax.broadcasted_iota(I32, (LANES, tq), 0)
    scale = jnp.asarray(DA_QK_DIM ** -0.5, BF16)
    for h in range(DA_HEADS):
        blk = qt_ref[h * LANES:(h + 1) * LANES, :] * scale
        zero = jnp.zeros_like(blk)
        q_scr[2 * h] = jnp.where(frow < DA_QK_DIM, blk, zero)
        q_scr[2 * h + 1] = jnp.where(frow >= DA_QK_DIM, blk, zero)
    acc_scr[...] = jnp.zeros(acc_scr.shape, F32)
    m_scr[...] = jnp.full(m_scr.shape, NEG_BIG, F32)
    l_scr[...] = jnp.zeros(l_scr.shape, F32)
    kpos0 = lax.broadcasted_iota(I32, (SUB_KEYS, tq), 0)
    qpos = i * tq + lax.broadcasted_iota(I32, (SUB_KEYS, tq), 1)
    subs = range(0, tk, SUB_KEYS)

    def chunk(j, masked):
        start = pl.multiple_of(j * tk, tk)

        def scores(c):
            head = c // 2
            out = []
            for r0 in subs:
                k = k_ref[pl.ds(start + r0, SUB_KEYS), head * LANES:(head + 1) * LANES]
                s = jnp.dot(k, q_scr[c], preferred_element_type=F32)
                if masked:
                    s = jnp.where(kpos0 + (j * tk + r0) <= qpos, s, NEG_BIG)
                out.append(s)
            return out

        def values(c, r0):
            head = c // 2
            return vt_ref[j, head * LANES:(head + 1) * LANES, r0:r0 + SUB_KEYS]

        _flash_chains(n_chains, scores, values, subs, m_scr, l_scr, acc_scr, DA_QK_AHEAD)

    def full_chunk(j, carry):
        chunk(j, False)
        return carry

    lax.fori_loop(0, i, full_chunk, 0)
    chunk(i, True)

    lp = lam_ref[...]
    lam = (jnp.exp(jnp.sum(lp[0:1] * lp[1:2], axis=1, keepdims=True))
           - jnp.exp(jnp.sum(lp[2:3] * lp[3:4], axis=1, keepdims=True)) + lam_init)
    for h in range(DA_HEADS):
        o = acc_scr[2 * h] / l_scr[2 * h][0:1] - lam * (acc_scr[2 * h + 1] / l_scr[2 * h + 1][0:1])
        var = jnp.mean(o * o, axis=0, keepdims=True)
        o = o * lax.rsqrt(var + EPS) * g_ref[...] * (1.0 - lam_init)
        o_ref[:, h * LANES:(h + 1) * LANES] = o.T.astype(o_ref.dtype)


def _diff_attention(lam_p, qt, pr, vt3, subln_g, b, s_len, tq, lam_init):
    nq = s_len // tq
    n = b * s_len
    n_chains = 2 * DA_HEADS
    kern = functools.partial(_da_kernel, tq=tq, lam_init=lam_init)
    return pl.pallas_call(
        kern,
        grid=(b, nq),
        in_specs=[
            pl.BlockSpec(lam_p.shape, lambda bi, i: (0, 0)),
            pl.BlockSpec((DA_HEADS * LANES, tq), lambda bi, i: (QT_DA // DA_HEADS, bi * nq + i)),
            pl.BlockSpec((s_len, DA_HEADS * LANES), lambda bi, i: (bi, PR_DAK // DA_HEADS)),
            pl.BlockSpec((s_len // tq, DA_HEADS * LANES, tq), lambda bi, i: (bi, VT_DA // DA_HEADS, 0)),
            pl.BlockSpec(subln_g.shape, lambda bi, i: (0, 0)),
        ],
        out_specs=pl.BlockSpec((tq, DA_WIDTH), lambda bi, i: (bi * nq + i, 0)),
        out_shape=jax.ShapeDtypeStruct((n, DA_WIDTH), BF16),
        scratch_shapes=[pltpu.VMEM((n_chains, LANES, tq), BF16), pltpu.VMEM((n_chains, DA_V_DIM, tq), F32),
                        pltpu.VMEM((n_chains, SUBLANES, tq), F32), pltpu.VMEM((n_chains, SUBLANES, tq), F32)],
        compiler_params=pltpu.CompilerParams(dimension_semantics=("arbitrary",) * 2, vmem_limit_bytes=VMEM_LIMIT),
        name="diff_attn",
    )(lam_p, qt, pr, vt3, subln_g)


def _dsa_kernel(qit_ref, ki_ref, wt_ref, qt_ref, k0_ref, k1_ref, vt_ref, o_ref,
                keys_scr, half_scr, qi_scr, q_scr, acc_scr, m_scr, l_scr, cut_scr, *, tq, n_sel, n_bits):
    tk = tq
    i = pl.program_id(1)
    nck = i + 1
    kpos0 = lax.broadcasted_iota(I32, (tk, tq), 0)
    qpos = i * tq + lax.broadcasted_iota(I32, (tk, tq), 1)
    frow = lax.broadcasted_iota(I32, (LANES, tq), 0)
    per_grp = DS_HEADS // DS_KV_HEADS

    scale = jnp.asarray(DS_HEAD_DIM ** -0.5, BF16)
    for h in range(DS_HEADS):
        keep = (frow < DS_HEAD_DIM) if h % 2 == 0 else (frow >= DS_HEAD_DIM)
        rows = slice((h // 2) * LANES, (h // 2 + 1) * LANES)
        blk = qit_ref[rows, :]
        qi_scr[h] = jnp.where(keep, blk, jnp.zeros_like(blk))
        blk = qt_ref[rows, :] * scale
        q_scr[h] = jnp.where(keep, blk, jnp.zeros_like(blk))

    kpos_sub = lax.broadcasted_iota(I32, (SUB_KEYS, tq), 0)
    qpos_sub = i * tq + lax.broadcasted_iota(I32, (SUB_KEYS, tq), 1)

    def score_chunk(c, carry):
        start = pl.multiple_of(c * tk, tk)
        for r0 in range(0, tk, SUB_KEYS):
            kc = ki_ref[pl.ds(start + r0, SUB_KEYS), :]
            acc = jnp.zeros((SUB_KEYS, tq), F32)
            for h in range(IDX_HEADS):
                s = jnp.dot(kc, qi_scr[h], preferred_element_type=F32)
                acc = acc + wt_ref[h:h + 1, :] * jnp.maximum(s, 0.0)
            bits = pltpu.bitcast(acc, I32)
            key = bits ^ ((bits >> 31) & 0x7FFFFFFF)
            key = jnp.where(acc == 0.0, 0, key)
            key = jnp.where(kpos_sub + (c * tk + r0) <= qpos_sub, key, INT_MIN)
            keys_scr[c, r0:r0 + SUB_KEYS, :] = key
            half_scr[c, r0:r0 + SUB_KEYS, :] = (key >> 16).astype(I16)
        return carry

    lax.fori_loop(0, nck, score_chunk, 0)

    def count(pred):
        def body(c, acc):
            hit = pred(keys_scr[c], kpos0 + c * tk)
            return acc + jnp.sum(hit.reshape(tk // SUBLANES, SUBLANES, tq), axis=0)
        acc = lax.fori_loop(0, nck, body, jnp.zeros((SUBLANES, tq), I32))
        return jnp.sum(acc, axis=0, keepdims=True)

    def count_half(pred):
        def body(c, acc):
            hit = pred(half_scr[c])
            for r0 in range(0, tk, HALF_ROWS):
                acc = acc + hit[r0:r0 + HALF_ROWS]
            return acc
        acc = lax.fori_loop(0, nck, body, jnp.zeros((HALF_ROWS, tq), I16))
        return jnp.sum(acc.astype(I32), axis=0, keepdims=True)

    one16, zero16 = jnp.int16(1), jnp.int16(0)

    def bisect_half(need):
        def bit(b, ans_u):
            cand_u = ans_u | jnp.left_shift(jnp.int32(1), 15 - b)
            cand = (cand_u - HALF_BIAS).astype(I16)
            cnt = count_half(lambda x: jnp.where(x >= cand, one16, zero16))
            return jnp.where(cnt >= need, cand_u, ans_u)
        return lax.fori_loop(0, 16, bit, jnp.zeros((1, tq), I32)) - HALF_BIAS

    hi_t = bisect_half(n_sel)
    hi_t16 = hi_t.astype(I16)
    n_hi_gt = count_half(lambda x: jnp.where(x > hi_t16, one16, zero16))

    def low_chunk(c, carry):
        lo = ((keys_scr[c] & 0xFFFF) - HALF_BIAS).astype(I16)
        half_scr[c] = jnp.where(half_scr[c] == hi_t16, lo, jnp.int16(-HALF_BIAS))
        return carry

    lax.fori_loop(0, nck, low_chunk, 0)
    lo_t = bisect_half(n_sel - n_hi_gt)
    thr = hi_t * (2 * HALF_BIAS) + (lo_t + HALF_BIAS)
    n_ge = count(lambda k, kpos: jnp.where(k >= thr, 1, 0))
    below_all = thr == INT_MIN
    cut_scr[...] = jnp.where(below_all, -1, (1 << n_bits) - 1)
    excess = jnp.max(jnp.where(below_all, 0, n_ge - n_sel))

    @pl.when(excess > 0)
    def _():
        n_gt = count(lambda k, kpos: jnp.where(k > thr, 1, 0))
        need = n_sel - n_gt

        def tie_bit(t, cut):
            cand = cut | jnp.left_shift(jnp.int32(1), n_bits - 1 - t)
            cnt = count(lambda k, kpos: jnp.where(k == thr, jnp.where(kpos < cand, 1, 0), 0))
            return jnp.where(cnt < need, cand, cut)

        cut = lax.fori_loop(0, n_bits, tie_bit, jnp.zeros((1, tq), I32))
        cut_scr[...] = jnp.where(below_all, -1, cut)

    cut = cut_scr[...]

    def bias_chunk(c, carry):
        k = keys_scr[c]
        sel = (k > thr) | ((k == thr) & (kpos0 + c * tk <= cut))
        keys_scr[c] = pltpu.bitcast(jnp.where(sel, 0.0, NEG_BIG).astype(F32), I32)
        return carry

    lax.fori_loop(0, nck, bias_chunk, 0)

    acc_scr[...] = jnp.zeros(acc_scr.shape, F32)
    m_scr[...] = jnp.full(m_scr.shape, NEG_BIG, F32)
    l_scr[...] = jnp.zeros(l_scr.shape, F32)

    def att_chunk(c, carry):
        start = pl.multiple_of(c * tk, tk)
        subs = range(0, tk, SUB_KEYS)

        def head_scores(h):
            k_ref = k0_ref if h // per_grp == 0 else k1_ref
            out = []
            for r0 in subs:
                s = jnp.dot(k_ref[pl.ds(start + r0, SUB_KEYS), :], q_scr[h], preferred_element_type=F32)
                out.append(s + pltpu.bitcast(keys_scr[c, r0:r0 + SUB_KEYS, :], F32))
            return out

        def head_values(h, r0):
            grp = h // per_grp
            return vt_ref[c, grp * DS_HEAD_DIM:(grp + 1) * DS_HEAD_DIM, r0:r0 + SUB_KEYS]

        _flash_chains(DS_HEADS, head_scores, head_values, subs, m_scr, l_scr, acc_scr, DS_QK_AHEAD)
        return carry

    lax.fori_loop(0, nck, att_chunk, 0)
    outs = [acc_scr[h] / l_scr[h][0:1] for h in range(DS_HEADS)]
    o_ref[...] = jnp.concatenate(outs, axis=0).T.astype(o_ref.dtype)


def _dsa_attention(qt, pr, vt3, iwt, b, s_len, tq):
    nq = s_len // tq
    n = b * s_len
    n_sel = min(TOPK_MAX, s_len // 4)
    n_bits = max(1, (s_len - 1).bit_length())
    kern = functools.partial(_dsa_kernel, tq=tq, n_sel=n_sel, n_bits=n_bits)
    seq = lambda blk: pl.BlockSpec((s_len, LANES), lambda bi, i: (bi, blk))
    qblk = lambda blk: pl.BlockSpec((4 * LANES, tq), lambda bi, i: (blk // 4, bi * nq + i))
    return pl.pallas_call(
        kern,
        grid=(b, nq),
        in_specs=[
            qblk(QT_IX), seq(PR_IXK),
            pl.BlockSpec((IDX_HEADS, tq), lambda bi, i: (0, bi * nq + i)),
            qblk(QT_DS), seq(PR_DSK), seq(PR_DSK + 1),
            pl.BlockSpec((s_len // tq, LANES, tq), lambda bi, i: (bi, VT_DS, 0)),
        ],
        out_specs=pl.BlockSpec((tq, DS_WIDTH), lambda bi, i: (bi * nq + i, 0)),
        out_shape=jax.ShapeDtypeStruct((n, DS_WIDTH), BF16),
        scratch_shapes=[pltpu.VMEM((nq, tq, tq), I32), pltpu.VMEM((nq, tq, tq), I16),
                        pltpu.VMEM((IDX_HEADS, LANES, tq), BF16), pltpu.VMEM((DS_HEADS, LANES, tq), BF16),
                        pltpu.VMEM((DS_HEADS, DS_HEAD_DIM, tq), F32),
                        pltpu.VMEM((DS_HEADS, SUBLANES, tq), F32), pltpu.VMEM((DS_HEADS, SUBLANES, tq), F32),
                        pltpu.VMEM((1, tq), I32)],
        compiler_params=pltpu.CompilerParams(dimension_semantics=("arbitrary",) * 2, vmem_limit_bytes=VMEM_LIMIT),
        name="dsa_attn",
    )(qt, pr, iwt, qt, pr, pr, vt3)


def _merge_kernel(oa_ref, ob_ref, gate_ref, x_ref, wa_ref, wb_ref, wo_ref, o_ref):
    ya = jnp.dot(oa_ref[...], wa_ref[...], preferred_element_type=F32)
    yb = jnp.dot(ob_ref[...], wb_ref[...], preferred_element_type=F32)
    mix = gate_ref[:, :D_MODEL] * ya + gate_ref[:, D_MODEL:] * yb
    o_ref[...] = x_ref[...] + jnp.dot(mix.astype(BF16), wo_ref[...], preferred_element_type=F32)


def _merge(oa, ob, gate, x2, wa, wb, wo, tm):
    n = x2.shape[0]
    full = lambda a: pl.BlockSpec(a.shape, lambda i: (0, 0))
    rows = lambda w: pl.BlockSpec((tm, w), lambda i: (i, 0))
    return pl.pallas_call(
        _merge_kernel,
        grid=(n // tm,),
        in_specs=[rows(DA_WIDTH), rows(DS_WIDTH), rows(2 * D_MODEL), rows(D_MODEL), full(wa), full(wb), full(wo)],
        out_specs=rows(D_MODEL),
        out_shape=jax.ShapeDtypeStruct((n, D_MODEL), F32),
        compiler_params=pltpu.CompilerParams(dimension_semantics=("arbitrary",), vmem_limit_bytes=VMEM_LIMIT),
        name="merge",
    )(oa, ob, gate, x2, wa, wb, wo)


def _norm_matmul_kernel(x_ref, g_ref, w_ref, o_ref):
    h = _rms(x_ref[...], g_ref[...]).astype(BF16)
    o_ref[...] = jnp.dot(h, w_ref[...], preferred_element_type=F32).astype(o_ref.dtype)


def _norm_matmul(x2, g, w, tm):
    n = x2.shape[0]
    return pl.pallas_call(
        _norm_matmul_kernel,
        grid=(n // tm,),
        in_specs=[pl.BlockSpec((tm, x2.shape[1]), lambda i: (i, 0)), pl.BlockSpec(g.shape, lambda i: (0, 0)),
                  pl.BlockSpec(w.shape, lambda i: (0, 0))],
        out_specs=pl.BlockSpec((tm, w.shape[1]), lambda i: (i, 0)),
        out_shape=jax.ShapeDtypeStruct((n, w.shape[1]), BF16),
        compiler_params=pltpu.CompilerParams(dimension_semantics=("arbitrary",), vmem_limit_bytes=VMEM_LIMIT),
        name="mem_kv_proj",
    )(x2, g, w)


def _mem_attn_kernel(x_ref, g_ref, wq_ref, kv_ref, wo_ref, o_ref):
    x = x_ref[...]
    hn = _rms(x, g_ref[...]).astype(BF16)
    q = jnp.dot(hn, wq_ref[...], preferred_element_type=F32).astype(BF16)
    scale = MEM_HEAD_DIM ** -0.5
    heads = []
    for h in range(MEM_HEADS):
        qh = q[:, h * MEM_HEAD_DIM:(h + 1) * MEM_HEAD_DIM]
        kh = kv_ref[:, h * MEM_HEAD_DIM:(h + 1) * MEM_HEAD_DIM]
        vh = kv_ref[:, MEM_WIDTH + h * MEM_HEAD_DIM:MEM_WIDTH + (h + 1) * MEM_HEAD_DIM]
        s = lax.dot_general(qh, kh, NT_DIMS, preferred_element_type=F32) * scale
        m = jnp.max(s, axis=1, keepdims=True)
        p = jnp.exp(s - m)
        l = jnp.sum(p, axis=1, keepdims=True)
        heads.append(jnp.dot(p.astype(BF16), vh, preferred_element_type=F32) / l)
    o = jnp.concatenate(heads, axis=1).astype(BF16)
    o_ref[...] = x + jnp.dot(o, wo_ref[...], preferred_element_type=F32)


def _mem_attn(x2, g, wq, kv, wo, b, s_len, m_tok, tm):
    nb = s_len // tm
    n = x2.shape[0]
    full = lambda a: pl.BlockSpec(a.shape, lambda bi, i: (0, 0))
    return pl.pallas_call(
        _mem_attn_kernel,
        grid=(b, nb),
        in_specs=[pl.BlockSpec((tm, D_MODEL), lambda bi, i: (bi * nb + i, 0)), full(g), full(wq),
                  pl.BlockSpec((m_tok, 2 * MEM_WIDTH), lambda bi, i: (bi, 0)), full(wo)],
        out_specs=pl.BlockSpec((tm, D_MODEL), lambda bi, i: (bi * nb + i, 0)),
        out_shape=jax.ShapeDtypeStruct((n, D_MODEL), F32),
        compiler_params=pltpu.CompilerParams(dimension_semantics=("arbitrary",) * 2, vmem_limit_bytes=VMEM_LIMIT),
        name="mem_attn",
    )(x2, g, wq, kv, wo)


N_SUB = 2 * PEER_HEADS
EXPERT_BITS = 14


def _route_kernel(x_ref, g_ref, wq_ref, keys_ref, hf_ref, idx_ref, row_ref, gate_ref, sc_scr, ts_scr, ti_scr):
    hf = _rms(x_ref[...], g_ref[...])
    tm = hf.shape[0]
    hb = hf.astype(BF16)
    bits = pltpu.bitcast(hb.astype(F32), jnp.uint32)
    half_d = D_MODEL // 2
    for s in range(SUBLANES // 2):
        lo = bits[:, s * LANES:(s + 1) * LANES] >> 16
        hi = bits[:, half_d + s * LANES:half_d + (s + 1) * LANES] & jnp.uint32(0xFFFF0000)
        hf_ref[:, s, :] = lo | hi
        hf_ref[:, s + SUBLANES // 2, :] = lo | hi
    q = jnp.dot(hb, wq_ref[...], preferred_element_type=F32).astype(BF16)
    for g in range(N_SUB):
        blk = q[:, (g // 2) * LANES:(g // 2 + 1) * LANES]
        sc_scr[g] = lax.dot_general(keys_ref[g], blk, NT_DIMS, preferred_element_type=F32)

    n_slot = PEER_N_KEYS // 2
    slot_id = lax.broadcasted_iota(I32, (n_slot, tm), 0)

    def sub_topk(g, carry):
        a = sc_scr[g, :n_slot, :]
        b = sc_scr[g, n_slot:, :]
        take_b = b > a
        cur = jnp.where(take_b, b, a)
        cur_i = jnp.where(take_b, slot_id + n_slot, slot_id)
        res = jnp.where(take_b, a, b)
        res_i = jnp.where(take_b, slot_id, slot_id + n_slot)
        vals, ids = [], []
        for _ in range(PEER_TOPK):
            m = jnp.max(cur, axis=0, keepdims=True)
            idx = jnp.min(jnp.where(cur == m, cur_i, PEER_N_KEYS), axis=0, keepdims=True)
            vals.append(m)
            ids.append(idx)
            hit = cur_i == idx
            cur = jnp.where(hit, res, cur)
            cur_i = jnp.where(hit, res_i, cur_i)
            res = jnp.where(hit, -jnp.inf, res)
        ts_scr[g] = jnp.concatenate(vals, axis=0)
        ti_scr[g] = jnp.concatenate(ids, axis=0)
        return carry

    lax.fori_loop(0, N_SUB, sub_topk, 0)

    iota16 = lax.broadcasted_iota(I32, (PEER_TOPK, tm), 0)
    iota8 = lax.broadcasted_iota(I32, (SUBLANES, tm), 0)
    lead_rows = SUBLANES

    def head_topk(h, carry):
        s0, s1 = ts_scr[2 * h], ts_scr[2 * h + 1]
        i0, i1 = ti_scr[2 * h], ti_scr[2 * h + 1]
        cands, codes = [], []
        for a in range(lead_rows):
            n_j = PEER_TOPK if a == 0 else SUBLANES
            cands.append(s0[a:a + 1] + s1[:n_j])
            j_iota = iota16 if a == 0 else iota8
            codes.append(((a * PEER_TOPK + j_iota) << EXPERT_BITS) | (i0[a:a + 1] * PEER_N_KEYS + i1[:n_j]))
        cands.append(s0[lead_rows:] + s1[0:1])
        codes.append((((iota8 + lead_rows) * PEER_TOPK) << EXPERT_BITS) | (i0[lead_rows:] * PEER_N_KEYS + i1[0:1]))
        cand = jnp.concatenate(cands, axis=0)
        code = jnp.concatenate(codes, axis=0)
        big = jnp.int32(2 ** 30)
        vals, ids = [], []
        for _ in range(PEER_TOPK):
            m = jnp.max(cand, axis=0, keepdims=True)
            best = jnp.min(jnp.where(cand == m, code, big), axis=0, keepdims=True)
            vals.append(m)
            ids.append(best & (2 ** EXPERT_BITS - 1))
            cand = jnp.where(code == best, -jnp.inf, cand)
        best_s = jnp.concatenate(vals, axis=0)
        e = jnp.exp(best_s - best_s[0:1])
        gate_ref[h] = e / jnp.sum(e, axis=0, keepdims=True)
        best_i = jnp.concatenate(ids, axis=0)
        idx_ref[h] = best_i
        row_ref[h] = (best_i >> 1) * SUBLANES
        return carry

    lax.fori_loop(0, PEER_HEADS, head_topk, 0)


def _route(x2, g, wq, keys_p, tm):
    n = x2.shape[0]
    return pl.pallas_call(
        _route_kernel,
        grid=(n // tm,),
        in_specs=[pl.BlockSpec((tm, D_MODEL), lambda i: (i, 0)), pl.BlockSpec(g.shape, lambda i: (0, 0)),
                  pl.BlockSpec(wq.shape, lambda i: (0, 0)), pl.BlockSpec(keys_p.shape, lambda i: (0, 0, 0))],
        out_specs=[pl.BlockSpec((tm, SUBLANES, LANES), lambda i: (i, 0, 0))]
        + [pl.BlockSpec((PEER_HEADS, PEER_TOPK, tm), lambda i: (0, 0, i))] * 3,
        out_shape=[jax.ShapeDtypeStruct((n, SUBLANES, LANES), jnp.uint32),
                   jax.ShapeDtypeStruct((PEER_HEADS, PEER_TOPK, n), I32),
                   jax.ShapeDtypeStruct((PEER_HEADS, PEER_TOPK, n), I32),
                   jax.ShapeDtypeStruct((PEER_HEADS, PEER_TOPK, n), F32)],
        scratch_shapes=[pltpu.VMEM((N_SUB, PEER_N_KEYS, tm), F32), pltpu.VMEM((N_SUB, PEER_TOPK, tm), F32),
                        pltpu.VMEM((N_SUB, PEER_TOPK, tm), I32)],
        compiler_params=pltpu.CompilerParams(dimension_semantics=("arbitrary",), vmem_limit_bytes=VMEM_LIMIT),
        name="peer_route",
    )(x2, g, wq, keys_p)


TILE_ROWS = 2 * SUBLANES
PAIRS_PER_KTILE = 16
TOK_UNROLL = 16
SUM_UNROLL = 16


def _pack_table(t):
    e = t.shape[0]
    bits = lax.bitcast_convert_type(t.astype(BF16), jnp.uint16).astype(jnp.uint32)
    words = bits[:, :D_MODEL // 2] | (bits[:, D_MODEL // 2:] << 16)
    return words.reshape(e * (SUBLANES // 2), LANES)


def _gather_tiles(off_ref, tab_ref, u):
    tiles = []
    for k in range(PEER_SEL):
        start = pl.multiple_of(off_ref[u * PEER_SEL + k], SUBLANES)
        tiles.append(pltpu.bitcast(tab_ref[pl.ds(start, SUBLANES), :], BF16))
    return tiles


def _for_trips(rows_hbm, bufs, sems, tok0, unroll, n_trip, body):
    assert n_trip % 2 == 0
    words = unroll * PEER_SEL

    def fetch(trip, par):
        src = rows_hbm.at[pl.ds(pl.multiple_of((tok0 + trip * unroll) * PEER_SEL, words), words)]
        return pltpu.make_async_copy(src, bufs[par], sems.at[par])

    fetch(0, 0).start()
    fetch(1, 1).start()

    def pair(p, carry):
        for par in range(2):
            trip = 2 * p + par
            fetch(trip, par).wait()
            body(trip, bufs[par], par)
            fetch(jnp.minimum(trip + 2, n_trip - 2 + par), par).start()
        return carry

    lax.fori_loop(0, n_trip // 2, pair, 0)
    for par in range(2):
        fetch(n_trip - 2 + par, par).wait()


def _peer_dot_kernel(rows_hbm, tab_ref, h_ref, gate_ref, idxv_ref, sel_ref, ce_ref, co_ref,
                     ye_scr, yo_scr, ze_scr, zo_scr, off_a, off_b, sems, *, tt):
    ones = jnp.ones((SUBLANES, LANES), BF16)
    half_rows = PAIRS_PER_KTILE
    n_trip = tt // TOK_UNROLL

    def row_sums(i, slot, off_ref):
        for u in range(TOK_UNROLL):
            t = i * TOK_UNROLL + u
            hp = pltpu.bitcast(h_ref[t], BF16)
            prods = [tile * hp for tile in _gather_tiles(off_ref, tab_ref, u)]
            for kt in range(PEER_SEL // PAIRS_PER_KTILE):
                stack = jnp.concatenate(prods[kt * PAIRS_PER_KTILE:(kt + 1) * PAIRS_PER_KTILE], axis=0)
                y = jnp.dot(sel_ref[...], stack, preferred_element_type=F32)
                r0 = u * PEER_SEL + kt * half_rows
                ye_scr[slot, r0:r0 + half_rows, :] = y[:half_rows].astype(BF16)
                yo_scr[slot, r0:r0 + half_rows, :] = y[half_rows:].astype(BF16)

    def lane_sums(i, slot):
        ze = lax.dot_general(ones, ye_scr[slot], NT_DIMS, preferred_element_type=F32)
        zo = lax.dot_general(ones, yo_scr[slot], NT_DIMS, preferred_element_type=F32)
        for u in range(TOK_UNROLL):
            t = i * TOK_UNROLL + u
            ze_scr[pl.ds(t, 1), :] = ze[0:1, u * PEER_SEL:(u + 1) * PEER_SEL]
            zo_scr[pl.ds(t, 1), :] = zo[0:1, u * PEER_SEL:(u + 1) * PEER_SEL]

    def trip(i, off_ref, par):
        lane_sums(jnp.maximum(i - 1, 0), 1 - par)
        row_sums(i, par, off_ref)

    ye_scr[1] = jnp.zeros(ye_scr.shape[1:], ye_scr.dtype)
    yo_scr[1] = jnp.zeros(yo_scr.shape[1:], yo_scr.dtype)
    _for_trips(rows_hbm, (off_a, off_b), sems, pl.program_id(0) * tt, TOK_UNROLL, n_trip, trip)
    lane_sums(n_trip - 1, (n_trip - 1) & 1)
    even = (idxv_ref[...] & 1) == 0
    a = jnp.where(even, ze_scr[...], zo_scr[...])
    c = 0.5 * a * (1.0 + lax.erf(a * (2.0 ** -0.5))) * gate_ref[...]
    ce_ref[...] = jnp.where(even, c, 0.0)
    co_ref[...] = jnp.where(even, 0.0, c)


def _peer_dot(rows_flat, tab, hpk, gate, idx2, sel, tt):
    n = hpk.shape[0]
    kern = functools.partial(_peer_dot_kernel, tt=tt)
    rows = pl.BlockSpec((tt, PEER_SEL), lambda i: (i, 0))
    return pl.pallas_call(
        kern,
        grid=(n // tt,),
        in_specs=[pl.BlockSpec(memory_space=pl.ANY),
                  pl.BlockSpec(memory_space=pltpu.VMEM),
                  pl.BlockSpec((tt, SUBLANES, LANES), lambda i: (i, 0, 0)),
                  rows, rows, pl.BlockSpec(sel.shape, lambda i: (0, 0))],
        out_specs=[rows, rows],
        out_shape=[jax.ShapeDtypeStruct((n, PEER_SEL), F32), jax.ShapeDtypeStruct((n, PEER_SEL), F32)],
        scratch_shapes=[pltpu.VMEM((2, TOK_UNROLL * PEER_SEL, LANES), BF16)] * 2
        + [pltpu.VMEM((tt, PEER_SEL), F32)] * 2
        + [pltpu.SMEM((TOK_UNROLL * PEER_SEL,), I32)] * 2 + [pltpu.SemaphoreType.DMA((2,))],
        compiler_params=pltpu.CompilerParams(dimension_semantics=("arbitrary",), vmem_limit_bytes=VMEM_LIMIT),
        name="peer_dot",
    )(rows_flat, tab, hpk, gate, idx2, sel)


def _peer_sum_kernel(rows_hbm, ce_ref, co_ref, spread_ref, tab_ref, o_ref, m1_scr, m2_scr, off_a, off_b, sems,
                     *, tt):
    width = PEER_SEL * TILE_ROWS
    cc = jnp.concatenate([ce_ref[...], co_ref[...]], axis=1)
    c1 = cc.astype(BF16)
    c2 = (cc - c1.astype(F32)).astype(BF16)
    m1_scr[...] = jnp.dot(c1, spread_ref[...], preferred_element_type=F32)
    m2_scr[...] = jnp.dot(c2, spread_ref[...], preferred_element_type=F32)
    row = lax.broadcasted_iota(I32, (SUBLANES, width), 0)
    lane = lax.broadcasted_iota(I32, (SUBLANES, width), 1)
    half = SUBLANES // 2
    on_row = (lane & (SUBLANES - 1)) == 2 * (row % half) + row // half

    def token(t, off_ref, u):
        w = jnp.concatenate(_gather_tiles(off_ref, tab_ref, u), axis=0)
        lhs = []
        for m_scr in (m1_scr, m2_scr):
            coef = jnp.broadcast_to(m_scr[pl.ds(t, 1), :], (SUBLANES, width))
            lhs.append(jnp.where(on_row, coef, 0.0).astype(BF16))
        res = jnp.dot(jnp.concatenate(lhs, axis=0), w, preferred_element_type=F32)
        o_ref[t] = res[:SUBLANES] + res[SUBLANES:]

    def trip(i, off_ref, par):
        for u in range(SUM_UNROLL):
            token(i * SUM_UNROLL + u, off_ref, u)

    _for_trips(rows_hbm, (off_a, off_b), sems, pl.program_id(0) * tt, SUM_UNROLL, tt // SUM_UNROLL, trip)


def _peer_sum(rows_flat, ce, co, spread, tab, tt):
    n = ce.shape[0]
    kern = functools.partial(_peer_sum_kernel, tt=tt)
    rows = pl.BlockSpec((tt, PEER_SEL), lambda i: (i, 0))
    width = PEER_SEL * TILE_ROWS
    return pl.pallas_call(
        kern,
        grid=(n // tt,),
        in_specs=[pl.BlockSpec(memory_space=pl.ANY), rows, rows,
                  pl.BlockSpec(spread.shape, lambda i: (0, 0)), pl.BlockSpec(memory_space=pltpu.VMEM)],
        out_specs=pl.BlockSpec((tt, SUBLANES, LANES), lambda i: (i, 0, 0)),
        out_shape=jax.ShapeDtypeStruct((n, SUBLANES, LANES), F32),
        scratch_shapes=[pltpu.VMEM((tt, width), F32), pltpu.VMEM((tt, width), F32)]
        + [pltpu.SMEM((SUM_UNROLL * PEER_SEL,), I32)] * 2 + [pltpu.SemaphoreType.DMA((2,))],
        compiler_params=pltpu.CompilerParams(dimension_semantics=("arbitrary",), vmem_limit_bytes=VMEM_LIMIT),
        name="peer_sum",
    )(rows_flat, ce, co, spread, tab)


def _final_kernel(x_ref, p_ref, g_ref, o_ref):
    peer = jnp.concatenate([p_ref[:, s, :] for s in range(SUBLANES)], axis=1)
    o_ref[...] = _rms(x_ref[...] + peer, g_ref[...])


def _final(x2, peer3, g, tm):
    n = x2.shape[0]
    return pl.pallas_call(
        _final_kernel,
        grid=(n // tm,),
        in_specs=[pl.BlockSpec((tm, D_MODEL), lambda i: (i, 0)),
                  pl.BlockSpec((tm, SUBLANES, LANES), lambda i: (i, 0, 0)),
                  pl.BlockSpec(g.shape, lambda i: (0, 0))],
        out_specs=pl.BlockSpec((tm, D_MODEL), lambda i: (i, 0)),
        out_shape=jax.ShapeDtypeStruct((n, D_MODEL), F32),
        compiler_params=pltpu.CompilerParams(dimension_semantics=("arbitrary",), vmem_limit_bytes=VMEM_LIMIT),
        name="final_norm",
    )(x2, peer3, g)


def _tile(n, pref):
    t = pref
    while n % t:
        t //= 2
    return t


def _layer(l, x2, mem2, pos, invf, b, s_len, m_tok, norm_mix_g, w_in, da_lambda, da_subln_g, w_branch_a,
           w_branch_b, gate_bias, w_out, norm_mem_g, mem_kv_norm_g, w_mem_q, w_mem_kv, w_mem_o, norm_ffn_g,
           peer_w_q, peer_sub_keys, peer_u, peer_v):
    n = b * s_len
    row2 = lambda v: v.reshape(1, -1)
    splits = [0]
    for c in COL_SIZES:
        splits.append(splits[-1] + c)
    da_q, da_k, da_v, ds_q, ds_k, ds_v, ix_q, ix_k, ix_w, gates = (
        w_in[l][:, splits[j]:splits[j + 1]] for j in range(len(COL_SIZES)))
    k0, k1 = ds_k[:, :DS_HEAD_DIM], ds_k[:, DS_HEAD_DIM:]
    w_keys = jnp.concatenate([da_k, k0, k0, k1, k1, ix_k, ix_k], axis=1).astype(BF16)
    w_q_t = jnp.concatenate([da_q, ds_q, ix_q], axis=1).T.astype(BF16)
    w_v_t = jnp.concatenate([da_v, ds_v], axis=1).T.astype(BF16)
    t_att = _tile(s_len, 256)
    t_proj = t_att
    pr, gate, qt, vt3, iwt = _in_proj(x2, row2(norm_mix_g[l]), pos, invf, pos.reshape(1, n),
                                      invf[:, :ROPE_HALF].reshape(ROPE_HALF, 1), row2(gate_bias[l]), w_keys,
                                      gates.astype(BF16), w_q_t, w_v_t, ix_w.T.astype(BF16), t_proj, t_att)

    lam_init = 0.8 - 0.6 * math.exp(-0.3 * l)
    o_a = _diff_attention(da_lambda[l], qt, pr, vt3, da_subln_g[l].reshape(-1, 1), b, s_len, t_att, lam_init)
    o_b = _dsa_attention(qt, pr, vt3, iwt, b, s_len, t_att)
    x2 = _merge(o_a, o_b, gate, x2, w_branch_a[l].astype(BF16), w_branch_b[l].astype(BF16),
                w_out[l].astype(BF16), _tile(n, 256))

    kv = _norm_matmul(mem2, row2(mem_kv_norm_g[l]), w_mem_kv[l].astype(BF16), _tile(mem2.shape[0], 256))
    x2 = _mem_attn(x2, row2(norm_mem_g[l]), w_mem_q[l].astype(BF16), kv, w_mem_o[l].astype(BF16),
                   b, s_len, m_tok, _tile(s_len, 256))

    sk = peer_sub_keys[l].reshape(N_SUB, PEER_N_KEYS, PEER_HALF)
    z = jnp.zeros_like(sk)
    keys_p = jnp.where((jnp.arange(N_SUB) % 2 == 0)[:, None, None],
                       jnp.concatenate([sk, z], axis=2), jnp.concatenate([z, sk], axis=2)).astype(BF16)
    hpk, idx_t, row_t, gate_t = _route(x2, row2(norm_ffn_g[l]), peer_w_q[l].astype(BF16), keys_p, _tile(n, 256))
    idx2 = idx_t.reshape(PEER_SEL, n).T
    rows2 = row_t.reshape(PEER_SEL, n).T
    gate2 = gate_t.reshape(PEER_SEL, n).T
    tt = _tile(n, 256)
    r32 = jnp.arange(2 * PAIRS_PER_KTILE)[:, None]
    c256 = jnp.arange(PAIRS_PER_KTILE * TILE_ROWS)[None, :]
    sel = ((c256 // TILE_ROWS == r32 % PAIRS_PER_KTILE)
           & ((c256 % TILE_ROWS) // SUBLANES == r32 // PAIRS_PER_KTILE)).astype(BF16)
    k256 = jnp.arange(2 * PEER_SEL)[:, None]
    c2048 = jnp.arange(PEER_SEL * TILE_ROWS)[None, :]
    spread = ((c2048 // TILE_ROWS == k256 % PEER_SEL)
              & ((c2048 % TILE_ROWS) // SUBLANES == k256 // PEER_SEL)).astype(BF16)
    rows_flat = rows2.reshape(-1)
    ce2, co2 = _peer_dot(rows_flat, _pack_table(peer_u[l]), hpk, gate2, idx2, sel, tt)
    peer3 = _peer_sum(rows_flat, ce2, co2, spread, _pack_table(peer_v[l]), tt)
    return x2, peer3


def kernel(x, mem, positions, norm_mix_g, w_in, da_lambda, da_subln_g, w_branch_a, w_branch_b, gate_bias, w_out, norm_mem_g, mem_kv_norm_g, w_mem_q, w_mem_kv, w_mem_o, norm_ffn_g, peer_w_q, peer_sub_keys, peer_u, peer_v, final_norm_g):
    b, s_len, d = x.shape
    m_tok = mem.shape[1]
    n = b * s_len
    depth = w_in.shape[0]
    x2 = x.reshape(n, d)
    mem2 = mem.reshape(b * m_tok, d)
    pos = positions.astype(F32).reshape(n, 1)
    inv_freq = ROPE_THETA ** (-(jnp.arange(ROPE_HALF, dtype=F32) * 2.0) / ROPE_DIM)
    invf = jnp.tile(inv_freq, LANES // ROPE_HALF).reshape(1, LANES)
    peer3 = None
    for l in range(depth):
        if peer3 is not None:
            x2 = x2 + peer3.reshape(n, d)
        x2, peer3 = _layer(l, x2, mem2, pos, invf, b, s_len, m_tok, norm_mix_g, w_in, da_lambda, da_subln_g,
                           w_branch_a, w_branch_b, gate_bias, w_out, norm_mem_g, mem_kv_norm_g, w_mem_q, w_mem_kv,
                           w_mem_o, norm_ffn_g, peer_w_q, peer_sub_keys, peer_u, peer_v)
    out = _final(x2, peer3, final_norm_g.reshape(1, d), _tile(n, 256))
    return out.reshape(b, s_len, d)
```

```python
import functools
import math

import jax
import jax.numpy as jnp
from jax import lax
from jax.experimental import pallas as pl
from jax.experimental.pallas import tpu as pltpu

F32 = jnp.float32
BF16 = jnp.bfloat16
I32 = jnp.int32
I16 = jnp.int16

D_MODEL = 1024
EPS = 1e-6
ROPE_THETA = 500000.0
ROT_HEAD_DIM = 64
ROPE_DIM = 16
ROPE_HALF = ROPE_DIM // 2

DA_HEADS = 4
DA_QK_DIM = 64
DA_V_DIM = 128
DA_WIDTH = 512
DS_HEADS = 8
DS_KV_HEADS = 2
DS_HEAD_DIM = 64
DS_WIDTH = 512
IDX_HEADS = 8
TOPK_MAX = 256
MEM_HEADS = 4
MEM_HEAD_DIM = 128
MEM_WIDTH = 512
PEER_HEADS = 8
PEER_N_KEYS = 128
PEER_HALF = 64
PEER_TOPK = 16
PEER_SEL = PEER_HEADS * PEER_TOPK

COL_SIZES = (512, 512, 512, 512, 128, 128, 512, 64, 8, 2048)

LANES = 128
SUBLANES = 8
NEG_BIG = -1e30
INT_MIN = -(2 ** 31)
HALF_BIAS = 2 ** 15
HALF_ROWS = 2 * SUBLANES
VMEM_LIMIT = 56 * 1024 * 1024

NT_DIMS = (((1,), (1,)), ((), ()))
PROJ_COLS = 512


def _rms(x, g):
    var = jnp.mean(x * x, axis=-1, keepdims=True)
    return x * lax.rsqrt(var + EPS) * g


def _rope_rows(y, cos8, sin8):
    pieces = []
    for r0 in range(0, y.shape[0], ROT_HEAD_DIM):
        t1 = y[r0:r0 + ROPE_HALF]
        t2 = y[r0 + ROPE_HALF:r0 + ROPE_DIM]
        pieces += [t1 * cos8 - t2 * sin8, t2 * cos8 + t1 * sin8, y[r0 + ROPE_DIM:r0 + ROT_HEAD_DIM]]
    return jnp.concatenate(pieces, axis=0)


def _in_proj_kernel(x_ref, g_ref, pos_ref, invf_ref, pos_t_ref, invf8_ref, bias_ref, wr_ref, wg_ref,
                    wqt_ref, wvt_ref, wwt_ref, pr_ref, gate_ref, qt_ref, vt_ref, iwt_ref):
    h = _rms(x_ref[...], g_ref[...]).astype(BF16)
    tm = h.shape[0]
    ang_t = invf8_ref[...] * pos_t_ref[...]
    cos8 = jnp.cos(ang_t)
    sin8 = jnp.sin(ang_t)
    row_chunk = 2 * LANES
    for r0 in range(0, wqt_ref.shape[0], row_chunk):
        y = lax.dot_general(wqt_ref[r0:r0 + row_chunk, :], h, NT_DIMS, preferred_element_type=F32)
        qt_ref[r0:r0 + row_chunk, :] = _rope_rows(y, cos8, sin8).astype(qt_ref.dtype)
    for r0 in range(0, wvt_ref.shape[0], row_chunk):
        r1 = min(r0 + row_chunk, wvt_ref.shape[0])
        y = lax.dot_general(wvt_ref[r0:r1, :], h, NT_DIMS, preferred_element_type=F32)
        chunk = vt_ref.shape[2]
        for j in range(vt_ref.shape[0]):
            vt_ref[j, r0:r1, :] = y[:, j * chunk:(j + 1) * chunk].astype(vt_ref.dtype)
    iwt_ref[...] = lax.dot_general(wwt_ref[...], h, NT_DIMS, preferred_element_type=F32)
    ang = pos_ref[...] * invf_ref[...]
    cos = jnp.cos(ang)
    sin = jnp.sin(ang)
    lane = lax.broadcasted_iota(I32, (tm, LANES), 1) % ROT_HEAD_DIM
    c_t = jnp.where(lane < ROPE_DIM, cos, 1.0)
    s_lo = jnp.where(lane < ROPE_HALF, -sin, 0.0)
    s_hi = jnp.where((lane >= ROPE_HALF) & (lane < ROPE_DIM), sin, 0.0)
    n_r = wr_ref.shape[1]
    for c0 in range(0, n_r, PROJ_COLS):
        c1 = min(c0 + PROJ_COLS, n_r)
        w = c1 - c0
        y = jnp.dot(h, wr_ref[:, c0:c1], preferred_element_type=F32)
        reps = w // LANES
        ct = jnp.concatenate([c_t] * reps, axis=1)
        sl = jnp.concatenate([s_lo] * reps, axis=1)
        sh = jnp.concatenate([s_hi] * reps, axis=1)
        y = y * ct + pltpu.roll(y, w - ROPE_HALF, 1) * sl + pltpu.roll(y, ROPE_HALF, 1) * sh
        pr_ref[:, c0:c1] = y.astype(pr_ref.dtype)
    n_g = wg_ref.shape[1]
    for c0 in range(0, n_g, PROJ_COLS):
        y = jnp.dot(h, wg_ref[:, c0:c0 + PROJ_COLS], preferred_element_type=F32)
        gate_ref[:, c0:c0 + PROJ_COLS] = jax.nn.sigmoid(y + bias_ref[:, c0:c0 + PROJ_COLS])


def _in_proj(x2, g, pos, invf, pos_t, invf8, bias, wr, wg, wqt, wvt, wwt, tm, chunk):
    n = x2.shape[0]
    per_step = tm // chunk
    full = lambda a: pl.BlockSpec(a.shape, lambda i: (0, 0))
    rows = lambda w: pl.BlockSpec((tm, w), lambda i: (i, 0))
    cols = lambda r: pl.BlockSpec((r, tm), lambda i: (0, i))
    return pl.pallas_call(
        _in_proj_kernel,
        grid=(n // tm,),
        in_specs=[rows(D_MODEL), full(g), rows(1), full(invf), cols(1), full(invf8), full(bias), full(wr), full(wg),
                  full(wqt), full(wvt), full(wwt)],
        out_specs=[rows(wr.shape[1]), rows(wg.shape[1]), cols(wqt.shape[0]),
                   pl.BlockSpec((per_step, wvt.shape[0], chunk), lambda i: (i, 0, 0)), cols(wwt.shape[0])],
        out_shape=[jax.ShapeDtypeStruct((n, wr.shape[1]), BF16), jax.ShapeDtypeStruct((n, wg.shape[1]), F32),
                   jax.ShapeDtypeStruct((wqt.shape[0], n), BF16),
                   jax.ShapeDtypeStruct((n // chunk, wvt.shape[0], chunk), BF16),
                   jax.ShapeDtypeStruct((wwt.shape[0], n), F32)],
        compiler_params=pltpu.CompilerParams(dimension_semantics=("arbitrary",), vmem_limit_bytes=VMEM_LIMIT),
        name="in_proj",
    )(x2, g, pos, invf, pos_t, invf8, bias, wr, wg, wqt, wvt, wwt)


QT_DA = 0
QT_DS = 4
QT_IX = 8
VT_DA = 0
VT_DS = 4
PR_DAK = 0
PR_DSK = 4
PR_IXK = 6


SUB_KEYS = 128
DA_QK_AHEAD = 1
DS_QK_AHEAD = 3


def _zero_after(x):
    bits = pltpu.bitcast(x, jnp.uint32)
    return pltpu.bitcast((bits >> 16) >> 16, F32)


def _flash_chains(n_chains, score_fn, value_fn, subs, m_scr, l_scr, acc_scr, ahead):
    pending = [score_fn(c) for c in range(ahead)]
    for c in range(n_chains):
        scores = pending.pop(0)
        m = m_scr[c][0:1]
        if c + ahead < n_chains:
            pending.append(score_fn(c + ahead))
            m = m + _zero_after(pending[-1][0][0:1])
        m_new = m
        for s in scores:
            m_new = jnp.maximum(m_new, jnp.max(s, axis=0, keepdims=True))
        alpha = jnp.exp(m - m_new)
        l = alpha * l_scr[c][0:1]
        a = alpha * acc_scr[c]
        for r0, s in zip(subs, scores):
            p = jnp.exp(s - m_new)
            l = l + jnp.sum(p, axis=0, keepdims=True)
            v = value_fn(c, r0)
            a = a + jnp.dot(v, p.astype(v.dtype), preferred_element_type=F32)
        m_scr[c] = jnp.broadcast_to(m_new, m_scr.shape[1:])
        l_scr[c] = jnp.broadcast_to(l, l_scr.shape[1:])
        acc_scr[c] = a


def _da_kernel(lam_ref, qt_ref, k_ref, vt_ref, g_ref, o_ref, q_scr, acc_scr, m_scr, l_scr, *, tq, lam_init):
    tk = tq
    i = pl.program_id(1)
    n_chains = 2 * DA_HEADS
    frow = lax.broadcasted_iota(I32, (LANES, tq), 0)
    scale = jnp.asarray(DA_QK_DIM ** -0.5, BF16)
    for h in range(DA_HEADS):
        blk = qt_ref[h * LANES:(h + 1) * LANES, :] * scale
        zero = jnp.zeros_like(blk)
        q_scr[2 * h] = jnp.where(frow < DA_QK_DIM, blk, zero)
        q_scr[2 * h + 1] = jnp.where(frow >= DA_QK_DIM, blk, zero)
    acc_scr[...] = jnp.zeros(acc_scr.shape, F32)
    m_scr[...] = jnp.full(m_scr.shape, NEG_BIG, F32)
    l_scr[...] = jnp.zeros(l_scr.shape, F32)
    kpos0 = lax.broadcasted_iota(I32, (SUB_KEYS, tq), 0)
    qpos = i * tq + lax.broadcasted_iota(I32, (SUB_KEYS, tq), 1)
    subs = range(0, tk, SUB_KEYS)

    def chunk(j, masked):
        start = pl.multiple_of(j * tk, tk)

        def scores(c):
            head = c // 2
            out = []
            for r0 in subs:
                k = k_ref[pl.ds(start + r0, SUB_KEYS), head * LANES:(head + 1) * LANES]
                s = jnp.dot(k, q_scr[c], preferred_element_type=F32)
                if masked:
                    s = jnp.where(kpos0 + (j * tk + r0) <= qpos, s, NEG_BIG)
                out.append(s)
            return out

        def values(c, r0):
            head = c // 2
            return vt_ref[j, head * LANES:(head + 1) * LANES, r0:r0 + SUB_KEYS]

        _flash_chains(n_chains, scores, values, subs, m_scr, l_scr, acc_scr, DA_QK_AHEAD)

    def full_chunk(j, carry):
        chunk(j, False)
        return carry

    lax.fori_loop(0, i, full_chunk, 0)
    chunk(i, True)

    lp = lam_ref[...]
    lam = (jnp.exp(jnp.sum(lp[0:1] * lp[1:2], axis=1, keepdims=True))
           - jnp.exp(jnp.sum(lp[2:3] * lp[3:4], axis=1, keepdims=True)) + lam_init)
    for h in range(DA_HEADS):
        o = acc_scr[2 * h] / l_scr[2 * h][0:1] - lam * (acc_scr[2 * h + 1] / l_scr[2 * h + 1][0:1])
        var = jnp.mean(o * o, axis=0, keepdims=True)
        o = o * lax.rsqrt(var + EPS) * g_ref[...] * (1.0 - lam_init)
        o_ref[:, h * LANES:(h + 1) * LANES] = o.T.astype(o_ref.dtype)


def _diff_attention(lam_p, qt, pr, vt3, subln_g, b, s_len, tq, lam_init):
    nq = s_len // tq
    n = b * s_len
    n_chains = 2 * DA_HEADS
    kern = functools.partial(_da_kernel, tq=tq, lam_init=lam_init)
    return pl.pallas_call(
        kern,
        grid=(b, nq),
        in_specs=[
            pl.BlockSpec(lam_p.shape, lambda bi, i: (0, 0)),
            pl.BlockSpec((DA_HEADS * LANES, tq), lambda bi, i: (QT_DA // DA_HEADS, bi * nq + i)),
            pl.BlockSpec((s_len, DA_HEADS * LANES), lambda bi, i: (bi, PR_DAK // DA_HEADS)),
            pl.BlockSpec((s_len // tq, DA_HEADS * LANES, tq), lambda bi, i: (bi, VT_DA // DA_HEADS, 0)),
            pl.BlockSpec(subln_g.shape, lambda bi, i: (0, 0)),
        ],
        out_specs=pl.BlockSpec((tq, DA_WIDTH), lambda bi, i: (bi * nq + i, 0)),
        out_shape=jax.ShapeDtypeStruct((n, DA_WIDTH), BF16),
        scratch_shapes=[pltpu.VMEM((n_chains, LANES, tq), BF16), pltpu.VMEM((n_chains, DA_V_DIM, tq), F32),
                        pltpu.VMEM((n_chains, SUBLANES, tq), F32), pltpu.VMEM((n_chains, SUBLANES, tq), F32)],
        compiler_params=pltpu.CompilerParams(dimension_semantics=("arbitrary",) * 2, vmem_limit_bytes=VMEM_LIMIT),
        name="diff_attn",
    )(lam_p, qt, pr, vt3, subln_g)


def _dsa_kernel(qit_ref, ki_ref, wt_ref, qt_ref, k0_ref, k1_ref, vt_ref, o_ref,
                keys_scr, half_scr, qi_scr, q_scr, acc_scr, m_scr, l_scr, cut_scr, *, tq, n_sel, n_bits):
    tk = tq
    i = pl.program_id(1)
    nck = i + 1
    kpos0 = lax.broadcasted_iota(I32, (tk, tq), 0)
    qpos = i * tq + lax.broadcasted_iota(I32, (tk, tq), 1)
    frow = lax.broadcasted_iota(I32, (LANES, tq), 0)
    per_grp = DS_HEADS // DS_KV_HEADS

    scale = jnp.asarray(DS_HEAD_DIM ** -0.5, BF16)
    for h in range(DS_HEADS):
        keep = (frow < DS_HEAD_DIM) if h % 2 == 0 else (frow >= DS_HEAD_DIM)
        rows = slice((h // 2) * LANES, (h // 2 + 1) * LANES)
        blk = qit_ref[rows, :]
        qi_scr[h] = jnp.where(keep, blk, jnp.zeros_like(blk))
        blk = qt_ref[rows, :] * scale
        q_scr[h] = jnp.where(keep, blk, jnp.zeros_like(blk))

    kpos_sub = lax.broadcasted_iota(I32, (SUB_KEYS, tq), 0)
    qpos_sub = i * tq + lax.broadcasted_iota(I32, (SUB_KEYS, tq), 1)

    def score_chunk(c, carry):
        start = pl.multiple_of(c * tk, tk)
        for r0 in range(0, tk, SUB_KEYS):
            kc = ki_ref[pl.ds(start + r0, SUB_KEYS), :]
            acc = jnp.zeros((SUB_KEYS, tq), F32)
            for h in range(IDX_HEADS):
                s = jnp.dot(kc, qi_scr[h], preferred_element_type=F32)
                acc = acc + wt_ref[h:h + 1, :] * jnp.maximum(s, 0.0)
            bits = pltpu.bitcast(acc, I32)
            key = bits ^ ((bits >> 31) & 0x7FFFFFFF)
            key = jnp.where(acc == 0.0, 0, key)
            key = jnp.where(kpos_sub + (c * tk + r0) <= qpos_sub, key, INT_MIN)
            keys_scr[c, r0:r0 + SUB_KEYS, :] = key
            half_scr[c, r0:r0 + SUB_KEYS, :] = (key >> 16).astype(I16)
        return carry

    lax.fori_loop(0, nck, score_chunk, 0)

    def count(pred):
        def body(c, acc):
            hit = pred(keys_scr[c], kpos0 + c * tk)
            return acc + jnp.sum(hit.reshape(tk // SUBLANES, SUBLANES, tq), axis=0)
        acc = lax.fori_loop(0, nck, body, jnp.zeros((SUBLANES, tq), I32))
        return jnp.sum(acc, axis=0, keepdims=True)

    def count_half(pred):
        def body(c, acc):
            hit = pred(half_scr[c])
            for r0 in range(0, tk, HALF_ROWS):
                acc = acc + hit[r0:r0 + HALF_ROWS]
            return acc
        acc = lax.fori_loop(0, nck, body, jnp.zeros((HALF_ROWS, tq), I16))
        return jnp.sum(acc.astype(I32), axis=0, keepdims=True)

    one16, zero16 = jnp.int16(1), jnp.int16(0)

    def bisect_half(need):
        def bit(b, ans_u):
            cand_u = ans_u | jnp.left_shift(jnp.int32(1), 15 - b)
            cand = (cand_u - HALF_BIAS).astype(I16)
            cnt = count_half(lambda x: jnp.where(x >= cand, one16, zero16))
            return jnp.where(cnt >= need, cand_u, ans_u)
        return lax.fori_loop(0, 16, bit, jnp.zeros((1, tq), I32)) - HALF_BIAS

    hi_t = bisect_half(n_sel)
    hi_t16 = hi_t.astype(I16)
    n_hi_gt = count_half(lambda x: jnp.where(x > hi_t16, one16, zero16))

    def low_chunk(c, carry):
        lo = ((keys_scr[c] & 0xFFFF) - HALF_BIAS).astype(I16)
        half_scr[c] = jnp.where(half_scr[c] == hi_t16, lo, jnp.int16(-HALF_BIAS))
        return carry

    lax.fori_loop(0, nck, low_chunk, 0)
    lo_t = bisect_half(n_sel - n_hi_gt)
    thr = hi_t * (2 * HALF_BIAS) + (lo_t + HALF_BIAS)
    n_ge = count(lambda k, kpos: jnp.where(k >= thr, 1, 0))
    below_all = thr == INT_MIN
    cut_scr[...] = jnp.where(below_all, -1, (1 << n_bits) - 1)
    excess = jnp.max(jnp.where(below_all, 0, n_ge - n_sel))

    @pl.when(excess > 0)
    def _():
        n_gt = count(lambda k, kpos: jnp.where(k > thr, 1, 0))
        need = n_sel - n_gt

        def tie_bit(t, cut):
            cand = cut | jnp.left_shift(jnp.int32(1), n_bits - 1 - t)
            cnt = count(lambda k, kpos: jnp.where(k == thr, jnp.where(kpos < cand, 1, 0), 0))
            return jnp.where(cnt < need, cand, cut)

        cut = lax.fori_loop(0, n_bits, tie_bit, jnp.zeros((1, tq), I32))
        cut_scr[...] = jnp.where(below_all, -1, cut)

    cut = cut_scr[...]

    def bias_chunk(c, carry):
        k = keys_scr[c]
        sel = (k > thr) | ((k == thr) & (kpos0 + c * tk <= cut))
        keys_scr[c] = pltpu.bitcast(jnp.where(sel, 0.0, NEG_BIG).astype(F32), I32)
        return carry

    lax.fori_loop(0, nck, bias_chunk, 0)

    acc_scr[...] = jnp.zeros(acc_scr.shape, F32)
    m_scr[...] = jnp.full(m_scr.shape, NEG_BIG, F32)
    l_scr[...] = jnp.zeros(l_scr.shape, F32)

    def att_chunk(c, carry):
        start = pl.multiple_of(c * tk, tk)
        subs = range(0, tk, SUB_KEYS)

        def head_scores(h):
            k_ref = k0_ref if h // per_grp == 0 else k1_ref
            out = []
            for r0 in subs:
                s = jnp.dot(k_ref[pl.ds(start + r0, SUB_KEYS), :], q_scr[h], preferred_element_type=F32)
                out.append(s + pltpu.bitcast(keys_scr[c, r0:r0 + SUB_KEYS, :], F32))
            return out

        def head_values(h, r0):
            grp = h // per_grp
            return vt_ref[c, grp * DS_HEAD_DIM:(grp + 1) * DS_HEAD_DIM, r0:r0 + SUB_KEYS]

        _flash_chains(DS_HEADS, head_scores, head_values, subs, m_scr, l_scr, acc_scr, DS_QK_AHEAD)
        return carry

    lax.fori_loop(0, nck, att_chunk, 0)
    outs = [acc_scr[h] / l_scr[h][0:1] for h in range(DS_HEADS)]
    o_ref[...] = jnp.concatenate(outs, axis=0).T.astype(o_ref.dtype)


def _dsa_attention(qt, pr, vt3, iwt, b, s_len, tq):
    nq = s_len // tq
    n = b * s_len
    n_sel = min(TOPK_MAX, s_len // 4)
    n_bits = max(1, (s_len - 1).bit_length())
    kern = functools.partial(_dsa_kernel, tq=tq, n_sel=n_sel, n_bits=n_bits)
    seq = lambda blk: pl.BlockSpec((s_len, LANES), lambda bi, i: (bi, blk))
    qblk = lambda blk: pl.BlockSpec((4 * LANES, tq), lambda bi, i: (blk // 4, bi * nq + i))
    return pl.pallas_call(
        kern,
        grid=(b, nq),
        in_specs=[
            qblk(QT_IX), seq(PR_IXK),
            pl.BlockSpec((IDX_HEADS, tq), lambda bi, i: (0, bi * nq + i)),
            qblk(QT_DS), seq(PR_DSK), seq(PR_DSK + 1),
            pl.BlockSpec((s_len // tq, LANES, tq), lambda bi, i: (bi, VT_DS, 0)),
        ],
        out_specs=pl.BlockSpec((tq, DS_WIDTH), lambda bi, i: (bi * nq + i, 0)),
        out_shape=jax.ShapeDtypeStruct((n, DS_WIDTH), BF16),
        scratch_shapes=[pltpu.VMEM((nq, tq, tq), I32), pltpu.VMEM((nq, tq, tq), I16),
                        pltpu.VMEM((IDX_HEADS, LANES, tq), BF16), pltpu.VMEM((DS_HEADS, LANES, tq), BF16),
                        pltpu.VMEM((DS_HEADS, DS_HEAD_DIM, tq), F32),
                        pltpu.VMEM((DS_HEADS, SUBLANES, tq), F32), pltpu.VMEM((DS_HEADS, SUBLANES, tq), F32),
                        pltpu.VMEM((1, tq), I32)],
        compiler_params=pltpu.CompilerParams(dimension_semantics=("arbitrary",) * 2, vmem_limit_bytes=VMEM_LIMIT),
        name="dsa_attn",
    )(qt, pr, iwt, qt, pr, pr, vt3)


def _merge_kernel(oa_ref, ob_ref, gate_ref, x_ref, wa_ref, wb_ref, wo_ref, o_ref):
    ya = jnp.dot(oa_ref[...], wa_ref[...], preferred_element_type=F32)
    yb = jnp.dot(ob_ref[...], wb_ref[...], preferred_element_type=F32)
    mix = gate_ref[:, :D_MODEL] * ya + gate_ref[:, D_MODEL:] * yb
    o_ref[...] = x_ref[...] + jnp.dot(mix.astype(BF16), wo_ref[...], preferred_element_type=F32)


def _merge(oa, ob, gate, x2, wa, wb, wo, tm):
    n = x2.shape[0]
    full = lambda a: pl.BlockSpec(a.shape, lambda i: (0, 0))
    rows = lambda w: pl.BlockSpec((tm, w), lambda i: (i, 0))
    return pl.pallas_call(
        _merge_kernel,
        grid=(n // tm,),
        in_specs=[rows(DA_WIDTH), rows(DS_WIDTH), rows(2 * D_MODEL), rows(D_MODEL), full(wa), full(wb), full(wo)],
        out_specs=rows(D_MODEL),
        out_shape=jax.ShapeDtypeStruct((n, D_MODEL), F32),
        compiler_params=pltpu.CompilerParams(dimension_semantics=("arbitrary",), vmem_limit_bytes=VMEM_LIMIT),
        name="merge",
    )(oa, ob, gate, x2, wa, wb, wo)


def _norm_matmul_kernel(x_ref, g_ref, w_ref, o_ref):
    h = _rms(x_ref[...], g_ref[...]).astype(BF16)
    o_ref[...] = jnp.dot(h, w_ref[...], preferred_element_type=F32).astype(o_ref.dtype)


def _norm_matmul(x2, g, w, tm):
    n = x2.shape[0]
    return pl.pallas_call(
        _norm_matmul_kernel,
        grid=(n // tm,),
        in_specs=[pl.BlockSpec((tm, x2.shape[1]), lambda i: (i, 0)), pl.BlockSpec(g.shape, lambda i: (0, 0)),
                  pl.BlockSpec(w.shape, lambda i: (0, 0))],
        out_specs=pl.BlockSpec((tm, w.shape[1]), lambda i: (i, 0)),
        out_shape=jax.ShapeDtypeStruct((n, w.shape[1]), BF16),
        compiler_params=pltpu.CompilerParams(dimension_semantics=("arbitrary",), vmem_limit_bytes=VMEM_LIMIT),
        name="mem_kv_proj",
    )(x2, g, w)


def _mem_attn_kernel(x_ref, g_ref, wq_ref, kv_ref, wo_ref, o_ref):
    x = x_ref[...]
    hn = _rms(x, g_ref[...]).astype(BF16)
    q = jnp.dot(hn, wq_ref[...], preferred_element_type=F32).astype(BF16)
    scale = MEM_HEAD_DIM ** -0.5
    heads = []
    for h in range(MEM_HEADS):
        qh = q[:, h * MEM_HEAD_DIM:(h + 1) * MEM_HEAD_DIM]
        kh = kv_ref[:, h * MEM_HEAD_DIM:(h + 1) * MEM_HEAD_DIM]
        vh = kv_ref[:, MEM_WIDTH + h * MEM_HEAD_DIM:MEM_WIDTH + (h + 1) * MEM_HEAD_DIM]
        s = lax.dot_general(qh, kh, NT_DIMS, preferred_element_type=F32) * scale
        m = jnp.max(s, axis=1, keepdims=True)
        p = jnp.exp(s - m)
        l = jnp.sum(p, axis=1, keepdims=True)
        heads.append(jnp.dot(p.astype(BF16), vh, preferred_element_type=F32) / l)
    o = jnp.concatenate(heads, axis=1).astype(BF16)
    o_ref[...] = x + jnp.dot(o, wo_ref[...], preferred_element_type=F32)


def _mem_attn(x2, g, wq, kv, wo, b, s_len, m_tok, tm):
    nb = s_len // tm
    n = x2.shape[0]
    full = lambda a: pl.BlockSpec(a.shape, lambda bi, i: (0, 0))
    return pl.pallas_call(
        _mem_attn_kernel,
        grid=(b, nb),
        in_specs=[pl.BlockSpec((tm, D_MODEL), lambda bi, i: (bi * nb + i, 0)), full(g), full(wq),
                  pl.BlockSpec((m_tok, 2 * MEM_WIDTH), lambda bi, i: (bi, 0)), full(wo)],
        out_specs=pl.BlockSpec((tm, D_MODEL), lambda bi, i: (bi * nb + i, 0)),
        out_shape=jax.ShapeDtypeStruct((n, D_MODEL), F32),
        compiler_params=pltpu.CompilerParams(dimension_semantics=("arbitrary",) * 2, vmem_limit_bytes=VMEM_LIMIT),
        name="mem_attn",
    )(x2, g, wq, kv, wo)


N_SUB = 2 * PEER_HEADS
EXPERT_BITS = 14


def _route_kernel(x_ref, g_ref, wq_ref, keys_ref, hf_ref, idx_ref, row_ref, gate_ref, sc_scr, ts_scr, ti_scr):
    hf = _rms(x_ref[...], g_ref[...])
    tm = hf.shape[0]
    hb = hf.astype(BF16)
    bits = pltpu.bitcast(hb.astype(F32), jnp.uint32)
    half_d = D_MODEL // 2
    for s in range(SUBLANES // 2):
        lo = bits[:, s * LANES:(s + 1) * LANES] >> 16
        hi = bits[:, half_d + s * LANES:half_d + (s + 1) * LANES] & jnp.uint32(0xFFFF0000)
        hf_ref[:, s, :] = lo | hi
        hf_ref[:, s + SUBLANES // 2, :] = lo | hi
    q = jnp.dot(hb, wq_ref[...], preferred_element_type=F32).astype(BF16)
    for g in range(N_SUB):
        blk = q[:, (g // 2) * LANES:(g // 2 + 1) * LANES]
        sc_scr[g] = lax.dot_general(keys_ref[g], blk, NT_DIMS, preferred_element_type=F32)

    n_slot = PEER_N_KEYS // 2
    slot_id = lax.broadcasted_iota(I32, (n_slot, tm), 0)

    def sub_topk(g, carry):
        a = sc_scr[g, :n_slot, :]
        b = sc_scr[g, n_slot:, :]
        take_b = b > a
        cur = jnp.where(take_b, b, a)
        cur_i = jnp.where(take_b, slot_id + n_slot, slot_id)
        res = jnp.where(take_b, a, b)
        res_i = jnp.where(take_b, slot_id, slot_id + n_slot)
        vals, ids = [], []
        for _ in range(PEER_TOPK):
            m = jnp.max(cur, axis=0, keepdims=True)
            idx = jnp.min(jnp.where(cur == m, cur_i, PEER_N_KEYS), axis=0, keepdims=True)
            vals.append(m)
            ids.append(idx)
            hit = cur_i == idx
            cur = jnp.where(hit, res, cur)
            cur_i = jnp.where(hit, res_i, cur_i)
            res = jnp.where(hit, -jnp.inf, res)
        ts_scr[g] = jnp.concatenate(vals, axis=0)
        ti_scr[g] = jnp.concatenate(ids, axis=0)
        return carry

    lax.fori_loop(0, N_SUB, sub_topk, 0)

    iota16 = lax.broadcasted_iota(I32, (PEER_TOPK, tm), 0)
    iota8 = lax.broadcasted_iota(I32, (SUBLANES, tm), 0)
    lead_rows = SUBLANES

    def head_topk(h, carry):
        s0, s1 = ts_scr[2 * h], ts_scr[2 * h + 1]
        i0, i1 = ti_scr[2 * h], ti_scr[2 * h + 1]
        cands, codes = [], []
        for a in range(lead_rows):
            n_j = PEER_TOPK if a == 0 else SUBLANES
            cands.append(s0[a:a + 1] + s1[:n_j])
            j_iota = iota16 if a == 0 else iota8
            codes.append(((a * PEER_TOPK + j_iota) << EXPERT_BITS) | (i0[a:a + 1] * PEER_N_KEYS + i1[:n_j]))
        cands.append(s0[lead_rows:] + s1[0:1])
        codes.append((((iota8 + lead_rows) * PEER_TOPK) << EXPERT_BITS) | (i0[lead_rows:] * PEER_N_KEYS + i1[0:1]))
        split = 4
        a_v, a_c = jnp.concatenate(cands[:split], axis=0), jnp.concatenate(codes[:split], axis=0)
        b_v, b_c = jnp.concatenate(cands[split:], axis=0), jnp.concatenate(codes[split:], axis=0)
        take_b = (b_v > a_v) | ((b_v == a_v) & (b_c < a_c))
        cur, cur_c = jnp.where(take_b, b_v, a_v), jnp.where(take_b, b_c, a_c)
        res, res_c = jnp.where(take_b, a_v, b_v), jnp.where(take_b, a_c, b_c)
        code_max = jnp.int32(2 ** 30)
        vals, ids = [], []
        for _ in range(PEER_TOPK):
            m = jnp.max(cur, axis=0, keepdims=True)
            best = jnp.min(jnp.where(cur == m, cur_c, code_max), axis=0, keepdims=True)
            vals.append(m)
            ids.append(best & (2 ** EXPERT_BITS - 1))
            hit = cur_c == best
            cur = jnp.where(hit, res, cur)
            cur_c = jnp.where(hit, res_c, cur_c)
            res = jnp.where(hit, -jnp.inf, res)
        best_s = jnp.concatenate(vals, axis=0)
        e = jnp.exp(best_s - best_s[0:1])
        gate_ref[h] = e / jnp.sum(e, axis=0, keepdims=True)
        best_i = jnp.concatenate(ids, axis=0)
        idx_ref[h] = best_i
        row_ref[h] = (best_i >> 1) * SUBLANES
        return carry

    lax.fori_loop(0, PEER_HEADS, head_topk, 0)


def _route(x2, g, wq, keys_p, tm):
    n = x2.shape[0]
    return pl.pallas_call(
        _route_kernel,
        grid=(n // tm,),
        in_specs=[pl.BlockSpec((tm, D_MODEL), lambda i: (i, 0)), pl.BlockSpec(g.shape, lambda i: (0, 0)),
                  pl.BlockSpec(wq.shape, lambda i: (0, 0)), pl.BlockSpec(keys_p.shape, lambda i: (0, 0, 0))],
        out_specs=[pl.BlockSpec((tm, SUBLANES, LANES), lambda i: (i, 0, 0))]
        + [pl.BlockSpec((PEER_HEADS, PEER_TOPK, tm), lambda i: (0, 0, i))] * 3,
        out_shape=[jax.ShapeDtypeStruct((n, SUBLANES, LANES), jnp.uint32),
                   jax.ShapeDtypeStruct((PEER_HEADS, PEER_TOPK, n), I32),
                   jax.ShapeDtypeStruct((PEER_HEADS, PEER_TOPK, n), I32),
                   jax.ShapeDtypeStruct((PEER_HEADS, PEER_TOPK, n), F32)],
        scratch_shapes=[pltpu.VMEM((N_SUB, PEER_N_KEYS, tm), F32), pltpu.VMEM((N_SUB, PEER_TOPK, tm), F32),
                        pltpu.VMEM((N_SUB, PEER_TOPK, tm), I32)],
        compiler_params=pltpu.CompilerParams(dimension_semantics=("arbitrary",), vmem_limit_bytes=VMEM_LIMIT),
        name="peer_route",
    )(x2, g, wq, keys_p)


TILE_ROWS = 2 * SUBLANES
PAIRS_PER_KTILE = 16
TOK_UNROLL = 16
SUM_UNROLL = 16


def _pack_table(t):
    e = t.shape[0]
    bits = lax.bitcast_convert_type(t.astype(BF16), jnp.uint16).astype(jnp.uint32)
    words = bits[:, :D_MODEL // 2] | (bits[:, D_MODEL // 2:] << 16)
    return words.reshape(e * (SUBLANES // 2), LANES)


def _gather_tiles(off_ref, tab_ref, u):
    tiles = []
    for k in range(PEER_SEL):
        start = pl.multiple_of(off_ref[u * PEER_SEL + k], SUBLANES)
        tiles.append(pltpu.bitcast(tab_ref[pl.ds(start, SUBLANES), :], BF16))
    return tiles


def _for_trips(rows_hbm, bufs, sems, tok0, unroll, n_trip, body):
    assert n_trip % 2 == 0
    words = unroll * PEER_SEL

    def fetch(trip, par):
        src = rows_hbm.at[pl.ds(pl.multiple_of((tok0 + trip * unroll) * PEER_SEL, words), words)]
        return pltpu.make_async_copy(src, bufs[par], sems.at[par])

    fetch(0, 0).start()
    fetch(1, 1).start()

    def pair(p, carry):
        for par in range(2):
            trip = 2 * p + par
            fetch(trip, par).wait()
            body(trip, bufs[par], par)
            fetch(jnp.minimum(trip + 2, n_trip - 2 + par), par).start()
        return carry

    lax.fori_loop(0, n_trip // 2, pair, 0)
    for par in range(2):
        fetch(n_trip - 2 + par, par).wait()


def _peer_dot_kernel(rows_hbm, tab_ref, h_ref, gate_ref, idxv_ref, sel_ref, ce_ref, co_ref,
                     ye_scr, yo_scr, ze_scr, zo_scr, off_a, off_b, sems, *, tt):
    ones = jnp.ones((SUBLANES, LANES), BF16)
    half_rows = PAIRS_PER_KTILE
    n_trip = tt // TOK_UNROLL

    def row_sums(i, slot, off_ref):
        for u in range(TOK_UNROLL):
            t = i * TOK_UNROLL + u
            hp = pltpu.bitcast(h_ref[t], BF16)
            prods = [tile * hp for tile in _gather_tiles(off_ref, tab_ref, u)]
            for kt in range(PEER_SEL // PAIRS_PER_KTILE):
                stack = jnp.concatenate(prods[kt * PAIRS_PER_KTILE:(kt + 1) * PAIRS_PER_KTILE], axis=0)
                y = jnp.dot(sel_ref[...], stack, preferred_element_type=F32)
                r0 = u * PEER_SEL + kt * half_rows
                ye_scr[slot, r0:r0 + half_rows, :] = y[:half_rows].astype(BF16)
                yo_scr[slot, r0:r0 + half_rows, :] = y[half_rows:].astype(BF16)

    def lane_sums(i, slot):
        ze = lax.dot_general(ones, ye_scr[slot], NT_DIMS, preferred_element_type=F32)
        zo = lax.dot_general(ones, yo_scr[slot], NT_DIMS, preferred_element_type=F32)
        for u in range(TOK_UNROLL):
            t = i * TOK_UNROLL + u
            ze_scr[pl.ds(t, 1), :] = ze[0:1, u * PEER_SEL:(u + 1) * PEER_SEL]
            zo_scr[pl.ds(t, 1), :] = zo[0:1, u * PEER_SEL:(u + 1) * PEER_SEL]

    def trip(i, off_ref, par):
        lane_sums(jnp.maximum(i - 1, 0), 1 - par)
        row_sums(i, par, off_ref)

    ye_scr[1] = jnp.zeros(ye_scr.shape[1:], ye_scr.dtype)
    yo_scr[1] = jnp.zeros(yo_scr.shape[1:], yo_scr.dtype)
    _for_trips(rows_hbm, (off_a, off_b), sems, pl.program_id(0) * tt, TOK_UNROLL, n_trip, trip)
    lane_sums(n_trip - 1, (n_trip - 1) & 1)
    even = (idxv_ref[...] & 1) == 0
    a = jnp.where(even, ze_scr[...], zo_scr[...])
    c = 0.5 * a * (1.0 + lax.erf(a * (2.0 ** -0.5))) * gate_ref[...]
    ce_ref[...] = jnp.where(even, c, 0.0)
    co_ref[...] = jnp.where(even, 0.0, c)


def _peer_dot(rows_flat, tab, hpk, gate, idx2, sel, tt):
    n = hpk.shape[0]
    kern = functools.partial(_peer_dot_kernel, tt=tt)
    rows = pl.BlockSpec((tt, PEER_SEL), lambda i: (i, 0))
    return pl.pallas_call(
        kern,
        grid=(n // tt,),
        in_specs=[pl.BlockSpec(memory_space=pl.ANY),
                  pl.BlockSpec(memory_space=pltpu.VMEM),
                  pl.BlockSpec((tt, SUBLANES, LANES), lambda i: (i, 0, 0)),
                  rows, rows, pl.BlockSpec(sel.shape, lambda i: (0, 0))],
        out_specs=[rows, rows],
        out_shape=[jax.ShapeDtypeStruct((n, PEER_SEL), F32), jax.ShapeDtypeStruct((n, PEER_SEL), F32)],
        scratch_shapes=[pltpu.VMEM((2, TOK_UNROLL * PEER_SEL, LANES), BF16)] * 2
        + [pltpu.VMEM((tt, PEER_SEL), F32)] * 2
        + [pltpu.SMEM((TOK_UNROLL * PEER_SEL,), I32)] * 2 + [pltpu.SemaphoreType.DMA((2,))],
        compiler_params=pltpu.CompilerParams(dimension_semantics=("arbitrary",), vmem_limit_bytes=VMEM_LIMIT),
        name="peer_dot",
    )(rows_flat, tab, hpk, gate, idx2, sel)


def _peer_sum_kernel(rows_hbm, ce_ref, co_ref, spread_ref, tab_ref, o_ref, m1_scr, m2_scr, off_a, off_b, sems,
                     *, tt):
    width = PEER_SEL * TILE_ROWS
    cc = jnp.concatenate([ce_ref[...], co_ref[...]], axis=1)
    c1 = cc.astype(BF16)
    c2 = (cc - c1.astype(F32)).astype(BF16)
    m1_scr[...] = jnp.dot(c1, spread_ref[...], preferred_element_type=F32)
    m2_scr[...] = jnp.dot(c2, spread_ref[...], preferred_element_type=F32)
    row = lax.broadcasted_iota(I32, (SUBLANES, width), 0)
    lane = lax.broadcasted_iota(I32, (SUBLANES, width), 1)
    half = SUBLANES // 2
    on_row = (lane & (SUBLANES - 1)) == 2 * (row % half) + row // half

    def token(t, off_ref, u):
        w = jnp.concatenate(_gather_tiles(off_ref, tab_ref, u), axis=0)
        lhs = []
        for m_scr in (m1_scr, m2_scr):
            coef = jnp.broadcast_to(m_scr[pl.ds(t, 1), :], (SUBLANES, width))
            lhs.append(jnp.where(on_row, coef, 0.0).astype(BF16))
        res = jnp.dot(jnp.concatenate(lhs, axis=0), w, preferred_element_type=F32)
        o_ref[t] = res[:SUBLANES] + res[SUBLANES:]

    def trip(i, off_ref, par):
        for u in range(SUM_UNROLL):
            token(i * SUM_UNROLL + u, off_ref, u)

    _for_trips(rows_hbm, (off_a, off_b), sems, pl.program_id(0) * tt, SUM_UNROLL, tt // SUM_UNROLL, trip)


def _peer_sum(rows_flat, ce, co, spread, tab, tt):
    n = ce.shape[0]
    kern = functools.partial(_peer_sum_kernel, tt=tt)
    rows = pl.BlockSpec((tt, PEER_SEL), lambda i: (i, 0))
    width = PEER_SEL * TILE_ROWS
    return pl.pallas_call(
        kern,
        grid=(n // tt,),
        in_specs=[pl.BlockSpec(memory_space=pl.ANY), rows, rows,
                  pl.BlockSpec(spread.shape, lambda i: (0, 0)), pl.BlockSpec(memory_space=pltpu.VMEM)],
        out_specs=pl.BlockSpec((tt, SUBLANES, LANES), lambda i: (i, 0, 0)),
        out_shape=jax.ShapeDtypeStruct((n, SUBLANES, LANES), F32),
        scratch_shapes=[pltpu.VMEM((tt, width), F32), pltpu.VMEM((tt, width), F32)]
        + [pltpu.SMEM((SUM_UNROLL * PEER_SEL,), I32)] * 2 + [pltpu.SemaphoreType.DMA((2,))],
        compiler_params=pltpu.CompilerParams(dimension_semantics=("arbitrary",), vmem_limit_bytes=VMEM_LIMIT),
        name="peer_sum",
    )(rows_flat, ce, co, spread, tab)


def _final_kernel(x_ref, p_ref, g_ref, o_ref):
    peer = jnp.concatenate([p_ref[:, s, :] for s in range(SUBLANES)], axis=1)
    o_ref[...] = _rms(x_ref[...] + peer, g_ref[...])


def _final(x2, peer3, g, tm):
    n = x2.shape[0]
    return pl.pallas_call(
        _final_kernel,
        grid=(n // tm,),
        in_specs=[pl.BlockSpec((tm, D_MODEL), lambda i: (i, 0)),
                  pl.BlockSpec((tm, SUBLANES, LANES), lambda i: (i, 0, 0)),
                  pl.BlockSpec(g.shape, lambda i: (0, 0))],
        out_specs=pl.BlockSpec((tm, D_MODEL), lambda i: (i, 0)),
        out_shape=jax.ShapeDtypeStruct((n, D_MODEL), F32),
        compiler_params=pltpu.CompilerParams(dimension_semantics=("arbitrary",), vmem_limit_bytes=VMEM_LIMIT),
        name="final_norm",
    )(x2, peer3, g)


ROW_BLOCK = 256
PEER_BLOCK = 256


def _tile(n, pref):
    t = pref
    while n % t:
        t //= 2
    return t


def _layer(l, x2, mem2, pos, invf, b, s_len, m_tok, norm_mix_g, w_in, da_lambda, da_subln_g, w_branch_a,
           w_branch_b, gate_bias, w_out, norm_mem_g, mem_kv_norm_g, w_mem_q, w_mem_kv, w_mem_o, norm_ffn_g,
           peer_w_q, peer_sub_keys, peer_u, peer_v):
    n = b * s_len
    row2 = lambda v: v.reshape(1, -1)
    splits = [0]
    for c in COL_SIZES:
        splits.append(splits[-1] + c)
    da_q, da_k, da_v, ds_q, ds_k, ds_v, ix_q, ix_k, ix_w, gates = (
        w_in[l][:, splits[j]:splits[j + 1]] for j in range(len(COL_SIZES)))
    k0, k1 = ds_k[:, :DS_HEAD_DIM], ds_k[:, DS_HEAD_DIM:]
    w_keys = jnp.concatenate([da_k, k0, k0, k1, k1, ix_k, ix_k], axis=1).astype(BF16)
    w_q_t = jnp.concatenate([da_q, ds_q, ix_q], axis=1).T.astype(BF16)
    w_v_t = jnp.concatenate([da_v, ds_v], axis=1).T.astype(BF16)
    t_att = _tile(s_len, ROW_BLOCK)
    t_proj = t_att
    pr, gate, qt, vt3, iwt = _in_proj(x2, row2(norm_mix_g[l]), pos, invf, pos.reshape(1, n),
                                      invf[:, :ROPE_HALF].reshape(ROPE_HALF, 1), row2(gate_bias[l]), w_keys,
                                      gates.astype(BF16), w_q_t, w_v_t, ix_w.T.astype(BF16), t_proj, t_att)

    lam_init = 0.8 - 0.6 * math.exp(-0.3 * l)
    o_a = _diff_attention(da_lambda[l], qt, pr, vt3, da_subln_g[l].reshape(-1, 1), b, s_len, t_att, lam_init)
    o_b = _dsa_attention(qt, pr, vt3, iwt, b, s_len, t_att)
    x2 = _merge(o_a, o_b, gate, x2, w_branch_a[l].astype(BF16), w_branch_b[l].astype(BF16),
                w_out[l].astype(BF16), _tile(n, ROW_BLOCK))

    kv = _norm_matmul(mem2, row2(mem_kv_norm_g[l]), w_mem_kv[l].astype(BF16), _tile(mem2.shape[0], ROW_BLOCK))
    x2 = _mem_attn(x2, row2(norm_mem_g[l]), w_mem_q[l].astype(BF16), kv, w_mem_o[l].astype(BF16),
                   b, s_len, m_tok, _tile(s_len, ROW_BLOCK))

    sk = peer_sub_keys[l].reshape(N_SUB, PEER_N_KEYS, PEER_HALF)
    z = jnp.zeros_like(sk)
    keys_p = jnp.where((jnp.arange(N_SUB) % 2 == 0)[:, None, None],
                       jnp.concatenate([sk, z], axis=2), jnp.concatenate([z, sk], axis=2)).astype(BF16)
    hpk, idx_t, row_t, gate_t = _route(x2, row2(norm_ffn_g[l]), peer_w_q[l].astype(BF16), keys_p, _tile(n, ROW_BLOCK))
    idx2 = idx_t.reshape(PEER_SEL, n).T
    rows2 = row_t.reshape(PEER_SEL, n).T
    gate2 = gate_t.reshape(PEER_SEL, n).T
    tt = _tile(n, PEER_BLOCK)
    r32 = jnp.arange(2 * PAIRS_PER_KTILE)[:, None]
    c256 = jnp.arange(PAIRS_PER_KTILE * TILE_ROWS)[None, :]
    sel = ((c256 // TILE_ROWS == r32 % PAIRS_PER_KTILE)
           & ((c256 % TILE_ROWS) // SUBLANES == r32 // PAIRS_PER_KTILE)).astype(BF16)
    k256 = jnp.arange(2 * PEER_SEL)[:, None]
    c2048 = jnp.arange(PEER_SEL * TILE_ROWS)[None, :]
    spread = ((c2048 // TILE_ROWS == k256 % PEER_SEL)
              & ((c2048 % TILE_ROWS) // SUBLANES == k256 // PEER_SEL)).astype(BF16)
    rows_flat = rows2.reshape(-1)
    ce2, co2 = _peer_dot(rows_flat, _pack_table(peer_u[l]), hpk, gate2, idx2, sel, tt)
    peer3 = _peer_sum(rows_flat, ce2, co2, spread, _pack_table(peer_v[l]), tt)
    return x2, peer3


def kernel(x, mem, positions, norm_mix_g, w_in, da_lambda, da_subln_g, w_branch_a, w_branch_b, gate_bias, w_out, norm_mem_g, mem_kv_norm_g, w_mem_q, w_mem_kv, w_mem_o, norm_ffn_g, peer_w_q, peer_sub_keys, peer_u, peer_v, final_norm_g):
    b, s_len, d = x.shape
    m_tok = mem.shape[1]
    n = b * s_len
    depth = w_in.shape[0]
    x2 = x.reshape(n, d)
    mem2 = mem.reshape(b * m_tok, d)
    pos = positions.astype(F32).reshape(n, 1)
    inv_freq = ROPE_THETA ** (-(jnp.arange(ROPE_HALF, dtype=F32) * 2.0) / ROPE_DIM)
    invf = jnp.tile(inv_freq, LANES // ROPE_HALF).reshape(1, LANES)
    peer3 = None
    for l in range(depth):
        if peer3 is not None:
            x2 = x2 + peer3.reshape(n, d)
        x2, peer3 = _layer(l, x2, mem2, pos, invf, b, s_len, m_tok, norm_mix_g, w_in, da_lambda, da_subln_g,
                           w_branch_a, w_branch_b, gate_bias, w_out, norm_mem_g, mem_kv_norm_g, w_mem_q, w_mem_kv,
                           w_mem_o, norm_ffn_g, peer_w_q, peer_sub_keys, peer_u, peer_v)
    out = _final(x2, peer3, final_norm_g.reshape(1, d), _tile(n, ROW_BLOCK))
    return out.reshape(b, s_len, d)
```

```python
import functools
import math

import jax
import jax.numpy as jnp
from jax import lax
from jax.experimental import pallas as pl
from jax.experimental.pallas import tpu as pltpu

F32 = jnp.float32
BF16 = jnp.bfloat16
I32 = jnp.int32
I16 = jnp.int16

D_MODEL = 1024
EPS = 1e-6
ROPE_THETA = 500000.0
ROT_HEAD_DIM = 64
ROPE_DIM = 16
ROPE_HALF = ROPE_DIM // 2

DA_HEADS = 4
DA_QK_DIM = 64
DA_V_DIM = 128
DA_WIDTH = 512
DS_HEADS = 8
DS_KV_HEADS = 2
DS_HEAD_DIM = 64
DS_WIDTH = 512
IDX_HEADS = 8
TOPK_MAX = 256
MEM_HEADS = 4
MEM_HEAD_DIM = 128
MEM_WIDTH = 512
PEER_HEADS = 8
PEER_N_KEYS = 128
PEER_HALF = 64
PEER_TOPK = 16
PEER_SEL = PEER_HEADS * PEER_TOPK

COL_SIZES = (512, 512, 512, 512, 128, 128, 512, 64, 8, 2048)

LANES = 128
SUBLANES = 8
NEG_BIG = -1e30
INT_MIN = -(2 ** 31)
HALF_BIAS = 2 ** 15
HALF_ROWS = 2 * SUBLANES
VMEM_LIMIT = 56 * 1024 * 1024

NT_DIMS = (((1,), (1,)), ((), ()))
PROJ_COLS = 512


def _rms(x, g):
    var = jnp.mean(x * x, axis=-1, keepdims=True)
    return x * lax.rsqrt(var + EPS) * g


def _rope_rows(y, cos8, sin8):
    pieces = []
    for r0 in range(0, y.shape[0], ROT_HEAD_DIM):
        t1 = y[r0:r0 + ROPE_HALF]
        t2 = y[r0 + ROPE_HALF:r0 + ROPE_DIM]
        pieces += [t1 * cos8 - t2 * sin8, t2 * cos8 + t1 * sin8, y[r0 + ROPE_DIM:r0 + ROT_HEAD_DIM]]
    return jnp.concatenate(pieces, axis=0)


def _in_proj_kernel(x_ref, g_ref, pos_ref, invf_ref, pos_t_ref, invf8_ref, bias_ref, wr_ref, wg_ref,
                    wqt_ref, wvt_ref, wwt_ref, pr_ref, gate_ref, qt_ref, vt_ref, iwt_ref):
    h = _rms(x_ref[...], g_ref[...]).astype(BF16)
    tm = h.shape[0]
    ang_t = invf8_ref[...] * pos_t_ref[...]
    cos8 = jnp.cos(ang_t)
    sin8 = jnp.sin(ang_t)
    row_chunk = 2 * LANES
    for r0 in range(0, wqt_ref.shape[0], row_chunk):
        y = lax.dot_general(wqt_ref[r0:r0 + row_chunk, :], h, NT_DIMS, preferred_element_type=F32)
        qt_ref[r0:r0 + row_chunk, :] = _rope_rows(y, cos8, sin8).astype(qt_ref.dtype)
    for r0 in range(0, wvt_ref.shape[0], row_chunk):
        r1 = min(r0 + row_chunk, wvt_ref.shape[0])
        y = lax.dot_general(wvt_ref[r0:r1, :], h, NT_DIMS, preferred_element_type=F32)
        chunk = vt_ref.shape[2]
        for j in range(vt_ref.shape[0]):
            vt_ref[j, r0:r1, :] = y[:, j * chunk:(j + 1) * chunk].astype(vt_ref.dtype)
    iwt_ref[...] = lax.dot_general(wwt_ref[...], h, NT_DIMS, preferred_element_type=F32)
    ang = pos_ref[...] * invf_ref[...]
    cos = jnp.cos(ang)
    sin = jnp.sin(ang)
    lane = lax.broadcasted_iota(I32, (tm, LANES), 1) % ROT_HEAD_DIM
    c_t = jnp.where(lane < ROPE_DIM, cos, 1.0)
    s_lo = jnp.where(lane < ROPE_HALF, -sin, 0.0)
    s_hi = jnp.where((lane >= ROPE_HALF) & (lane < ROPE_DIM), sin, 0.0)
    n_r = wr_ref.shape[1]
    for c0 in range(0, n_r, PROJ_COLS):
        c1 = min(c0 + PROJ_COLS, n_r)
        w = c1 - c0
        y = jnp.dot(h, wr_ref[:, c0:c1], preferred_element_type=F32)
        reps = w // LANES
        ct = jnp.concatenate([c_t] * reps, axis=1)
        sl = jnp.concatenate([s_lo] * reps, axis=1)
        sh = jnp.concatenate([s_hi] * reps, axis=1)
        y = y * ct + pltpu.roll(y, w - ROPE_HALF, 1) * sl + pltpu.roll(y, ROPE_HALF, 1) * sh
        pr_ref[:, c0:c1] = y.astype(pr_ref.dtype)
    n_g = wg_ref.shape[1]
    for c0 in range(0, n_g, PROJ_COLS):
        y = jnp.dot(h, wg_ref[:, c0:c0 + PROJ_COLS], preferred_element_type=F32)
        gate_ref[:, c0:c0 + PROJ_COLS] = jax.nn.sigmoid(y + bias_ref[:, c0:c0 + PROJ_COLS])


def _in_proj(x2, g, pos, invf, pos_t, invf8, bias, wr, wg, wqt, wvt, wwt, tm, chunk):
    n = x2.shape[0]
    per_step = tm // chunk
    full = lambda a: pl.BlockSpec(a.shape, lambda i: (0, 0))
    rows = lambda w: pl.BlockSpec((tm, w), lambda i: (i, 0))
    cols = lambda r: pl.BlockSpec((r, tm), lambda i: (0, i))
    return pl.pallas_call(
        _in_proj_kernel,
        grid=(n // tm,),
        in_specs=[rows(D_MODEL), full(g), rows(1), full(invf), cols(1), full(invf8), full(bias), full(wr), full(wg),
                  full(wqt), full(wvt), full(wwt)],
        out_specs=[rows(wr.shape[1]), rows(wg.shape[1]), cols(wqt.shape[0]),
                   pl.BlockSpec((per_step, wvt.shape[0], chunk), lambda i: (i, 0, 0)), cols(wwt.shape[0])],
        out_shape=[jax.ShapeDtypeStruct((n, wr.shape[1]), BF16), jax.ShapeDtypeStruct((n, wg.shape[1]), F32),
                   jax.ShapeDtypeStruct((wqt.shape[0], n), BF16),
                   jax.ShapeDtypeStruct((n // chunk, wvt.shape[0], chunk), BF16),
                   jax.ShapeDtypeStruct((wwt.shape[0], n), F32)],
        compiler_params=pltpu.CompilerParams(dimension_semantics=("arbitrary",), vmem_limit_bytes=VMEM_LIMIT),
        name="in_proj",
    )(x2, g, pos, invf, pos_t, invf8, bias, wr, wg, wqt, wvt, wwt)


QT_DA = 0
QT_DS = 4
QT_IX = 8
VT_DA = 0
VT_DS = 4
PR_DAK = 0
PR_DSK = 4
PR_IXK = 6


DA_SUB = 128
DS_SUB = 64
DA_QK_AHEAD = 1
DS_QK_AHEAD = 3


def _zero_after(x):
    bits = pltpu.bitcast(x, jnp.uint32)
    return pltpu.bitcast((bits >> 16) >> 16, F32)


def _flash_chains(n_chains, score_fn, value_fn, subs, m_scr, l_scr, acc_scr, ahead):
    pending = [score_fn(c) for c in range(ahead)]
    for c in range(n_chains):
        scores = pending.pop(0)
        m = m_scr[c][0:1]
        if c + ahead < n_chains:
            pending.append(score_fn(c + ahead))
            m = m + _zero_after(pending[-1][0][0:1])
        m_new = m
        for s in scores:
            m_new = jnp.maximum(m_new, jnp.max(s, axis=0, keepdims=True))
        alpha = jnp.exp(m - m_new)
        l = alpha * l_scr[c][0:1]
        a = alpha * acc_scr[c]
        for r0, s in zip(subs, scores):
            p = jnp.exp(s - m_new)
            l = l + jnp.sum(p, axis=0, keepdims=True)
            v = value_fn(c, r0)
            a = a + jnp.dot(v, p.astype(v.dtype), preferred_element_type=F32)
        m_scr[c] = jnp.broadcast_to(m_new, m_scr.shape[1:])
        l_scr[c] = jnp.broadcast_to(l, l_scr.shape[1:])
        acc_scr[c] = a


def _da_kernel(lam_ref, qt_ref, k_ref, vt_ref, g_ref, o_ref, q_scr, acc_scr, m_scr, l_scr, *, tq, lam_init):
    tk = tq
    sub_keys = DA_SUB
    i = pl.program_id(1)
    n_chains = 2 * DA_HEADS
    frow = lax.broadcasted_iota(I32, (LANES, tq), 0)
    scale = jnp.asarray(DA_QK_DIM ** -0.5, BF16)
    for h in range(DA_HEADS):
        blk = qt_ref[h * LANES:(h + 1) * LANES, :] * scale
        zero = jnp.zeros_like(blk)
        q_scr[2 * h] = jnp.where(frow < DA_QK_DIM, blk, zero)
        q_scr[2 * h + 1] = jnp.where(frow >= DA_QK_DIM, blk, zero)
    acc_scr[...] = jnp.zeros(acc_scr.shape, F32)
    m_scr[...] = jnp.full(m_scr.shape, NEG_BIG, F32)
    l_scr[...] = jnp.zeros(l_scr.shape, F32)
    kpos0 = lax.broadcasted_iota(I32, (sub_keys, tq), 0)
    qpos = i * tq + lax.broadcasted_iota(I32, (sub_keys, tq), 1)
    subs = range(0, tk, sub_keys)

    def chunk(j, masked):
        start = pl.multiple_of(j * tk, tk)

        def scores(c):
            head = c // 2
            out = []
            for r0 in subs:
                k = k_ref[pl.ds(start + r0, sub_keys), head * LANES:(head + 1) * LANES]
                s = jnp.dot(k, q_scr[c], preferred_element_type=F32)
                if masked:
                    s = jnp.where(kpos0 + (j * tk + r0) <= qpos, s, NEG_BIG)
                out.append(s)
            return out

        def values(c, r0):
            head = c // 2
            return vt_ref[j, head * LANES:(head + 1) * LANES, r0:r0 + sub_keys]

        _flash_chains(n_chains, scores, values, subs, m_scr, l_scr, acc_scr, DA_QK_AHEAD)

    def full_chunk(j, carry):
        chunk(j, False)
        return carry

    lax.fori_loop(0, i, full_chunk, 0)
    chunk(i, True)

    lp = lam_ref[...]
    lam = (jnp.exp(jnp.sum(lp[0:1] * lp[1:2], axis=1, keepdims=True))
           - jnp.exp(jnp.sum(lp[2:3] * lp[3:4], axis=1, keepdims=True)) + lam_init)
    for h in range(DA_HEADS):
        o = acc_scr[2 * h] / l_scr[2 * h][0:1] - lam * (acc_scr[2 * h + 1] / l_scr[2 * h + 1][0:1])
        var = jnp.mean(o * o, axis=0, keepdims=True)
        o = o * lax.rsqrt(var + EPS) * g_ref[...] * (1.0 - lam_init)
        o_ref[:, h * LANES:(h + 1) * LANES] = o.T.astype(o_ref.dtype)


def _diff_attention(lam_p, qt, pr, vt3, subln_g, b, s_len, tq, lam_init):
    nq = s_len // tq
    n = b * s_len
    n_chains = 2 * DA_HEADS
    kern = functools.partial(_da_kernel, tq=tq, lam_init=lam_init)
    return pl.pallas_call(
        kern,
        grid=(b, nq),
        in_specs=[
            pl.BlockSpec(lam_p.shape, lambda bi, i: (0, 0)),
            pl.BlockSpec((DA_HEADS * LANES, tq), lambda bi, i: (QT_DA // DA_HEADS, bi * nq + i)),
            pl.BlockSpec((s_len, DA_HEADS * LANES), lambda bi, i: (bi, PR_DAK // DA_HEADS)),
            pl.BlockSpec((s_len // tq, DA_HEADS * LANES, tq), lambda bi, i: (bi, VT_DA // DA_HEADS, 0)),
            pl.BlockSpec(subln_g.shape, lambda bi, i: (0, 0)),
        ],
        out_specs=pl.BlockSpec((tq, DA_WIDTH), lambda bi, i: (bi * nq + i, 0)),
        out_shape=jax.ShapeDtypeStruct((n, DA_WIDTH), BF16),
        scratch_shapes=[pltpu.VMEM((n_chains, LANES, tq), BF16), pltpu.VMEM((n_chains, DA_V_DIM, tq), F32),
                        pltpu.VMEM((n_chains, SUBLANES, tq), F32), pltpu.VMEM((n_chains, SUBLANES, tq), F32)],
        compiler_params=pltpu.CompilerParams(dimension_semantics=("arbitrary",) * 2, vmem_limit_bytes=VMEM_LIMIT),
        name="diff_attn",
    )(lam_p, qt, pr, vt3, subln_g)


def _dsa_kernel(qit_ref, ki_ref, wt_ref, qt_ref, k0_ref, k1_ref, vt_ref, o_ref,
                keys_scr, half_scr, qi_scr, q_scr, acc_scr, m_scr, l_scr, cut_scr, *, tq, n_sel, n_bits):
    tk = tq
    sub_keys = DS_SUB
    i = pl.program_id(1)
    nck = i + 1
    kpos0 = lax.broadcasted_iota(I32, (tk, tq), 0)
    qpos = i * tq + lax.broadcasted_iota(I32, (tk, tq), 1)
    frow = lax.broadcasted_iota(I32, (LANES, tq), 0)
    per_grp = DS_HEADS // DS_KV_HEADS

    scale = jnp.asarray(DS_HEAD_DIM ** -0.5, BF16)
    for h in range(DS_HEADS):
        keep = (frow < DS_HEAD_DIM) if h % 2 == 0 else (frow >= DS_HEAD_DIM)
        rows = slice((h // 2) * LANES, (h // 2 + 1) * LANES)
        blk = qit_ref[rows, :]
        qi_scr[h] = jnp.where(keep, blk, jnp.zeros_like(blk))
        blk = qt_ref[rows, :] * scale
        q_scr[h] = jnp.where(keep, blk, jnp.zeros_like(blk))

    kpos_sub = lax.broadcasted_iota(I32, (sub_keys, tq), 0)
    qpos_sub = i * tq + lax.broadcasted_iota(I32, (sub_keys, tq), 1)

    def score_chunk(c, carry):
        start = pl.multiple_of(c * tk, tk)
        for r0 in range(0, tk, sub_keys):
            kc = ki_ref[pl.ds(start + r0, sub_keys), :]
            acc = jnp.zeros((sub_keys, tq), F32)
            for h in range(IDX_HEADS):
                s = jnp.dot(kc, qi_scr[h], preferred_element_type=F32)
                acc = acc + wt_ref[h:h + 1, :] * jnp.maximum(s, 0.0)
            bits = pltpu.bitcast(acc, I32)
            key = bits ^ ((bits >> 31) & 0x7FFFFFFF)
            key = jnp.where(acc == 0.0, 0, key)
            key = jnp.where(kpos_sub + (c * tk + r0) <= qpos_sub, key, INT_MIN)
            keys_scr[c, r0:r0 + sub_keys, :] = key
            half_scr[c, r0:r0 + sub_keys, :] = (key >> 16).astype(I16)
        return carry

    lax.fori_loop(0, nck, score_chunk, 0)

    def count(pred):
        def body(c, acc):
            hit = pred(keys_scr[c], kpos0 + c * tk)
            return acc + jnp.sum(hit.reshape(tk // SUBLANES, SUBLANES, tq), axis=0)
        acc = lax.fori_loop(0, nck, body, jnp.zeros((SUBLANES, tq), I32))
        return jnp.sum(acc, axis=0, keepdims=True)

    def count_half(pred):
        def body(c, acc):
            hit = pred(half_scr[c])
            for r0 in range(0, tk, HALF_ROWS):
                acc = acc + hit[r0:r0 + HALF_ROWS]
            return acc
        acc = lax.fori_loop(0, nck, body, jnp.zeros((HALF_ROWS, tq), I16))
        return jnp.sum(acc.astype(I32), axis=0, keepdims=True)

    one16, zero16 = jnp.int16(1), jnp.int16(0)

    def bisect_half(need):
        def bit(b, ans_u):
            cand_u = ans_u | jnp.left_shift(jnp.int32(1), 15 - b)
            cand = (cand_u - HALF_BIAS).astype(I16)
            cnt = count_half(lambda x: jnp.where(x >= cand, one16, zero16))
            return jnp.where(cnt >= need, cand_u, ans_u)
        return lax.fori_loop(0, 16, bit, jnp.zeros((1, tq), I32)) - HALF_BIAS

    hi_t = bisect_half(n_sel)
    hi_t16 = hi_t.astype(I16)
    n_hi_gt = count_half(lambda x: jnp.where(x > hi_t16, one16, zero16))

    def low_chunk(c, carry):
        lo = ((keys_scr[c] & 0xFFFF) - HALF_BIAS).astype(I16)
        half_scr[c] = jnp.where(half_scr[c] == hi_t16, lo, jnp.int16(-HALF_BIAS))
        return carry

    lax.fori_loop(0, nck, low_chunk, 0)
    lo_t = bisect_half(n_sel - n_hi_gt)
    thr = hi_t * (2 * HALF_BIAS) + (lo_t + HALF_BIAS)
    n_ge = count(lambda k, kpos: jnp.where(k >= thr, 1, 0))
    below_all = thr == INT_MIN
    cut_scr[...] = jnp.where(below_all, -1, (1 << n_bits) - 1)
    excess = jnp.max(jnp.where(below_all, 0, n_ge - n_sel))

    @pl.when(excess > 0)
    def _():
        n_gt = count(lambda k, kpos: jnp.where(k > thr, 1, 0))
        need = n_sel - n_gt

        def tie_bit(t, cut):
            cand = cut | jnp.left_shift(jnp.int32(1), n_bits - 1 - t)
            cnt = count(lambda k, kpos: jnp.where(k == thr, jnp.where(kpos < cand, 1, 0), 0))
            return jnp.where(cnt < need, cand, cut)

        cut = lax.fori_loop(0, n_bits, tie_bit, jnp.zeros((1, tq), I32))
        cut_scr[...] = jnp.where(below_all, -1, cut)

    cut = cut_scr[...]

    def bias_chunk(c, carry):
        k = keys_scr[c]
        sel = (k > thr) | ((k == thr) & (kpos0 + c * tk <= cut))
        keys_scr[c] = pltpu.bitcast(jnp.where(sel, 0.0, NEG_BIG).astype(F32), I32)
        return carry

    lax.fori_loop(0, nck, bias_chunk, 0)

    acc_scr[...] = jnp.zeros(acc_scr.shape, F32)
    m_scr[...] = jnp.full(m_scr.shape, NEG_BIG, F32)
    l_scr[...] = jnp.zeros(l_scr.shape, F32)

    def att_chunk(c, carry):
        start = pl.multiple_of(c * tk, tk)
        subs = range(0, tk, sub_keys)

        def head_scores(h):
            k_ref = k0_ref if h // per_grp == 0 else k1_ref
            out = []
            for r0 in subs:
                s = jnp.dot(k_ref[pl.ds(start + r0, sub_keys), :], q_scr[h], preferred_element_type=F32)
                out.append(s + pltpu.bitcast(keys_scr[c, r0:r0 + sub_keys, :], F32))
            return out

        def head_values(h, r0):
            grp = h // per_grp
            return vt_ref[c, grp * DS_HEAD_DIM:(grp + 1) * DS_HEAD_DIM, r0:r0 + sub_keys]

        _flash_chains(DS_HEADS, head_scores, head_values, subs, m_scr, l_scr, acc_scr, DS_QK_AHEAD)
        return carry

    lax.fori_loop(0, nck, att_chunk, 0)
    outs = [acc_scr[h] / l_scr[h][0:1] for h in range(DS_HEADS)]
    o_ref[...] = jnp.concatenate(outs, axis=0).T.astype(o_ref.dtype)


def _dsa_attention(qt, pr, vt3, iwt, b, s_len, tq):
    nq = s_len // tq
    n = b * s_len
    n_sel = min(TOPK_MAX, s_len // 4)
    n_bits = max(1, (s_len - 1).bit_length())
    kern = functools.partial(_dsa_kernel, tq=tq, n_sel=n_sel, n_bits=n_bits)
    seq = lambda blk: pl.BlockSpec((s_len, LANES), lambda bi, i: (bi, blk))
    qblk = lambda blk: pl.BlockSpec((4 * LANES, tq), lambda bi, i: (blk // 4, bi * nq + i))
    return pl.pallas_call(
        kern,
        grid=(b, nq),
        in_specs=[
            qblk(QT_IX), seq(PR_IXK),
            pl.BlockSpec((IDX_HEADS, tq), lambda bi, i: (0, bi * nq + i)),
            qblk(QT_DS), seq(PR_DSK), seq(PR_DSK + 1),
            pl.BlockSpec((s_len // tq, LANES, tq), lambda bi, i: (bi, VT_DS, 0)),
        ],
        out_specs=pl.BlockSpec((tq, DS_WIDTH), lambda bi, i: (bi * nq + i, 0)),
        out_shape=jax.ShapeDtypeStruct((n, DS_WIDTH), BF16),
        scratch_shapes=[pltpu.VMEM((nq, tq, tq), I32), pltpu.VMEM((nq, tq, tq), I16),
                        pltpu.VMEM((IDX_HEADS, LANES, tq), BF16), pltpu.VMEM((DS_HEADS, LANES, tq), BF16),
                        pltpu.VMEM((DS_HEADS, DS_HEAD_DIM, tq), F32),
                        pltpu.VMEM((DS_HEADS, SUBLANES, tq), F32), pltpu.VMEM((DS_HEADS, SUBLANES, tq), F32),
                        pltpu.VMEM((1, tq), I32)],
        compiler_params=pltpu.CompilerParams(dimension_semantics=("arbitrary",) * 2, vmem_limit_bytes=VMEM_LIMIT),
        name="dsa_attn",
    )(qt, pr, iwt, qt, pr, pr, vt3)


def _merge_kernel(oa_ref, ob_ref, gate_ref, x_ref, wa_ref, wb_ref, wo_ref, o_ref):
    ya = jnp.dot(oa_ref[...], wa_ref[...], preferred_element_type=F32)
    yb = jnp.dot(ob_ref[...], wb_ref[...], preferred_element_type=F32)
    mix = gate_ref[:, :D_MODEL] * ya + gate_ref[:, D_MODEL:] * yb
    o_ref[...] = x_ref[...] + jnp.dot(mix.astype(BF16), wo_ref[...], preferred_element_type=F32)


def _merge(oa, ob, gate, x2, wa, wb, wo, tm):
    n = x2.shape[0]
    full = lambda a: pl.BlockSpec(a.shape, lambda i: (0, 0))
    rows = lambda w: pl.BlockSpec((tm, w), lambda i: (i, 0))
    return pl.pallas_call(
        _merge_kernel,
        grid=(n // tm,),
        in_specs=[rows(DA_WIDTH), rows(DS_WIDTH), rows(2 * D_MODEL), rows(D_MODEL), full(wa), full(wb), full(wo)],
        out_specs=rows(D_MODEL),
        out_shape=jax.ShapeDtypeStruct((n, D_MODEL), F32),
        compiler_params=pltpu.CompilerParams(dimension_semantics=("arbitrary",), vmem_limit_bytes=VMEM_LIMIT),
        name="merge",
    )(oa, ob, gate, x2, wa, wb, wo)


def _norm_matmul_kernel(x_ref, g_ref, w_ref, o_ref):
    h = _rms(x_ref[...], g_ref[...]).astype(BF16)
    o_ref[...] = jnp.dot(h, w_ref[...], preferred_element_type=F32).astype(o_ref.dtype)


def _norm_matmul(x2, g, w, tm):
    n = x2.shape[0]
    return pl.pallas_call(
        _norm_matmul_kernel,
        grid=(n // tm,),
        in_specs=[pl.BlockSpec((tm, x2.shape[1]), lambda i: (i, 0)), pl.BlockSpec(g.shape, lambda i: (0, 0)),
                  pl.BlockSpec(w.shape, lambda i: (0, 0))],
        out_specs=pl.BlockSpec((tm, w.shape[1]), lambda i: (i, 0)),
        out_shape=jax.ShapeDtypeStruct((n, w.shape[1]), BF16),
        compiler_params=pltpu.CompilerParams(dimension_semantics=("arbitrary",), vmem_limit_bytes=VMEM_LIMIT),
        name="mem_kv_proj",
    )(x2, g, w)


def _mem_attn_kernel(x_ref, g_ref, wq_ref, kv_ref, wo_ref, o_ref):
    x = x_ref[...]
    hn = _rms(x, g_ref[...]).astype(BF16)
    q = jnp.dot(hn, wq_ref[...], preferred_element_type=F32).astype(BF16)
    scale = MEM_HEAD_DIM ** -0.5
    heads = []
    for h in range(MEM_HEADS):
        qh = q[:, h * MEM_HEAD_DIM:(h + 1) * MEM_HEAD_DIM]
        kh = kv_ref[:, h * MEM_HEAD_DIM:(h + 1) * MEM_HEAD_DIM]
        vh = kv_ref[:, MEM_WIDTH + h * MEM_HEAD_DIM:MEM_WIDTH + (h + 1) * MEM_HEAD_DIM]
        s = lax.dot_general(qh, kh, NT_DIMS, preferred_element_type=F32) * scale
        m = jnp.max(s, axis=1, keepdims=True)
        p = jnp.exp(s - m)
        l = jnp.sum(p, axis=1, keepdims=True)
        heads.append(jnp.dot(p.astype(BF16), vh, preferred_element_type=F32) / l)
    o = jnp.concatenate(heads, axis=1).astype(BF16)
    o_ref[...] = x + jnp.dot(o, wo_ref[...], preferred_element_type=F32)


def _mem_attn(x2, g, wq, kv, wo, b, s_len, m_tok, tm):
    nb = s_len // tm
    n = x2.shape[0]
    full = lambda a: pl.BlockSpec(a.shape, lambda bi, i: (0, 0))
    return pl.pallas_call(
        _mem_attn_kernel,
        grid=(b, nb),
        in_specs=[pl.BlockSpec((tm, D_MODEL), lambda bi, i: (bi * nb + i, 0)), full(g), full(wq),
                  pl.BlockSpec((m_tok, 2 * MEM_WIDTH), lambda bi, i: (bi, 0)), full(wo)],
        out_specs=pl.BlockSpec((tm, D_MODEL), lambda bi, i: (bi * nb + i, 0)),
        out_shape=jax.ShapeDtypeStruct((n, D_MODEL), F32),
        compiler_params=pltpu.CompilerParams(dimension_semantics=("arbitrary",) * 2, vmem_limit_bytes=VMEM_LIMIT),
        name="mem_attn",
    )(x2, g, wq, kv, wo)


N_SUB = 2 * PEER_HEADS
EXPERT_BITS = 14
SLOT_DEPTH = 4


def _route_kernel(x_ref, g_ref, wq_ref, keys_ref, hf_ref, idx_ref, row_ref, gate_ref, sc_scr, ts_scr, ti_scr):
    hf = _rms(x_ref[...], g_ref[...])
    tm = hf.shape[0]
    hb = hf.astype(BF16)
    bits = pltpu.bitcast(hb.astype(F32), jnp.uint32)
    half_d = D_MODEL // 2
    for s in range(SUBLANES // 2):
        lo = bits[:, s * LANES:(s + 1) * LANES] >> 16
        hi = bits[:, half_d + s * LANES:half_d + (s + 1) * LANES] & jnp.uint32(0xFFFF0000)
        hf_ref[:, s, :] = lo | hi
        hf_ref[:, s + SUBLANES // 2, :] = lo | hi
    q = jnp.dot(hb, wq_ref[...], preferred_element_type=F32).astype(BF16)
    for g in range(N_SUB):
        blk = q[:, (g // 2) * LANES:(g // 2 + 1) * LANES]
        sc_scr[g] = lax.dot_general(keys_ref[g], blk, NT_DIMS, preferred_element_type=F32)

    n_slot = PEER_N_KEYS // SLOT_DEPTH
    slot_id = lax.broadcasted_iota(I32, (n_slot, tm), 0)

    def sub_topk(g, carry):
        lv = [sc_scr[g, d * n_slot:(d + 1) * n_slot, :] for d in range(SLOT_DEPTH)]
        li = [slot_id + d * n_slot for d in range(SLOT_DEPTH)]
        for x, y in ((0, 1), (2, 3), (0, 2), (1, 3), (1, 2)):
            swap = (lv[y] > lv[x]) | ((lv[y] == lv[x]) & (li[y] < li[x]))
            lv[x], lv[y] = jnp.where(swap, lv[y], lv[x]), jnp.where(swap, lv[x], lv[y])
            li[x], li[y] = jnp.where(swap, li[y], li[x]), jnp.where(swap, li[x], li[y])
        vals, ids = [], []
        for _ in range(PEER_TOPK):
            m = jnp.max(lv[0], axis=0, keepdims=True)
            idx = jnp.min(jnp.where(lv[0] == m, li[0], PEER_N_KEYS), axis=0, keepdims=True)
            vals.append(m)
            ids.append(idx)
            hit = li[0] == idx
            for d in range(SLOT_DEPTH - 1):
                lv[d] = jnp.where(hit, lv[d + 1], lv[d])
                li[d] = jnp.where(hit, li[d + 1], li[d])
            lv[-1] = jnp.where(hit, -jnp.inf, lv[-1])
        ts_scr[g] = jnp.concatenate(vals, axis=0)
        ti_scr[g] = jnp.concatenate(ids, axis=0)
        return carry

    def sub_topk_pair(p, carry):
        sub_topk(2 * p, carry)
        return sub_topk(2 * p + 1, carry)

    lax.fori_loop(0, N_SUB // 2, sub_topk_pair, 0)

    iota16 = lax.broadcasted_iota(I32, (PEER_TOPK, tm), 0)
    iota8 = lax.broadcasted_iota(I32, (SUBLANES, tm), 0)
    lead_rows = SUBLANES

    def head_topk(h, carry):
        s0, s1 = ts_scr[2 * h], ts_scr[2 * h + 1]
        i0, i1 = ti_scr[2 * h], ti_scr[2 * h + 1]
        cands, codes = [], []
        for a in range(lead_rows):
            n_j = PEER_TOPK if a == 0 else SUBLANES
            cands.append(s0[a:a + 1] + s1[:n_j])
            j_iota = iota16 if a == 0 else iota8
            codes.append(((a * PEER_TOPK + j_iota) << EXPERT_BITS) | (i0[a:a + 1] * PEER_N_KEYS + i1[:n_j]))
        cands.append(s0[lead_rows:] + s1[0:1])
        codes.append((((iota8 + lead_rows) * PEER_TOPK) << EXPERT_BITS) | (i0[lead_rows:] * PEER_N_KEYS + i1[0:1]))
        split = 4
        a_v, a_c = jnp.concatenate(cands[:split], axis=0), jnp.concatenate(codes[:split], axis=0)
        b_v, b_c = jnp.concatenate(cands[split:], axis=0), jnp.concatenate(codes[split:], axis=0)
        take_b = (b_v > a_v) | ((b_v == a_v) & (b_c < a_c))
        cur, cur_c = jnp.where(take_b, b_v, a_v), jnp.where(take_b, b_c, a_c)
        res, res_c = jnp.where(take_b, a_v, b_v), jnp.where(take_b, a_c, b_c)
        code_max = jnp.int32(2 ** 30)
        vals, ids = [], []
        for _ in range(PEER_TOPK):
            m = jnp.max(cur, axis=0, keepdims=True)
            best = jnp.min(jnp.where(cur == m, cur_c, code_max), axis=0, keepdims=True)
            vals.append(m)
            ids.append(best & (2 ** EXPERT_BITS - 1))
            hit = cur_c == best
            cur = jnp.where(hit, res, cur)
            cur_c = jnp.where(hit, res_c, cur_c)
            res = jnp.where(hit, -jnp.inf, res)
        best_s = jnp.concatenate(vals, axis=0)
        e = jnp.exp(best_s - best_s[0:1])
        gate_ref[h] = e / jnp.sum(e, axis=0, keepdims=True)
        best_i = jnp.concatenate(ids, axis=0)
        idx_ref[h] = best_i
        row_ref[h] = (best_i >> 1) * SUBLANES
        return carry

    def head_topk_pair(p, carry):
        head_topk(2 * p, carry)
        return head_topk(2 * p + 1, carry)

    lax.fori_loop(0, PEER_HEADS // 2, head_topk_pair, 0)


def _route(x2, g, wq, keys_p, tm):
    n = x2.shape[0]
    return pl.pallas_call(
        _route_kernel,
        grid=(n // tm,),
        in_specs=[pl.BlockSpec((tm, D_MODEL), lambda i: (i, 0)), pl.BlockSpec(g.shape, lambda i: (0, 0)),
                  pl.BlockSpec(wq.shape, lambda i: (0, 0)), pl.BlockSpec(keys_p.shape, lambda i: (0, 0, 0))],
        out_specs=[pl.BlockSpec((tm, SUBLANES, LANES), lambda i: (i, 0, 0))]
        + [pl.BlockSpec((PEER_HEADS, PEER_TOPK, tm), lambda i: (0, 0, i))] * 3,
        out_shape=[jax.ShapeDtypeStruct((n, SUBLANES, LANES), jnp.uint32),
                   jax.ShapeDtypeStruct((PEER_HEADS, PEER_TOPK, n), I32),
                   jax.ShapeDtypeStruct((PEER_HEADS, PEER_TOPK, n), I32),
                   jax.ShapeDtypeStruct((PEER_HEADS, PEER_TOPK, n), F32)],
        scratch_shapes=[pltpu.VMEM((N_SUB, PEER_N_KEYS, tm), F32), pltpu.VMEM((N_SUB, PEER_TOPK, tm), F32),
                        pltpu.VMEM((N_SUB, PEER_TOPK, tm), I32)],
        compiler_params=pltpu.CompilerParams(dimension_semantics=("arbitrary",), vmem_limit_bytes=VMEM_LIMIT),
        name="peer_route",
    )(x2, g, wq, keys_p)


TILE_ROWS = 2 * SUBLANES
PAIRS_PER_KTILE = 16
TOK_UNROLL = 16
SUM_UNROLL = 16


def _pack_table(t):
    e = t.shape[0]
    bits = lax.bitcast_convert_type(t.astype(BF16), jnp.uint16).astype(jnp.uint32)
    words = bits[:, :D_MODEL // 2] | (bits[:, D_MODEL // 2:] << 16)
    return words.reshape(e * (SUBLANES // 2), LANES)


def _gather_tiles(off_ref, tab_ref, u):
    tiles = []
    for k in range(PEER_SEL):
        start = pl.multiple_of(off_ref[u * PEER_SEL + k], SUBLANES)
        tiles.append(pltpu.bitcast(tab_ref[pl.ds(start, SUBLANES), :], BF16))
    return tiles


def _for_trips(rows_hbm, bufs, sems, tok0, unroll, n_trip, body):
    assert n_trip % 2 == 0
    words = unroll * PEER_SEL

    def fetch(trip, par):
        src = rows_hbm.at[pl.ds(pl.multiple_of((tok0 + trip * unroll) * PEER_SEL, words), words)]
        return pltpu.make_async_copy(src, bufs[par], sems.at[par])

    fetch(0, 0).start()
    fetch(1, 1).start()

    def pair(p, carry):
        for par in range(2):
            trip = 2 * p + par
            fetch(trip, par).wait()
            body(trip, bufs[par], par)
            fetch(jnp.minimum(trip + 2, n_trip - 2 + par), par).start()
        return carry

    lax.fori_loop(0, n_trip // 2, pair, 0)
    for par in range(2):
        fetch(n_trip - 2 + par, par).wait()


def _peer_dot_kernel(rows_hbm, tab_ref, h_ref, gate_ref, idxv_ref, sel_ref, ce_ref, co_ref,
                     ye_scr, yo_scr, ze_scr, zo_scr, off_a, off_b, sems, *, tt):
    ones = jnp.ones((SUBLANES, LANES), BF16)
    half_rows = PAIRS_PER_KTILE
    n_trip = tt // TOK_UNROLL

    def row_sums(i, slot, off_ref):
        for u in range(TOK_UNROLL):
            t = i * TOK_UNROLL + u
            hp = pltpu.bitcast(h_ref[t], BF16)
            prods = [tile * hp for tile in _gather_tiles(off_ref, tab_ref, u)]
            for kt in range(PEER_SEL // PAIRS_PER_KTILE):
                stack = jnp.concatenate(prods[kt * PAIRS_PER_KTILE:(kt + 1) * PAIRS_PER_KTILE], axis=0)
                y = jnp.dot(sel_ref[...], stack, preferred_element_type=F32)
                r0 = u * PEER_SEL + kt * half_rows
                ye_scr[slot, r0:r0 + half_rows, :] = y[:half_rows].astype(BF16)
                yo_scr[slot, r0:r0 + half_rows, :] = y[half_rows:].astype(BF16)

    def lane_sums(i, slot):
        ze = lax.dot_general(ones, ye_scr[slot], NT_DIMS, preferred_element_type=F32)
        zo = lax.dot_general(ones, yo_scr[slot], NT_DIMS, preferred_element_type=F32)
        for u in range(TOK_UNROLL):
            t = i * TOK_UNROLL + u
            ze_scr[pl.ds(t, 1), :] = ze[0:1, u * PEER_SEL:(u + 1) * PEER_SEL]
            zo_scr[pl.ds(t, 1), :] = zo[0:1, u * PEER_SEL:(u + 1) * PEER_SEL]

    def trip(i, off_ref, par):
        lane_sums(jnp.maximum(i - 1, 0), 1 - par)
        row_sums(i, par, off_ref)

    ye_scr[1] = jnp.zeros(ye_scr.shape[1:], ye_scr.dtype)
    yo_scr[1] = jnp.zeros(yo_scr.shape[1:], yo_scr.dtype)
    _for_trips(rows_hbm, (off_a, off_b), sems, pl.program_id(0) * tt, TOK_UNROLL, n_trip, trip)
    lane_sums(n_trip - 1, (n_trip - 1) & 1)
    even = (idxv_ref[...] & 1) == 0
    a = jnp.where(even, ze_scr[...], zo_scr[...])
    c = 0.5 * a * (1.0 + lax.erf(a * (2.0 ** -0.5))) * gate_ref[...]
    ce_ref[...] = jnp.where(even, c, 0.0)
    co_ref[...] = jnp.where(even, 0.0, c)


def _peer_dot(rows_flat, tab, hpk, gate, idx2, sel, tt):
    n = hpk.shape[0]
    kern = functools.partial(_peer_dot_kernel, tt=tt)
    rows = pl.BlockSpec((tt, PEER_SEL), lambda i: (i, 0))
    return pl.pallas_call(
        kern,
        grid=(n // tt,),
        in_specs=[pl.BlockSpec(memory_space=pl.ANY),
                  pl.BlockSpec(memory_space=pltpu.VMEM),
                  pl.BlockSpec((tt, SUBLANES, LANES), lambda i: (i, 0, 0)),
                  rows, rows, pl.BlockSpec(sel.shape, lambda i: (0, 0))],
        out_specs=[rows, rows],
        out_shape=[jax.ShapeDtypeStruct((n, PEER_SEL), F32), jax.ShapeDtypeStruct((n, PEER_SEL), F32)],
        scratch_shapes=[pltpu.VMEM((2, TOK_UNROLL * PEER_SEL, LANES), BF16)] * 2
        + [pltpu.VMEM((tt, PEER_SEL), F32)] * 2
        + [pltpu.SMEM((TOK_UNROLL * PEER_SEL,), I32)] * 2 + [pltpu.SemaphoreType.DMA((2,))],
        compiler_params=pltpu.CompilerParams(dimension_semantics=("arbitrary",), vmem_limit_bytes=VMEM_LIMIT),
        name="peer_dot",
    )(rows_flat, tab, hpk, gate, idx2, sel)


def _peer_sum_kernel(rows_hbm, ce_ref, co_ref, spread_ref, tab_ref, o_ref, m1_scr, m2_scr, off_a, off_b, sems,
                     *, tt):
    width = PEER_SEL * TILE_ROWS
    cc = jnp.concatenate([ce_ref[...], co_ref[...]], axis=1)
    c1 = cc.astype(BF16)
    c2 = (cc - c1.astype(F32)).astype(BF16)
    m1_scr[...] = jnp.dot(c1, spread_ref[...], preferred_element_type=F32)
    m2_scr[...] = jnp.dot(c2, spread_ref[...], preferred_element_type=F32)
    row = lax.broadcasted_iota(I32, (SUBLANES, width), 0)
    lane = lax.broadcasted_iota(I32, (SUBLANES, width), 1)
    half = SUBLANES // 2
    on_row = (lane & (SUBLANES - 1)) == 2 * (row % half) + row // half

    def token(t, off_ref, u):
        w = jnp.concatenate(_gather_tiles(off_ref, tab_ref, u), axis=0)
        lhs = []
        for m_scr in (m1_scr, m2_scr):
            coef = jnp.broadcast_to(m_scr[pl.ds(t, 1), :], (SUBLANES, width))
            lhs.append(jnp.where(on_row, coef, 0.0).astype(BF16))
        res = jnp.dot(jnp.concatenate(lhs, axis=0), w, preferred_element_type=F32)
        o_ref[t] = res[:SUBLANES] + res[SUBLANES:]

    def trip(i, off_ref, par):
        for u in range(SUM_UNROLL):
            token(i * SUM_UNROLL + u, off_ref, u)

    _for_trips(rows_hbm, (off_a, off_b), sems, pl.program_id(0) * tt, SUM_UNROLL, tt // SUM_UNROLL, trip)


def _peer_sum(rows_flat, ce, co, spread, tab, tt):
    n = ce.shape[0]
    kern = functools.partial(_peer_sum_kernel, tt=tt)
    rows = pl.BlockSpec((tt, PEER_SEL), lambda i: (i, 0))
    width = PEER_SEL * TILE_ROWS
    return pl.pallas_call(
        kern,
        grid=(n // tt,),
        in_specs=[pl.BlockSpec(memory_space=pl.ANY), rows, rows,
                  pl.BlockSpec(spread.shape, lambda i: (0, 0)), pl.BlockSpec(memory_space=pltpu.VMEM)],
        out_specs=pl.BlockSpec((tt, SUBLANES, LANES), lambda i: (i, 0, 0)),
        out_shape=jax.ShapeDtypeStruct((n, SUBLANES, LANES), F32),
        scratch_shapes=[pltpu.VMEM((tt, width), F32), pltpu.VMEM((tt, width), F32)]
        + [pltpu.SMEM((SUM_UNROLL * PEER_SEL,), I32)] * 2 + [pltpu.SemaphoreType.DMA((2,))],
        compiler_params=pltpu.CompilerParams(dimension_semantics=("arbitrary",), vmem_limit_bytes=VMEM_LIMIT),
        name="peer_sum",
    )(rows_flat, ce, co, spread, tab)


def _final_kernel(x_ref, p_ref, g_ref, o_ref):
    peer = jnp.concatenate([p_ref[:, s, :] for s in range(SUBLANES)], axis=1)
    o_ref[...] = _rms(x_ref[...] + peer, g_ref[...])


def _final(x2, peer3, g, tm):
    n = x2.shape[0]
    return pl.pallas_call(
        _final_kernel,
        grid=(n // tm,),
        in_specs=[pl.BlockSpec((tm, D_MODEL), lambda i: (i, 0)),
                  pl.BlockSpec((tm, SUBLANES, LANES), lambda i: (i, 0, 0)),
                  pl.BlockSpec(g.shape, lambda i: (0, 0))],
        out_specs=pl.BlockSpec((tm, D_MODEL), lambda i: (i, 0)),
        out_shape=jax.ShapeDtypeStruct((n, D_MODEL), F32),
        compiler_params=pltpu.CompilerParams(dimension_semantics=("arbitrary",), vmem_limit_bytes=VMEM_LIMIT),
        name="final_norm",
    )(x2, peer3, g)


ROW_BLOCK = 256
PEER_BLOCK = 256


def _tile(n, pref):
    t = pref
    while n % t:
        t //= 2
    return t


def _layer(l, x2, mem2, pos, invf, b, s_len, m_tok, norm_mix_g, w_in, da_lambda, da_subln_g, w_branch_a,
           w_branch_b, gate_bias, w_out, norm_mem_g, mem_kv_norm_g, w_mem_q, w_mem_kv, w_mem_o, norm_ffn_g,
           peer_w_q, peer_sub_keys, peer_u, peer_v):
    n = b * s_len
    row2 = lambda v: v.reshape(1, -1)
    splits = [0]
    for c in COL_SIZES:
        splits.append(splits[-1] + c)
    da_q, da_k, da_v, ds_q, ds_k, ds_v, ix_q, ix_k, ix_w, gates = (
        w_in[l][:, splits[j]:splits[j + 1]] for j in range(len(COL_SIZES)))
    k0, k1 = ds_k[:, :DS_HEAD_DIM], ds_k[:, DS_HEAD_DIM:]
    w_keys = jnp.concatenate([da_k, k0, k0, k1, k1, ix_k, ix_k], axis=1).astype(BF16)
    w_q_t = jnp.concatenate([da_q, ds_q, ix_q], axis=1).T.astype(BF16)
    w_v_t = jnp.concatenate([da_v, ds_v], axis=1).T.astype(BF16)
    t_att = _tile(s_len, ROW_BLOCK)
    t_proj = t_att
    pr, gate, qt, vt3, iwt = _in_proj(x2, row2(norm_mix_g[l]), pos, invf, pos.reshape(1, n),
                                      invf[:, :ROPE_HALF].reshape(ROPE_HALF, 1), row2(gate_bias[l]), w_keys,
                                      gates.astype(BF16), w_q_t, w_v_t, ix_w.T.astype(BF16), t_proj, t_att)

    lam_init = 0.8 - 0.6 * math.exp(-0.3 * l)
    o_a = _diff_attention(da_lambda[l], qt, pr, vt3, da_subln_g[l].reshape(-1, 1), b, s_len, t_att, lam_init)
    o_b = _dsa_attention(qt, pr, vt3, iwt, b, s_len, t_att)
    x2 = _merge(o_a, o_b, gate, x2, w_branch_a[l].astype(BF16), w_branch_b[l].astype(BF16),
                w_out[l].astype(BF16), _tile(n, ROW_BLOCK))

    kv = _norm_matmul(mem2, row2(mem_kv_norm_g[l]), w_mem_kv[l].astype(BF16), _tile(mem2.shape[0], ROW_BLOCK))
    x2 = _mem_attn(x2, row2(norm_mem_g[l]), w_mem_q[l].astype(BF16), kv, w_mem_o[l].astype(BF16),
                   b, s_len, m_tok, _tile(s_len, ROW_BLOCK))

    sk = peer_sub_keys[l].reshape(N_SUB, PEER_N_KEYS, PEER_HALF)
    z = jnp.zeros_like(sk)
    keys_p = jnp.where((jnp.arange(N_SUB) % 2 == 0)[:, None, None],
                       jnp.concatenate([sk, z], axis=2), jnp.concatenate([z, sk], axis=2)).astype(BF16)
    hpk, idx_t, row_t, gate_t = _route(x2, row2(norm_ffn_g[l]), peer_w_q[l].astype(BF16), keys_p, _tile(n, ROW_BLOCK))
    idx2 = idx_t.reshape(PEER_SEL, n).T
    rows2 = row_t.reshape(PEER_SEL, n).T
    gate2 = gate_t.reshape(PEER_SEL, n).T
    tt = _tile(n, PEER_BLOCK)
    r32 = jnp.arange(2 * PAIRS_PER_KTILE)[:, None]
    c256 = jnp.arange(PAIRS_PER_KTILE * TILE_ROWS)[None, :]
    sel = ((c256 // TILE_ROWS == r32 % PAIRS_PER_KTILE)
           & ((c256 % TILE_ROWS) // SUBLANES == r32 // PAIRS_PER_KTILE)).astype(BF16)
    k256 = jnp.arange(2 * PEER_SEL)[:, None]
    c2048 = jnp.arange(PEER_SEL * TILE_ROWS)[None, :]
    spread = ((c2048 // TILE_ROWS == k256 % PEER_SEL)
              & ((c2048 % TILE_ROWS) // SUBLANES == k256 // PEER_SEL)).astype(BF16)
    rows_flat = rows2.reshape(-1)
    ce2, co2 = _peer_dot(rows_flat, _pack_table(peer_u[l]), hpk, gate2, idx2, sel, tt)
    peer3 = _peer_sum(rows_flat, ce2, co2, spread, _pack_table(peer_v[l]), tt)
    return x2, peer3


def kernel(x, mem, positions, norm_mix_g, w_in, da_lambda, da_subln_g, w_branch_a, w_branch_b, gate_bias, w_out, norm_mem_g, mem_kv_norm_g, w_mem_q, w_mem_kv, w_mem_o, norm_ffn_g, peer_w_q, peer_sub_keys, peer_u, peer_v, final_norm_g):
    b, s_len, d = x.shape
    m_tok = mem.shape[1]
    n = b * s_len
    depth = w_in.shape[0]
    x2 = x.reshape(n, d)
    mem2 = mem.reshape(b * m_tok, d)
    pos = positions.astype(F32).reshape(n, 1)
    inv_freq = ROPE_THETA ** (-(jnp.arange(ROPE_HALF, dtype=F32) * 2.0) / ROPE_DIM)
    invf = jnp.tile(inv_freq, LANES // ROPE_HALF).reshape(1, LANES)
    peer3 = None
    for l in range(depth):
        if peer3 is not None:
            x2 = x2 + peer3.reshape(n, d)
        x2, peer3 = _layer(l, x2, mem2, pos, invf, b, s_len, m_tok, norm_mix_g, w_in, da_lambda, da_subln_g,
                           w_branch_a, w_branch_b, gate_bias, w_out, norm_mem_g, mem_kv_norm_g, w_mem_q, w_mem_kv,
                           w_mem_o, norm_ffn_g, peer_w_q, peer_sub_keys, peer_u, peer_v)
    out = _final(x2, peer3, final_norm_g.reshape(1, d), _tile(n, ROW_BLOCK))
    return out.reshape(b, s_len, d)
```

```python
import functools
import math

import jax
import jax.numpy as jnp
from jax import lax
from jax.experimental import pallas as pl
from jax.experimental.pallas import tpu as pltpu

F32 = jnp.float32
BF16 = jnp.bfloat16
I32 = jnp.int32
I16 = jnp.int16

D_MODEL = 1024
EPS = 1e-6
ROPE_THETA = 500000.0
ROT_HEAD_DIM = 64
ROPE_DIM = 16
ROPE_HALF = ROPE_DIM // 2

DA_HEADS = 4
DA_QK_DIM = 64
DA_V_DIM = 128
DA_WIDTH = 512
DS_HEADS = 8
DS_KV_HEADS = 2
DS_HEAD_DIM = 64
DS_WIDTH = 512
IDX_HEADS = 8
TOPK_MAX = 256
MEM_HEADS = 4
MEM_HEAD_DIM = 128
MEM_WIDTH = 512
PEER_HEADS = 8
PEER_N_KEYS = 128
PEER_HALF = 64
PEER_TOPK = 16
PEER_SEL = PEER_HEADS * PEER_TOPK

COL_SIZES = (512, 512, 512, 512, 128, 128, 512, 64, 8, 2048)

LANES = 128
SUBLANES = 8
NEG_BIG = -1e30
INT_MIN = -(2 ** 31)
HALF_BIAS = 2 ** 15
HALF_ROWS = 2 * SUBLANES
VMEM_LIMIT = 56 * 1024 * 1024

NT_DIMS = (((1,), (1,)), ((), ()))
PROJ_COLS = 512


def _rms(x, g):
    var = jnp.mean(x * x, axis=-1, keepdims=True)
    return x * lax.rsqrt(var + EPS) * g


def _rope_rows(y, cos8, sin8):
    pieces = []
    for r0 in range(0, y.shape[0], ROT_HEAD_DIM):
        t1 = y[r0:r0 + ROPE_HALF]
        t2 = y[r0 + ROPE_HALF:r0 + ROPE_DIM]
        pieces += [t1 * cos8 - t2 * sin8, t2 * cos8 + t1 * sin8, y[r0 + ROPE_DIM:r0 + ROT_HEAD_DIM]]
    return jnp.concatenate(pieces, axis=0)


def _in_proj_kernel(x_ref, g_ref, pos_ref, invf_ref, pos_t_ref, invf8_ref, bias_ref, wr_ref, wg_ref,
                    wqt_ref, wvt_ref, wwt_ref, pr_ref, gate_ref, qt_ref, vt_ref, iwt_ref):
    h = _rms(x_ref[...], g_ref[...]).astype(BF16)
    tm = h.shape[0]
    ang_t = invf8_ref[...] * pos_t_ref[...]
    cos8 = jnp.cos(ang_t)
    sin8 = jnp.sin(ang_t)
    row_chunk = 2 * LANES
    for r0 in range(0, wqt_ref.shape[0], row_chunk):
        y = lax.dot_general(wqt_ref[r0:r0 + row_chunk, :], h, NT_DIMS, preferred_element_type=F32)
        qt_ref[r0:r0 + row_chunk, :] = _rope_rows(y, cos8, sin8).astype(qt_ref.dtype)
    for r0 in range(0, wvt_ref.shape[0], row_chunk):
        r1 = min(r0 + row_chunk, wvt_ref.shape[0])
        y = lax.dot_general(wvt_ref[r0:r1, :], h, NT_DIMS, preferred_element_type=F32)
        chunk = vt_ref.shape[2]
        for j in range(vt_ref.shape[0]):
            vt_ref[j, r0:r1, :] = y[:, j * chunk:(j + 1) * chunk].astype(vt_ref.dtype)
    iwt_ref[...] = lax.dot_general(wwt_ref[...], h, NT_DIMS, preferred_element_type=F32)
    ang = pos_ref[...] * invf_ref[...]
    cos = jnp.cos(ang)
    sin = jnp.sin(ang)
    lane = lax.broadcasted_iota(I32, (tm, LANES), 1) % ROT_HEAD_DIM
    c_t = jnp.where(lane < ROPE_DIM, cos, 1.0)
    s_lo = jnp.where(lane < ROPE_HALF, -sin, 0.0)
    s_hi = jnp.where((lane >= ROPE_HALF) & (lane < ROPE_DIM), sin, 0.0)
    n_r = wr_ref.shape[1]
    for c0 in range(0, n_r, PROJ_COLS):
        c1 = min(c0 + PROJ_COLS, n_r)
        w = c1 - c0
        y = jnp.dot(h, wr_ref[:, c0:c1], preferred_element_type=F32)
        reps = w // LANES
        ct = jnp.concatenate([c_t] * reps, axis=1)
        sl = jnp.concatenate([s_lo] * reps, axis=1)
        sh = jnp.concatenate([s_hi] * reps, axis=1)
        y = y * ct + pltpu.roll(y, w - ROPE_HALF, 1) * sl + pltpu.roll(y, ROPE_HALF, 1) * sh
        pr_ref[:, c0:c1] = y.astype(pr_ref.dtype)
    n_g = wg_ref.shape[1]
    for c0 in range(0, n_g, PROJ_COLS):
        y = jnp.dot(h, wg_ref[:, c0:c0 + PROJ_COLS], preferred_element_type=F32)
        gate_ref[:, c0:c0 + PROJ_COLS] = jax.nn.sigmoid(y + bias_ref[:, c0:c0 + PROJ_COLS])


def _in_proj(x2, g, pos, invf, pos_t, invf8, bias, wr, wg, wqt, wvt, wwt, tm, chunk):
    n = x2.shape[0]
    per_step = tm // chunk
    full = lambda a: pl.BlockSpec(a.shape, lambda i: (0, 0))
    rows = lambda w: pl.BlockSpec((tm, w), lambda i: (i, 0))
    cols = lambda r: pl.BlockSpec((r, tm), lambda i: (0, i))
    return pl.pallas_call(
        _in_proj_kernel,
        grid=(n // tm,),
        in_specs=[rows(D_MODEL), full(g), rows(1), full(invf), cols(1), full(invf8), full(bias), full(wr), full(wg),
                  full(wqt), full(wvt), full(wwt)],
        out_specs=[rows(wr.shape[1]), rows(wg.shape[1]), cols(wqt.shape[0]),
                   pl.BlockSpec((per_step, wvt.shape[0], chunk), lambda i: (i, 0, 0)), cols(wwt.shape[0])],
        out_shape=[jax.ShapeDtypeStruct((n, wr.shape[1]), BF16), jax.ShapeDtypeStruct((n, wg.shape[1]), F32),
                   jax.ShapeDtypeStruct((wqt.shape[0], n), BF16),
                   jax.ShapeDtypeStruct((n // chunk, wvt.shape[0], chunk), BF16),
                   jax.ShapeDtypeStruct((wwt.shape[0], n), F32)],
        compiler_params=pltpu.CompilerParams(dimension_semantics=("arbitrary",), vmem_limit_bytes=VMEM_LIMIT),
        name="in_proj",
    )(x2, g, pos, invf, pos_t, invf8, bias, wr, wg, wqt, wvt, wwt)


QT_DA = 0
QT_DS = 4
QT_IX = 8
VT_DA = 0
VT_DS = 4
PR_DAK = 0
PR_DSK = 4
PR_IXK = 6


DA_SUB = 128
DS_SUB = 64
DA_QK_AHEAD = 1
DS_QK_AHEAD = 3


def _zero_after(x):
    bits = pltpu.bitcast(x, jnp.uint32)
    return pltpu.bitcast((bits >> 16) >> 16, F32)


def _flash_chains(n_chains, score_fn, value_fn, subs, m_scr, l_scr, acc_scr, ahead):
    pending = [score_fn(c) for c in range(ahead)]
    for c in range(n_chains):
        scores = pending.pop(0)
        m = m_scr[c][0:1]
        if c + ahead < n_chains:
            pending.append(score_fn(c + ahead))
            m = m + _zero_after(pending[-1][0][0:1])
        m_new = m
        for s in scores:
            m_new = jnp.maximum(m_new, jnp.max(s, axis=0, keepdims=True))
        alpha = jnp.exp(m - m_new)
        l = alpha * l_scr[c][0:1]
        a = alpha * acc_scr[c]
        for r0, s in zip(subs, scores):
            p = jnp.exp(s - m_new)
            l = l + jnp.sum(p, axis=0, keepdims=True)
            v = value_fn(c, r0)
            a = a + jnp.dot(v, p.astype(v.dtype), preferred_element_type=F32)
        m_scr[c] = jnp.broadcast_to(m_new, m_scr.shape[1:])
        l_scr[c] = jnp.broadcast_to(l, l_scr.shape[1:])
        acc_scr[c] = a


def _da_kernel(lam_ref, qt_ref, k_ref, vt_ref, g_ref, o_ref, q_scr, acc_scr, m_scr, l_scr, *, tq, lam_init):
    tk = tq
    sub_keys = DA_SUB
    i = pl.program_id(1)
    n_chains = 2 * DA_HEADS
    frow = lax.broadcasted_iota(I32, (LANES, tq), 0)
    scale = jnp.asarray(DA_QK_DIM ** -0.5, BF16)
    for h in range(DA_HEADS):
        blk = qt_ref[h * LANES:(h + 1) * LANES, :] * scale
        zero = jnp.zeros_like(blk)
        q_scr[2 * h] = jnp.where(frow < DA_QK_DIM, blk, zero)
        q_scr[2 * h + 1] = jnp.where(frow >= DA_QK_DIM, blk, zero)
    acc_scr[...] = jnp.zeros(acc_scr.shape, F32)
    m_scr[...] = jnp.full(m_scr.shape, NEG_BIG, F32)
    l_scr[...] = jnp.zeros(l_scr.shape, F32)
    kpos0 = lax.broadcasted_iota(I32, (sub_keys, tq), 0)
    qpos = i * tq + lax.broadcasted_iota(I32, (sub_keys, tq), 1)
    subs = range(0, tk, sub_keys)

    def chunk(j, masked):
        start = pl.multiple_of(j * tk, tk)

        def scores(c):
            head = c // 2
            out = []
            for r0 in subs:
                k = k_ref[pl.ds(start + r0, sub_keys), head * LANES:(head + 1) * LANES]
                s = jnp.dot(k, q_scr[c], preferred_element_type=F32)
                if masked:
                    s = jnp.where(kpos0 + (j * tk + r0) <= qpos, s, NEG_BIG)
                out.append(s)
            return out

        def values(c, r0):
            head = c // 2
            return vt_ref[j, head * LANES:(head + 1) * LANES, r0:r0 + sub_keys]

        _flash_chains(n_chains, scores, values, subs, m_scr, l_scr, acc_scr, DA_QK_AHEAD)

    def full_chunk(j, carry):
        chunk(j, False)
        return carry

    lax.fori_loop(0, i, full_chunk, 0)
    chunk(i, True)

    lp = lam_ref[...]
    lam = (jnp.exp(jnp.sum(lp[0:1] * lp[1:2], axis=1, keepdims=True))
           - jnp.exp(jnp.sum(lp[2:3] * lp[3:4], axis=1, keepdims=True)) + lam_init)
    for h in range(DA_HEADS):
        o = acc_scr[2 * h] / l_scr[2 * h][0:1] - lam * (acc_scr[2 * h + 1] / l_scr[2 * h + 1][0:1])
        var = jnp.mean(o * o, axis=0, keepdims=True)
        o = o * lax.rsqrt(var + EPS) * g_ref[...] * (1.0 - lam_init)
        o_ref[:, h * LANES:(h + 1) * LANES] = o.T.astype(o_ref.dtype)


def _diff_attention(lam_p, qt, pr, vt3, subln_g, b, s_len, tq, lam_init):
    nq = s_len // tq
    n = b * s_len
    n_chains = 2 * DA_HEADS
    kern = functools.partial(_da_kernel, tq=tq, lam_init=lam_init)
    return pl.pallas_call(
        kern,
        grid=(b, nq),
        in_specs=[
            pl.BlockSpec(lam_p.shape, lambda bi, i: (0, 0)),
            pl.BlockSpec((DA_HEADS * LANES, tq), lambda bi, i: (QT_DA // DA_HEADS, bi * nq + i)),
            pl.BlockSpec((s_len, DA_HEADS * LANES), lambda bi, i: (bi, PR_DAK // DA_HEADS)),
            pl.BlockSpec((s_len // tq, DA_HEADS * LANES, tq), lambda bi, i: (bi, VT_DA // DA_HEADS, 0)),
            pl.BlockSpec(subln_g.shape, lambda bi, i: (0, 0)),
        ],
        out_specs=pl.BlockSpec((tq, DA_WIDTH), lambda bi, i: (bi * nq + i, 0)),
        out_shape=jax.ShapeDtypeStruct((n, DA_WIDTH), BF16),
        scratch_shapes=[pltpu.VMEM((n_chains, LANES, tq), BF16), pltpu.VMEM((n_chains, DA_V_DIM, tq), F32),
                        pltpu.VMEM((n_chains, SUBLANES, tq), F32), pltpu.VMEM((n_chains, SUBLANES, tq), F32)],
        compiler_params=pltpu.CompilerParams(dimension_semantics=("arbitrary",) * 2, vmem_limit_bytes=VMEM_LIMIT),
        name="diff_attn",
    )(lam_p, qt, pr, vt3, subln_g)


def _dsa_kernel(qit_ref, ki_ref, wt_ref, qt_ref, k0_ref, k1_ref, vt_ref, o_ref,
                keys_scr, half_scr, qi_scr, q_scr, acc_scr, m_scr, l_scr, cut_scr, *, tq, n_sel, n_bits):
    tk = tq
    sub_keys = DS_SUB
    i = pl.program_id(1)
    nck = i + 1
    kpos0 = lax.broadcasted_iota(I32, (tk, tq), 0)
    qpos = i * tq + lax.broadcasted_iota(I32, (tk, tq), 1)
    frow = lax.broadcasted_iota(I32, (LANES, tq), 0)
    per_grp = DS_HEADS // DS_KV_HEADS

    scale = jnp.asarray(DS_HEAD_DIM ** -0.5, BF16)
    for h in range(DS_HEADS):
        keep = (frow < DS_HEAD_DIM) if h % 2 == 0 else (frow >= DS_HEAD_DIM)
        rows = slice((h // 2) * LANES, (h // 2 + 1) * LANES)
        blk = qit_ref[rows, :]
        qi_scr[h] = jnp.where(keep, blk, jnp.zeros_like(blk))
        blk = qt_ref[rows, :] * scale
        q_scr[h] = jnp.where(keep, blk, jnp.zeros_like(blk))

    kpos_sub = lax.broadcasted_iota(I32, (sub_keys, tq), 0)
    qpos_sub = i * tq + lax.broadcasted_iota(I32, (sub_keys, tq), 1)

    def score_chunk(c, carry):
        start = pl.multiple_of(c * tk, tk)
        for r0 in range(0, tk, sub_keys):
            kc = ki_ref[pl.ds(start + r0, sub_keys), :]
            acc = jnp.zeros((sub_keys, tq), F32)
            for h in range(IDX_HEADS):
                s = jnp.dot(kc, qi_scr[h], preferred_element_type=F32)
                acc = acc + wt_ref[h:h + 1, :] * jnp.maximum(s, 0.0)
            bits = pltpu.bitcast(acc, I32)
            key = bits ^ ((bits >> 31) & 0x7FFFFFFF)
            key = jnp.where(acc == 0.0, 0, key)
            key = jnp.where(kpos_sub + (c * tk + r0) <= qpos_sub, key, INT_MIN)
            keys_scr[c, r0:r0 + sub_keys, :] = key
            half_scr[c, r0:r0 + sub_keys, :] = (key >> 16).astype(I16)
        return carry

    lax.fori_loop(0, nck, score_chunk, 0)

    def count(pred):
        def body(c, acc):
            hit = pred(keys_scr[c], kpos0 + c * tk)
            return acc + jnp.sum(hit.reshape(tk // SUBLANES, SUBLANES, tq), axis=0)
        acc = lax.fori_loop(0, nck, body, jnp.zeros((SUBLANES, tq), I32))
        return jnp.sum(acc, axis=0, keepdims=True)

    def count_half(pred):
        def body(c, acc):
            hit = pred(half_scr[c])
            for r0 in range(0, tk, HALF_ROWS):
                acc = acc + hit[r0:r0 + HALF_ROWS]
            return acc
        acc = lax.fori_loop(0, nck, body, jnp.zeros((HALF_ROWS, tq), I16))
        return jnp.sum(acc.astype(I32), axis=0, keepdims=True)

    one16, zero16 = jnp.int16(1), jnp.int16(0)

    def bisect_half(need):
        def bit(b, ans_u):
            cand_u = ans_u | jnp.left_shift(jnp.int32(1), 15 - b)
            cand = (cand_u - HALF_BIAS).astype(I16)
            cnt = count_half(lambda x: jnp.where(x >= cand, one16, zero16))
            return jnp.where(cnt >= need, cand_u, ans_u)
        return lax.fori_loop(0, 16, bit, jnp.zeros((1, tq), I32)) - HALF_BIAS

    hi_t = bisect_half(n_sel)
    hi_t16 = hi_t.astype(I16)
    n_hi_gt = count_half(lambda x: jnp.where(x > hi_t16, one16, zero16))

    def low_chunk(c, carry):
        lo = ((keys_scr[c] & 0xFFFF) - HALF_BIAS).astype(I16)
        half_scr[c] = jnp.where(half_scr[c] == hi_t16, lo, jnp.int16(-HALF_BIAS))
        return carry

    lax.fori_loop(0, nck, low_chunk, 0)
    lo_t = bisect_half(n_sel - n_hi_gt)
    thr = hi_t * (2 * HALF_BIAS) + (lo_t + HALF_BIAS)
    n_ge = count(lambda k, kpos: jnp.where(k >= thr, 1, 0))
    below_all = thr == INT_MIN
    cut_scr[...] = jnp.where(below_all, -1, (1 << n_bits) - 1)
    excess = jnp.max(jnp.where(below_all, 0, n_ge - n_sel))

    @pl.when(excess > 0)
    def _():
        n_gt = count(lambda k, kpos: jnp.where(k > thr, 1, 0))
        need = n_sel - n_gt

        def tie_bit(t, cut):
            cand = cut | jnp.left_shift(jnp.int32(1), n_bits - 1 - t)
            cnt = count(lambda k, kpos: jnp.where(k == thr, jnp.where(kpos < cand, 1, 0), 0))
            return jnp.where(cnt < need, cand, cut)

        cut = lax.fori_loop(0, n_bits, tie_bit, jnp.zeros((1, tq), I32))
        cut_scr[...] = jnp.where(below_all, -1, cut)

    cut = cut_scr[...]

    def bias_chunk(c, carry):
        k = keys_scr[c]
        sel = (k > thr) | ((k == thr) & (kpos0 + c * tk <= cut))
        keys_scr[c] = pltpu.bitcast(jnp.where(sel, 0.0, NEG_BIG).astype(F32), I32)
        return carry

    lax.fori_loop(0, nck, bias_chunk, 0)

    acc_scr[...] = jnp.zeros(acc_scr.shape, F32)
    m_scr[...] = jnp.full(m_scr.shape, NEG_BIG, F32)
    l_scr[...] = jnp.zeros(l_scr.shape, F32)

    def att_chunk(c, carry):
        start = pl.multiple_of(c * tk, tk)
        subs = range(0, tk, sub_keys)

        def head_scores(h):
            k_ref = k0_ref if h // per_grp == 0 else k1_ref
            out = []
            for r0 in subs:
                s = jnp.dot(k_ref[pl.ds(start + r0, sub_keys), :], q_scr[h], preferred_element_type=F32)
                out.append(s + pltpu.bitcast(keys_scr[c, r0:r0 + sub_keys, :], F32))
            return out

        def head_values(h, r0):
            grp = h // per_grp
            return vt_ref[c, grp * DS_HEAD_DIM:(grp + 1) * DS_HEAD_DIM, r0:r0 + sub_keys]

        _flash_chains(DS_HEADS, head_scores, head_values, subs, m_scr, l_scr, acc_scr, DS_QK_AHEAD)
        return carry

    lax.fori_loop(0, nck, att_chunk, 0)
    outs = [acc_scr[h] / l_scr[h][0:1] for h in range(DS_HEADS)]
    o_ref[...] = jnp.concatenate(outs, axis=0).T.astype(o_ref.dtype)


def _dsa_attention(qt, pr, vt3, iwt, b, s_len, tq):
    nq = s_len // tq
    n = b * s_len
    n_sel = min(TOPK_MAX, s_len // 4)
    n_bits = max(1, (s_len - 1).bit_length())
    kern = functools.partial(_dsa_kernel, tq=tq, n_sel=n_sel, n_bits=n_bits)
    seq = lambda blk: pl.BlockSpec((s_len, LANES), lambda bi, i: (bi, blk))
    qblk = lambda blk: pl.BlockSpec((4 * LANES, tq), lambda bi, i: (blk // 4, bi * nq + i))
    return pl.pallas_call(
        kern,
        grid=(b, nq),
        in_specs=[
            qblk(QT_IX), seq(PR_IXK),
            pl.BlockSpec((IDX_HEADS, tq), lambda bi, i: (0, bi * nq + i)),
            qblk(QT_DS), seq(PR_DSK), seq(PR_DSK + 1),
            pl.BlockSpec((s_len // tq, LANES, tq), lambda bi, i: (bi, VT_DS, 0)),
        ],
        out_specs=pl.BlockSpec((tq, DS_WIDTH), lambda bi, i: (bi * nq + i, 0)),
        out_shape=jax.ShapeDtypeStruct((n, DS_WIDTH), BF16),
        scratch_shapes=[pltpu.VMEM((nq, tq, tq), I32), pltpu.VMEM((nq, tq, tq), I16),
                        pltpu.VMEM((IDX_HEADS, LANES, tq), BF16), pltpu.VMEM((DS_HEADS, LANES, tq), BF16),
                        pltpu.VMEM((DS_HEADS, DS_HEAD_DIM, tq), F32),
                        pltpu.VMEM((DS_HEADS, SUBLANES, tq), F32), pltpu.VMEM((DS_HEADS, SUBLANES, tq), F32),
                        pltpu.VMEM((1, tq), I32)],
        compiler_params=pltpu.CompilerParams(dimension_semantics=("arbitrary",) * 2, vmem_limit_bytes=VMEM_LIMIT),
        name="dsa_attn",
    )(qt, pr, iwt, qt, pr, pr, vt3)


def _merge_kernel(oa_ref, ob_ref, gate_ref, x_ref, wa_ref, wb_ref, wo_ref, o_ref):
    ya = jnp.dot(oa_ref[...], wa_ref[...], preferred_element_type=F32)
    yb = jnp.dot(ob_ref[...], wb_ref[...], preferred_element_type=F32)
    mix = gate_ref[:, :D_MODEL] * ya + gate_ref[:, D_MODEL:] * yb
    o_ref[...] = x_ref[...] + jnp.dot(mix.astype(BF16), wo_ref[...], preferred_element_type=F32)


def _merge(oa, ob, gate, x2, wa, wb, wo, tm):
    n = x2.shape[0]
    full = lambda a: pl.BlockSpec(a.shape, lambda i: (0, 0))
    rows = lambda w: pl.BlockSpec((tm, w), lambda i: (i, 0))
    return pl.pallas_call(
        _merge_kernel,
        grid=(n // tm,),
        in_specs=[rows(DA_WIDTH), rows(DS_WIDTH), rows(2 * D_MODEL), rows(D_MODEL), full(wa), full(wb), full(wo)],
        out_specs=rows(D_MODEL),
        out_shape=jax.ShapeDtypeStruct((n, D_MODEL), F32),
        compiler_params=pltpu.CompilerParams(dimension_semantics=("arbitrary",), vmem_limit_bytes=VMEM_LIMIT),
        name="merge",
    )(oa, ob, gate, x2, wa, wb, wo)


def _norm_matmul_kernel(x_ref, g_ref, w_ref, o_ref):
    h = _rms(x_ref[...], g_ref[...]).astype(BF16)
    o_ref[...] = jnp.dot(h, w_ref[...], preferred_element_type=F32).astype(o_ref.dtype)


def _norm_matmul(x2, g, w, tm):
    n = x2.shape[0]
    return pl.pallas_call(
        _norm_matmul_kernel,
        grid=(n // tm,),
        in_specs=[pl.BlockSpec((tm, x2.shape[1]), lambda i: (i, 0)), pl.BlockSpec(g.shape, lambda i: (0, 0)),
                  pl.BlockSpec(w.shape, lambda i: (0, 0))],
        out_specs=pl.BlockSpec((tm, w.shape[1]), lambda i: (i, 0)),
        out_shape=jax.ShapeDtypeStruct((n, w.shape[1]), BF16),
        compiler_params=pltpu.CompilerParams(dimension_semantics=("arbitrary",), vmem_limit_bytes=VMEM_LIMIT),
        name="mem_kv_proj",
    )(x2, g, w)


def _mem_attn_kernel(x_ref, g_ref, wq_ref, kv_ref, wo_ref, o_ref):
    x = x_ref[...]
    hn = _rms(x, g_ref[...]).astype(BF16)
    q = jnp.dot(hn, wq_ref[...], preferred_element_type=F32).astype(BF16)
    scale = MEM_HEAD_DIM ** -0.5
    heads = []
    for h in range(MEM_HEADS):
        qh = q[:, h * MEM_HEAD_DIM:(h + 1) * MEM_HEAD_DIM]
        kh = kv_ref[:, h * MEM_HEAD_DIM:(h + 1) * MEM_HEAD_DIM]
        vh = kv_ref[:, MEM_WIDTH + h * MEM_HEAD_DIM:MEM_WIDTH + (h + 1) * MEM_HEAD_DIM]
        s = lax.dot_general(qh, kh, NT_DIMS, preferred_element_type=F32) * scale
        m = jnp.max(s, axis=1, keepdims=True)
        p = jnp.exp(s - m)
        l = jnp.sum(p, axis=1, keepdims=True)
        heads.append(jnp.dot(p.astype(BF16), vh, preferred_element_type=F32) / l)
    o = jnp.concatenate(heads, axis=1).astype(BF16)
    o_ref[...] = x + jnp.dot(o, wo_ref[...], preferred_element_type=F32)


def _mem_attn(x2, g, wq, kv, wo, b, s_len, m_tok, tm):
    nb = s_len // tm
    n = x2.shape[0]
    full = lambda a: pl.BlockSpec(a.shape, lambda bi, i: (0, 0))
    return pl.pallas_call(
        _mem_attn_kernel,
        grid=(b, nb),
        in_specs=[pl.BlockSpec((tm, D_MODEL), lambda bi, i: (bi * nb + i, 0)), full(g), full(wq),
                  pl.BlockSpec((m_tok, 2 * MEM_WIDTH), lambda bi, i: (bi, 0)), full(wo)],
        out_specs=pl.BlockSpec((tm, D_MODEL), lambda bi, i: (bi * nb + i, 0)),
        out_shape=jax.ShapeDtypeStruct((n, D_MODEL), F32),
        compiler_params=pltpu.CompilerParams(dimension_semantics=("arbitrary",) * 2, vmem_limit_bytes=VMEM_LIMIT),
        name="mem_attn",
    )(x2, g, wq, kv, wo)


N_SUB = 2 * PEER_HEADS
EXPERT_BITS = 14
SLOT_DEPTH = 4
ROUTE_UNROLL = 4


def _route_kernel(x_ref, g_ref, wq_ref, keys_ref, hf_ref, idx_ref, row_ref, gate_ref, sc_scr, ts_scr, ti_scr):
    hf = _rms(x_ref[...], g_ref[...])
    tm = hf.shape[0]
    hb = hf.astype(BF16)
    bits = pltpu.bitcast(hb.astype(F32), jnp.uint32)
    half_d = D_MODEL // 2
    for s in range(SUBLANES // 2):
        lo = bits[:, s * LANES:(s + 1) * LANES] >> 16
        hi = bits[:, half_d + s * LANES:half_d + (s + 1) * LANES] & jnp.uint32(0xFFFF0000)
        hf_ref[:, s, :] = lo | hi
        hf_ref[:, s + SUBLANES // 2, :] = lo | hi
    q = jnp.dot(hb, wq_ref[...], preferred_element_type=F32).astype(BF16)
    for g in range(N_SUB):
        blk = q[:, (g // 2) * LANES:(g // 2 + 1) * LANES]
        sc_scr[g] = lax.dot_general(keys_ref[g], blk, NT_DIMS, preferred_element_type=F32)

    n_slot = PEER_N_KEYS // SLOT_DEPTH
    slot_id = lax.broadcasted_iota(I32, (n_slot, tm), 0)

    def sub_topk(g, carry):
        lv = [sc_scr[g, d * n_slot:(d + 1) * n_slot, :] for d in range(SLOT_DEPTH)]
        li = [slot_id + d * n_slot for d in range(SLOT_DEPTH)]
        for x, y in ((0, 1), (2, 3), (0, 2), (1, 3), (1, 2)):
            swap = (lv[y] > lv[x]) | ((lv[y] == lv[x]) & (li[y] < li[x]))
            lv[x], lv[y] = jnp.where(swap, lv[y], lv[x]), jnp.where(swap, lv[x], lv[y])
            li[x], li[y] = jnp.where(swap, li[y], li[x]), jnp.where(swap, li[x], li[y])
        vals, ids = [], []
        for _ in range(PEER_TOPK):
            m = jnp.max(lv[0], axis=0, keepdims=True)
            idx = jnp.min(jnp.where(lv[0] == m, li[0], PEER_N_KEYS), axis=0, keepdims=True)
            vals.append(m)
            ids.append(idx)
            hit = li[0] == idx
            for d in range(SLOT_DEPTH - 1):
                lv[d] = jnp.where(hit, lv[d + 1], lv[d])
                li[d] = jnp.where(hit, li[d + 1], li[d])
            lv[-1] = jnp.where(hit, -jnp.inf, lv[-1])
        ts_scr[g] = jnp.concatenate(vals, axis=0)
        ti_scr[g] = jnp.concatenate(ids, axis=0)
        return carry

    def sub_topk_trip(p, carry):
        for u in range(ROUTE_UNROLL):
            sub_topk(ROUTE_UNROLL * p + u, carry)
        return carry

    lax.fori_loop(0, N_SUB // ROUTE_UNROLL, sub_topk_trip, 0)

    iota16 = lax.broadcasted_iota(I32, (PEER_TOPK, tm), 0)
    iota8 = lax.broadcasted_iota(I32, (SUBLANES, tm), 0)
    lead_rows = SUBLANES

    def head_topk(h, carry):
        s0, s1 = ts_scr[2 * h], ts_scr[2 * h + 1]
        i0, i1 = ti_scr[2 * h], ti_scr[2 * h + 1]
        cands, codes = [], []
        for a in range(lead_rows):
            n_j = PEER_TOPK if a == 0 else SUBLANES
            cands.append(s0[a:a + 1] + s1[:n_j])
            j_iota = iota16 if a == 0 else iota8
            codes.append(((a * PEER_TOPK + j_iota) << EXPERT_BITS) | (i0[a:a + 1] * PEER_N_KEYS + i1[:n_j]))
        cands.append(s0[lead_rows:] + s1[0:1])
        codes.append((((iota8 + lead_rows) * PEER_TOPK) << EXPERT_BITS) | (i0[lead_rows:] * PEER_N_KEYS + i1[0:1]))
        split = 4
        a_v, a_c = jnp.concatenate(cands[:split], axis=0), jnp.concatenate(codes[:split], axis=0)
        b_v, b_c = jnp.concatenate(cands[split:], axis=0), jnp.concatenate(codes[split:], axis=0)
        take_b = (b_v > a_v) | ((b_v == a_v) & (b_c < a_c))
        cur, cur_c = jnp.where(take_b, b_v, a_v), jnp.where(take_b, b_c, a_c)
        res, res_c = jnp.where(take_b, a_v, b_v), jnp.where(take_b, a_c, b_c)
        code_max = jnp.int32(2 ** 30)
        vals, ids = [], []
        for _ in range(PEER_TOPK):
            m = jnp.max(cur, axis=0, keepdims=True)
            best = jnp.min(jnp.where(cur == m, cur_c, code_max), axis=0, keepdims=True)
            vals.append(m)
            ids.append(best & (2 ** EXPERT_BITS - 1))
            hit = cur_c == best
            cur = jnp.where(hit, res, cur)
            cur_c = jnp.where(hit, res_c, cur_c)
            res = jnp.where(hit, -jnp.inf, res)
        best_s = jnp.concatenate(vals, axis=0)
        e = jnp.exp(best_s - best_s[0:1])
        gate_ref[h] = e / jnp.sum(e, axis=0, keepdims=True)
        best_i = jnp.concatenate(ids, axis=0)
        idx_ref[h] = best_i
        row_ref[h] = (best_i >> 1) * SUBLANES
        return carry

    def head_topk_trip(p, carry):
        for u in range(ROUTE_UNROLL):
            head_topk(ROUTE_UNROLL * p + u, carry)
        return carry

    lax.fori_loop(0, PEER_HEADS // ROUTE_UNROLL, head_topk_trip, 0)


def _route(x2, g, wq, keys_p, tm):
    n = x2.shape[0]
    return pl.pallas_call(
        _route_kernel,
        grid=(n // tm,),
        in_specs=[pl.BlockSpec((tm, D_MODEL), lambda i: (i, 0)), pl.BlockSpec(g.shape, lambda i: (0, 0)),
                  pl.BlockSpec(wq.shape, lambda i: (0, 0)), pl.BlockSpec(keys_p.shape, lambda i: (0, 0, 0))],
        out_specs=[pl.BlockSpec((tm, SUBLANES, LANES), lambda i: (i, 0, 0))]
        + [pl.BlockSpec((PEER_HEADS, PEER_TOPK, tm), lambda i: (0, 0, i))] * 3,
        out_shape=[jax.ShapeDtypeStruct((n, SUBLANES, LANES), jnp.uint32),
                   jax.ShapeDtypeStruct((PEER_HEADS, PEER_TOPK, n), I32),
                   jax.ShapeDtypeStruct((PEER_HEADS, PEER_TOPK, n), I32),
                   jax.ShapeDtypeStruct((PEER_HEADS, PEER_TOPK, n), F32)],
        scratch_shapes=[pltpu.VMEM((N_SUB, PEER_N_KEYS, tm), F32), pltpu.VMEM((N_SUB, PEER_TOPK, tm), F32),
                        pltpu.VMEM((N_SUB, PEER_TOPK, tm), I32)],
        compiler_params=pltpu.CompilerParams(dimension_semantics=("arbitrary",), vmem_limit_bytes=VMEM_LIMIT),
        name="peer_route",
    )(x2, g, wq, keys_p)


TILE_ROWS = 2 * SUBLANES
PAIRS_PER_KTILE = 16
TOK_UNROLL = 16
SUM_UNROLL = 16


def _pack_table(t):
    e = t.shape[0]
    bits = lax.bitcast_convert_type(t.astype(BF16), jnp.uint16).astype(jnp.uint32)
    words = bits[:, :D_MODEL // 2] | (bits[:, D_MODEL // 2:] << 16)
    return words.reshape(e * (SUBLANES // 2), LANES)


def _gather_tiles(off_ref, tab_ref, u):
    tiles = []
    for k in range(PEER_SEL):
        start = pl.multiple_of(off_ref[u * PEER_SEL + k], SUBLANES)
        tiles.append(pltpu.bitcast(tab_ref[pl.ds(start, SUBLANES), :], BF16))
    return tiles


def _for_trips(rows_hbm, bufs, sems, tok0, unroll, n_trip, body):
    assert n_trip % 2 == 0
    words = unroll * PEER_SEL

    def fetch(trip, par):
        src = rows_hbm.at[pl.ds(pl.multiple_of((tok0 + trip * unroll) * PEER_SEL, words), words)]
        return pltpu.make_async_copy(src, bufs[par], sems.at[par])

    fetch(0, 0).start()
    fetch(1, 1).start()

    def pair(p, carry):
        for par in range(2):
            trip = 2 * p + par
            fetch(trip, par).wait()
            body(trip, bufs[par], par)
            fetch(jnp.minimum(trip + 2, n_trip - 2 + par), par).start()
        return carry

    lax.fori_loop(0, n_trip // 2, pair, 0)
    for par in range(2):
        fetch(n_trip - 2 + par, par).wait()


def _peer_dot_kernel(rows_hbm, tab_ref, h_ref, gate_ref, idxv_ref, sel_ref, ce_ref, co_ref,
                     ye_scr, yo_scr, ze_scr, zo_scr, off_a, off_b, sems, *, tt):
    ones = jnp.ones((SUBLANES, LANES), BF16)
    half_rows = PAIRS_PER_KTILE
    n_trip = tt // TOK_UNROLL

    def row_sums(i, slot, off_ref):
        for u in range(TOK_UNROLL):
            t = i * TOK_UNROLL + u
            hp = pltpu.bitcast(h_ref[t], BF16)
            prods = [tile * hp for tile in _gather_tiles(off_ref, tab_ref, u)]
            for kt in range(PEER_SEL // PAIRS_PER_KTILE):
                stack = jnp.concatenate(prods[kt * PAIRS_PER_KTILE:(kt + 1) * PAIRS_PER_KTILE], axis=0)
                y = jnp.dot(sel_ref[...], stack, preferred_element_type=F32)
                r0 = u * PEER_SEL + kt * half_rows
                ye_scr[slot, r0:r0 + half_rows, :] = y[:half_rows].astype(BF16)
                yo_scr[slot, r0:r0 + half_rows, :] = y[half_rows:].astype(BF16)

    def lane_sums(i, slot):
        ze = lax.dot_general(ones, ye_scr[slot], NT_DIMS, preferred_element_type=F32)
        zo = lax.dot_general(ones, yo_scr[slot], NT_DIMS, preferred_element_type=F32)
        for u in range(TOK_UNROLL):
            t = i * TOK_UNROLL + u
            ze_scr[pl.ds(t, 1), :] = ze[0:1, u * PEER_SEL:(u + 1) * PEER_SEL]
            zo_scr[pl.ds(t, 1), :] = zo[0:1, u * PEER_SEL:(u + 1) * PEER_SEL]

    def trip(i, off_ref, par):
        lane_sums(jnp.maximum(i - 1, 0), 1 - par)
        row_sums(i, par, off_ref)

    ye_scr[1] = jnp.zeros(ye_scr.shape[1:], ye_scr.dtype)
    yo_scr[1] = jnp.zeros(yo_scr.shape[1:], yo_scr.dtype)
    _for_trips(rows_hbm, (off_a, off_b), sems, pl.program_id(0) * tt, TOK_UNROLL, n_trip, trip)
    lane_sums(n_trip - 1, (n_trip - 1) & 1)
    even = (idxv_ref[...] & 1) == 0
    a = jnp.where(even, ze_scr[...], zo_scr[...])
    c = 0.5 * a * (1.0 + lax.erf(a * (2.0 ** -0.5))) * gate_ref[...]
    ce_ref[...] = jnp.where(even, c, 0.0)
    co_ref[...] = jnp.where(even, 0.0, c)


def _peer_dot(rows_flat, tab, hpk, gate, idx2, sel, tt):
    n = hpk.shape[0]
    kern = functools.partial(_peer_dot_kernel, tt=tt)
    rows = pl.BlockSpec((tt, PEER_SEL), lambda i: (i, 0))
    return pl.pallas_call(
        kern,
        grid=(n // tt,),
        in_specs=[pl.BlockSpec(memory_space=pl.ANY),
                  pl.BlockSpec(memory_space=pltpu.VMEM),
                  pl.BlockSpec((tt, SUBLANES, LANES), lambda i: (i, 0, 0)),
                  rows, rows, pl.BlockSpec(sel.shape, lambda i: (0, 0))],
        out_specs=[rows, rows],
        out_shape=[jax.ShapeDtypeStruct((n, PEER_SEL), F32), jax.ShapeDtypeStruct((n, PEER_SEL), F32)],
        scratch_shapes=[pltpu.VMEM((2, TOK_UNROLL * PEER_SEL, LANES), BF16)] * 2
        + [pltpu.VMEM((tt, PEER_SEL), F32)] * 2
        + [pltpu.SMEM((TOK_UNROLL * PEER_SEL,), I32)] * 2 + [pltpu.SemaphoreType.DMA((2,))],
        compiler_params=pltpu.CompilerParams(dimension_semantics=("arbitrary",), vmem_limit_bytes=VMEM_LIMIT),
        name="peer_dot",
    )(rows_flat, tab, hpk, gate, idx2, sel)


def _peer_sum_kernel(rows_hbm, ce_ref, co_ref, spread_ref, tab_ref, o_ref, m1_scr, m2_scr, off_a, off_b, sems,
                     *, tt):
    width = PEER_SEL * TILE_ROWS
    cc = jnp.concatenate([ce_ref[...], co_ref[...]], axis=1)
    c1 = cc.astype(BF16)
    c2 = (cc - c1.astype(F32)).astype(BF16)
    m1_scr[...] = jnp.dot(c1, spread_ref[...], preferred_element_type=F32)
    m2_scr[...] = jnp.dot(c2, spread_ref[...], preferred_element_type=F32)
    row = lax.broadcasted_iota(I32, (SUBLANES, width), 0)
    lane = lax.broadcasted_iota(I32, (SUBLANES, width), 1)
    half = SUBLANES // 2
    on_row = (lane & (SUBLANES - 1)) == 2 * (row % half) + row // half

    def token(t, off_ref, u):
        w = jnp.concatenate(_gather_tiles(off_ref, tab_ref, u), axis=0)
        lhs = []
        for m_scr in (m1_scr, m2_scr):
            coef = jnp.broadcast_to(m_scr[pl.ds(t, 1), :], (SUBLANES, width))
            lhs.append(jnp.where(on_row, coef, 0.0).astype(BF16))
        res = jnp.dot(jnp.concatenate(lhs, axis=0), w, preferred_element_type=F32)
        o_ref[t] = res[:SUBLANES] + res[SUBLANES:]

    def trip(i, off_ref, par):
        for u in range(SUM_UNROLL):
            token(i * SUM_UNROLL + u, off_ref, u)

    _for_trips(rows_hbm, (off_a, off_b), sems, pl.program_id(0) * tt, SUM_UNROLL, tt // SUM_UNROLL, trip)


def _peer_sum(rows_flat, ce, co, spread, tab, tt):
    n = ce.shape[0]
    kern = functools.partial(_peer_sum_kernel, tt=tt)
    rows = pl.BlockSpec((tt, PEER_SEL), lambda i: (i, 0))
    width = PEER_SEL * TILE_ROWS
    return pl.pallas_call(
        kern,
        grid=(n // tt,),
        in_specs=[pl.BlockSpec(memory_space=pl.ANY), rows, rows,
                  pl.BlockSpec(spread.shape, lambda i: (0, 0)), pl.BlockSpec(memory_space=pltpu.VMEM)],
        out_specs=pl.BlockSpec((tt, SUBLANES, LANES), lambda i: (i, 0, 0)),
        out_shape=jax.ShapeDtypeStruct((n, SUBLANES, LANES), F32),
        scratch_shapes=[pltpu.VMEM((tt, width), F32), pltpu.VMEM((tt, width), F32)]
        + [pltpu.SMEM((SUM_UNROLL * PEER_SEL,), I32)] * 2 + [pltpu.SemaphoreType.DMA((2,))],
        compiler_params=pltpu.CompilerParams(dimension_semantics=("arbitrary",), vmem_limit_bytes=VMEM_LIMIT),
        name="peer_sum",
    )(rows_flat, ce, co, spread, tab)


def _final_kernel(x_ref, p_ref, g_ref, o_ref):
    peer = jnp.concatenate([p_ref[:, s, :] for s in range(SUBLANES)], axis=1)
    o_ref[...] = _rms(x_ref[...] + peer, g_ref[...])


def _final(x2, peer3, g, tm):
    n = x2.shape[0]
    return pl.pallas_call(
        _final_kernel,
        grid=(n // tm,),
        in_specs=[pl.BlockSpec((tm, D_MODEL), lambda i: (i, 0)),
                  pl.BlockSpec((tm, SUBLANES, LANES), lambda i: (i, 0, 0)),
                  pl.BlockSpec(g.shape, lambda i: (0, 0))],
        out_specs=pl.BlockSpec((tm, D_MODEL), lambda i: (i, 0)),
        out_shape=jax.ShapeDtypeStruct((n, D_MODEL), F32),
        compiler_params=pltpu.CompilerParams(dimension_semantics=("arbitrary",), vmem_limit_bytes=VMEM_LIMIT),
        name="final_norm",
    )(x2, peer3, g)


ROW_BLOCK = 256
PEER_BLOCK = 256


def _tile(n, pref):
    t = pref
    while n % t:
        t //= 2
    return t


def _layer(l, x2, mem2, pos, invf, b, s_len, m_tok, norm_mix_g, w_in, da_lambda, da_subln_g, w_branch_a,
           w_branch_b, gate_bias, w_out, norm_mem_g, mem_kv_norm_g, w_mem_q, w_mem_kv, w_mem_o, norm_ffn_g,
           peer_w_q, peer_sub_keys, peer_u, peer_v):
    n = b * s_len
    row2 = lambda v: v.reshape(1, -1)
    splits = [0]
    for c in COL_SIZES:
        splits.append(splits[-1] + c)
    da_q, da_k, da_v, ds_q, ds_k, ds_v, ix_q, ix_k, ix_w, gates = (
        w_in[l][:, splits[j]:splits[j + 1]] for j in range(len(COL_SIZES)))
    k0, k1 = ds_k[:, :DS_HEAD_DIM], ds_k[:, DS_HEAD_DIM:]
    w_keys = jnp.concatenate([da_k, k0, k0, k1, k1, ix_k, ix_k], axis=1).astype(BF16)
    w_q_t = jnp.concatenate([da_q, ds_q, ix_q], axis=1).T.astype(BF16)
    w_v_t = jnp.concatenate([da_v, ds_v], axis=1).T.astype(BF16)
    t_att = _tile(s_len, ROW_BLOCK)
    t_proj = t_att
    pr, gate, qt, vt3, iwt = _in_proj(x2, row2(norm_mix_g[l]), pos, invf, pos.reshape(1, n),
                                      invf[:, :ROPE_HALF].reshape(ROPE_HALF, 1), row2(gate_bias[l]), w_keys,
                                      gates.astype(BF16), w_q_t, w_v_t, ix_w.T.astype(BF16), t_proj, t_att)

    lam_init = 0.8 - 0.6 * math.exp(-0.3 * l)
    o_a = _diff_attention(da_lambda[l], qt, pr, vt3, da_subln_g[l].reshape(-1, 1), b, s_len, t_att, lam_init)
    o_b = _dsa_attention(qt, pr, vt3, iwt, b, s_len, t_att)
    x2 = _merge(o_a, o_b, gate, x2, w_branch_a[l].astype(BF16), w_branch_b[l].astype(BF16),
                w_out[l].astype(BF16), _tile(n, ROW_BLOCK))

    kv = _norm_matmul(mem2, row2(mem_kv_norm_g[l]), w_mem_kv[l].astype(BF16), _tile(mem2.shape[0], ROW_BLOCK))
    x2 = _mem_attn(x2, row2(norm_mem_g[l]), w_mem_q[l].astype(BF16), kv, w_mem_o[l].astype(BF16),
                   b, s_len, m_tok, _tile(s_len, ROW_BLOCK))

    sk = peer_sub_keys[l].reshape(N_SUB, PEER_N_KEYS, PEER_HALF)
    z = jnp.zeros_like(sk)
    keys_p = jnp.where((jnp.arange(N_SUB) % 2 == 0)[:, None, None],
                       jnp.concatenate([sk, z], axis=2), jnp.concatenate([z, sk], axis=2)).astype(BF16)
    hpk, idx_t, row_t, gate_t = _route(x2, row2(norm_ffn_g[l]), peer_w_q[l].astype(BF16), keys_p, _tile(n, ROW_BLOCK))
    idx2 = idx_t.reshape(PEER_SEL, n).T
    rows2 = row_t.reshape(PEER_SEL, n).T
    gate2 = gate_t.reshape(PEER_SEL, n).T
    tt = _tile(n, PEER_BLOCK)
    r32 = jnp.arange(2 * PAIRS_PER_KTILE)[:, None]
    c256 = jnp.arange(PAIRS_PER_KTILE * TILE_ROWS)[None, :]
    sel = ((c256 // TILE_ROWS == r32 % PAIRS_PER_KTILE)
           & ((c256 % TILE_ROWS) // SUBLANES == r32 // PAIRS_PER_KTILE)).astype(BF16)
    k256 = jnp.arange(2 * PEER_SEL)[:, None]
    c2048 = jnp.arange(PEER_SEL * TILE_ROWS)[None, :]
    spread = ((c2048 // TILE_ROWS == k256 % PEER_SEL)
              & ((c2048 % TILE_ROWS) // SUBLANES == k256 // PEER_SEL)).astype(BF16)
    rows_flat = rows2.reshape(-1)
    ce2, co2 = _peer_dot(rows_flat, _pack_table(peer_u[l]), hpk, gate2, idx2, sel, tt)
    peer3 = _peer_sum(rows_flat, ce2, co2, spread, _pack_table(peer_v[l]), tt)
    return x2, peer3


def kernel(x, mem, positions, norm_mix_g, w_in, da_lambda, da_subln_g, w_branch_a, w_branch_b, gate_bias, w_out, norm_mem_g, mem_kv_norm_g, w_mem_q, w_mem_kv, w_mem_o, norm_ffn_g, peer_w_q, peer_sub_keys, peer_u, peer_v, final_norm_g):
    b, s_len, d = x.shape
    m_tok = mem.shape[1]
    n = b * s_len
    depth = w_in.shape[0]
    x2 = x.reshape(n, d)
    mem2 = mem.reshape(b * m_tok, d)
    pos = positions.astype(F32).reshape(n, 1)
    inv_freq = ROPE_THETA ** (-(jnp.arange(ROPE_HALF, dtype=F32) * 2.0) / ROPE_DIM)
    invf = jnp.tile(inv_freq, LANES // ROPE_HALF).reshape(1, LANES)
    peer3 = None
    for l in range(depth):
        if peer3 is not None:
            x2 = x2 + peer3.reshape(n, d)
        x2, peer3 = _layer(l, x2, mem2, pos, invf, b, s_len, m_tok, norm_mix_g, w_in, da_lambda, da_subln_g,
                           w_branch_a, w_branch_b, gate_bias, w_out, norm_mem_g, mem_kv_norm_g, w_mem_q, w_mem_kv,
                           w_mem_o, norm_ffn_g, peer_w_q, peer_sub_keys, peer_u, peer_v)
    out = _final(x2, peer3, final_norm_g.reshape(1, d), _tile(n, ROW_BLOCK))
    return out.reshape(b, s_len, d)
```

```python
import functools
import math

import jax
import jax.numpy as jnp
from jax import lax
from jax.experimental import pallas as pl
from jax.experimental.pallas import tpu as pltpu

F32 = jnp.float32
BF16 = jnp.bfloat16
I32 = jnp.int32
I16 = jnp.int16

D_MODEL = 1024
EPS = 1e-6
ROPE_THETA = 500000.0
ROT_HEAD_DIM = 64
ROPE_DIM = 16
ROPE_HALF = ROPE_DIM // 2

DA_HEADS = 4
DA_QK_DIM = 64
DA_V_DIM = 128
DA_WIDTH = 512
DS_HEADS = 8
DS_KV_HEADS = 2
DS_HEAD_DIM = 64
DS_WIDTH = 512
IDX_HEADS = 8
TOPK_MAX = 256
MEM_HEADS = 4
MEM_HEAD_DIM = 128
MEM_WIDTH = 512
PEER_HEADS = 8
PEER_N_KEYS = 128
PEER_HALF = 64
PEER_TOPK = 16
PEER_SEL = PEER_HEADS * PEER_TOPK

COL_SIZES = (512, 512, 512, 512, 128, 128, 512, 64, 8, 2048)

LANES = 128
SUBLANES = 8
NEG_BIG = -1e30
INT_MIN = -(2 ** 31)
HALF_BIAS = 2 ** 15
HALF_ROWS = 2 * SUBLANES
VMEM_LIMIT = 56 * 1024 * 1024

NT_DIMS = (((1,), (1,)), ((), ()))
PROJ_COLS = 512


def _rms(x, g):
    var = jnp.mean(x * x, axis=-1, keepdims=True)
    return x * lax.rsqrt(var + EPS) * g


def _rope_rows(y, cos8, sin8):
    pieces = []
    for r0 in range(0, y.shape[0], ROT_HEAD_DIM):
        t1 = y[r0:r0 + ROPE_HALF]
        t2 = y[r0 + ROPE_HALF:r0 + ROPE_DIM]
        pieces += [t1 * cos8 - t2 * sin8, t2 * cos8 + t1 * sin8, y[r0 + ROPE_DIM:r0 + ROT_HEAD_DIM]]
    return jnp.concatenate(pieces, axis=0)


def _in_proj_kernel(x_ref, g_ref, pos_ref, invf_ref, pos_t_ref, invf8_ref, bias_ref, wr_ref, wg_ref,
                    wqt_ref, wvt_ref, wwt_ref, pr_ref, gate_ref, qt_ref, vt_ref, iwt_ref):
    h = _rms(x_ref[...], g_ref[...]).astype(BF16)
    tm = h.shape[0]
    ang_t = invf8_ref[...] * pos_t_ref[...]
    cos8 = jnp.cos(ang_t)
    sin8 = jnp.sin(ang_t)
    row_chunk = 2 * LANES
    for r0 in range(0, wqt_ref.shape[0], row_chunk):
        y = lax.dot_general(wqt_ref[r0:r0 + row_chunk, :], h, NT_DIMS, preferred_element_type=F32)
        qt_ref[r0:r0 + row_chunk, :] = _rope_rows(y, cos8, sin8).astype(qt_ref.dtype)
    for r0 in range(0, wvt_ref.shape[0], row_chunk):
        r1 = min(r0 + row_chunk, wvt_ref.shape[0])
        y = lax.dot_general(wvt_ref[r0:r1, :], h, NT_DIMS, preferred_element_type=F32)
        chunk = vt_ref.shape[2]
        for j in range(vt_ref.shape[0]):
            vt_ref[j, r0:r1, :] = y[:, j * chunk:(j + 1) * chunk].astype(vt_ref.dtype)
    iwt_ref[...] = lax.dot_general(wwt_ref[...], h, NT_DIMS, preferred_element_type=F32)
    ang = pos_ref[...] * invf_ref[...]
    cos = jnp.cos(ang)
    sin = jnp.sin(ang)
    lane = lax.broadcasted_iota(I32, (tm, LANES), 1) % ROT_HEAD_DIM
    c_t = jnp.where(lane < ROPE_DIM, cos, 1.0)
    s_lo = jnp.where(lane < ROPE_HALF, -sin, 0.0)
    s_hi = jnp.where((lane >= ROPE_HALF) & (lane < ROPE_DIM), sin, 0.0)
    n_r = wr_ref.shape[1]
    for c0 in range(0, n_r, PROJ_COLS):
        c1 = min(c0 + PROJ_COLS, n_r)
        w = c1 - c0
        y = jnp.dot(h, wr_ref[:, c0:c1], preferred_element_type=F32)
        reps = w // LANES
        ct = jnp.concatenate([c_t] * reps, axis=1)
        sl = jnp.concatenate([s_lo] * reps, axis=1)
        sh = jnp.concatenate([s_hi] * reps, axis=1)
        y = y * ct + pltpu.roll(y, w - ROPE_HALF, 1) * sl + pltpu.roll(y, ROPE_HALF, 1) * sh
        pr_ref[:, c0:c1] = y.astype(pr_ref.dtype)
    n_g = wg_ref.shape[1]
    for c0 in range(0, n_g, PROJ_COLS):
        y = jnp.dot(h, wg_ref[:, c0:c0 + PROJ_COLS], preferred_element_type=F32)
        gate_ref[:, c0:c0 + PROJ_COLS] = jax.nn.sigmoid(y + bias_ref[:, c0:c0 + PROJ_COLS])


def _in_proj(x2, g, pos, invf, pos_t, invf8, bias, wr, wg, wqt, wvt, wwt, tm, chunk):
    n = x2.shape[0]
    per_step = tm // chunk
    full = lambda a: pl.BlockSpec(a.shape, lambda i: (0, 0))
    rows = lambda w: pl.BlockSpec((tm, w), lambda i: (i, 0))
    cols = lambda r: pl.BlockSpec((r, tm), lambda i: (0, i))
    return pl.pallas_call(
        _in_proj_kernel,
        grid=(n // tm,),
        in_specs=[rows(D_MODEL), full(g), rows(1), full(invf), cols(1), full(invf8), full(bias), full(wr), full(wg),
                  full(wqt), full(wvt), full(wwt)],
        out_specs=[rows(wr.shape[1]), rows(wg.shape[1]), cols(wqt.shape[0]),
                   pl.BlockSpec((per_step, wvt.shape[0], chunk), lambda i: (i, 0, 0)), cols(wwt.shape[0])],
        out_shape=[jax.ShapeDtypeStruct((n, wr.shape[1]), BF16), jax.ShapeDtypeStruct((n, wg.shape[1]), F32),
                   jax.ShapeDtypeStruct((wqt.shape[0], n), BF16),
                   jax.ShapeDtypeStruct((n // chunk, wvt.shape[0], chunk), BF16),
                   jax.ShapeDtypeStruct((wwt.shape[0], n), F32)],
        compiler_params=pltpu.CompilerParams(dimension_semantics=("arbitrary",), vmem_limit_bytes=VMEM_LIMIT),
        name="in_proj",
    )(x2, g, pos, invf, pos_t, invf8, bias, wr, wg, wqt, wvt, wwt)


QT_DA = 0
QT_DS = 4
QT_IX = 8
VT_DA = 0
VT_DS = 4
PR_DAK = 0
PR_DSK = 4
PR_IXK = 6


DA_SUB = 128
DS_SUB = 64
DA_QK_AHEAD = 1
DS_QK_AHEAD = 3


def _zero_after(x):
    bits = pltpu.bitcast(x, jnp.uint32)
    return pltpu.bitcast((bits >> 16) >> 16, F32)


def _flash_chains(n_chains, score_fn, value_fn, subs, m_scr, l_scr, acc_scr, ahead):
    pending = [score_fn(c) for c in range(ahead)]
    for c in range(n_chains):
        scores = pending.pop(0)
        m = m_scr[c][0:1]
        if c + ahead < n_chains:
            pending.append(score_fn(c + ahead))
            m = m + _zero_after(pending[-1][0][0:1])
        m_new = m
        for s in scores:
            m_new = jnp.maximum(m_new, jnp.max(s, axis=0, keepdims=True))
        alpha = jnp.exp(m - m_new)
        l = alpha * l_scr[c][0:1]
        a = alpha * acc_scr[c]
        for r0, s in zip(subs, scores):
            p = jnp.exp(s - m_new)
            l = l + jnp.sum(p, axis=0, keepdims=True)
            v = value_fn(c, r0)
            a = a + jnp.dot(v, p.astype(v.dtype), preferred_element_type=F32)
        m_scr[c] = jnp.broadcast_to(m_new, m_scr.shape[1:])
        l_scr[c] = jnp.broadcast_to(l, l_scr.shape[1:])
        acc_scr[c] = a


def _da_kernel(lam_ref, qt_ref, k_ref, vt_ref, g_ref, o_ref, q_scr, acc_scr, m_scr, l_scr, *, tq, lam_init):
    tk = tq
    sub_keys = DA_SUB
    i = pl.program_id(1)
    n_chains = 2 * DA_HEADS
    frow = lax.broadcasted_iota(I32, (LANES, tq), 0)
    scale = jnp.asarray(DA_QK_DIM ** -0.5, BF16)
    for h in range(DA_HEADS):
        blk = qt_ref[h * LANES:(h + 1) * LANES, :] * scale
        zero = jnp.zeros_like(blk)
        q_scr[2 * h] = jnp.where(frow < DA_QK_DIM, blk, zero)
        q_scr[2 * h + 1] = jnp.where(frow >= DA_QK_DIM, blk, zero)
    acc_scr[...] = jnp.zeros(acc_scr.shape, F32)
    m_scr[...] = jnp.full(m_scr.shape, NEG_BIG, F32)
    l_scr[...] = jnp.zeros(l_scr.shape, F32)
    kpos0 = lax.broadcasted_iota(I32, (sub_keys, tq), 0)
    qpos = i * tq + lax.broadcasted_iota(I32, (sub_keys, tq), 1)
    subs = range(0, tk, sub_keys)

    def chunk(j, masked):
        start = pl.multiple_of(j * tk, tk)

        def scores(c):
            head = c // 2
            out = []
            for r0 in subs:
                k = k_ref[pl.ds(start + r0, sub_keys), head * LANES:(head + 1) * LANES]
                s = jnp.dot(k, q_scr[c], preferred_element_type=F32)
                if masked:
                    s = jnp.where(kpos0 + (j * tk + r0) <= qpos, s, NEG_BIG)
                out.append(s)
            return out

        def values(c, r0):
            head = c // 2
            return vt_ref[j, head * LANES:(head + 1) * LANES, r0:r0 + sub_keys]

        _flash_chains(n_chains, scores, values, subs, m_scr, l_scr, acc_scr, DA_QK_AHEAD)

    def full_chunk(j, carry):
        chunk(j, False)
        return carry

    lax.fori_loop(0, i, full_chunk, 0)
    chunk(i, True)

    lp = lam_ref[...]
    lam = (jnp.exp(jnp.sum(lp[0:1] * lp[1:2], axis=1, keepdims=True))
           - jnp.exp(jnp.sum(lp[2:3] * lp[3:4], axis=1, keepdims=True)) + lam_init)
    for h in range(DA_HEADS):
        o = acc_scr[2 * h] / l_scr[2 * h][0:1] - lam * (acc_scr[2 * h + 1] / l_scr[2 * h + 1][0:1])
        var = jnp.mean(o * o, axis=0, keepdims=True)
        o = o * lax.rsqrt(var + EPS) * g_ref[...] * (1.0 - lam_init)
        o_ref[:, h * LANES:(h + 1) * LANES] = o.T.astype(o_ref.dtype)


def _diff_attention(lam_p, qt, pr, vt3, subln_g, b, s_len, tq, lam_init):
    nq = s_len // tq
    n = b * s_len
    n_chains = 2 * DA_HEADS
    kern = functools.partial(_da_kernel, tq=tq, lam_init=lam_init)
    return pl.pallas_call(
        kern,
        grid=(b, nq),
        in_specs=[
            pl.BlockSpec(lam_p.shape, lambda bi, i: (0, 0)),
            pl.BlockSpec((DA_HEADS * LANES, tq), lambda bi, i: (QT_DA // DA_HEADS, bi * nq + i)),
            pl.BlockSpec((s_len, DA_HEADS * LANES), lambda bi, i: (bi, PR_DAK // DA_HEADS)),
            pl.BlockSpec((s_len // tq, DA_HEADS * LANES, tq), lambda bi, i: (bi, VT_DA // DA_HEADS, 0)),
            pl.BlockSpec(subln_g.shape, lambda bi, i: (0, 0)),
        ],
        out_specs=pl.BlockSpec((tq, DA_WIDTH), lambda bi, i: (bi * nq + i, 0)),
        out_shape=jax.ShapeDtypeStruct((n, DA_WIDTH), BF16),
        scratch_shapes=[pltpu.VMEM((n_chains, LANES, tq), BF16), pltpu.VMEM((n_chains, DA_V_DIM, tq), F32),
                        pltpu.VMEM((n_chains, SUBLANES, tq), F32), pltpu.VMEM((n_chains, SUBLANES, tq), F32)],
        compiler_params=pltpu.CompilerParams(dimension_semantics=("arbitrary",) * 2, vmem_limit_bytes=VMEM_LIMIT),
        name="diff_attn",
    )(lam_p, qt, pr, vt3, subln_g)


def _dsa_kernel(qit_ref, ki_ref, wt_ref, qt_ref, k0_ref, k1_ref, vt_ref, o_ref,
                keys_scr, half_scr, qi_scr, q_scr, acc_scr, m_scr, l_scr, cut_scr, *, tq, n_sel, n_bits):
    tk = tq
    sub_keys = DS_SUB
    i = pl.program_id(1)
    nck = i + 1
    kpos0 = lax.broadcasted_iota(I32, (tk, tq), 0)
    qpos = i * tq + lax.broadcasted_iota(I32, (tk, tq), 1)
    frow = lax.broadcasted_iota(I32, (LANES, tq), 0)
    per_grp = DS_HEADS // DS_KV_HEADS

    scale = jnp.asarray(DS_HEAD_DIM ** -0.5, BF16)
    for h in range(DS_HEADS):
        keep = (frow < DS_HEAD_DIM) if h % 2 == 0 else (frow >= DS_HEAD_DIM)
        rows = slice((h // 2) * LANES, (h // 2 + 1) * LANES)
        blk = qit_ref[rows, :]
        qi_scr[h] = jnp.where(keep, blk, jnp.zeros_like(blk))
        blk = qt_ref[rows, :] * scale
        q_scr[h] = jnp.where(keep, blk, jnp.zeros_like(blk))

    kpos_sub = lax.broadcasted_iota(I32, (sub_keys, tq), 0)
    qpos_sub = i * tq + lax.broadcasted_iota(I32, (sub_keys, tq), 1)

    def score_chunk(c, carry):
        start = pl.multiple_of(c * tk, tk)
        for r0 in range(0, tk, sub_keys):
            kc = ki_ref[pl.ds(start + r0, sub_keys), :]
            acc = jnp.zeros((sub_keys, tq), F32)
            for h in range(IDX_HEADS):
                s = jnp.dot(kc, qi_scr[h], preferred_element_type=F32)
                acc = acc + wt_ref[h:h + 1, :] * jnp.maximum(s, 0.0)
            bits = pltpu.bitcast(acc, I32)
            key = bits ^ ((bits >> 31) & 0x7FFFFFFF)
            key = jnp.where(acc == 0.0, 0, key)
            key = jnp.where(kpos_sub + (c * tk + r0) <= qpos_sub, key, INT_MIN)
            keys_scr[c, r0:r0 + sub_keys, :] = key
            half_scr[c, r0:r0 + sub_keys, :] = (key >> 16).astype(I16)
        return carry

    lax.fori_loop(0, nck, score_chunk, 0)

    def count(pred):
        def body(c, acc):
            hit = pred(keys_scr[c], kpos0 + c * tk)
            return acc + jnp.sum(hit.reshape(tk // SUBLANES, SUBLANES, tq), axis=0)
        acc = lax.fori_loop(0, nck, body, jnp.zeros((SUBLANES, tq), I32))
        return jnp.sum(acc, axis=0, keepdims=True)

    def count_half(pred):
        def body(c, acc):
            hit = pred(half_scr[c])
            for r0 in range(0, tk, HALF_ROWS):
                acc = acc + hit[r0:r0 + HALF_ROWS]
            return acc
        acc = lax.fori_loop(0, nck, body, jnp.zeros((HALF_ROWS, tq), I16))
        return jnp.sum(acc.astype(I32), axis=0, keepdims=True)

    one16, zero16 = jnp.int16(1), jnp.int16(0)

    def bisect_half(need):
        def bit(b, ans_u):
            cand_u = ans_u | jnp.left_shift(jnp.int32(1), 15 - b)
            cand = (cand_u - HALF_BIAS).astype(I16)
            cnt = count_half(lambda x: jnp.where(x >= cand, one16, zero16))
            return jnp.where(cnt >= need, cand_u, ans_u)
        return lax.fori_loop(0, 16, bit, jnp.zeros((1, tq), I32)) - HALF_BIAS

    hi_t = bisect_half(n_sel)
    hi_t16 = hi_t.astype(I16)
    n_hi_gt = count_half(lambda x: jnp.where(x > hi_t16, one16, zero16))

    def low_chunk(c, carry):
        lo = ((keys_scr[c] & 0xFFFF) - HALF_BIAS).astype(I16)
        half_scr[c] = jnp.where(half_scr[c] == hi_t16, lo, jnp.int16(-HALF_BIAS))
        return carry

    lax.fori_loop(0, nck, low_chunk, 0)
    lo_t = bisect_half(n_sel - n_hi_gt)
    thr = hi_t * (2 * HALF_BIAS) + (lo_t + HALF_BIAS)
    n_ge = count(lambda k, kpos: jnp.where(k >= thr, 1, 0))
    below_all = thr == INT_MIN
    cut_scr[...] = jnp.where(below_all, -1, (1 << n_bits) - 1)
    excess = jnp.max(jnp.where(below_all, 0, n_ge - n_sel))

    @pl.when(excess > 0)
    def _():
        n_gt = count(lambda k, kpos: jnp.where(k > thr, 1, 0))
        need = n_sel - n_gt

        def tie_bit(t, cut):
            cand = cut | jnp.left_shift(jnp.int32(1), n_bits - 1 - t)
            cnt = count(lambda k, kpos: jnp.where(k == thr, jnp.where(kpos < cand, 1, 0), 0))
            return jnp.where(cnt < need, cand, cut)

        cut = lax.fori_loop(0, n_bits, tie_bit, jnp.zeros((1, tq), I32))
        cut_scr[...] = jnp.where(below_all, -1, cut)

    cut = cut_scr[...]

    def bias_chunk(c, carry):
        k = keys_scr[c]
        sel = (k > thr) | ((k == thr) & (kpos0 + c * tk <= cut))
        keys_scr[c] = pltpu.bitcast(jnp.where(sel, 0.0, NEG_BIG).astype(F32), I32)
        return carry

    lax.fori_loop(0, nck, bias_chunk, 0)

    acc_scr[...] = jnp.zeros(acc_scr.shape, F32)
    m_scr[...] = jnp.full(m_scr.shape, NEG_BIG, F32)
    l_scr[...] = jnp.zeros(l_scr.shape, F32)

    def att_chunk(c, carry):
        start = pl.multiple_of(c * tk, tk)
        subs = range(0, tk, sub_keys)

        def head_scores(h):
            k_ref = k0_ref if h // per_grp == 0 else k1_ref
            out = []
            for r0 in subs:
                s = jnp.dot(k_ref[pl.ds(start + r0, sub_keys), :], q_scr[h], preferred_element_type=F32)
                out.append(s + pltpu.bitcast(keys_scr[c, r0:r0 + sub_keys, :], F32))
            return out

        def head_values(h, r0):
            grp = h // per_grp
            return vt_ref[c, grp * DS_HEAD_DIM:(grp + 1) * DS_HEAD_DIM, r0:r0 + sub_keys]

        _flash_chains(DS_HEADS, head_scores, head_values, subs, m_scr, l_scr, acc_scr, DS_QK_AHEAD)
        return carry

    lax.fori_loop(0, nck, att_chunk, 0)
    outs = [acc_scr[h] / l_scr[h][0:1] for h in range(DS_HEADS)]
    o_ref[...] = jnp.concatenate(outs, axis=0).T.astype(o_ref.dtype)


def _dsa_attention(qt, pr, vt3, iwt, b, s_len, tq):
    nq = s_len // tq
    n = b * s_len
    n_sel = min(TOPK_MAX, s_len // 4)
    n_bits = max(1, (s_len - 1).bit_length())
    kern = functools.partial(_dsa_kernel, tq=tq, n_sel=n_sel, n_bits=n_bits)
    seq = lambda blk: pl.BlockSpec((s_len, LANES), lambda bi, i: (bi, blk))
    qblk = lambda blk: pl.BlockSpec((4 * LANES, tq), lambda bi, i: (blk // 4, bi * nq + i))
    return pl.pallas_call(
        kern,
        grid=(b, nq),
        in_specs=[
            qblk(QT_IX), seq(PR_IXK),
            pl.BlockSpec((IDX_HEADS, tq), lambda bi, i: (0, bi * nq + i)),
            qblk(QT_DS), seq(PR_DSK), seq(PR_DSK + 1),
            pl.BlockSpec((s_len // tq, LANES, tq), lambda bi, i: (bi, VT_DS, 0)),
        ],
        out_specs=pl.BlockSpec((tq, DS_WIDTH), lambda bi, i: (bi * nq + i, 0)),
        out_shape=jax.ShapeDtypeStruct((n, DS_WIDTH), BF16),
        scratch_shapes=[pltpu.VMEM((nq, tq, tq), I32), pltpu.VMEM((nq, tq, tq), I16),
                        pltpu.VMEM((IDX_HEADS, LANES, tq), BF16), pltpu.VMEM((DS_HEADS, LANES, tq), BF16),
                        pltpu.VMEM((DS_HEADS, DS_HEAD_DIM, tq), F32),
                        pltpu.VMEM((DS_HEADS, SUBLANES, tq), F32), pltpu.VMEM((DS_HEADS, SUBLANES, tq), F32),
                        pltpu.VMEM((1, tq), I32)],
        compiler_params=pltpu.CompilerParams(dimension_semantics=("arbitrary",) * 2, vmem_limit_bytes=VMEM_LIMIT),
        name="dsa_attn",
    )(qt, pr, iwt, qt, pr, pr, vt3)


def _merge_kernel(oa_ref, ob_ref, gate_ref, x_ref, wa_ref, wb_ref, wo_ref, o_ref):
    ya = jnp.dot(oa_ref[...], wa_ref[...], preferred_element_type=F32)
    yb = jnp.dot(ob_ref[...], wb_ref[...], preferred_element_type=F32)
    mix = gate_ref[:, :D_MODEL] * ya + gate_ref[:, D_MODEL:] * yb
    o_ref[...] = x_ref[...] + jnp.dot(mix.astype(BF16), wo_ref[...], preferred_element_type=F32)


def _merge(oa, ob, gate, x2, wa, wb, wo, tm):
    n = x2.shape[0]
    full = lambda a: pl.BlockSpec(a.shape, lambda i: (0, 0))
    rows = lambda w: pl.BlockSpec((tm, w), lambda i: (i, 0))
    return pl.pallas_call(
        _merge_kernel,
        grid=(n // tm,),
        in_specs=[rows(DA_WIDTH), rows(DS_WIDTH), rows(2 * D_MODEL), rows(D_MODEL), full(wa), full(wb), full(wo)],
        out_specs=rows(D_MODEL),
        out_shape=jax.ShapeDtypeStruct((n, D_MODEL), F32),
        compiler_params=pltpu.CompilerParams(dimension_semantics=("arbitrary",), vmem_limit_bytes=VMEM_LIMIT),
        name="merge",
    )(oa, ob, gate, x2, wa, wb, wo)


def _norm_matmul_kernel(x_ref, g_ref, w_ref, o_ref):
    h = _rms(x_ref[...], g_ref[...]).astype(BF16)
    o_ref[...] = jnp.dot(h, w_ref[...], preferred_element_type=F32).astype(o_ref.dtype)


def _norm_matmul(x2, g, w, tm):
    n = x2.shape[0]
    return pl.pallas_call(
        _norm_matmul_kernel,
        grid=(n // tm,),
        in_specs=[pl.BlockSpec((tm, x2.shape[1]), lambda i: (i, 0)), pl.BlockSpec(g.shape, lambda i: (0, 0)),
                  pl.BlockSpec(w.shape, lambda i: (0, 0))],
        out_specs=pl.BlockSpec((tm, w.shape[1]), lambda i: (i, 0)),
        out_shape=jax.ShapeDtypeStruct((n, w.shape[1]), BF16),
        compiler_params=pltpu.CompilerParams(dimension_semantics=("arbitrary",), vmem_limit_bytes=VMEM_LIMIT),
        name="mem_kv_proj",
    )(x2, g, w)


def _mem_attn_kernel(x_ref, g_ref, wq_ref, kv_ref, wo_ref, o_ref):
    x = x_ref[...]
    hn = _rms(x, g_ref[...]).astype(BF16)
    q = jnp.dot(hn, wq_ref[...], preferred_element_type=F32).astype(BF16)
    scale = MEM_HEAD_DIM ** -0.5
    heads = []
    for h in range(MEM_HEADS):
        qh = q[:, h * MEM_HEAD_DIM:(h + 1) * MEM_HEAD_DIM]
        kh = kv_ref[:, h * MEM_HEAD_DIM:(h + 1) * MEM_HEAD_DIM]
        vh = kv_ref[:, MEM_WIDTH + h * MEM_HEAD_DIM:MEM_WIDTH + (h + 1) * MEM_HEAD_DIM]
        s = lax.dot_general(qh, kh, NT_DIMS, preferred_element_type=F32) * scale
        m = jnp.max(s, axis=1, keepdims=True)
        p = jnp.exp(s - m)
        l = jnp.sum(p, axis=1, keepdims=True)
        heads.append(jnp.dot(p.astype(BF16), vh, preferred_element_type=F32) / l)
    o = jnp.concatenate(heads, axis=1).astype(BF16)
    o_ref[...] = x + jnp.dot(o, wo_ref[...], preferred_element_type=F32)


def _mem_attn(x2, g, wq, kv, wo, b, s_len, m_tok, tm):
    nb = s_len // tm
    n = x2.shape[0]
    full = lambda a: pl.BlockSpec(a.shape, lambda bi, i: (0, 0))
    return pl.pallas_call(
        _mem_attn_kernel,
        grid=(b, nb),
        in_specs=[pl.BlockSpec((tm, D_MODEL), lambda bi, i: (bi * nb + i, 0)), full(g), full(wq),
                  pl.BlockSpec((m_tok, 2 * MEM_WIDTH), lambda bi, i: (bi, 0)), full(wo)],
        out_specs=pl.BlockSpec((tm, D_MODEL), lambda bi, i: (bi * nb + i, 0)),
        out_shape=jax.ShapeDtypeStruct((n, D_MODEL), F32),
        compiler_params=pltpu.CompilerParams(dimension_semantics=("arbitrary",) * 2, vmem_limit_bytes=VMEM_LIMIT),
        name="mem_attn",
    )(x2, g, wq, kv, wo)


N_SUB = 2 * PEER_HEADS
EXPERT_BITS = 14
SLOT_DEPTH = 4
ROUTE_UNROLL = 8


def _route_kernel(x_ref, g_ref, wq_ref, keys_ref, hf_ref, idx_ref, row_ref, gate_ref, sc_scr, ts_scr, ti_scr):
    hf = _rms(x_ref[...], g_ref[...])
    tm = hf.shape[0]
    hb = hf.astype(BF16)
    bits = pltpu.bitcast(hb.astype(F32), jnp.uint32)
    half_d = D_MODEL // 2
    for s in range(SUBLANES // 2):
        lo = bits[:, s * LANES:(s + 1) * LANES] >> 16
        hi = bits[:, half_d + s * LANES:half_d + (s + 1) * LANES] & jnp.uint32(0xFFFF0000)
        hf_ref[:, s, :] = lo | hi
        hf_ref[:, s + SUBLANES // 2, :] = lo | hi
    q = jnp.dot(hb, wq_ref[...], preferred_element_type=F32).astype(BF16)
    for g in range(N_SUB):
        blk = q[:, (g // 2) * LANES:(g // 2 + 1) * LANES]
        sc_scr[g] = lax.dot_general(keys_ref[g], blk, NT_DIMS, preferred_element_type=F32)

    n_slot = PEER_N_KEYS // SLOT_DEPTH
    slot_id = lax.broadcasted_iota(I32, (n_slot, tm), 0)

    def sub_topk(g, carry):
        lv = [sc_scr[g, d * n_slot:(d + 1) * n_slot, :] for d in range(SLOT_DEPTH)]
        li = [slot_id + d * n_slot for d in range(SLOT_DEPTH)]
        for x, y in ((0, 1), (2, 3), (0, 2), (1, 3), (1, 2)):
            swap = (lv[y] > lv[x]) | ((lv[y] == lv[x]) & (li[y] < li[x]))
            lv[x], lv[y] = jnp.where(swap, lv[y], lv[x]), jnp.where(swap, lv[x], lv[y])
            li[x], li[y] = jnp.where(swap, li[y], li[x]), jnp.where(swap, li[x], li[y])
        vals, ids = [], []
        for _ in range(PEER_TOPK):
            m = jnp.max(lv[0], axis=0, keepdims=True)
            idx = jnp.min(jnp.where(lv[0] == m, li[0], PEER_N_KEYS), axis=0, keepdims=True)
            vals.append(m)
            ids.append(idx)
            hit = li[0] == idx
            for d in range(SLOT_DEPTH - 1):
                lv[d] = jnp.where(hit, lv[d + 1], lv[d])
                li[d] = jnp.where(hit, li[d + 1], li[d])
            lv[-1] = jnp.where(hit, -jnp.inf, lv[-1])
        ts_scr[g] = jnp.concatenate(vals, axis=0)
        ti_scr[g] = jnp.concatenate(ids, axis=0)
        return carry

    def sub_topk_trip(p, carry):
        for u in range(ROUTE_UNROLL):
            sub_topk(ROUTE_UNROLL * p + u, carry)
        return carry

    lax.fori_loop(0, N_SUB // ROUTE_UNROLL, sub_topk_trip, 0)

    iota16 = lax.broadcasted_iota(I32, (PEER_TOPK, tm), 0)
    iota8 = lax.broadcasted_iota(I32, (SUBLANES, tm), 0)
    lead_rows = SUBLANES

    def head_topk(h, carry):
        s0, s1 = ts_scr[2 * h], ts_scr[2 * h + 1]
        i0, i1 = ti_scr[2 * h], ti_scr[2 * h + 1]
        cands, codes = [], []
        for a in range(lead_rows):
            n_j = PEER_TOPK if a == 0 else SUBLANES
            cands.append(s0[a:a + 1] + s1[:n_j])
            j_iota = iota16 if a == 0 else iota8
            codes.append(((a * PEER_TOPK + j_iota) << EXPERT_BITS) | (i0[a:a + 1] * PEER_N_KEYS + i1[:n_j]))
        cands.append(s0[lead_rows:] + s1[0:1])
        codes.append((((iota8 + lead_rows) * PEER_TOPK) << EXPERT_BITS) | (i0[lead_rows:] * PEER_N_KEYS + i1[0:1]))
        split = 4
        a_v, a_c = jnp.concatenate(cands[:split], axis=0), jnp.concatenate(codes[:split], axis=0)
        b_v, b_c = jnp.concatenate(cands[split:], axis=0), jnp.concatenate(codes[split:], axis=0)
        take_b = (b_v > a_v) | ((b_v == a_v) & (b_c < a_c))
        cur, cur_c = jnp.where(take_b, b_v, a_v), jnp.where(take_b, b_c, a_c)
        res, res_c = jnp.where(take_b, a_v, b_v), jnp.where(take_b, a_c, b_c)
        code_max = jnp.int32(2 ** 30)
        vals, ids = [], []
        for _ in range(PEER_TOPK):
            m = jnp.max(cur, axis=0, keepdims=True)
            best = jnp.min(jnp.where(cur == m, cur_c, code_max), axis=0, keepdims=True)
            vals.append(m)
            ids.append(best & (2 ** EXPERT_BITS - 1))
            hit = cur_c == best
            cur = jnp.where(hit, res, cur)
            cur_c = jnp.where(hit, res_c, cur_c)
            res = jnp.where(hit, -jnp.inf, res)
        best_s = jnp.concatenate(vals, axis=0)
        e = jnp.exp(best_s - best_s[0:1])
        gate_ref[h] = e / jnp.sum(e, axis=0, keepdims=True)
        best_i = jnp.concatenate(ids, axis=0)
        idx_ref[h] = best_i
        row_ref[h] = (best_i >> 1) * SUBLANES
        return carry

    def head_topk_trip(p, carry):
        for u in range(ROUTE_UNROLL):
            head_topk(ROUTE_UNROLL * p + u, carry)
        return carry

    lax.fori_loop(0, PEER_HEADS // ROUTE_UNROLL, head_topk_trip, 0)


def _route(x2, g, wq, keys_p, tm):
    n = x2.shape[0]
    return pl.pallas_call(
        _route_kernel,
        grid=(n // tm,),
        in_specs=[pl.BlockSpec((tm, D_MODEL), lambda i: (i, 0)), pl.BlockSpec(g.shape, lambda i: (0, 0)),
                  pl.BlockSpec(wq.shape, lambda i: (0, 0)), pl.BlockSpec(keys_p.shape, lambda i: (0, 0, 0))],
        out_specs=[pl.BlockSpec((tm, SUBLANES, LANES), lambda i: (i, 0, 0))]
        + [pl.BlockSpec((PEER_HEADS, PEER_TOPK, tm), lambda i: (0, 0, i))] * 3,
        out_shape=[jax.ShapeDtypeStruct((n, SUBLANES, LANES), jnp.uint32),
                   jax.ShapeDtypeStruct((PEER_HEADS, PEER_TOPK, n), I32),
                   jax.ShapeDtypeStruct((PEER_HEADS, PEER_TOPK, n), I32),
                   jax.ShapeDtypeStruct((PEER_HEADS, PEER_TOPK, n), F32)],
        scratch_shapes=[pltpu.VMEM((N_SUB, PEER_N_KEYS, tm), F32), pltpu.VMEM((N_SUB, PEER_TOPK, tm), F32),
                        pltpu.VMEM((N_SUB, PEER_TOPK, tm), I32)],
        compiler_params=pltpu.CompilerParams(dimension_semantics=("arbitrary",), vmem_limit_bytes=VMEM_LIMIT),
        name="peer_route",
    )(x2, g, wq, keys_p)


TILE_ROWS = 2 * SUBLANES
PAIRS_PER_KTILE = 16
TOK_UNROLL = 16
SUM_UNROLL = 16


def _pack_table(t):
    e = t.shape[0]
    bits = lax.bitcast_convert_type(t.astype(BF16), jnp.uint16).astype(jnp.uint32)
    words = bits[:, :D_MODEL // 2] | (bits[:, D_MODEL // 2:] << 16)
    return words.reshape(e * (SUBLANES // 2), LANES)


def _gather_tiles(off_ref, tab_ref, u):
    tiles = []
    for k in range(PEER_SEL):
        start = pl.multiple_of(off_ref[u * PEER_SEL + k], SUBLANES)
        tiles.append(pltpu.bitcast(tab_ref[pl.ds(start, SUBLANES), :], BF16))
    return tiles


def _for_trips(rows_hbm, bufs, sems, tok0, unroll, n_trip, body):
    assert n_trip % 2 == 0
    words = unroll * PEER_SEL

    def fetch(trip, par):
        src = rows_hbm.at[pl.ds(pl.multiple_of((tok0 + trip * unroll) * PEER_SEL, words), words)]
        return pltpu.make_async_copy(src, bufs[par], sems.at[par])

    fetch(0, 0).start()
    fetch(1, 1).start()

    def pair(p, carry):
        for par in range(2):
            trip = 2 * p + par
            fetch(trip, par).wait()
            body(trip, bufs[par], par)
            fetch(jnp.minimum(trip + 2, n_trip - 2 + par), par).start()
        return carry

    lax.fori_loop(0, n_trip // 2, pair, 0)
    for par in range(2):
        fetch(n_trip - 2 + par, par).wait()


def _peer_dot_kernel(rows_hbm, tab_ref, h_ref, gate_ref, idxv_ref, sel_ref, ce_ref, co_ref,
                     ye_scr, yo_scr, ze_scr, zo_scr, off_a, off_b, sems, *, tt):
    ones = jnp.ones((SUBLANES, LANES), BF16)
    half_rows = PAIRS_PER_KTILE
    n_trip = tt // TOK_UNROLL

    def row_sums(i, slot, off_ref):
        for u in range(TOK_UNROLL):
            t = i * TOK_UNROLL + u
            hp = pltpu.bitcast(h_ref[t], BF16)
            prods = [tile * hp for tile in _gather_tiles(off_ref, tab_ref, u)]
            for kt in range(PEER_SEL // PAIRS_PER_KTILE):
                stack = jnp.concatenate(prods[kt * PAIRS_PER_KTILE:(kt + 1) * PAIRS_PER_KTILE], axis=0)
                y = jnp.dot(sel_ref[...], stack, preferred_element_type=F32)
                r0 = u * PEER_SEL + kt * half_rows
                ye_scr[slot, r0:r0 + half_rows, :] = y[:half_rows].astype(BF16)
                yo_scr[slot, r0:r0 + half_rows, :] = y[half_rows:].astype(BF16)

    def lane_sums(i, slot):
        ze = lax.dot_general(ones, ye_scr[slot], NT_DIMS, preferred_element_type=F32)
        zo = lax.dot_general(ones, yo_scr[slot], NT_DIMS, preferred_element_type=F32)
        for u in range(TOK_UNROLL):
            t = i * TOK_UNROLL + u
            ze_scr[pl.ds(t, 1), :] = ze[0:1, u * PEER_SEL:(u + 1) * PEER_SEL]
            zo_scr[pl.ds(t, 1), :] = zo[0:1, u * PEER_SEL:(u + 1) * PEER_SEL]

    def trip(i, off_ref, par):
        lane_sums(jnp.maximum(i - 1, 0), 1 - par)
        row_sums(i, par, off_ref)

    ye_scr[1] = jnp.zeros(ye_scr.shape[1:], ye_scr.dtype)
    yo_scr[1] = jnp.zeros(yo_scr.shape[1:], yo_scr.dtype)
    _for_trips(rows_hbm, (off_a, off_b), sems, pl.program_id(0) * tt, TOK_UNROLL, n_trip, trip)
    lane_sums(n_trip - 1, (n_trip - 1) & 1)
    even = (idxv_ref[...] & 1) == 0
    a = jnp.where(even, ze_scr[...], zo_scr[...])
    c = 0.5 * a * (1.0 + lax.erf(a * (2.0 ** -0.5))) * gate_ref[...]
    ce_ref[...] = jnp.where(even, c, 0.0)
    co_ref[...] = jnp.where(even, 0.0, c)


def _peer_dot(rows_flat, tab, hpk, gate, idx2, sel, tt):
    n = hpk.shape[0]
    kern = functools.partial(_peer_dot_kernel, tt=tt)
    rows = pl.BlockSpec((tt, PEER_SEL), lambda i: (i, 0))
    return pl.pallas_call(
        kern,
        grid=(n // tt,),
        in_specs=[pl.BlockSpec(memory_space=pl.ANY),
                  pl.BlockSpec(memory_space=pltpu.VMEM),
                  pl.BlockSpec((tt, SUBLANES, LANES), lambda i: (i, 0, 0)),
                  rows, rows, pl.BlockSpec(sel.shape, lambda i: (0, 0))],
        out_specs=[rows, rows],
        out_shape=[jax.ShapeDtypeStruct((n, PEER_SEL), F32), jax.ShapeDtypeStruct((n, PEER_SEL), F32)],
        scratch_shapes=[pltpu.VMEM((2, TOK_UNROLL * PEER_SEL, LANES), BF16)] * 2
        + [pltpu.VMEM((tt, PEER_SEL), F32)] * 2
        + [pltpu.SMEM((TOK_UNROLL * PEER_SEL,), I32)] * 2 + [pltpu.SemaphoreType.DMA((2,))],
        compiler_params=pltpu.CompilerParams(dimension_semantics=("arbitrary",), vmem_limit_bytes=VMEM_LIMIT),
        name="peer_dot",
    )(rows_flat, tab, hpk, gate, idx2, sel)


def _peer_sum_kernel(rows_hbm, ce_ref, co_ref, spread_ref, tab_ref, o_ref, m1_scr, m2_scr, off_a, off_b, sems,
                     *, tt):
    width = PEER_SEL * TILE_ROWS
    cc = jnp.concatenate([ce_ref[...], co_ref[...]], axis=1)
    c1 = cc.astype(BF16)
    c2 = (cc - c1.astype(F32)).astype(BF16)
    m1_scr[...] = jnp.dot(c1, spread_ref[...], preferred_element_type=F32)
    m2_scr[...] = jnp.dot(c2, spread_ref[...], preferred_element_type=F32)
    row = lax.broadcasted_iota(I32, (SUBLANES, width), 0)
    lane = lax.broadcasted_iota(I32, (SUBLANES, width), 1)
    half = SUBLANES // 2
    on_row = (lane & (SUBLANES - 1)) == 2 * (row % half) + row // half

    def token(t, off_ref, u):
        w = jnp.concatenate(_gather_tiles(off_ref, tab_ref, u), axis=0)
        lhs = []
        for m_scr in (m1_scr, m2_scr):
            coef = jnp.broadcast_to(m_scr[pl.ds(t, 1), :], (SUBLANES, width))
            lhs.append(jnp.where(on_row, coef, 0.0).astype(BF16))
        res = jnp.dot(jnp.concatenate(lhs, axis=0), w, preferred_element_type=F32)
        o_ref[t] = res[:SUBLANES] + res[SUBLANES:]

    def trip(i, off_ref, par):
        for u in range(SUM_UNROLL):
            token(i * SUM_UNROLL + u, off_ref, u)

    _for_trips(rows_hbm, (off_a, off_b), sems, pl.program_id(0) * tt, SUM_UNROLL, tt // SUM_UNROLL, trip)


def _peer_sum(rows_flat, ce, co, spread, tab, tt):
    n = ce.shape[0]
    kern = functools.partial(_peer_sum_kernel, tt=tt)
    rows = pl.BlockSpec((tt, PEER_SEL), lambda i: (i, 0))
    width = PEER_SEL * TILE_ROWS
    return pl.pallas_call(
        kern,
        grid=(n // tt,),
        in_specs=[pl.BlockSpec(memory_space=pl.ANY), rows, rows,
                  pl.BlockSpec(spread.shape, lambda i: (0, 0)), pl.BlockSpec(memory_space=pltpu.VMEM)],
        out_specs=pl.BlockSpec((tt, SUBLANES, LANES), lambda i: (i, 0, 0)),
        out_shape=jax.ShapeDtypeStruct((n, SUBLANES, LANES), F32),
        scratch_shapes=[pltpu.VMEM((tt, width), F32), pltpu.VMEM((tt, width), F32)]
        + [pltpu.SMEM((SUM_UNROLL * PEER_SEL,), I32)] * 2 + [pltpu.SemaphoreType.DMA((2,))],
        compiler_params=pltpu.CompilerParams(dimension_semantics=("arbitrary",), vmem_limit_bytes=VMEM_LIMIT),
        name="peer_sum",
    )(rows_flat, ce, co, spread, tab)


def _final_kernel(x_ref, p_ref, g_ref, o_ref):
    peer = jnp.concatenate([p_ref[:, s, :] for s in range(SUBLANES)], axis=1)
    o_ref[...] = _rms(x_ref[...] + peer, g_ref[...])


def _final(x2, peer3, g, tm):
    n = x2.shape[0]
    return pl.pallas_call(
        _final_kernel,
        grid=(n // tm,),
        in_specs=[pl.BlockSpec((tm, D_MODEL), lambda i: (i, 0)),
                  pl.BlockSpec((tm, SUBLANES, LANES), lambda i: (i, 0, 0)),
                  pl.BlockSpec(g.shape, lambda i: (0, 0))],
        out_specs=pl.BlockSpec((tm, D_MODEL), lambda i: (i, 0)),
        out_shape=jax.ShapeDtypeStruct((n, D_MODEL), F32),
        compiler_params=pltpu.CompilerParams(dimension_semantics=("arbitrary",), vmem_limit_bytes=VMEM_LIMIT),
        name="final_norm",
    )(x2, peer3, g)


ROW_BLOCK = 256
PEER_BLOCK = 256


def _tile(n, pref):
    t = pref
    while n % t:
        t //= 2
    return t


def _layer(l, x2, mem2, pos, invf, b, s_len, m_tok, norm_mix_g, w_in, da_lambda, da_subln_g, w_branch_a,
           w_branch_b, gate_bias, w_out, norm_mem_g, mem_kv_norm_g, w_mem_q, w_mem_kv, w_mem_o, norm_ffn_g,
           peer_w_q, peer_sub_keys, peer_u, peer_v):
    n = b * s_len
    row2 = lambda v: v.reshape(1, -1)
    splits = [0]
    for c in COL_SIZES:
        splits.append(splits[-1] + c)
    da_q, da_k, da_v, ds_q, ds_k, ds_v, ix_q, ix_k, ix_w, gates = (
        w_in[l][:, splits[j]:splits[j + 1]] for j in range(len(COL_SIZES)))
    k0, k1 = ds_k[:, :DS_HEAD_DIM], ds_k[:, DS_HEAD_DIM:]
    w_keys = jnp.concatenate([da_k, k0, k0, k1, k1, ix_k, ix_k], axis=1).astype(BF16)
    w_q_t = jnp.concatenate([da_q, ds_q, ix_q], axis=1).T.astype(BF16)
    w_v_t = jnp.concatenate([da_v, ds_v], axis=1).T.astype(BF16)
    t_att = _tile(s_len, ROW_BLOCK)
    t_proj = t_att
    pr, gate, qt, vt3, iwt = _in_proj(x2, row2(norm_mix_g[l]), pos, invf, pos.reshape(1, n),
                                      invf[:, :ROPE_HALF].reshape(ROPE_HALF, 1), row2(gate_bias[l]), w_keys,
                                      gates.astype(BF16), w_q_t, w_v_t, ix_w.T.astype(BF16), t_proj, t_att)

    lam_init = 0.8 - 0.6 * math.exp(-0.3 * l)
    o_a = _diff_attention(da_lambda[l], qt, pr, vt3, da_subln_g[l].reshape(-1, 1), b, s_len, t_att, lam_init)
    o_b = _dsa_attention(qt, pr, vt3, iwt, b, s_len, t_att)
    x2 = _merge(o_a, o_b, gate, x2, w_branch_a[l].astype(BF16), w_branch_b[l].astype(BF16),
                w_out[l].astype(BF16), _tile(n, ROW_BLOCK))

    kv = _norm_matmul(mem2, row2(mem_kv_norm_g[l]), w_mem_kv[l].astype(BF16), _tile(mem2.shape[0], ROW_BLOCK))
    x2 = _mem_attn(x2, row2(norm_mem_g[l]), w_mem_q[l].astype(BF16), kv, w_mem_o[l].astype(BF16),
                   b, s_len, m_tok, _tile(s_len, ROW_BLOCK))

    sk = peer_sub_keys[l].reshape(N_SUB, PEER_N_KEYS, PEER_HALF)
    z = jnp.zeros_like(sk)
    keys_p = jnp.where((jnp.arange(N_SUB) % 2 == 0)[:, None, None],
                       jnp.concatenate([sk, z], axis=2), jnp.concatenate([z, sk], axis=2)).astype(BF16)
    hpk, idx_t, row_t, gate_t = _route(x2, row2(norm_ffn_g[l]), peer_w_q[l].astype(BF16), keys_p, _tile(n, ROW_BLOCK))
    idx2 = idx_t.reshape(PEER_SEL, n).T
    rows2 = row_t.reshape(PEER_SEL, n).T
    gate2 = gate_t.reshape(PEER_SEL, n).T
    tt = _tile(n, PEER_BLOCK)
    r32 = jnp.arange(2 * PAIRS_PER_KTILE)[:, None]
    c256 = jnp.arange(PAIRS_PER_KTILE * TILE_ROWS)[None, :]
    sel = ((c256 // TILE_ROWS == r32 % PAIRS_PER_KTILE)
           & ((c256 % TILE_ROWS) // SUBLANES == r32 // PAIRS_PER_KTILE)).astype(BF16)
    k256 = jnp.arange(2 * PEER_SEL)[:, None]
    c2048 = jnp.arange(PEER_SEL * TILE_ROWS)[None, :]
    spread = ((c2048 // TILE_ROWS == k256 % PEER_SEL)
              & ((c2048 % TILE_ROWS) // SUBLANES == k256 // PEER_SEL)).astype(BF16)
    rows_flat = rows2.reshape(-1)
    ce2, co2 = _peer_dot(rows_flat, _pack_table(peer_u[l]), hpk, gate2, idx2, sel, tt)
    peer3 = _peer_sum(rows_flat, ce2, co2, spread, _pack_table(peer_v[l]), tt)
    return x2, peer3


def kernel(x, mem, positions, norm_mix_g, w_in, da_lambda, da_subln_g, w_branch_a, w_branch_b, gate_bias, w_out, norm_mem_g, mem_kv_norm_g, w_mem_q, w_mem_kv, w_mem_o, norm_ffn_g, peer_w_q, peer_sub_keys, peer_u, peer_v, final_norm_g):
    b, s_len, d = x.shape
    m_tok = mem.shape[1]
    n = b * s_len
    depth = w_in.shape[0]
    x2 = x.reshape(n, d)
    mem2 = mem.reshape(b * m_tok, d)
    pos = positions.astype(F32).reshape(n, 1)
    inv_freq = ROPE_THETA ** (-(jnp.arange(ROPE_HALF, dtype=F32) * 2.0) / ROPE_DIM)
    invf = jnp.tile(inv_freq, LANES // ROPE_HALF).reshape(1, LANES)
    peer3 = None
    for l in range(depth):
        if peer3 is not None:
            x2 = x2 + peer3.reshape(n, d)
        x2, peer3 = _layer(l, x2, mem2, pos, invf, b, s_len, m_tok, norm_mix_g, w_in, da_lambda, da_subln_g,
                           w_branch_a, w_branch_b, gate_bias, w_out, norm_mem_g, mem_kv_norm_g, w_mem_q, w_mem_kv,
                           w_mem_o, norm_ffn_g, peer_w_q, peer_sub_keys, peer_u, peer_v)
    out = _final(x2, peer3, final_norm_g.reshape(1, d), _tile(n, ROW_BLOCK))
    return out.reshape(b, s_len, d)
```
